```python
import jax, jax.numpy as jnp
from jax import lax
import numpy as np

D_MODEL = 1024
BATCH = 8
SEQ = 8192
DEPTH = 2

HEAD_DIM = 64
ATTN_GROUPS = ((128, 1), (512, 4), (2048, 16))
N_GROUPS = len(ATTN_GROUPS)
HEADS_PER_GROUP = 6
N_ATTN_HEADS = N_GROUPS * HEADS_PER_GROUP
ATTN_WIDTH = N_ATTN_HEADS * HEAD_DIM
ATTN_OUT_WIDTH = HEADS_PER_GROUP * HEAD_DIM
NUM_BUCKETS = 32
MAX_DISTANCE = 2048
RET_HEADS = 4
RET_QK_DIM = 256
RET_V_DIM = 2 * RET_QK_DIM
RET_QK_WIDTH = RET_HEADS * RET_QK_DIM
RET_V_WIDTH = RET_HEADS * RET_V_DIM
RET_CHUNK = 128
ROPE_BASE = 10000.0
D_FF = -(-8 * D_MODEL // (3 * 256)) * 256
ALPHA = (2 * DEPTH) ** 0.25
BETA = (8 * DEPTH) ** -0.25
LN_EPS = 1e-5
GN_EPS = 1e-5
SPLIT_SIZES = (ATTN_WIDTH, ATTN_WIDTH, ATTN_WIDTH,
               RET_QK_WIDTH, RET_QK_WIDTH, RET_V_WIDTH, RET_V_WIDTH,
               D_MODEL, D_MODEL)
IN_COLS = sum(SPLIT_SIZES)
SPLIT_POINTS = tuple(int(v) for v in np.cumsum(SPLIT_SIZES)[:-1])

kernel_name = "hybrid_dilated_attn_retention_deepnorm"


def _t5_bucket(dist):
    max_exact = NUM_BUCKETS // 2
    large = max_exact + (np.log(np.maximum(dist, max_exact) / max_exact)
                         / np.log(MAX_DISTANCE / max_exact)
                         * (NUM_BUCKETS - max_exact)).astype(np.int32)
    large = np.minimum(large, NUM_BUCKETS - 1)
    return np.where(dist < max_exact, dist, large).astype(np.int32)


def _layer_norm(x, g, b):
    xf = x.astype(jnp.float32)
    mu = jnp.mean(xf, axis=-1, keepdims=True)
    var = jnp.mean(jnp.square(xf - mu), axis=-1, keepdims=True)
    return ((xf - mu) * lax.rsqrt(var + LN_EPS) * g + b).astype(x.dtype)


def _dilated_window_attention(q, k, v, bias_table, window, dilation):
    B, S, H, Dh = q.shape
    W = window // dilation
    L = S // dilation
    nb = -(-L // W)
    Lp = nb * W

    def to_sub(t):
        t = t.reshape(B, L, dilation, H, Dh).transpose(0, 2, 3, 1, 4)
        return jnp.pad(t, ((0, 0), (0, 0), (0, 0), (0, Lp - L), (0, 0)))

    def band(t):
        t = jnp.pad(to_sub(t), ((0, 0), (0, 0), (0, 0), (W, 0), (0, 0)))
        prev = t[:, :, :, :Lp].reshape(B, dilation, H, nb, W, Dh)
        cur = t[:, :, :, W:].reshape(B, dilation, H, nb, W, Dh)
        return jnp.concatenate([prev, cur], axis=-2)

    qs = to_sub(q).reshape(B, dilation, H, nb, W, Dh)
    kb, vb = band(k), band(v)

    qi = np.arange(W)[:, None]
    kj = np.arange(2 * W)[None, :]
    rel = qi + W - kj
    in_win = (rel >= 0) & (rel <= W)
    key_idx = np.arange(nb)[:, None, None] * W + kj[None] - W
    mask = in_win[None] & (key_idx >= 0)
    buckets = _t5_bucket(np.clip(rel, 0, W) * dilation)
    bias = jnp.moveaxis(jnp.take(bias_table, buckets, axis=0), -1, 0).astype(jnp.float32)

    s = jnp.einsum('bghnqe,bghnke->bghnqk', qs, kb).astype(jnp.float32) * (Dh ** -0.5)
    s = s + bias[None, None, :, None]
    s = jnp.where(mask[None, None, None], s, -jnp.inf)
    m = jnp.max(s, axis=-1, keepdims=True)
    p = jnp.exp(s - m)
    l = jnp.sum(p, axis=-1, keepdims=True)
    o = jnp.einsum('bghnqk,bghnke->bghnqe', (p / l).astype(v.dtype), vb)
    lse = (m + jnp.log(l))[..., 0]
    o = o.reshape(B, dilation, H, Lp, Dh)[:, :, :, :L].transpose(0, 3, 1, 2, 4).reshape(B, S, H, Dh)
    lse = lse.reshape(B, dilation, H, Lp)[..., :L].transpose(0, 3, 1, 2).reshape(B, S, H)
    return o, lse


def _retention(q, k, v):
    B, S, H, dk = q.shape
    dv = v.shape[-1]
    half = dk // 2
    pos = jnp.arange(S, dtype=jnp.float32)
    inv_freq = ROPE_BASE ** (-jnp.arange(half, dtype=jnp.float32) / half)
    ang = pos[:, None] * inv_freq[None]
    cos = jnp.cos(ang)[None, :, None]
    sin = jnp.sin(ang)[None, :, None]

    def rot(t):
        t1, t2 = t[..., :half], t[..., half:]
        return jnp.concatenate([t1 * cos - t2 * sin, t1 * sin + t2 * cos], axis=-1).astype(t.dtype)

    q = rot(q)
    k = rot(k) * (dk ** -0.5)
    log_g = jnp.log(1.0 - 2.0 ** (-5.0 - jnp.arange(H, dtype=jnp.float32)))
    C = RET_CHUNK
    nC = S // C
    n = jnp.arange(C, dtype=jnp.float32)
    diff = n[:, None] - n[None, :]
    decay_mask = jnp.where(diff >= 0, jnp.exp(log_g[:, None, None] * jnp.maximum(diff, 0.0)), 0.0)
    q_dec = jnp.exp(log_g[:, None] * (n + 1.0))
    k_dec = jnp.exp(log_g[:, None] * (C - 1.0 - n))
    chunk_dec = jnp.exp(log_g * C)

    def chunks(t):
        return t.reshape(B, nC, C, H, t.shape[-1]).transpose(1, 0, 3, 2, 4)

    def step(state, xs):
        qc, kc, vc = xs
        sc = jnp.einsum('bhnd,bhmd->bhnm', qc, kc) * decay_mask
        o = (jnp.einsum('bhnm,bhmv->bhnv', sc, vc)
             + jnp.einsum('bhnd,bhdv->bhnv', qc * q_dec[..., None], state))
        state = (state * chunk_dec[:, None, None]
                 + jnp.einsum('bhmd,bhmv->bhdv', kc * k_dec[..., None], vc))
        return state.astype(jnp.float32), o.astype(jnp.float32)

    state0 = jnp.zeros((B, H, dk, dv), jnp.float32)
    _, ys = lax.scan(step, state0, (chunks(q), chunks(k), chunks(v)))
    o = ys.transpose(1, 0, 3, 2, 4).reshape(B, S, H, dv)
    mu = jnp.mean(o, axis=-1, keepdims=True)
    var = jnp.mean(jnp.square(o - mu), axis=-1, keepdims=True)
    return (o - mu) * lax.rsqrt(var + GN_EPS)


def _hybrid_mixer(x, rel_bias, w_in, b_in, w_attn_proj, w_ret_proj, w_out):
    B, S, _ = x.shape
    z = jnp.einsum('bsd,dc->bsc', x, w_in) + b_in
    q_a, k_a, v_a, q_r, k_r, v_r, g_r, gate_a, gate_b = jnp.split(z, SPLIT_POINTS, axis=-1)
    q_a = q_a.reshape(B, S, N_GROUPS, HEADS_PER_GROUP, HEAD_DIM)
    k_a = k_a.reshape(B, S, N_GROUPS, HEADS_PER_GROUP, HEAD_DIM)
    v_a = v_a.reshape(B, S, N_GROUPS, HEADS_PER_GROUP, HEAD_DIM)
    outs, lses = [], []
    for gi, (window, dilation) in enumerate(ATTN_GROUPS):
        o, lse = _dilated_window_attention(
            q_a[:, :, gi], k_a[:, :, gi], v_a[:, :, gi],
            rel_bias[:, gi * HEADS_PER_GROUP:(gi + 1) * HEADS_PER_GROUP], window, dilation)
        outs.append(o.astype(jnp.float32))
        lses.append(lse)
    wts = jax.nn.softmax(jnp.stack(lses, axis=0), axis=0)
    y_a = jnp.sum(wts[..., None] * jnp.stack(outs, axis=0), axis=0)
    y_a = y_a.astype(x.dtype).reshape(B, S, ATTN_OUT_WIDTH)
    y_r = _retention(q_r.reshape(B, S, RET_HEADS, RET_QK_DIM),
                     k_r.reshape(B, S, RET_HEADS, RET_QK_DIM),
                     v_r.reshape(B, S, RET_HEADS, RET_V_DIM))
    y_b = (jax.nn.silu(g_r) * y_r.reshape(B, S, RET_V_WIDTH)).astype(x.dtype)
    merged = (jax.nn.sigmoid(gate_a) * jnp.einsum('bsc,cd->bsd', y_a, w_attn_proj)
              + jax.nn.sigmoid(gate_b) * jnp.einsum('bsc,cd->bsd', y_b, w_ret_proj))
    return jnp.einsum('bsd,de->bse', merged, w_out)


def _swiglu(x, w_gate, w_up, w_down):
    h = jax.nn.silu(jnp.einsum('bsd,df->bsf', x, w_gate)) * jnp.einsum('bsd,df->bsf', x, w_up)
    return jnp.einsum('bsf,fd->bsd', h, w_down)


def _fwd_setup_inputs(seed: int = 0) -> dict:
    key = jax.random.key(seed)
    ks = jax.random.split(key, 14)
    f32 = jnp.float32

    def nrm(k, shape, scale):
        return jax.random.normal(k, shape, f32) * scale

    return {
        "x": nrm(ks[0], (BATCH, SEQ, D_MODEL), 1.0),
        "rel_bias": nrm(ks[1], (NUM_BUCKETS, N_ATTN_HEADS), 0.2),
        "w_in": nrm(ks[2], (DEPTH, D_MODEL, IN_COLS), D_MODEL ** -0.5),
        "b_in": nrm(ks[3], (DEPTH, IN_COLS), 0.02),
        "w_attn_proj": nrm(ks[4], (DEPTH, ATTN_OUT_WIDTH, D_MODEL), BETA * ATTN_OUT_WIDTH ** -0.5),
        "w_ret_proj": nrm(ks[5], (DEPTH, RET_V_WIDTH, D_MODEL), BETA * RET_V_WIDTH ** -0.5),
        "w_out": nrm(ks[6], (DEPTH, D_MODEL, D_MODEL), BETA * D_MODEL ** -0.5),
        "ln1_g": 1.0 + nrm(ks[7], (DEPTH, D_MODEL), 0.02),
        "ln1_b": nrm(ks[8], (DEPTH, D_MODEL), 0.02),
        "w_ffn_gate": nrm(ks[9], (DEPTH, D_MODEL, D_FF), D_MODEL ** -0.5),
        "w_ffn_up": nrm(ks[10], (DEPTH, D_MODEL, D_FF), D_MODEL ** -0.5),
        "w_ffn_down": nrm(ks[11], (DEPTH, D_FF, D_MODEL), BETA * D_FF ** -0.5),
        "ln2_g": 1.0 + nrm(ks[12], (DEPTH, D_MODEL), 0.02),
        "ln2_b": nrm(ks[13], (DEPTH, D_MODEL), 0.02),
    }


def _fwd_reference(x, rel_bias, w_in, b_in, w_attn_proj, w_ret_proj, w_out, ln1_g, ln1_b,
              w_ffn_gate, w_ffn_up, w_ffn_down, ln2_g, ln2_b):
    for l in range(DEPTH):
        mix = _hybrid_mixer(x, rel_bias, w_in[l], b_in[l], w_attn_proj[l], w_ret_proj[l], w_out[l])
        x = _layer_norm(ALPHA * x + mix, ln1_g[l], ln1_b[l])
        ffn = _swiglu(x, w_ffn_gate[l], w_ffn_up[l], w_ffn_down[l])
        x = _layer_norm(ALPHA * x + ffn, ln2_g[l], ln2_b[l])
    return x


import jax as _jax
import jax.numpy as _jnp

TWIN_FORMAT = 'train_step'
FWD_PARAMS = ['x', 'rel_bias', 'w_in', 'b_in', 'w_attn_proj', 'w_ret_proj', 'w_out', 'ln1_g', 'ln1_b', 'w_ffn_gate', 'w_ffn_up', 'w_ffn_down', 'ln2_g', 'ln2_b']
TWIN_WEIGHTS = ['rel_bias', 'w_in', 'b_in', 'w_attn_proj', 'w_ret_proj', 'w_out', 'ln1_g', 'ln1_b', 'w_ffn_gate', 'w_ffn_up', 'w_ffn_down', 'ln2_g', 'ln2_b']
TWIN_DIFF_INPUT = 'x'
TWIN_INPUTS = ['x', 'rel_bias', 'w_in', 'b_in', 'w_attn_proj', 'w_ret_proj', 'w_out', 'ln1_g', 'ln1_b', 'w_ffn_gate', 'w_ffn_up', 'w_ffn_down', 'ln2_g', 'ln2_b', 'loss_target', 'm_rel_bias', 'm_w_in', 'm_b_in', 'm_w_attn_proj', 'm_w_ret_proj', 'm_w_out', 'm_ln1_g', 'm_ln1_b', 'm_w_ffn_gate', 'm_w_ffn_up', 'm_w_ffn_down', 'm_ln2_g', 'm_ln2_b', 'v_rel_bias', 'v_w_in', 'v_b_in', 'v_w_attn_proj', 'v_w_ret_proj', 'v_w_out', 'v_ln1_g', 'v_ln1_b', 'v_w_ffn_gate', 'v_w_ffn_up', 'v_w_ffn_down', 'v_ln2_g', 'v_ln2_b']
TWIN_OUTPUTS = ['loss', 'grad_x', 'grad_rel_bias', 'grad_w_in', 'grad_b_in', 'grad_w_attn_proj', 'grad_w_ret_proj', 'grad_w_out', 'grad_ln1_g', 'grad_ln1_b', 'grad_w_ffn_gate', 'grad_w_ffn_up', 'grad_w_ffn_down', 'grad_ln2_g', 'grad_ln2_b', 'delta_rel_bias', 'delta_w_in', 'delta_b_in', 'delta_w_attn_proj', 'delta_w_ret_proj', 'delta_w_out', 'delta_ln1_g', 'delta_ln1_b', 'delta_w_ffn_gate', 'delta_w_ffn_up', 'delta_w_ffn_down', 'delta_ln2_g', 'delta_ln2_b', 'new_m_rel_bias', 'new_m_w_in', 'new_m_b_in', 'new_m_w_attn_proj', 'new_m_w_ret_proj', 'new_m_w_out', 'new_m_ln1_g', 'new_m_ln1_b', 'new_m_w_ffn_gate', 'new_m_w_ffn_up', 'new_m_w_ffn_down', 'new_m_ln2_g', 'new_m_ln2_b', 'new_v_rel_bias', 'new_v_w_in', 'new_v_b_in', 'new_v_w_attn_proj', 'new_v_w_ret_proj', 'new_v_w_out', 'new_v_ln1_g', 'new_v_ln1_b', 'new_v_w_ffn_gate', 'new_v_w_ffn_up', 'new_v_w_ffn_down', 'new_v_ln2_g', 'new_v_ln2_b']
TWIN_LEAF_KINDS = {'loss': 'loss', 'grad_x': 'grad_x', 'grad_rel_bias': 'grad_w', 'grad_w_in': 'grad_w', 'grad_b_in': 'grad_w', 'grad_w_attn_proj': 'grad_w', 'grad_w_ret_proj': 'grad_w', 'grad_w_out': 'grad_w', 'grad_ln1_g': 'grad_w', 'grad_ln1_b': 'grad_w', 'grad_w_ffn_gate': 'grad_w', 'grad_w_ffn_up': 'grad_w', 'grad_w_ffn_down': 'grad_w', 'grad_ln2_g': 'grad_w', 'grad_ln2_b': 'grad_w', 'delta_rel_bias': 'delta_w', 'delta_w_in': 'delta_w', 'delta_b_in': 'delta_w', 'delta_w_attn_proj': 'delta_w', 'delta_w_ret_proj': 'delta_w', 'delta_w_out': 'delta_w', 'delta_ln1_g': 'delta_w', 'delta_ln1_b': 'delta_w', 'delta_w_ffn_gate': 'delta_w', 'delta_w_ffn_up': 'delta_w', 'delta_w_ffn_down': 'delta_w', 'delta_ln2_g': 'delta_w', 'delta_ln2_b': 'delta_w', 'new_m_rel_bias': 'new_m', 'new_m_w_in': 'new_m', 'new_m_b_in': 'new_m', 'new_m_w_attn_proj': 'new_m', 'new_m_w_ret_proj': 'new_m', 'new_m_w_out': 'new_m', 'new_m_ln1_g': 'new_m', 'new_m_ln1_b': 'new_m', 'new_m_w_ffn_gate': 'new_m', 'new_m_w_ffn_up': 'new_m', 'new_m_w_ffn_down': 'new_m', 'new_m_ln2_g': 'new_m', 'new_m_ln2_b': 'new_m', 'new_v_rel_bias': 'new_v', 'new_v_w_in': 'new_v', 'new_v_b_in': 'new_v', 'new_v_w_attn_proj': 'new_v', 'new_v_w_ret_proj': 'new_v', 'new_v_w_out': 'new_v', 'new_v_ln1_g': 'new_v', 'new_v_ln1_b': 'new_v', 'new_v_w_ffn_gate': 'new_v', 'new_v_w_ffn_up': 'new_v', 'new_v_w_ffn_down': 'new_v', 'new_v_ln2_g': 'new_v', 'new_v_ln2_b': 'new_v'}


def _forward(args):
    return _fwd_reference(*[args[k] for k in FWD_PARAMS])


def _output_shape():
    def fwd():
        inp = _fwd_setup_inputs(0)
        return _fwd_reference(*[inp[k] for k in FWD_PARAMS])
    out = _jax.eval_shape(fwd)
    return out.shape, out.dtype

N_MICROBATCH = 1
ADAM_LR = 0.001
ADAM_B1 = 0.9
ADAM_B2 = 0.999
ADAM_EPS = 1e-08
ADAM_WD = 0.01
ADAM_STEP = 10
PER_EXAMPLE_BATCH_AXIS = {'x': 0, 'loss_target': 0}
SHARED_INPUTS = []
_WEIGHT_DTYPES = {'rel_bias': _jnp.float32, 'w_in': _jnp.float32, 'b_in': _jnp.float32, 'w_attn_proj': _jnp.float32, 'w_ret_proj': _jnp.float32, 'w_out': _jnp.float32, 'ln1_g': _jnp.float32, 'ln1_b': _jnp.float32, 'w_ffn_gate': _jnp.float32, 'w_ffn_up': _jnp.float32, 'w_ffn_down': _jnp.float32, 'ln2_g': _jnp.float32, 'ln2_b': _jnp.float32}
MOMENT_SCALE = {'rel_bias': 7.395246e-03, 'w_in': 9.392948e-03, 'b_in': 1.539910e-02, 'w_attn_proj': 9.694584e-03, 'w_ret_proj': 2.959924e-02, 'w_out': 3.114468e-02, 'ln1_g': 2.008668e+00, 'ln1_b': 9.979390e-01, 'w_ffn_gate': 3.262237e-02, 'w_ffn_up': 3.165694e-02, 'w_ffn_down': 1.049248e-01, 'ln2_g': 4.536471e+01, 'ln2_b': 1.595471e+00}


def _to_microbatches(a, axis):
    t = _jnp.moveaxis(a, axis, 0)
    t = t.reshape((N_MICROBATCH, t.shape[0] // N_MICROBATCH) + t.shape[1:])
    return _jnp.moveaxis(t, 1, axis + 1)


def setup_inputs(seed: int = 0) -> dict:
    inp = _fwd_setup_inputs(seed)
    key = _jax.random.fold_in(_jax.random.key(seed), 7919)
    shape, _ = _output_shape()
    out = dict(inp)
    out["loss_target"] = _jax.random.normal(_jax.random.fold_in(key, 0), shape, _jnp.float32)
    for i, name in enumerate(TWIN_WEIGHTS):
        w = inp[name].astype(_jnp.float32)
        if MOMENT_SCALE is None:
            s = _jnp.sqrt(_jnp.mean(_jnp.square(w)) + 1e-30)
        else:
            s = MOMENT_SCALE[name]
        km, kv = _jax.random.split(_jax.random.fold_in(key, i + 1))
        out[name] = w
        out["m_" + name] = s * _jax.random.normal(km, w.shape, _jnp.float32)
        out["v_" + name] = (s * s) * _jax.random.uniform(kv, w.shape, _jnp.float32, 0.5, 1.5)
    if N_MICROBATCH > 1:
        for name, axis in PER_EXAMPLE_BATCH_AXIS.items():
            out[name] = _to_microbatches(out[name], axis)
    return {'x': out['x'], 'rel_bias': out['rel_bias'], 'w_in': out['w_in'], 'b_in': out['b_in'], 'w_attn_proj': out['w_attn_proj'], 'w_ret_proj': out['w_ret_proj'], 'w_out': out['w_out'], 'ln1_g': out['ln1_g'], 'ln1_b': out['ln1_b'], 'w_ffn_gate': out['w_ffn_gate'], 'w_ffn_up': out['w_ffn_up'], 'w_ffn_down': out['w_ffn_down'], 'ln2_g': out['ln2_g'], 'ln2_b': out['ln2_b'], 'loss_target': out['loss_target'], 'm_rel_bias': out['m_rel_bias'], 'm_w_in': out['m_w_in'], 'm_b_in': out['m_b_in'], 'm_w_attn_proj': out['m_w_attn_proj'], 'm_w_ret_proj': out['m_w_ret_proj'], 'm_w_out': out['m_w_out'], 'm_ln1_g': out['m_ln1_g'], 'm_ln1_b': out['m_ln1_b'], 'm_w_ffn_gate': out['m_w_ffn_gate'], 'm_w_ffn_up': out['m_w_ffn_up'], 'm_w_ffn_down': out['m_w_ffn_down'], 'm_ln2_g': out['m_ln2_g'], 'm_ln2_b': out['m_ln2_b'], 'v_rel_bias': out['v_rel_bias'], 'v_w_in': out['v_w_in'], 'v_b_in': out['v_b_in'], 'v_w_attn_proj': out['v_w_attn_proj'], 'v_w_ret_proj': out['v_w_ret_proj'], 'v_w_out': out['v_w_out'], 'v_ln1_g': out['v_ln1_g'], 'v_ln1_b': out['v_ln1_b'], 'v_w_ffn_gate': out['v_w_ffn_gate'], 'v_w_ffn_up': out['v_w_ffn_up'], 'v_w_ffn_down': out['v_w_ffn_down'], 'v_ln2_g': out['v_ln2_g'], 'v_ln2_b': out['v_ln2_b']}


def _loss(weights, diff, rest, loss_target):
    with _jax.named_scope("forward"):
        args = {**rest, TWIN_DIFF_INPUT: diff, **{k: w.astype(_WEIGHT_DTYPES[k]) for k, w in weights.items()}}
        y = _forward(args)
    with _jax.named_scope("loss_head"):
        err = _jnp.square(y.astype(_jnp.float32) - loss_target)
        return 0.5 * _jnp.sum(_jnp.mean(err, axis=-1)) if err.ndim else 0.5 * err


def _adamw(w, g, m, v):
    m = ADAM_B1 * m + (1.0 - ADAM_B1) * g
    v = ADAM_B2 * v + (1.0 - ADAM_B2) * _jnp.square(g)
    m_hat = m / (1.0 - ADAM_B1 ** ADAM_STEP)
    v_hat = v / (1.0 - ADAM_B2 ** ADAM_STEP)
    delta = -ADAM_LR * (m_hat / (_jnp.sqrt(v_hat) + ADAM_EPS) + ADAM_WD * w)
    return delta, m, v


def reference(x, rel_bias, w_in, b_in, w_attn_proj, w_ret_proj, w_out, ln1_g, ln1_b, w_ffn_gate, w_ffn_up, w_ffn_down, ln2_g, ln2_b, loss_target, m_rel_bias, m_w_in, m_b_in, m_w_attn_proj, m_w_ret_proj, m_w_out, m_ln1_g, m_ln1_b, m_w_ffn_gate, m_w_ffn_up, m_w_ffn_down, m_ln2_g, m_ln2_b, v_rel_bias, v_w_in, v_b_in, v_w_attn_proj, v_w_ret_proj, v_w_out, v_ln1_g, v_ln1_b, v_w_ffn_gate, v_w_ffn_up, v_w_ffn_down, v_ln2_g, v_ln2_b):
    given = dict(x=x, rel_bias=rel_bias, w_in=w_in, b_in=b_in, w_attn_proj=w_attn_proj, w_ret_proj=w_ret_proj, w_out=w_out, ln1_g=ln1_g, ln1_b=ln1_b, w_ffn_gate=w_ffn_gate, w_ffn_up=w_ffn_up, w_ffn_down=w_ffn_down, ln2_g=ln2_g, ln2_b=ln2_b, loss_target=loss_target, m_rel_bias=m_rel_bias, m_w_in=m_w_in, m_b_in=m_b_in, m_w_attn_proj=m_w_attn_proj, m_w_ret_proj=m_w_ret_proj, m_w_out=m_w_out, m_ln1_g=m_ln1_g, m_ln1_b=m_ln1_b, m_w_ffn_gate=m_w_ffn_gate, m_w_ffn_up=m_w_ffn_up, m_w_ffn_down=m_w_ffn_down, m_ln2_g=m_ln2_g, m_ln2_b=m_ln2_b, v_rel_bias=v_rel_bias, v_w_in=v_w_in, v_b_in=v_b_in, v_w_attn_proj=v_w_attn_proj, v_w_ret_proj=v_w_ret_proj, v_w_out=v_w_out, v_ln1_g=v_ln1_g, v_ln1_b=v_ln1_b, v_w_ffn_gate=v_w_ffn_gate, v_w_ffn_up=v_w_ffn_up, v_w_ffn_down=v_w_ffn_down, v_ln2_g=v_ln2_g, v_ln2_b=v_ln2_b)
    weights = {n: given[n] for n in TWIN_WEIGHTS}
    shared = {n: given[n] for n in SHARED_INPUTS}
    per_example = {n: given[n] for n in ['x']}
    grad_fn = _jax.value_and_grad(_loss, argnums=(0, 1))

    def one_microbatch(ex, loss_target):
        ex = dict(ex)
        diff = ex.pop(TWIN_DIFF_INPUT)
        return grad_fn(weights, diff, {**shared, **ex}, loss_target)

    if N_MICROBATCH == 1:
        loss, (grad_w, grad_x) = one_microbatch(per_example, given["loss_target"])
    else:
        def body(carry, xs):
            loss_sum, grad_sum = carry
            l_k, (gw_k, gx_k) = one_microbatch(xs[0], xs[1])
            with _jax.named_scope("update"):
                return (loss_sum + l_k, _jax.tree.map(_jnp.add, grad_sum, gw_k)), gx_k

        init = (_jnp.zeros((), _jnp.float32), _jax.tree.map(_jnp.zeros_like, weights))
        (loss, grad_w), grad_x = _jax.lax.scan(body, init, (per_example, given["loss_target"]))
    with _jax.named_scope("update"):
        delta_w, new_m, new_v = {}, {}, {}
        for n in TWIN_WEIGHTS:
            delta_w[n], new_m[n], new_v[n] = _adamw(weights[n], grad_w[n], given["m_" + n], given["v_" + n])
    return (loss, grad_x, *[grad_w[n] for n in TWIN_WEIGHTS], *[delta_w[n] for n in TWIN_WEIGHTS],
            *[new_m[n] for n in TWIN_WEIGHTS], *[new_v[n] for n in TWIN_WEIGHTS])
```

```python
import functools
import math

import numpy as np
import jax
import jax.numpy as jnp
from jax import lax
from jax.experimental import pallas as pl
from jax.experimental.pallas import tpu as pltpu

F32 = jnp.float32
BF16 = jnp.bfloat16
MESH = pl.DeviceIdType.MESH

D_MODEL = 1024
DEPTH = 2
HEAD_DIM = 64
ATTN_GROUPS = ((128, 1), (512, 4), (2048, 16))
HEADS_PER_GROUP = 6
GROUP_WIDTH = HEADS_PER_GROUP * HEAD_DIM
ATTN_BLOCK = 128
NUM_BUCKETS = 32
MAX_DISTANCE = 2048
RET_HEADS = 4
RET_QK_DIM = 256
RET_V_DIM = 512
RET_CHUNK = 128
ROPE_BASE = 10000.0
D_FF = 2816
ALPHA = (2 * DEPTH) ** 0.25
LN_EPS = 1e-5
GN_EPS = 1e-5
ATTN_W = 3 * GROUP_WIDTH
IN_COLS = 3 * ATTN_W + 2 * 1024 + 2 * 2048 + 2 * 1024
ADAM_LR, ADAM_B1, ADAM_B2, ADAM_EPS, ADAM_WD, ADAM_STEP = 0.001, 0.9, 0.999, 1e-08, 0.01, 10
N_DEV = 8
NEG = -1e30
LANES = 128
VMEM_LIMIT = 56 * 1024 * 1024

BIG_WEIGHTS = ("w_in", "w_attn_proj", "w_ret_proj", "w_out", "w_ffn_gate", "w_ffn_up", "w_ffn_down")
COL_SHARDED = ("w_in", "w_attn_proj", "w_ffn_gate", "w_ffn_up")
FULL_SHAPE = {"w_in": (D_MODEL, IN_COLS), "w_attn_proj": (GROUP_WIDTH, D_MODEL), "w_ret_proj": (2048, D_MODEL),
              "w_out": (D_MODEL, D_MODEL), "w_ffn_gate": (D_MODEL, D_FF), "w_ffn_up": (D_MODEL, D_FF),
              "w_ffn_down": (D_FF, D_MODEL)}
SMALL_WEIGHTS = ("rel_bias", "b_in", "ln1_g", "ln1_b", "ln2_g", "ln2_b")
SMALL_SHAPE = {"rel_bias": (NUM_BUCKETS, 18), "b_in": (DEPTH, IN_COLS), "ln1_g": (DEPTH, D_MODEL),
               "ln1_b": (DEPTH, D_MODEL), "ln2_g": (DEPTH, D_MODEL), "ln2_b": (DEPTH, D_MODEL)}
SMALL_ROWS = 256
ALL_WEIGHTS = ("rel_bias", "w_in", "b_in", "w_attn_proj", "w_ret_proj", "w_out", "ln1_g", "ln1_b",
               "w_ffn_gate", "w_ffn_up", "w_ffn_down", "ln2_g", "ln2_b")


def _cparams(sem=None):
    return pltpu.CompilerParams(dimension_semantics=sem, vmem_limit_bytes=VMEM_LIMIT)


def _div_tile(n, cap, unit):
    if n <= cap:
        return n
    best = None
    for t in range(unit, cap + 1, unit):
        if n % t == 0:
            best = t
    assert best is not None, (n, cap, unit)
    return best


def _mm(a, b, *, name, out_dtype=F32, bias=None, add=None, groups=None):
    M, K = a.shape
    K2, N = b.shape
    assert K == K2 and a.dtype == BF16 and b.dtype == BF16
    tm = _div_tile(M, 1024, 16)
    tn = N // groups if groups else _div_tile(N, 1536, LANES)
    tk = _div_tile(K, 1536, LANES)
    nk = K // tk
    has_bias, has_add = bias is not None, add is not None

    def body(*refs):
        a_ref, b_ref = refs[0], refs[1]
        pos = 2
        bias_ref = add_ref = None
        if has_bias:
            bias_ref = refs[pos]
            pos += 1
        if has_add:
            add_ref = refs[pos]
            pos += 1
        o_ref = refs[pos]

        def finish(r):
            if has_bias:
                r = r + bias_ref[...]
            if has_add:
                r = r + add_ref[...]
            o_ref[...] = r.astype(o_ref.dtype)

        part = jnp.dot(a_ref[...], b_ref[...], preferred_element_type=F32)
        if nk == 1:
            finish(part)
        else:
            acc_ref = refs[pos + 1]
            k = pl.program_id(2)

            @pl.when(k == 0)
            def _():
                acc_ref[...] = part

            @pl.when(k > 0)
            def _():
                acc_ref[...] += part

            @pl.when(k == nk - 1)
            def _():
                finish(acc_ref[...])

    in_specs = [pl.BlockSpec((tm, tk), lambda i, j, k: (i, k)),
                pl.BlockSpec((tk, tn), lambda i, j, k: (k, j))]
    args = [a, b]
    if has_bias:
        in_specs.append(pl.BlockSpec((1, tn), lambda i, j, k: (0, j)))
        args.append(bias.reshape(1, N).astype(F32))
    if has_add:
        in_specs.append(pl.BlockSpec((tm, tn), lambda i, j, k: (i, j)))
        args.append(add)
    if groups:
        out_shape = jax.ShapeDtypeStruct((groups, M, tn), out_dtype)
        out_spec = pl.BlockSpec((None, tm, tn), lambda i, j, k: (j, i, 0))
    else:
        out_shape = jax.ShapeDtypeStruct((M, N), out_dtype)
        out_spec = pl.BlockSpec((tm, tn), lambda i, j, k: (i, j))
    scratch = [pltpu.VMEM((tm, tn), F32)] if nk > 1 else []
    return pl.pallas_call(
        body, name=name, grid=(M // tm, N // tn, nk), in_specs=in_specs, out_specs=out_spec,
        out_shape=out_shape, scratch_shapes=scratch,
        compiler_params=_cparams(("parallel", "parallel", "arbitrary")))(*args)


def _row_spec(tr, w):
    return pl.BlockSpec((tr, w), lambda i: (i, 0))


def _vec_spec(w):
    return pl.BlockSpec((1, w), lambda i: (0, 0))


def _ln_fwd(x, sub, g, b, *, name):
    S, W = x.shape
    tr = 512

    def body(x_ref, s_ref, g_ref, b_ref, h_ref, y_ref, yb_ref):
        h = ALPHA * x_ref[...] + s_ref[...]
        mu = jnp.mean(h, axis=-1, keepdims=True)
        d = h - mu
        var = jnp.mean(d * d, axis=-1, keepdims=True)
        y = d * lax.rsqrt(var + LN_EPS) * g_ref[...] + b_ref[...]
        h_ref[...] = h
        y_ref[...] = y
        yb_ref[...] = y.astype(BF16)

    return pl.pallas_call(
        body, name=name, grid=(S // tr,),
        in_specs=[_row_spec(tr, W), _row_spec(tr, W), _vec_spec(W), _vec_spec(W)],
        out_specs=[_row_spec(tr, W)] * 3,
        out_shape=[jax.ShapeDtypeStruct((S, W), F32), jax.ShapeDtypeStruct((S, W), F32),
                   jax.ShapeDtypeStruct((S, W), BF16)],
        compiler_params=_cparams(("parallel",)))(x, sub, g.reshape(1, W), b.reshape(1, W))


def _ln_bwd(dy, h, g, *, name):
    S, W = dy.shape
    tr = 512

    def body(dy_ref, h_ref, g_ref, dhb_ref, res_ref, dg_ref, db_ref):
        @pl.when(pl.program_id(0) == 0)
        def _():
            dg_ref[...] = jnp.zeros_like(dg_ref)
            db_ref[...] = jnp.zeros_like(db_ref)

        hh = h_ref[...]
        mu = jnp.mean(hh, axis=-1, keepdims=True)
        d = hh - mu
        var = jnp.mean(d * d, axis=-1, keepdims=True)
        rstd = lax.rsqrt(var + LN_EPS)
        xhat = d * rstd
        dyv = dy_ref[...]
        dg_ref[...] += jnp.sum(dyv * xhat, axis=0, keepdims=True)
        db_ref[...] += jnp.sum(dyv, axis=0, keepdims=True)
        dxh = dyv * g_ref[...]
        dh = rstd * (dxh - jnp.mean(dxh, axis=-1, keepdims=True)
                     - xhat * jnp.mean(dxh * xhat, axis=-1, keepdims=True))
        dhb_ref[...] = dh.astype(BF16)
        res_ref[...] = ALPHA * dh

    return pl.pallas_call(
        body, name=name, grid=(S // tr,),
        in_specs=[_row_spec(tr, W), _row_spec(tr, W), _vec_spec(W)],
        out_specs=[_row_spec(tr, W), _row_spec(tr, W), _vec_spec(W), _vec_spec(W)],
        out_shape=[jax.ShapeDtypeStruct((S, W), BF16), jax.ShapeDtypeStruct((S, W), F32),
                   jax.ShapeDtypeStruct((1, W), F32), jax.ShapeDtypeStruct((1, W), F32)],
        compiler_params=_cparams(("arbitrary",)))(dy, h, g.reshape(1, W))


def _loss_fwd_bwd(y, target, *, name):
    S, W = y.shape
    tr = 512

    def body(y_ref, t_ref, dy_ref, acc_ref):
        @pl.when(pl.program_id(0) == 0)
        def _():
            acc_ref[...] = jnp.zeros_like(acc_ref)

        e = y_ref[...] - t_ref[...]
        acc_ref[...] += jnp.sum(jnp.sum(e * e, axis=-1, keepdims=True), axis=0, keepdims=True)
        dy_ref[...] = e * (1.0 / W)

    return pl.pallas_call(
        body, name=name, grid=(S // tr,),
        in_specs=[_row_spec(tr, W), _row_spec(tr, W)],
        out_specs=[_row_spec(tr, W), pl.BlockSpec((1, 1), lambda i: (0, 0))],
        out_shape=[jax.ShapeDtypeStruct((S, W), F32), jax.ShapeDtypeStruct((1, 1), F32)],
        compiler_params=_cparams(("arbitrary",)))(y, target)


def _combine_fwd(os_, ls_, *, name):
    S, W = os_[0].shape
    tr = 512

    def body(o0, o1, o2, l0, l1, l2, yb_ref, y_ref, w_ref):
        la, lb, lc = l0[...], l1[...], l2[...]
        m = jnp.maximum(jnp.maximum(la, lb), lc)
        ea, eb, ec = jnp.exp(la - m), jnp.exp(lb - m), jnp.exp(lc - m)
        inv = 1.0 / (ea + eb + ec)
        wa, wb, wc = ea * inv, eb * inv, ec * inv
        y = wa * o0[...] + wb * o1[...] + wc * o2[...]
        y_ref[...] = y
        yb_ref[...] = y.astype(BF16)
        w_ref[0] = wa
        w_ref[1] = wb
        w_ref[2] = wc

    return pl.pallas_call(
        body, name=name, grid=(S // tr,),
        in_specs=[_row_spec(tr, W)] * 6,
        out_specs=[_row_spec(tr, W), _row_spec(tr, W), pl.BlockSpec((3, tr, W), lambda i: (0, i, 0))],
        out_shape=[jax.ShapeDtypeStruct((S, W), BF16), jax.ShapeDtypeStruct((S, W), F32),
                   jax.ShapeDtypeStruct((3, S, W), F32)],
        compiler_params=_cparams(("parallel",)))(*os_, *ls_)


def _merge_fwd(gates, pa, pr, *, name):
    S, W = pa.shape
    tr = 512

    def body(g_ref, pa_ref, pr_ref, o_ref):
        o_ref[...] = (jax.nn.sigmoid(g_ref[0]) * pa_ref[...] + jax.nn.sigmoid(g_ref[1]) * pr_ref[...]).astype(BF16)

    return pl.pallas_call(
        body, name=name, grid=(S // tr,),
        in_specs=[pl.BlockSpec((2, tr, W), lambda i: (0, i, 0)), _row_spec(tr, W), _row_spec(tr, W)],
        out_specs=_row_spec(tr, W), out_shape=jax.ShapeDtypeStruct((S, W), BF16),
        compiler_params=_cparams(("parallel",)))(gates, pa, pr)


def _merge_bwd(dm, gates, pa, pr, *, name):
    S, W = pa.shape
    tr = 256

    def body(dm_ref, g_ref, pa_ref, pr_ref, dpa_ref, dpr_ref, dg_ref):
        dmv = dm_ref[...]
        sa, sb = jax.nn.sigmoid(g_ref[0]), jax.nn.sigmoid(g_ref[1])
        dpa_ref[...] = (dmv * sa).astype(BF16)
        dpr_ref[...] = (dmv * sb).astype(BF16)
        dg_ref[0] = dmv * pa_ref[...] * (sa * (1.0 - sa))
        dg_ref[1] = dmv * pr_ref[...] * (sb * (1.0 - sb))

    g3 = pl.BlockSpec((2, tr, W), lambda i: (0, i, 0))
    return pl.pallas_call(
        body, name=name, grid=(S // tr,),
        in_specs=[_row_spec(tr, W), g3, _row_spec(tr, W), _row_spec(tr, W)],
        out_specs=[_row_spec(tr, W), _row_spec(tr, W), g3],
        out_shape=[jax.ShapeDtypeStruct((S, W), BF16), jax.ShapeDtypeStruct((S, W), BF16),
                   jax.ShapeDtypeStruct((2, S, W), F32)],
        compiler_params=_cparams(("parallel",)))(dm, gates, pa, pr)


def _swiglu_fwd(uv, *, name):
    _, S, W = uv.shape
    tr = 256

    def body(uv_ref, o_ref):
        u = uv_ref[0]
        o_ref[...] = (u * jax.nn.sigmoid(u) * uv_ref[1]).astype(BF16)

    return pl.pallas_call(
        body, name=name, grid=(S // tr,),
        in_specs=[pl.BlockSpec((2, tr, W), lambda i: (0, i, 0))],
        out_specs=_row_spec(tr, W), out_shape=jax.ShapeDtypeStruct((S, W), BF16),
        compiler_params=_cparams(("parallel",)))(uv)


def _swiglu_bwd(dh, uv, *, name):
    _, S, W = uv.shape
    tr = 256

    def body(dh_ref, uv_ref, o_ref):
        u, v, d = uv_ref[0], uv_ref[1], dh_ref[...]
        sg = jax.nn.sigmoid(u)
        o_ref[:, 0:W] = (d * v * (sg * (1.0 + u * (1.0 - sg)))).astype(BF16)
        o_ref[:, W:2 * W] = (d * (u * sg)).astype(BF16)

    return pl.pallas_call(
        body, name=name, grid=(S // tr,),
        in_specs=[_row_spec(tr, W), pl.BlockSpec((2, tr, W), lambda i: (0, i, 0))],
        out_specs=_row_spec(tr, 2 * W), out_shape=jax.ShapeDtypeStruct((S, 2 * W), BF16),
        compiler_params=_cparams(("parallel",)))(dh, uv)


def _assemble_dz(da, dq_r, dk_r, dv_r, dg_r, dgates, *, name):
    S = dv_r.shape[0]
    tr = 128
    GW = GROUP_WIDTH

    def body(*refs):
        a_refs = refs[0:9]
        q_ref, k_ref, v_ref, g_ref, gt_ref, dz_ref, cs_ref = refs[9:]

        @pl.when(pl.program_id(0) == 0)
        def _():
            cs_ref[...] = jnp.zeros_like(cs_ref)

        def put(off, val):
            w = val.shape[-1]
            dz_ref[:, off:off + w] = val.astype(BF16)
            cs_ref[:, off:off + w] += jnp.sum(val, axis=0, keepdims=True)

        for which in range(3):
            for gi in range(3):
                put(which * ATTN_W + gi * GW, a_refs[3 * gi + which][...])
        off = 3 * ATTN_W
        put(off, q_ref[...])
        put(off + 1024, k_ref[...])
        put(off + 2048, v_ref[...])
        put(off + 4096, g_ref[...])
        put(off + 6144, gt_ref[0])
        put(off + 7168, gt_ref[1])

    flat_a = [t for grp in da for t in grp]
    return pl.pallas_call(
        body, name=name, grid=(S // tr,),
        in_specs=[_row_spec(tr, GW)] * 9 + [_row_spec(tr, 1024), _row_spec(tr, 1024),
                  _row_spec(tr, 2048), _row_spec(tr, 2048), pl.BlockSpec((2, tr, 1024), lambda i: (0, i, 0))],
        out_specs=[_row_spec(tr, IN_COLS), _vec_spec(IN_COLS)],
        out_shape=[jax.ShapeDtypeStruct((S, IN_COLS), BF16), jax.ShapeDtypeStruct((1, IN_COLS), F32)],
        compiler_params=_cparams(("arbitrary",)))(*flat_a, dq_r, dk_r, dv_r, dg_r, dgates)


def _t5_bucket(dist):
    max_exact = NUM_BUCKETS // 2
    large = max_exact + (np.log(np.maximum(dist, max_exact) / max_exact)
                         / np.log(MAX_DISTANCE / max_exact) * (NUM_BUCKETS - max_exact)).astype(np.int32)
    large = np.minimum(large, NUM_BUCKETS - 1)
    return np.where(dist < max_exact, dist, large).astype(np.int32)


def _attn_tables(dilation):
    W = ATTN_BLOCK
    qi = np.arange(W)[:, None]
    kj = np.arange(2 * W)[None, :]
    rel = qi + W - kj
    valid = (rel >= 0) & (rel <= W)
    buckets = _t5_bucket(np.clip(rel, 0, W) * dilation)
    return buckets, valid


def _attn_bias(rel_bias, gi, dilation):
    buckets, valid = _attn_tables(dilation)
    table = rel_bias[:, gi * HEADS_PER_GROUP:(gi + 1) * HEADS_PER_GROUP]
    bias = jnp.moveaxis(jnp.take(table, jnp.asarray(buckets), axis=0), -1, 0).astype(F32)
    return jnp.where(jnp.asarray(valid)[None], bias, NEG)


def _dot_nt(a, b):
    return lax.dot_general(a, b, (((1,), (1,)), ((), ())), preferred_element_type=F32)


def _dot_tn(a, b):
    return lax.dot_general(a, b, (((0,), (0,)), ((), ())), preferred_element_type=F32)


def _dot(a, b):
    return jnp.dot(a, b, preferred_element_type=F32)


def _attn_fwd(qkv, bias, gi, *, name):
    _, L, WD = qkv.shape
    d = WD // GROUP_WIDTH
    nb = L // ATTN_BLOCK
    B, GW = ATTN_BLOCK, GROUP_WIDTH
    scale = HEAD_DIM ** -0.5

    def body(q_ref, kp_ref, kc_ref, vp_ref, vc_ref, b_ref, o_ref, l_ref):
        has_prev = pl.program_id(1) > 0
        for h in range(HEADS_PER_GROUP):
            sl = slice(HEAD_DIM * h, HEAD_DIM * (h + 1))
            q = q_ref[:, sl]
            sp = _dot_nt(q, kp_ref[:, sl]) * scale + b_ref[h, :, 0:B]
            sc = _dot_nt(q, kc_ref[:, sl]) * scale + b_ref[h, :, B:2 * B]
            sp = jnp.where(has_prev, sp, NEG)
            m = jnp.maximum(jnp.max(sp, axis=-1, keepdims=True), jnp.max(sc, axis=-1, keepdims=True))
            pp = jnp.exp(sp - m)
            pc = jnp.exp(sc - m)
            l = jnp.sum(pp, axis=-1, keepdims=True) + jnp.sum(pc, axis=-1, keepdims=True)
            inv = 1.0 / l
            o = _dot((pp * inv).astype(BF16), vp_ref[:, sl]) + _dot((pc * inv).astype(BF16), vc_ref[:, sl])
            o_ref[:, sl] = o
            l_ref[:, sl] = jnp.broadcast_to(m + jnp.log(l), (B, HEAD_DIM))

    def blk(which, prev):
        if prev:
            return pl.BlockSpec((None, B, GW), lambda r, n: (which * 3 + gi, jnp.maximum(n - 1, 0), r))
        return pl.BlockSpec((None, B, GW), lambda r, n: (which * 3 + gi, n, r))

    out_spec = pl.BlockSpec((B, GW), lambda r, n: (n, r))
    return pl.pallas_call(
        body, name=name, grid=(d, nb),
        in_specs=[blk(0, False), blk(1, True), blk(1, False), blk(2, True), blk(2, False),
                  pl.BlockSpec((HEADS_PER_GROUP, B, 2 * B), lambda r, n: (0, 0, 0))],
        out_specs=[out_spec, out_spec],
        out_shape=[jax.ShapeDtypeStruct((L, WD), F32), jax.ShapeDtypeStruct((L, WD), F32)],
        compiler_params=_cparams(("parallel", "arbitrary")))(qkv, qkv, qkv, qkv, qkv, bias)


def _attn_bwd(qkv, bias, lse, dya, ya, wts, gi, *, name):
    _, L, WD = qkv.shape
    d = WD // GROUP_WIDTH
    nb = L // ATTN_BLOCK
    B, GW = ATTN_BLOCK, GROUP_WIDTH
    scale = HEAD_DIM ** -0.5

    def body(q_ref, kp_ref, kc_ref, vp_ref, vc_ref, b_ref, l_ref, dya_ref, ya_ref, w_ref,
             dq_ref, dk_ref, dv_ref, db_ref, dk_carry, dv_carry):
        r, n = pl.program_id(0), pl.program_id(1)

        @pl.when((r == 0) & (n == 0))
        def _():
            db_ref[...] = jnp.zeros_like(db_ref)

        @pl.when(n == 0)
        def _():
            dk_carry[...] = jnp.zeros_like(dk_carry)
            dv_carry[...] = jnp.zeros_like(dv_carry)

        @pl.when(n < nb)
        def _():
            has_prev = n > 0
            for h in range(HEADS_PER_GROUP):
                sl = slice(HEAD_DIM * h, HEAD_DIM * (h + 1))
                q, kp, kc, vp, vc = q_ref[:, sl], kp_ref[:, sl], kc_ref[:, sl], vp_ref[:, sl], vc_ref[:, sl]
                lse_h = l_ref[:, HEAD_DIM * h:HEAD_DIM * h + 1]
                sp = _dot_nt(q, kp) * scale + b_ref[h, :, 0:B]
                sc = _dot_nt(q, kc) * scale + b_ref[h, :, B:2 * B]
                sp = jnp.where(has_prev, sp, NEG)
                pp = jnp.exp(sp - lse_h)
                pc = jnp.exp(sc - lse_h)
                dy = dya_ref[:, sl]
                w = w_ref[:, sl]
                cbar = jnp.sum(dy * ya_ref[:, sl], axis=-1, keepdims=True)
                shift = w[:, 0:1] * cbar
                do = (w * dy).astype(BF16)
                dsp = pp * (_dot_nt(do, vp) - shift)
                dsc = pc * (_dot_nt(do, vc) - shift)
                db_ref[h, :, 0:B] += dsp
                db_ref[h, :, B:2 * B] += dsc
                dspb, dscb = dsp.astype(BF16), dsc.astype(BF16)
                ppb, pcb = pp.astype(BF16), pc.astype(BF16)
                dq_ref[:, sl] = (_dot(dspb, kp) + _dot(dscb, kc)) * scale
                dk_ref[:, sl] = dk_carry[:, sl] + _dot_tn(dspb, q) * scale
                dv_ref[:, sl] = dv_carry[:, sl] + _dot_tn(ppb, do)
                dk_carry[:, sl] = _dot_tn(dscb, q) * scale
                dv_carry[:, sl] = _dot_tn(pcb, do)

        @pl.when(n == nb)
        def _():
            dk_ref[...] = dk_carry[...]
            dv_ref[...] = dv_carry[...]

    last = nb - 1

    def blk(which, prev):
        if prev:
            return pl.BlockSpec((None, B, GW),
                                lambda r, n: (which * 3 + gi, jnp.clip(n - 1, 0, last), r))
        return pl.BlockSpec((None, B, GW), lambda r, n: (which * 3 + gi, jnp.minimum(n, last), r))

    cur = pl.BlockSpec((B, GW), lambda r, n: (jnp.minimum(n, last), r))
    lag = pl.BlockSpec((B, GW), lambda r, n: (jnp.maximum(n - 1, 0), r))
    full_b = pl.BlockSpec((HEADS_PER_GROUP, B, 2 * B), lambda r, n: (0, 0, 0))
    dq, dk, dv, db = pl.pallas_call(
        body, name=name, grid=(d, nb + 1),
        in_specs=[blk(0, False), blk(1, True), blk(1, False), blk(2, True), blk(2, False), full_b,
                  cur, cur, cur, cur],
        out_specs=[cur, lag, lag, full_b],
        out_shape=[jax.ShapeDtypeStruct((L, WD), F32)] * 3
        + [jax.ShapeDtypeStruct((HEADS_PER_GROUP, B, 2 * B), F32)],
        scratch_shapes=[pltpu.VMEM((B, GW), F32), pltpu.VMEM((B, GW), F32)],
        compiler_params=_cparams(("arbitrary", "arbitrary")))(qkv, qkv, qkv, qkv, qkv, bias, lse, dya, ya, wts)
    return (dq, dk, dv), db


def _bias_grad(dbs, *, name):
    nk = ATTN_BLOCK * 2 * ATTN_BLOCK
    buckets = []
    for (_, dil) in ATTN_GROUPS:
        b, valid = _attn_tables(dil)
        buckets.append(np.where(valid, b, -1).reshape(1, nk))
    bk = jnp.asarray(np.stack(buckets).astype(np.int32))
    flat = [x.reshape(HEADS_PER_GROUP, nk) for x in dbs]

    def body(bk_ref, d0, d1, d2, o_ref):
        ids = lax.broadcasted_iota(jnp.int32, (NUM_BUCKETS, nk), 0)
        for gi, dref in enumerate((d0, d1, d2)):
            onehot = (ids == bk_ref[gi]).astype(F32)
            o_ref[gi] = lax.dot_general(onehot, dref[...], (((1,), (1,)), ((), ())),
                                        preferred_element_type=F32, precision=lax.Precision.HIGHEST)

    out = pl.pallas_call(
        body, name=name,
        out_shape=jax.ShapeDtypeStruct((3, NUM_BUCKETS, HEADS_PER_GROUP), F32),
        compiler_params=_cparams())(bk, *flat)
    return jnp.transpose(out, (1, 0, 2)).reshape(NUM_BUCKETS, 3 * HEADS_PER_GROUP)


def _ret_tables(S):
    half = RET_QK_DIM // 2
    pos = jnp.arange(S, dtype=F32)
    inv_freq = ROPE_BASE ** (-jnp.arange(half, dtype=F32) / half)
    ang = pos[:, None] * inv_freq[None]
    cos, sin = jnp.cos(ang), jnp.sin(ang)
    H, C = RET_HEADS, RET_CHUNK
    log_g = jnp.log(1.0 - 2.0 ** (-5.0 - jnp.arange(H, dtype=F32)))
    n = jnp.arange(C, dtype=F32)
    diff = n[:, None] - n[None, :]
    dmask = jnp.where(diff >= 0, jnp.exp(log_g[:, None, None] * jnp.maximum(diff, 0.0)), 0.0)
    q_dec = jnp.exp(log_g[:, None] * (n + 1.0))
    k_dec = jnp.exp(log_g[:, None] * (C - 1.0 - n))
    chunk_dec = jnp.exp(log_g * C)
    qd = jnp.broadcast_to(q_dec[:, :, None], (H, C, RET_QK_DIM))
    kd = jnp.broadcast_to(k_dec[:, :, None], (H, C, RET_QK_DIM))
    cd = jnp.broadcast_to(chunk_dec[:, None, None], (H, 1, RET_V_DIM))
    return cos, sin, dmask, qd, kd, cd


def _rot(t, cos, sin):
    half = RET_QK_DIM // 2
    t1, t2 = t[:, :half], t[:, half:]
    return jnp.concatenate([t1 * cos - t2 * sin, t1 * sin + t2 * cos], axis=-1)


def _unrot(t, cos, sin):
    half = RET_QK_DIM // 2
    t1, t2 = t[:, :half], t[:, half:]
    return jnp.concatenate([t1 * cos + t2 * sin, t2 * cos - t1 * sin], axis=-1)


def _ret_specs(rev, nC):
    C, DK, DV = RET_CHUNK, RET_QK_DIM, RET_V_DIM
    cidx = (lambda c: nC - 1 - c) if rev else (lambda c: c)
    return dict(
        qk=lambda which: pl.BlockSpec((None, C, DK), lambda h, c: (which, cidx(c), h)),
        v=pl.BlockSpec((C, DV), lambda h, c: (cidx(c), h)),
        cs=pl.BlockSpec((C, DK // 2), lambda h, c: (cidx(c), 0)),
        dmask=pl.BlockSpec((None, C, C), lambda h, c: (h, 0, 0)),
        dec=pl.BlockSpec((None, C, DK), lambda h, c: (h, 0, 0)),
        cd=pl.BlockSpec((None, 1, DV), lambda h, c: (h, 0, 0)),
        st=pl.BlockSpec((None, None, DK, DV), lambda h, c: (h, cidx(c), 0, 0)),
    )


def _ret_fwd(qk, v, g, tables, *, name):
    _, S, _ = qk.shape
    nC = S // RET_CHUNK
    C, DK, DV, H = RET_CHUNK, RET_QK_DIM, RET_V_DIM, RET_HEADS
    cos, sin, dmask, qd, kd, cd = tables
    kscale = DK ** -0.5

    def body(q_ref, k_ref, v_ref, g_ref, cos_ref, sin_ref, dm_ref, qd_ref, kd_ref, cd_ref,
             o_ref, yb_ref, st_ref, state):
        @pl.when(pl.program_id(1) == 0)
        def _():
            state[...] = jnp.zeros_like(state)

        cs, sn = cos_ref[...], sin_ref[...]
        Q = _rot(q_ref[...], cs, sn)
        K = _rot(k_ref[...], cs, sn) * kscale
        Qb, Kb, V = Q.astype(BF16), K.astype(BF16), v_ref[...]
        sb = state[...].astype(BF16)
        st_ref[...] = sb
        A = _dot_nt(Qb, Kb) * dm_ref[...]
        o = _dot(A.astype(BF16), V) + _dot((Q * qd_ref[...]).astype(BF16), sb)
        state[...] = state[...] * cd_ref[...] + _dot_tn((K * kd_ref[...]).astype(BF16), V)
        mu = jnp.mean(o, axis=-1, keepdims=True)
        dd = o - mu
        var = jnp.mean(dd * dd, axis=-1, keepdims=True)
        yn = dd * lax.rsqrt(var + GN_EPS)
        gv = g_ref[...]
        o_ref[...] = o
        yb_ref[...] = (gv * jax.nn.sigmoid(gv) * yn).astype(BF16)

    sp = _ret_specs(False, nC)
    return pl.pallas_call(
        body, name=name, grid=(H, nC),
        in_specs=[sp["qk"](0), sp["qk"](1), sp["v"], sp["v"], sp["cs"], sp["cs"], sp["dmask"],
                  sp["dec"], sp["dec"], sp["cd"]],
        out_specs=[sp["v"], sp["v"], sp["st"]],
        out_shape=[jax.ShapeDtypeStruct((S, H * DV), F32), jax.ShapeDtypeStruct((S, H * DV), BF16),
                   jax.ShapeDtypeStruct((H, nC, DK, DV), BF16)],
        scratch_shapes=[pltpu.VMEM((DK, DV), F32)],
        compiler_params=_cparams(("parallel", "arbitrary")))(qk, qk, v, g, cos, sin, dmask, qd, kd, cd)


def _ret_bwd(dyb, qk, v, g, o, states, tables, *, name):
    _, S, _ = qk.shape
    nC = S // RET_CHUNK
    C, DK, DV, H = RET_CHUNK, RET_QK_DIM, RET_V_DIM, RET_HEADS
    cos, sin, dmask, qd, kd, cd = tables
    kscale = DK ** -0.5

    def body(dy_ref, q_ref, k_ref, v_ref, g_ref, o_ref, st_ref, cos_ref, sin_ref, dm_ref, qd_ref, kd_ref,
             cd_ref, dq_ref, dk_ref, dv_ref, dg_ref, dstate):
        @pl.when(pl.program_id(1) == 0)
        def _():
            dstate[...] = jnp.zeros_like(dstate)

        ov = o_ref[...]
        mu = jnp.mean(ov, axis=-1, keepdims=True)
        dd = ov - mu
        var = jnp.mean(dd * dd, axis=-1, keepdims=True)
        rstd = lax.rsqrt(var + GN_EPS)
        yn = dd * rstd
        gv, dy = g_ref[...], dy_ref[...]
        sg = jax.nn.sigmoid(gv)
        dg_ref[...] = dy * yn * (sg * (1.0 + gv * (1.0 - sg)))
        dyn = dy * (gv * sg)
        dO = rstd * (dyn - jnp.mean(dyn, axis=-1, keepdims=True)
                     - yn * jnp.mean(dyn * yn, axis=-1, keepdims=True))
        dOb = dO.astype(BF16)

        cs, sn = cos_ref[...], sin_ref[...]
        Q = _rot(q_ref[...], cs, sn)
        K = _rot(k_ref[...], cs, sn) * kscale
        Qb, Kb, V = Q.astype(BF16), K.astype(BF16), v_ref[...]
        dm = dm_ref[...]
        Sb = st_ref[...]
        dSb = dstate[...].astype(BF16)
        Ab = (_dot_nt(Qb, Kb) * dm).astype(BF16)
        dAb = (_dot_nt(dOb, V) * dm).astype(BF16)
        Qd = (Q * qd_ref[...]).astype(BF16)
        Kd = (K * kd_ref[...]).astype(BF16)
        dQ = _dot(dAb, Kb) + _dot_nt(dOb, Sb) * qd_ref[...]
        dK = _dot_tn(dAb, Qb) + _dot_nt(V, dSb) * kd_ref[...]
        dv_ref[...] = _dot_tn(Ab, dOb) + _dot(Kd, dSb)
        dstate[...] = dstate[...] * cd_ref[...] + _dot_tn(Qd, dOb)
        dq_ref[...] = _unrot(dQ, cs, sn)
        dk_ref[...] = _unrot(dK, cs, sn) * kscale

    sp = _ret_specs(True, nC)
    dq, dk, dv, dg = pl.pallas_call(
        body, name=name, grid=(H, nC),
        in_specs=[sp["v"], sp["qk"](0), sp["qk"](1), sp["v"], sp["v"], sp["v"], sp["st"], sp["cs"], sp["cs"],
                  sp["dmask"], sp["dec"], sp["dec"], sp["cd"]],
        out_specs=[pl.BlockSpec((C, DK), lambda h, c: (nC - 1 - c, h)),
                   pl.BlockSpec((C, DK), lambda h, c: (nC - 1 - c, h)), sp["v"], sp["v"]],
        out_shape=[jax.ShapeDtypeStruct((S, H * DK), F32), jax.ShapeDtypeStruct((S, H * DK), F32),
                   jax.ShapeDtypeStruct((S, H * DV), F32), jax.ShapeDtypeStruct((S, H * DV), F32)],
        scratch_shapes=[pltpu.VMEM((DK, DV), F32)],
        compiler_params=_cparams(("parallel", "arbitrary")))(dyb, qk, qk, v, g, o, states, cos, sin, dmask, qd, kd, cd)
    return dq, dk, dv, dg


def _layer_fwd(l, x, xb, W, b_in, biases, ln, tables):
    S = x.shape[0]
    tag = f"l{l}"
    win = W["w_in"]
    c0, c1, c2, c3, c4 = 3 * ATTN_W, 3 * ATTN_W + 2048, 3 * ATTN_W + 4096, 3 * ATTN_W + 6144, IN_COLS
    qkv_a = _mm(xb, win[:, :c0], bias=b_in[:c0], out_dtype=BF16, groups=9, name=f"{tag}_in_attn")
    qk_r = _mm(xb, win[:, c0:c1], bias=b_in[c0:c1], groups=2, name=f"{tag}_in_retqk")
    v_r = _mm(xb, win[:, c1:c2], bias=b_in[c1:c2], out_dtype=BF16, name=f"{tag}_in_retv")
    g_r = _mm(xb, win[:, c2:c3], bias=b_in[c2:c3], name=f"{tag}_in_retg")
    gates = _mm(xb, win[:, c3:c4], bias=b_in[c3:c4], groups=2, name=f"{tag}_in_gates")

    os_, ls_ = [], []
    for gi, (_, dil) in enumerate(ATTN_GROUPS):
        view = qkv_a.reshape(9, S // dil, dil * GROUP_WIDTH)
        o, lse = _attn_fwd(view, biases[gi], gi, name=f"{tag}_attn_fwd{gi}")
        os_.append(o.reshape(S, GROUP_WIDTH))
        ls_.append(lse.reshape(S, GROUP_WIDTH))
    ya_b, ya, wts = _combine_fwd(os_, ls_, name=f"{tag}_combine")

    o_r, yb, states = _ret_fwd(qk_r, v_r, g_r, tables, name=f"{tag}_ret_fwd")

    pa = _mm(ya_b, W["w_attn_proj"], name=f"{tag}_attn_proj")
    pr = _mm(yb, W["w_ret_proj"], name=f"{tag}_ret_proj")
    merged = _merge_fwd(gates, pa, pr, name=f"{tag}_merge")
    mix = _mm(merged, W["w_out"], name=f"{tag}_out_proj")
    h1, x1, x1b = _ln_fwd(x, mix, ln["ln1_g"], ln["ln1_b"], name=f"{tag}_ln1")
    uv = _mm(x1b, W["w_gu"], groups=2, name=f"{tag}_ffn_in")
    hh = _swiglu_fwd(uv, name=f"{tag}_swiglu")
    f = _mm(hh, W["w_ffn_down"], name=f"{tag}_ffn_down")
    h2, x2, x2b = _ln_fwd(x1, f, ln["ln2_g"], ln["ln2_b"], name=f"{tag}_ln2")
    saved = dict(xb=xb, qkv_a=qkv_a, qk_r=qk_r, v_r=v_r, g_r=g_r, gates=gates, ls=ls_, ya_b=ya_b, ya=ya, wts=wts,
                 o_r=o_r, yb=yb, states=states, pa=pa, pr=pr, merged=merged, h1=h1, x1b=x1b, uv=uv, hh=hh, h2=h2)
    return x2, x2b, saved


def _layer_bwd(l, dx2, sv, W, WT, biases, ln, tables):
    S = dx2.shape[0]
    tag = f"l{l}"
    g = {}
    dh2b, res2, g["ln2_g"], g["ln2_b"] = _ln_bwd(dx2, sv["h2"], ln["ln2_g"], name=f"{tag}_ln2_bwd")
    dhh = _mm(dh2b, WT["w_ffn_down"], name=f"{tag}_d_hh")
    g["w_ffn_down"] = _mm(sv["hh"].T, dh2b, name=f"{tag}_dw_down")
    dudv = _swiglu_bwd(dhh, sv["uv"], name=f"{tag}_swiglu_bwd")
    dx1 = _mm(dudv, WT["w_gu"], add=res2, name=f"{tag}_d_x1")
    dwgu = _mm(sv["x1b"].T, dudv, name=f"{tag}_dw_gu")
    g["w_ffn_gate"], g["w_ffn_up"] = dwgu[:, :D_FF], dwgu[:, D_FF:]

    dh1b, res1, g["ln1_g"], g["ln1_b"] = _ln_bwd(dx1, sv["h1"], ln["ln1_g"], name=f"{tag}_ln1_bwd")
    dmerged = _mm(dh1b, WT["w_out"], name=f"{tag}_d_merged")
    g["w_out"] = _mm(sv["merged"].T, dh1b, name=f"{tag}_dw_out")
    dpa, dpr, dgates = _merge_bwd(dmerged, sv["gates"], sv["pa"], sv["pr"], name=f"{tag}_merge_bwd")
    dya = _mm(dpa, WT["w_attn_proj"], name=f"{tag}_d_ya")
    g["w_attn_proj"] = _mm(sv["ya_b"].T, dpa, name=f"{tag}_dw_ap")
    dyb = _mm(dpr, WT["w_ret_proj"], name=f"{tag}_d_yb")
    g["w_ret_proj"] = _mm(sv["yb"].T, dpr, name=f"{tag}_dw_rp")

    da, dbs = [], []
    for gi, (_, dil) in enumerate(ATTN_GROUPS):
        L, WD = S // dil, dil * GROUP_WIDTH
        dqkv, db = _attn_bwd(sv["qkv_a"].reshape(9, L, WD), biases[gi], sv["ls"][gi].reshape(L, WD),
                             dya.reshape(L, WD), sv["ya"].reshape(L, WD), sv["wts"][gi].reshape(L, WD), gi,
                             name=f"{tag}_attn_bwd{gi}")
        da.append(tuple(t.reshape(S, GROUP_WIDTH) for t in dqkv))
        dbs.append(db)
    dq_r, dk_r, dv_r, dg_r = _ret_bwd(dyb, sv["qk_r"], sv["v_r"], sv["g_r"], sv["o_r"], sv["states"], tables,
                                 name=f"{tag}_ret_bwd")
    dz, colsum = _assemble_dz(da, dq_r, dk_r, dv_r, dg_r, dgates, name=f"{tag}_assemble_dz")
    g["b_in"] = colsum.reshape(IN_COLS)
    dx = _mm(dz, WT["w_in"], add=res1, name=f"{tag}_d_x")
    g["w_in"] = _mm(sv["xb"].T, dz, name=f"{tag}_dw_in")
    return dx, g, dbs


HBM_SPEC = pl.BlockSpec(memory_space=pltpu.HBM)
OTHER_CHIPS = ((1, 0), (0, 1), (1, 1))


def _flip(v, f):
    return 1 - v if f else v


def _all_gather(shard, *, name):
    R, Wd = shard.shape

    def body(x_ref, out_ref, send_sems, recv_sems, local_sem):
        x, y, c = lax.axis_index("x"), lax.axis_index("y"), lax.axis_index("c")
        me, sibling = (x, y, c), (x, y, 1 - c)
        chips = [(_flip(x, fx), _flip(y, fy)) for fx, fy in OTHER_CHIPS]

        def rows(px, py, pc):
            return out_ref.at[4 * px + 2 * py + pc]

        def copy(k, block, to, src=None):
            return pltpu.make_async_remote_copy(
                src_ref=rows(*block) if src is None else src, dst_ref=rows(*block),
                send_sem=send_sems.at[k], recv_sem=recv_sems.at[k], device_id=to, device_id_type=MESH)

        mine = pltpu.make_async_copy(x_ref, rows(*me), local_sem)
        mine.start()
        first = [copy(0, me, sibling, src=x_ref)]
        first += [copy(1 + j, me, (*chip, c), src=x_ref) for j, chip in enumerate(chips)]
        for cp in first:
            cp.start()
        passed = [copy(4 + j, (*chip, c), sibling) for j, chip in enumerate(chips)]
        for j, chip in enumerate(chips):
            copy(1 + j, (*chip, c), me).wait_recv()
            passed[j].start()
        copy(0, sibling, me).wait_recv()
        for j, chip in enumerate(chips):
            copy(4 + j, (*chip, 1 - c), me).wait_recv()
        for cp in first + passed:
            cp.wait_send()
        mine.wait()

    return pl.pallas_call(
        body, name=name, out_shape=jax.ShapeDtypeStruct((N_DEV, R, Wd), shard.dtype),
        in_specs=[HBM_SPEC], out_specs=HBM_SPEC,
        scratch_shapes=[pltpu.SemaphoreType.DMA((7,)), pltpu.SemaphoreType.DMA((7,)), pltpu.SemaphoreType.DMA],
    )(shard)


def _rs_sibling_exchange(g8, *, name):
    _, _, R, Wd = g8.shape

    def body(g_ref, recv_ref, send_sems, recv_sems):
        x, y, c = lax.axis_index("x"), lax.axis_index("y"), lax.axis_index("c")
        copies = []
        for k in range(4):
            cp = pltpu.make_async_remote_copy(
                src_ref=g_ref.at[k, 1 - c], dst_ref=recv_ref.at[k], send_sem=send_sems.at[k],
                recv_sem=recv_sems.at[k], device_id=(x, y, 1 - c), device_id_type=MESH)
            cp.start()
            copies.append(cp)
        for cp in copies:
            cp.wait()

    return pl.pallas_call(
        body, name=name, out_shape=jax.ShapeDtypeStruct((4, R, Wd), g8.dtype),
        in_specs=[HBM_SPEC], out_specs=HBM_SPEC,
        scratch_shapes=[pltpu.SemaphoreType.DMA((4,)), pltpu.SemaphoreType.DMA((4,))],
    )(g8)


def _rs_chip_sum(g8, recv, core, *, name):
    _, _, R, Wd = g8.shape
    tr = _div_tile(R, 512, 16)

    def body(core_ref, g_ref, r_ref, o_ref):
        o_ref[...] = (g_ref[...] + r_ref[...]).astype(BF16)

    grid_spec = pltpu.PrefetchScalarGridSpec(
        num_scalar_prefetch=1, grid=(4, R // tr),
        in_specs=[pl.BlockSpec((None, None, tr, Wd), lambda k, i, core_ref: (k, core_ref[0], i, 0)),
                  pl.BlockSpec((None, tr, Wd), lambda k, i, core_ref: (k, i, 0))],
        out_specs=pl.BlockSpec((None, tr, Wd), lambda k, i, core_ref: (k, i, 0)))
    return pl.pallas_call(
        body, name=name, grid_spec=grid_spec, out_shape=jax.ShapeDtypeStruct((4, R, Wd), BF16),
        compiler_params=_cparams(("parallel", "parallel")))(core, g8, recv)


def _rs_chip_exchange(p, *, name):
    _, R, Wd = p.shape

    def body(p_ref, out_ref, send_sems, recv_sems, local_sem):
        x, y, c = lax.axis_index("x"), lax.axis_index("y"), lax.axis_index("c")
        my_chip = 2 * x + y
        mine = pltpu.make_async_copy(p_ref.at[my_chip], out_ref.at[my_chip], local_sem)
        mine.start()
        copies = []
        for j, (fx, fy) in enumerate(OTHER_CHIPS):
            px, py = _flip(x, fx), _flip(y, fy)
            cp = pltpu.make_async_remote_copy(
                src_ref=p_ref.at[2 * px + py], dst_ref=out_ref.at[my_chip], send_sem=send_sems.at[j],
                recv_sem=recv_sems.at[j], device_id=(px, py, c), device_id_type=MESH)
            cp.start()
            copies.append(cp)
        for cp in copies:
            cp.wait()
        mine.wait()

    return pl.pallas_call(
        body, name=name, out_shape=jax.ShapeDtypeStruct((4, R, Wd), p.dtype),
        in_specs=[HBM_SPEC], out_specs=HBM_SPEC,
        scratch_shapes=[pltpu.SemaphoreType.DMA((3,)), pltpu.SemaphoreType.DMA((3,)), pltpu.SemaphoreType.DMA],
    )(p)


def _all_reduce_small(v, *, name):
    R, Wd = v.shape

    def body(v_ref, out_ref, slots, send_sems, recv_sems):
        x, y, c = lax.axis_index("x"), lax.axis_index("y"), lax.axis_index("c")
        me = 4 * x + 2 * y + c
        slots[me] = v_ref[...]
        copies = []
        for rel in range(1, N_DEV):
            peer = (_flip(x, rel & 4), _flip(y, rel & 2), _flip(c, rel & 1))
            cp = pltpu.make_async_remote_copy(
                src_ref=v_ref, dst_ref=slots.at[me], send_sem=send_sems.at[rel - 1],
                recv_sem=recv_sems.at[rel - 1], device_id=peer, device_id_type=MESH)
            cp.start()
            copies.append(cp)
        for cp in copies:
            cp.wait()
        acc = slots[0]
        for j in range(1, N_DEV):
            acc = acc + slots[j]
        out_ref[...] = acc

    vm = pl.BlockSpec(memory_space=pltpu.VMEM)
    return pl.pallas_call(
        body, name=name, out_shape=jax.ShapeDtypeStruct((R, Wd), F32),
        in_specs=[vm], out_specs=vm,
        scratch_shapes=[pltpu.VMEM((N_DEV, R, Wd), F32), pltpu.SemaphoreType.DMA((7,)),
                        pltpu.SemaphoreType.DMA((7,))],
    )(v)


def _adam_math(w, g, m, v):
    m2 = ADAM_B1 * m + (1.0 - ADAM_B1) * g
    v2 = ADAM_B2 * v + (1.0 - ADAM_B2) * (g * g)
    m_hat = m2 / (1.0 - ADAM_B1 ** ADAM_STEP)
    v_hat = v2 / (1.0 - ADAM_B2 ** ADAM_STEP)
    delta = -ADAM_LR * (m_hat / (jnp.sqrt(v_hat) + ADAM_EPS) + ADAM_WD * w)
    return delta, m2, v2


def _adam_sharded(parts, w, m, v, *, name):
    R, Wd = w.shape
    tr = _div_tile(R, 512, 16)

    def body(p_ref, w_ref, m_ref, v_ref, g_ref, d_ref, m2_ref, v2_ref):
        g = p_ref[0].astype(F32)
        for s in range(1, 4):
            g = g + p_ref[s].astype(F32)
        delta, m2, v2 = _adam_math(w_ref[...], g, m_ref[...], v_ref[...])
        g_ref[...] = g
        d_ref[...] = delta
        m2_ref[...] = m2
        v2_ref[...] = v2

    return pl.pallas_call(
        body, name=name, grid=(R // tr,),
        in_specs=[pl.BlockSpec((4, tr, Wd), lambda i: (0, i, 0))] + [_row_spec(tr, Wd)] * 3,
        out_specs=[_row_spec(tr, Wd)] * 4, out_shape=[jax.ShapeDtypeStruct((R, Wd), F32)] * 4,
        compiler_params=_cparams(("parallel",)))(parts, w, m, v)


def _adam_small(g, w, m, v, *, name):
    R, Wd = w.shape

    def body(g_ref, w_ref, m_ref, v_ref, d_ref, m2_ref, v2_ref):
        delta, m2, v2 = _adam_math(w_ref[...], g_ref[...], m_ref[...], v_ref[...])
        d_ref[...] = delta
        m2_ref[...] = m2
        v2_ref[...] = v2

    return pl.pallas_call(
        body, name=name, out_shape=[jax.ShapeDtypeStruct((R, Wd), F32)] * 3,
        compiler_params=_cparams())(g, w, m, v)


def _shard_shape(name):
    r, c = FULL_SHAPE[name]
    return (r, c // N_DEV) if name in COL_SHARDED else (r // N_DEV, c)


def _pack_shards(t):
    return jnp.concatenate([t[n][l].reshape(-1, D_MODEL) for l in range(DEPTH) for n in BIG_WEIGHTS], axis=0)


def _unpack_shards(flat):
    out = {n: [] for n in BIG_WEIGHTS}
    off = 0
    for l in range(DEPTH):
        for n in BIG_WEIGHTS:
            r, c = _shard_shape(n)
            rows = r * c // D_MODEL
            out[n].append(flat[off:off + rows].reshape(r, c))
            off += rows
    return {n: jnp.stack(v) for n, v in out.items()}


def _unpack_gathered(gathered):
    layers = []
    off = 0
    for l in range(DEPTH):
        W = {}
        for n in BIG_WEIGHTS:
            r, c = _shard_shape(n)
            rows = r * c // D_MODEL
            blk = gathered[:, off:off + rows].reshape(N_DEV, r, c)
            if n in COL_SHARDED:
                W[n] = jnp.transpose(blk, (1, 0, 2)).reshape(FULL_SHAPE[n])
            else:
                W[n] = blk.reshape(FULL_SHAPE[n])
            off += rows
        layers.append(W)
    return layers


def _pack_full_grads(grads):
    pieces = []
    for l in range(DEPTH):
        for n in BIG_WEIGHTS:
            r, c = _shard_shape(n)
            gfull = grads[l][n]
            if n in COL_SHARDED:
                blk = jnp.transpose(gfull.reshape(r, N_DEV, c), (1, 0, 2))
            else:
                blk = gfull.reshape(N_DEV, r, c)
            pieces.append(blk.reshape(N_DEV, r * c // D_MODEL, D_MODEL))
    return jnp.concatenate(pieces, axis=1)


def _pack_small(t):
    flat = jnp.concatenate([t[n].reshape(-1).astype(F32) for n in SMALL_WEIGHTS])
    return jnp.pad(flat, (0, SMALL_ROWS * LANES - flat.shape[0])).reshape(SMALL_ROWS, LANES)


def _unpack_small(packed):
    flat = packed.reshape(-1)
    out, off = {}, 0
    for n in SMALL_WEIGHTS:
        size = math.prod(SMALL_SHAPE[n])
        out[n] = flat[off:off + size].reshape(SMALL_SHAPE[n])
        off += size
    return out


def _local_step(x, target, rel_bias, b_in, lns, layers_w):
    S = x.shape[0]
    tables = _ret_tables(S)
    biases = [_attn_bias(rel_bias, gi, dil) for gi, (_, dil) in enumerate(ATTN_GROUPS)]
    Ws, WTs = [], []
    for W in layers_w:
        W = dict(W)
        W["w_gu"] = jnp.concatenate([W["w_ffn_gate"], W["w_ffn_up"]], axis=1)
        Ws.append(W)
        WTs.append({n: W[n].T for n in ("w_in", "w_attn_proj", "w_ret_proj", "w_out", "w_gu", "w_ffn_down")})

    h, hb = x, x.astype(BF16)
    saved = []
    for l in range(DEPTH):
        h, hb, sv = _layer_fwd(l, h, hb, Ws[l], b_in[l], biases, lns[l], tables)
        saved.append(sv)
    dy, sq = _loss_fwd_bwd(h, target, name="loss")
    loss_local = 0.5 * sq[0, 0] / D_MODEL

    grads = [None] * DEPTH
    db_tot = None
    dx = dy
    for l in reversed(range(DEPTH)):
        dx, g, dbs = _layer_bwd(l, dx, saved[l], Ws[l], WTs[l], biases, lns[l], tables)
        grads[l] = g
        db_tot = dbs if db_tot is None else [a + b for a, b in zip(db_tot, dbs)]
    small = {"rel_bias": _bias_grad(db_tot, name="bias_grad"),
             "b_in": jnp.stack([grads[l]["b_in"] for l in range(DEPTH)])}
    for n in ("ln1_g", "ln1_b", "ln2_g", "ln2_b"):
        small[n] = jnp.stack([grads[l][n].reshape(D_MODEL) for l in range(DEPTH)])
    return loss_local, dx, grads, small


def kernel(x, rel_bias, w_in, b_in, w_attn_proj, w_ret_proj, w_out, ln1_g, ln1_b, w_ffn_gate, w_ffn_up, w_ffn_down, ln2_g, ln2_b, loss_target, m_rel_bias, m_w_in, m_b_in, m_w_attn_proj, m_w_ret_proj, m_w_out, m_ln1_g, m_ln1_b, m_w_ffn_gate, m_w_ffn_up, m_w_ffn_down, m_ln2_g, m_ln2_b, v_rel_bias, v_w_in, v_b_in, v_w_attn_proj, v_w_ret_proj, v_w_out, v_ln1_g, v_ln1_b, v_w_ffn_gate, v_w_ffn_up, v_w_ffn_down, v_ln2_g, v_ln2_b):
    w = dict(rel_bias=rel_bias, w_in=w_in, b_in=b_in, w_attn_proj=w_attn_proj, w_ret_proj=w_ret_proj, w_out=w_out,
             ln1_g=ln1_g, ln1_b=ln1_b, w_ffn_gate=w_ffn_gate, w_ffn_up=w_ffn_up, w_ffn_down=w_ffn_down,
             ln2_g=ln2_g, ln2_b=ln2_b)
    m = dict(rel_bias=m_rel_bias, w_in=m_w_in, b_in=m_b_in, w_attn_proj=m_w_attn_proj, w_ret_proj=m_w_ret_proj,
             w_out=m_w_out, ln1_g=m_ln1_g, ln1_b=m_ln1_b, w_ffn_gate=m_w_ffn_gate, w_ffn_up=m_w_ffn_up,
             w_ffn_down=m_w_ffn_down, ln2_g=m_ln2_g, ln2_b=m_ln2_b)
    v = dict(rel_bias=v_rel_bias, w_in=v_w_in, b_in=v_b_in, w_attn_proj=v_w_attn_proj, w_ret_proj=v_w_ret_proj,
             w_out=v_w_out, ln1_g=v_ln1_g, ln1_b=v_ln1_b, w_ffn_gate=v_w_ffn_gate, w_ffn_up=v_w_ffn_up,
             w_ffn_down=v_w_ffn_down, ln2_g=v_ln2_g, ln2_b=v_ln2_b)

    w_flat = _pack_shards(w)
    gathered = _all_gather(w_flat.astype(BF16), name="all_gather_weights")
    layers_w = _unpack_gathered(gathered)
    lns = [{n: w[n][l] for n in ("ln1_g", "ln1_b", "ln2_g", "ln2_b")} for l in range(DEPTH)]

    loss_local, grad_x, grads, small = _local_step(x[0], loss_target[0], rel_bias, b_in, lns, layers_w)
    loss = lax.psum(loss_local, ("x", "y", "c"))

    g8 = _pack_full_grads(grads)
    rows = g8.shape[1]
    g8 = g8.reshape(4, 2, rows, D_MODEL)
    from_sibling = _rs_sibling_exchange(g8, name="rs_sibling_exchange")
    core = lax.axis_index("c").astype(jnp.int32).reshape(1)
    chip_parts = _rs_chip_sum(g8, from_sibling, core, name="rs_chip_sum")
    parts = _rs_chip_exchange(chip_parts, name="rs_chip_exchange")
    g_flat, d_flat, m_flat, v_flat = _adam_sharded(parts, w_flat, _pack_shards(m), _pack_shards(v),
                                                   name="adam_sharded")
    big = [_unpack_shards(t) for t in (g_flat, d_flat, m_flat, v_flat)]

    gs = _all_reduce_small(_pack_small(small), name="all_reduce_small")
    ds, ms, vs = _adam_small(gs, _pack_small(w), _pack_small(m), _pack_small(v), name="adam_small")
    sm = [_unpack_small(t) for t in (gs, ds, ms, vs)]

    outs = [loss, grad_x[None]]
    for kind in range(4):
        for n in ALL_WEIGHTS:
            outs.append(big[kind][n] if n in BIG_WEIGHTS else sm[kind][n])
    return tuple(outs)
```

```python
import functools
import math

import numpy as np
import jax
import jax.numpy as jnp
from jax import lax
from jax.experimental import pallas as pl
from jax.experimental.pallas import tpu as pltpu

F32 = jnp.float32
BF16 = jnp.bfloat16
MESH = pl.DeviceIdType.MESH

D_MODEL = 1024
DEPTH = 2
HEAD_DIM = 64
ATTN_GROUPS = ((128, 1), (512, 4), (2048, 16))
HEADS_PER_GROUP = 6
GROUP_WIDTH = HEADS_PER_GROUP * HEAD_DIM
ATTN_BLOCK = 128
NUM_BUCKETS = 32
MAX_DISTANCE = 2048
RET_HEADS = 4
RET_QK_DIM = 256
RET_V_DIM = 512
RET_CHUNK = 128
ROPE_BASE = 10000.0
D_FF = 2816
ALPHA = (2 * DEPTH) ** 0.25
LN_EPS = 1e-5
GN_EPS = 1e-5
ATTN_W = 3 * GROUP_WIDTH
IN_COLS = 3 * ATTN_W + 2 * 1024 + 2 * 2048 + 2 * 1024
ADAM_LR, ADAM_B1, ADAM_B2, ADAM_EPS, ADAM_WD, ADAM_STEP = 0.001, 0.9, 0.999, 1e-08, 0.01, 10
N_DEV = 8
NEG = -1e30
LANES = 128
VMEM_LIMIT = 56 * 1024 * 1024

BIG_WEIGHTS = ("w_in", "w_attn_proj", "w_ret_proj", "w_out", "w_ffn_gate", "w_ffn_up", "w_ffn_down")
COL_SHARDED = ("w_in", "w_attn_proj", "w_ffn_gate", "w_ffn_up")
FULL_SHAPE = {"w_in": (D_MODEL, IN_COLS), "w_attn_proj": (GROUP_WIDTH, D_MODEL), "w_ret_proj": (2048, D_MODEL),
              "w_out": (D_MODEL, D_MODEL), "w_ffn_gate": (D_MODEL, D_FF), "w_ffn_up": (D_MODEL, D_FF),
              "w_ffn_down": (D_FF, D_MODEL)}
SMALL_WEIGHTS = ("rel_bias", "b_in", "ln1_g", "ln1_b", "ln2_g", "ln2_b")
SMALL_SHAPE = {"rel_bias": (NUM_BUCKETS, 18), "b_in": (DEPTH, IN_COLS), "ln1_g": (DEPTH, D_MODEL),
               "ln1_b": (DEPTH, D_MODEL), "ln2_g": (DEPTH, D_MODEL), "ln2_b": (DEPTH, D_MODEL)}
SMALL_ROWS = 256
ALL_WEIGHTS = ("rel_bias", "w_in", "b_in", "w_attn_proj", "w_ret_proj", "w_out", "ln1_g", "ln1_b",
               "w_ffn_gate", "w_ffn_up", "w_ffn_down", "ln2_g", "ln2_b")


def _cparams(sem=None):
    return pltpu.CompilerParams(dimension_semantics=sem, vmem_limit_bytes=VMEM_LIMIT)


def _div_tile(n, cap, unit):
    if n <= cap:
        return n
    best = None
    for t in range(unit, cap + 1, unit):
        if n % t == 0:
            best = t
    assert best is not None, (n, cap, unit)
    return best


def _mm(a, b, *, name, out_dtype=F32, bias=None, add=None, groups=None, lane_chunks=False):
    M, K = a.shape
    K2, N = b.shape
    assert K == K2 and a.dtype == BF16 and b.dtype == BF16
    tm = _div_tile(M, 1024, 16)
    tn = N // groups if groups else _div_tile(N, 1536, LANES)
    tk = _div_tile(K, 1536, LANES)
    nk = K // tk
    nch = tn // LANES
    has_bias, has_add = bias is not None, add is not None

    def body(*refs):
        a_ref, b_ref = refs[0], refs[1]
        pos = 2
        bias_ref = add_ref = None
        if has_bias:
            bias_ref = refs[pos]
            pos += 1
        if has_add:
            add_ref = refs[pos]
            pos += 1
        o_ref = refs[pos]

        def finish(r):
            if has_bias:
                r = r + bias_ref[...]
            if has_add:
                r = r + add_ref[...]
            if lane_chunks:
                for c in range(nch):
                    o_ref[c] = r[:, c * LANES:(c + 1) * LANES].astype(o_ref.dtype)
            else:
                o_ref[...] = r.astype(o_ref.dtype)

        part = jnp.dot(a_ref[...], b_ref[...], preferred_element_type=F32)
        if nk == 1:
            finish(part)
        else:
            acc_ref = refs[pos + 1]
            k = pl.program_id(2)

            @pl.when(k == 0)
            def _():
                acc_ref[...] = part

            @pl.when(k > 0)
            def _():
                acc_ref[...] += part

            @pl.when(k == nk - 1)
            def _():
                finish(acc_ref[...])

    in_specs = [pl.BlockSpec((tm, tk), lambda i, j, k: (i, k)),
                pl.BlockSpec((tk, tn), lambda i, j, k: (k, j))]
    args = [a, b]
    if has_bias:
        in_specs.append(pl.BlockSpec((1, tn), lambda i, j, k: (0, j)))
        args.append(bias.reshape(1, N).astype(F32))
    if has_add:
        in_specs.append(pl.BlockSpec((tm, tn), lambda i, j, k: (i, j)))
        args.append(add)
    if lane_chunks:
        assert groups
        out_shape = jax.ShapeDtypeStruct((groups, nch, M, LANES), out_dtype)
        out_spec = pl.BlockSpec((None, nch, tm, LANES), lambda i, j, k: (j, 0, i, 0))
    elif groups:
        out_shape = jax.ShapeDtypeStruct((groups, M, tn), out_dtype)
        out_spec = pl.BlockSpec((None, tm, tn), lambda i, j, k: (j, i, 0))
    else:
        out_shape = jax.ShapeDtypeStruct((M, N), out_dtype)
        out_spec = pl.BlockSpec((tm, tn), lambda i, j, k: (i, j))
    scratch = [pltpu.VMEM((tm, tn), F32)] if nk > 1 else []
    out = pl.pallas_call(
        body, name=name, grid=(M // tm, N // tn, nk), in_specs=in_specs, out_specs=out_spec,
        out_shape=out_shape, scratch_shapes=scratch,
        compiler_params=_cparams(("parallel", "parallel", "arbitrary")))(*args)
    return out.reshape(groups * nch, M, LANES) if lane_chunks else out


def _row_spec(tr, w):
    return pl.BlockSpec((tr, w), lambda i: (i, 0))


def _vec_spec(w):
    return pl.BlockSpec((1, w), lambda i: (0, 0))


def _col_spec(w, tr):
    return pl.BlockSpec((w, tr), lambda i: (0, i))


def _ln_fwd(x, sub, g, b, *, name):
    S, W = x.shape
    tr = 512

    def body(x_ref, s_ref, g_ref, b_ref, h_ref, y_ref, yb_ref, ybt_ref):
        h = ALPHA * x_ref[...] + s_ref[...]
        mu = jnp.mean(h, axis=-1, keepdims=True)
        d = h - mu
        var = jnp.mean(d * d, axis=-1, keepdims=True)
        y = d * lax.rsqrt(var + LN_EPS) * g_ref[...] + b_ref[...]
        h_ref[...] = h
        y_ref[...] = y
        yb_ref[...] = y.astype(BF16)
        ybt_ref[...] = y.T.astype(BF16)

    return pl.pallas_call(
        body, name=name, grid=(S // tr,),
        in_specs=[_row_spec(tr, W), _row_spec(tr, W), _vec_spec(W), _vec_spec(W)],
        out_specs=[_row_spec(tr, W)] * 3 + [_col_spec(W, tr)],
        out_shape=[jax.ShapeDtypeStruct((S, W), F32), jax.ShapeDtypeStruct((S, W), F32),
                   jax.ShapeDtypeStruct((S, W), BF16), jax.ShapeDtypeStruct((W, S), BF16)],
        compiler_params=_cparams(("parallel",)))(x, sub, g.reshape(1, W), b.reshape(1, W))


def _ln_bwd(dy, h, g, *, name):
    S, W = dy.shape
    tr = 512

    def body(dy_ref, h_ref, g_ref, dhb_ref, res_ref, dg_ref, db_ref):
        @pl.when(pl.program_id(0) == 0)
        def _():
            dg_ref[...] = jnp.zeros_like(dg_ref)
            db_ref[...] = jnp.zeros_like(db_ref)

        hh = h_ref[...]
        mu = jnp.mean(hh, axis=-1, keepdims=True)
        d = hh - mu
        var = jnp.mean(d * d, axis=-1, keepdims=True)
        rstd = lax.rsqrt(var + LN_EPS)
        xhat = d * rstd
        dyv = dy_ref[...]
        dg_ref[...] += jnp.sum(dyv * xhat, axis=0, keepdims=True)
        db_ref[...] += jnp.sum(dyv, axis=0, keepdims=True)
        dxh = dyv * g_ref[...]
        dh = rstd * (dxh - jnp.mean(dxh, axis=-1, keepdims=True)
                     - xhat * jnp.mean(dxh * xhat, axis=-1, keepdims=True))
        dhb_ref[...] = dh.astype(BF16)
        res_ref[...] = ALPHA * dh

    return pl.pallas_call(
        body, name=name, grid=(S // tr,),
        in_specs=[_row_spec(tr, W), _row_spec(tr, W), _vec_spec(W)],
        out_specs=[_row_spec(tr, W), _row_spec(tr, W), _vec_spec(W), _vec_spec(W)],
        out_shape=[jax.ShapeDtypeStruct((S, W), BF16), jax.ShapeDtypeStruct((S, W), F32),
                   jax.ShapeDtypeStruct((1, W), F32), jax.ShapeDtypeStruct((1, W), F32)],
        compiler_params=_cparams(("arbitrary",)))(dy, h, g.reshape(1, W))


def _loss_fwd_bwd(y, target, *, name):
    S, W = y.shape
    tr = 512

    def body(y_ref, t_ref, dy_ref, acc_ref):
        @pl.when(pl.program_id(0) == 0)
        def _():
            acc_ref[...] = jnp.zeros_like(acc_ref)

        e = y_ref[...] - t_ref[...]
        acc_ref[...] += jnp.sum(jnp.sum(e * e, axis=-1, keepdims=True), axis=0, keepdims=True)
        dy_ref[...] = e * (1.0 / W)

    return pl.pallas_call(
        body, name=name, grid=(S // tr,),
        in_specs=[_row_spec(tr, W), _row_spec(tr, W)],
        out_specs=[_row_spec(tr, W), pl.BlockSpec((1, 1), lambda i: (0, 0))],
        out_shape=[jax.ShapeDtypeStruct((S, W), F32), jax.ShapeDtypeStruct((1, 1), F32)],
        compiler_params=_cparams(("arbitrary",)))(y, target)


def _combine_fwd(os_, ls_, *, name):
    NCH, S, _ = os_[0].shape
    W = NCH * LANES
    tr = 512

    def body(o0, o1, o2, l0, l1, l2, yb_ref, ybt_ref, y_ref, w0_ref, w1_ref, w2_ref):
        for c in range(NCH):
            la, lb, lc = l0[c], l1[c], l2[c]
            m = jnp.maximum(jnp.maximum(la, lb), lc)
            ea, eb, ec = jnp.exp(la - m), jnp.exp(lb - m), jnp.exp(lc - m)
            inv = 1.0 / (ea + eb + ec)
            wa, wb, wc = ea * inv, eb * inv, ec * inv
            y = wa * o0[c] + wb * o1[c] + wc * o2[c]
            y_ref[c] = y
            yb_ref[:, c * LANES:(c + 1) * LANES] = y.astype(BF16)
            ybt_ref[c * LANES:(c + 1) * LANES, :] = y.T.astype(BF16)
            w0_ref[c] = wa
            w1_ref[c] = wb
            w2_ref[c] = wc

    ch = pl.BlockSpec((NCH, tr, LANES), lambda i: (0, i, 0))
    yb, ybt, y, w0, w1, w2 = pl.pallas_call(
        body, name=name, grid=(S // tr,),
        in_specs=[ch] * 6,
        out_specs=[_row_spec(tr, W), _col_spec(W, tr)] + [ch] * 4,
        out_shape=[jax.ShapeDtypeStruct((S, W), BF16), jax.ShapeDtypeStruct((W, S), BF16)]
        + [jax.ShapeDtypeStruct((NCH, S, LANES), F32)] * 4,
        compiler_params=_cparams(("parallel",)))(*os_, *ls_)
    return yb, ybt, y, (w0, w1, w2)


def _merge_fwd(gates, pa, pr, *, name):
    S, W = pa.shape
    tr = 512

    def body(g_ref, pa_ref, pr_ref, o_ref, ot_ref):
        m = jax.nn.sigmoid(g_ref[0]) * pa_ref[...] + jax.nn.sigmoid(g_ref[1]) * pr_ref[...]
        o_ref[...] = m.astype(BF16)
        ot_ref[...] = m.T.astype(BF16)

    return pl.pallas_call(
        body, name=name, grid=(S // tr,),
        in_specs=[pl.BlockSpec((2, tr, W), lambda i: (0, i, 0)), _row_spec(tr, W), _row_spec(tr, W)],
        out_specs=[_row_spec(tr, W), _col_spec(W, tr)],
        out_shape=[jax.ShapeDtypeStruct((S, W), BF16), jax.ShapeDtypeStruct((W, S), BF16)],
        compiler_params=_cparams(("parallel",)))(gates, pa, pr)


def _merge_bwd(dm, gates, pa, pr, *, name):
    S, W = pa.shape
    tr = 256

    def body(dm_ref, g_ref, pa_ref, pr_ref, dpa_ref, dpr_ref, dg_ref):
        dmv = dm_ref[...]
        sa, sb = jax.nn.sigmoid(g_ref[0]), jax.nn.sigmoid(g_ref[1])
        dpa_ref[...] = (dmv * sa).astype(BF16)
        dpr_ref[...] = (dmv * sb).astype(BF16)
        dg_ref[0] = dmv * pa_ref[...] * (sa * (1.0 - sa))
        dg_ref[1] = dmv * pr_ref[...] * (sb * (1.0 - sb))

    g3 = pl.BlockSpec((2, tr, W), lambda i: (0, i, 0))
    return pl.pallas_call(
        body, name=name, grid=(S // tr,),
        in_specs=[_row_spec(tr, W), g3, _row_spec(tr, W), _row_spec(tr, W)],
        out_specs=[_row_spec(tr, W), _row_spec(tr, W), g3],
        out_shape=[jax.ShapeDtypeStruct((S, W), BF16), jax.ShapeDtypeStruct((S, W), BF16),
                   jax.ShapeDtypeStruct((2, S, W), F32)],
        compiler_params=_cparams(("parallel",)))(dm, gates, pa, pr)


def _swiglu_fwd(uv, *, name):
    _, S, W = uv.shape
    tr = 256

    def body(uv_ref, o_ref, ot_ref):
        u = uv_ref[0]
        hh = u * jax.nn.sigmoid(u) * uv_ref[1]
        o_ref[...] = hh.astype(BF16)
        ot_ref[...] = hh.T.astype(BF16)

    return pl.pallas_call(
        body, name=name, grid=(S // tr,),
        in_specs=[pl.BlockSpec((2, tr, W), lambda i: (0, i, 0))],
        out_specs=[_row_spec(tr, W), _col_spec(W, tr)],
        out_shape=[jax.ShapeDtypeStruct((S, W), BF16), jax.ShapeDtypeStruct((W, S), BF16)],
        compiler_params=_cparams(("parallel",)))(uv)


def _swiglu_bwd(dh, uv, *, name):
    _, S, W = uv.shape
    tr = 256

    def body(dh_ref, uv_ref, o_ref):
        u, v, d = uv_ref[0], uv_ref[1], dh_ref[...]
        sg = jax.nn.sigmoid(u)
        o_ref[:, 0:W] = (d * v * (sg * (1.0 + u * (1.0 - sg)))).astype(BF16)
        o_ref[:, W:2 * W] = (d * (u * sg)).astype(BF16)

    return pl.pallas_call(
        body, name=name, grid=(S // tr,),
        in_specs=[_row_spec(tr, W), pl.BlockSpec((2, tr, W), lambda i: (0, i, 0))],
        out_specs=_row_spec(tr, 2 * W), out_shape=jax.ShapeDtypeStruct((S, 2 * W), BF16),
        compiler_params=_cparams(("parallel",)))(dh, uv)


def _assemble_dz(da, dq_r, dk_r, dv_r, dg_r, dgates, *, name):
    S = dv_r.shape[0]
    tr = 128
    GW = GROUP_WIDTH
    NCH = GW // LANES

    def body(*refs):
        a_refs = refs[0:9]
        q_ref, k_ref, v_ref, g_ref, gt_ref, dz_ref, cs_ref = refs[9:]

        @pl.when(pl.program_id(0) == 0)
        def _():
            cs_ref[...] = jnp.zeros_like(cs_ref)

        def put(off, val):
            w = val.shape[-1]
            dz_ref[:, off:off + w] = val.astype(BF16)
            cs_ref[:, off:off + w] += jnp.sum(val, axis=0, keepdims=True)

        for which in range(3):
            for gi in range(3):
                for c in range(NCH):
                    put(which * ATTN_W + gi * GW + c * LANES, a_refs[3 * gi + which][c])
        off = 3 * ATTN_W
        put(off, q_ref[...])
        put(off + 1024, k_ref[...])
        put(off + 2048, v_ref[...])
        put(off + 4096, g_ref[...])
        put(off + 6144, gt_ref[0])
        put(off + 7168, gt_ref[1])

    flat_a = [t for grp in da for t in grp]
    return pl.pallas_call(
        body, name=name, grid=(S // tr,),
        in_specs=[pl.BlockSpec((NCH, tr, LANES), lambda i: (0, i, 0))] * 9 + [_row_spec(tr, 1024), _row_spec(tr, 1024),
                  _row_spec(tr, 2048), _row_spec(tr, 2048), pl.BlockSpec((2, tr, 1024), lambda i: (0, i, 0))],
        out_specs=[_row_spec(tr, IN_COLS), _vec_spec(IN_COLS)],
        out_shape=[jax.ShapeDtypeStruct((S, IN_COLS), BF16), jax.ShapeDtypeStruct((1, IN_COLS), F32)],
        compiler_params=_cparams(("arbitrary",)))(*flat_a, dq_r, dk_r, dv_r, dg_r, dgates)


def _t5_bucket(dist):
    max_exact = NUM_BUCKETS // 2
    large = max_exact + (np.log(np.maximum(dist, max_exact) / max_exact)
                         / np.log(MAX_DISTANCE / max_exact) * (NUM_BUCKETS - max_exact)).astype(np.int32)
    large = np.minimum(large, NUM_BUCKETS - 1)
    return np.where(dist < max_exact, dist, large).astype(np.int32)


def _attn_tables(dilation):
    W = ATTN_BLOCK
    qi = np.arange(W)[:, None]
    kj = np.arange(2 * W)[None, :]
    rel = qi + W - kj
    valid = (rel >= 0) & (rel <= W)
    buckets = _t5_bucket(np.clip(rel, 0, W) * dilation)
    return buckets, valid


def _attn_bias(rel_bias, gi, dilation):
    buckets, valid = _attn_tables(dilation)
    table = rel_bias[:, gi * HEADS_PER_GROUP:(gi + 1) * HEADS_PER_GROUP]
    onehot = (jnp.asarray(buckets.reshape(-1, 1)) == jnp.arange(NUM_BUCKETS)[None, :]).astype(F32)
    bias = jnp.dot(onehot, table.astype(F32), precision=lax.Precision.HIGHEST)
    bias = bias.T.reshape(HEADS_PER_GROUP, ATTN_BLOCK, 2 * ATTN_BLOCK)
    return jnp.where(jnp.asarray(valid)[None], bias, NEG)


def _dot_nt(a, b):
    return lax.dot_general(a, b, (((1,), (1,)), ((), ())), preferred_element_type=F32)


def _dot_tn(a, b):
    return lax.dot_general(a, b, (((0,), (0,)), ((), ())), preferred_element_type=F32)


def _dot(a, b):
    return jnp.dot(a, b, preferred_element_type=F32)


ATTN_RESIDUES_PER_STEP = 4
ATTN_UNITS_AT_ONCE = 4
HEADS_PER_CHUNK = LANES // HEAD_DIM
N_CHUNKS = GROUP_WIDTH // LANES


def _first_block_mask(has_prev):
    col = lax.broadcasted_iota(jnp.int32, (1, 2 * ATTN_BLOCK), 1)
    return jnp.where(jnp.logical_or(has_prev, col >= ATTN_BLOCK), 0.0, NEG).astype(F32)


def _head_lanes(hh):
    return slice(HEAD_DIM * hh, HEAD_DIM * (hh + 1))


def _attn_geometry(S, d):
    rows_per_block = ATTN_BLOCK * d
    rps = min(d, ATTN_RESIDUES_PER_STEP)
    return rows_per_block, S // rows_per_block, rps, d // rps


def _residue_rows(d, rps, rg, rr):
    if d == 1:
        return slice(None)
    return pl.ds(rg * rps + rr, ATTN_BLOCK, stride=d)


def _attn_in_specs(gi, RB, last):
    def spec(which, prev):
        if prev:
            return pl.BlockSpec((None, RB, LANES),
                                lambda j, n, rg: (9 * which + 3 * gi + j, jnp.clip(n - 1, 0, last), 0))
        return pl.BlockSpec((None, RB, LANES), lambda j, n, rg: (9 * which + 3 * gi + j, jnp.minimum(n, last), 0))
    bias = pl.BlockSpec((HEADS_PER_CHUNK, ATTN_BLOCK, 2 * ATTN_BLOCK), lambda j, n, rg: (j, 0, 0))
    return [spec(0, False), spec(1, True), spec(1, False), spec(2, True), spec(2, False), bias]


def _attn_fwd(qkv, bias, gi, d, *, name):
    _, S, _ = qkv.shape
    B = ATTN_BLOCK
    RB, nb, rps, nrg = _attn_geometry(S, d)
    scale = HEAD_DIM ** -0.5
    units = [(rr, hh) for rr in range(rps) for hh in range(HEADS_PER_CHUNK)]

    def body(q_ref, kp_ref, kc_ref, vp_ref, vc_ref, b_ref, o_ref, l_ref):
        n, rg = pl.program_id(1), pl.program_id(2)
        edge = _first_block_mask(n > 0)
        rows = [_residue_rows(d, rps, rg, rr) for rr in range(rps)]
        q = [q_ref[r_, :].astype(BF16) for r_ in rows]
        k2 = [jnp.concatenate([kp_ref[r_, :], kc_ref[r_, :]], axis=0).astype(BF16) for r_ in rows]
        v2 = [jnp.concatenate([vp_ref[r_, :], vc_ref[r_, :]], axis=0).astype(BF16) for r_ in rows]
        o_part, l_part = {}, {}
        for u0 in range(0, len(units), ATTN_UNITS_AT_ONCE):
            us = units[u0:u0 + ATTN_UNITS_AT_ONCE]
            s = [_dot_nt(q[rr][:, _head_lanes(hh)], k2[rr][:, _head_lanes(hh)]) * scale + b_ref[hh] + edge
                 for rr, hh in us]
            m = [jnp.max(x, axis=-1, keepdims=True) for x in s]
            p = [jnp.exp(x - mm) for x, mm in zip(s, m)]
            l = [jnp.sum(x, axis=-1, keepdims=True) for x in p]
            pb = [(x * (1.0 / ll)).astype(BF16) for x, ll in zip(p, l)]
            o = [_dot(x, v2[rr][:, _head_lanes(hh)]) for x, (rr, hh) in zip(pb, us)]
            for u, oo, mm, ll in zip(us, o, m, l):
                o_part[u] = oo
                l_part[u] = jnp.broadcast_to(mm + jnp.log(ll), (B, HEAD_DIM))
        for rr in range(rps):
            o_ref[rows[rr], :] = jnp.concatenate([o_part[(rr, hh)] for hh in range(HEADS_PER_CHUNK)], axis=1)
            l_ref[rows[rr], :] = jnp.concatenate([l_part[(rr, hh)] for hh in range(HEADS_PER_CHUNK)], axis=1)

    out_spec = pl.BlockSpec((None, RB, LANES), lambda j, n, rg: (j, n, 0))
    return pl.pallas_call(
        body, name=name, grid=(N_CHUNKS, nb, nrg),
        in_specs=_attn_in_specs(gi, RB, nb - 1),
        out_specs=[out_spec, out_spec],
        out_shape=[jax.ShapeDtypeStruct((N_CHUNKS, S, LANES), F32)] * 2,
        compiler_params=_cparams(("parallel", "arbitrary", "arbitrary")))(qkv, qkv, qkv, qkv, qkv, bias)


def _attn_bwd(qkv, bias, lse, dya, ya, wts, gi, d, *, name):
    _, S, _ = qkv.shape
    B = ATTN_BLOCK
    RB, nb, rps, nrg = _attn_geometry(S, d)
    scale = HEAD_DIM ** -0.5
    units = [(rr, hh) for rr in range(rps) for hh in range(HEADS_PER_CHUNK)]

    def body(q_ref, kp_ref, kc_ref, vp_ref, vc_ref, b_ref, l_ref, dya_ref, ya_ref, w_ref,
             dq_ref, dk_ref, dv_ref, db_ref, dk_carry, dv_carry):
        n, rg = pl.program_id(1), pl.program_id(2)
        rows = [_residue_rows(d, rps, rg, rr) for rr in range(rps)]

        @pl.when((n == 0) & (rg == 0))
        def _():
            db_ref[...] = jnp.zeros_like(db_ref)
            dk_carry[...] = jnp.zeros_like(dk_carry)
            dv_carry[...] = jnp.zeros_like(dv_carry)

        @pl.when(n < nb)
        def _():
            edge = _first_block_mask(n > 0)
            q = [q_ref[r_, :].astype(BF16) for r_ in rows]
            k2 = [jnp.concatenate([kp_ref[r_, :], kc_ref[r_, :]], axis=0).astype(BF16) for r_ in rows]
            v2 = [jnp.concatenate([vp_ref[r_, :], vc_ref[r_, :]], axis=0).astype(BF16) for r_ in rows]
            lse_c = [l_ref[r_, :] for r_ in rows]
            dy_c = [dya_ref[r_, :] for r_ in rows]
            ya_c = [ya_ref[r_, :] for r_ in rows]
            w_c = [w_ref[r_, :] for r_ in rows]
            ds_sum = [None] * HEADS_PER_CHUNK
            dq_part, dk_part, dv_part = {}, {}, {}
            for u0 in range(0, len(units), ATTN_UNITS_AT_ONCE):
                us = units[u0:u0 + ATTN_UNITS_AT_ONCE]
                hl = [_head_lanes(hh) for _, hh in us]
                qh = [q[rr][:, sl] for (rr, _), sl in zip(us, hl)]
                kh = [k2[rr][:, sl] for (rr, _), sl in zip(us, hl)]
                vh = [v2[rr][:, sl] for (rr, _), sl in zip(us, hl)]
                s = [_dot_nt(a, k) * scale + b_ref[hh] + edge for a, k, (_, hh) in zip(qh, kh, us)]
                p = [jnp.exp(x - lse_c[rr][:, HEAD_DIM * hh:HEAD_DIM * hh + 1]) for x, (rr, hh) in zip(s, us)]
                dy = [dy_c[rr][:, sl] for (rr, _), sl in zip(us, hl)]
                w = [w_c[rr][:, sl] for (rr, _), sl in zip(us, hl)]
                shift = [ww[:, 0:1] * jnp.sum(d_ * ya_c[rr][:, sl], axis=-1, keepdims=True)
                         for ww, d_, (rr, _), sl in zip(w, dy, us, hl)]
                do = [(ww * d_).astype(BF16) for ww, d_ in zip(w, dy)]
                ds = [pp * (_dot_nt(o_, v) - sh) for pp, o_, v, sh in zip(p, do, vh, shift)]
                for x, (_, hh) in zip(ds, us):
                    ds_sum[hh] = x if ds_sum[hh] is None else ds_sum[hh] + x
                dsb = [x.astype(BF16) for x in ds]
                pb = [x.astype(BF16) for x in p]
                for u, x, pp, a, k, o_ in zip(us, dsb, pb, qh, kh, do):
                    dq_part[u] = _dot(x, k) * scale
                    dk_part[u] = _dot_tn(x, a) * scale
                    dv_part[u] = _dot_tn(pp, o_)
            for hh in range(HEADS_PER_CHUNK):
                db_ref[hh] += ds_sum[hh]
            for rr in range(rps):
                r_ = rows[rr]
                dq_ref[r_, :] = jnp.concatenate([dq_part[(rr, hh)] for hh in range(HEADS_PER_CHUNK)], axis=1)
                dk2 = jnp.concatenate([dk_part[(rr, hh)] for hh in range(HEADS_PER_CHUNK)], axis=1)
                dv2 = jnp.concatenate([dv_part[(rr, hh)] for hh in range(HEADS_PER_CHUNK)], axis=1)
                dk_ref[r_, :] = dk_carry[r_, :] + dk2[0:B]
                dv_ref[r_, :] = dv_carry[r_, :] + dv2[0:B]
                dk_carry[r_, :] = dk2[B:2 * B]
                dv_carry[r_, :] = dv2[B:2 * B]

        @pl.when(n == nb)
        def _():
            for r_ in rows:
                dk_ref[r_, :] = dk_carry[r_, :]
                dv_ref[r_, :] = dv_carry[r_, :]

    last = nb - 1
    cur = pl.BlockSpec((None, RB, LANES), lambda j, n, rg: (j, jnp.minimum(n, last), 0))
    lag = pl.BlockSpec((None, RB, LANES), lambda j, n, rg: (j, jnp.maximum(n - 1, 0), 0))
    db_spec = pl.BlockSpec((HEADS_PER_CHUNK, B, 2 * B), lambda j, n, rg: (j, 0, 0))
    dq, dk, dv, db = pl.pallas_call(
        body, name=name, grid=(N_CHUNKS, nb + 1, nrg),
        in_specs=_attn_in_specs(gi, RB, last) + [cur, cur, cur, cur],
        out_specs=[cur, lag, lag, db_spec],
        out_shape=[jax.ShapeDtypeStruct((N_CHUNKS, S, LANES), F32)] * 3
        + [jax.ShapeDtypeStruct((HEADS_PER_GROUP, B, 2 * B), F32)],
        scratch_shapes=[pltpu.VMEM((RB, LANES), F32), pltpu.VMEM((RB, LANES), F32)],
        compiler_params=_cparams(("arbitrary", "arbitrary", "arbitrary")))(
            qkv, qkv, qkv, qkv, qkv, bias, lse, dya, ya, wts)
    return (dq, dk, dv), db


def _bias_grad(dbs, *, name):
    nk = ATTN_BLOCK * 2 * ATTN_BLOCK
    buckets = []
    for (_, dil) in ATTN_GROUPS:
        b, valid = _attn_tables(dil)
        buckets.append(np.where(valid, b, -1).reshape(1, nk))
    bk = jnp.asarray(np.stack(buckets).astype(np.int32))
    flat = [x.reshape(HEADS_PER_GROUP, nk) for x in dbs]

    def body(bk_ref, d0, d1, d2, o_ref):
        ids = lax.broadcasted_iota(jnp.int32, (NUM_BUCKETS, nk), 0)
        for gi, dref in enumerate((d0, d1, d2)):
            onehot = (ids == bk_ref[gi]).astype(F32)
            o_ref[gi] = lax.dot_general(onehot, dref[...], (((1,), (1,)), ((), ())),
                                        preferred_element_type=F32, precision=lax.Precision.HIGHEST)

    out = pl.pallas_call(
        body, name=name,
        out_shape=jax.ShapeDtypeStruct((3, NUM_BUCKETS, HEADS_PER_GROUP), F32),
        compiler_params=_cparams())(bk, *flat)
    return jnp.transpose(out, (1, 0, 2)).reshape(NUM_BUCKETS, 3 * HEADS_PER_GROUP)


def _ret_tables(S):
    half = RET_QK_DIM // 2
    pos = jnp.arange(S, dtype=F32)
    inv_freq = ROPE_BASE ** (-jnp.arange(half, dtype=F32) / half)
    ang = pos[:, None] * inv_freq[None]
    cos, sin = jnp.cos(ang), jnp.sin(ang)
    H, C = RET_HEADS, RET_CHUNK
    log_g = jnp.log(1.0 - 2.0 ** (-5.0 - jnp.arange(H, dtype=F32)))
    n = jnp.arange(C, dtype=F32)
    diff = n[:, None] - n[None, :]
    dmask = jnp.where(diff >= 0, jnp.exp(log_g[:, None, None] * jnp.maximum(diff, 0.0)), 0.0)
    q_dec = jnp.exp(log_g[:, None] * (n + 1.0))
    k_dec = jnp.exp(log_g[:, None] * (C - 1.0 - n))
    chunk_dec = jnp.exp(log_g * C)
    qd = jnp.broadcast_to(q_dec[:, :, None], (H, C, RET_QK_DIM))
    kd = jnp.broadcast_to(k_dec[:, :, None], (H, C, RET_QK_DIM))
    cd = jnp.broadcast_to(chunk_dec[:, None, None], (H, 1, RET_V_DIM))
    return cos, sin, dmask, qd, kd, cd


def _rot(t, cos, sin):
    half = RET_QK_DIM // 2
    t1, t2 = t[:, :half], t[:, half:]
    return jnp.concatenate([t1 * cos - t2 * sin, t1 * sin + t2 * cos], axis=-1)


def _unrot(t, cos, sin):
    half = RET_QK_DIM // 2
    t1, t2 = t[:, :half], t[:, half:]
    return jnp.concatenate([t1 * cos + t2 * sin, t2 * cos - t1 * sin], axis=-1)


def _ret_specs(rev, nC):
    C, DK, DV = RET_CHUNK, RET_QK_DIM, RET_V_DIM
    cidx = (lambda c: nC - 1 - c) if rev else (lambda c: c)
    H = RET_HEADS
    return dict(
        qk=lambda which: pl.BlockSpec((None, C, H * DK), lambda c: (which, cidx(c), 0)),
        q=pl.BlockSpec((C, H * DK), lambda c: (cidx(c), 0)),
        v=pl.BlockSpec((C, H * DV), lambda c: (cidx(c), 0)),
        cs=pl.BlockSpec((C, DK // 2), lambda c: (cidx(c), 0)),
        dmask=pl.BlockSpec((H, C, C), lambda c: (0, 0, 0)),
        dec=pl.BlockSpec((H, C, DK), lambda c: (0, 0, 0)),
        cd=pl.BlockSpec((H, 1, DV), lambda c: (0, 0, 0)),
        st=pl.BlockSpec((H, None, DK, DV), lambda c: (0, cidx(c), 0, 0)),
    )


def _ret_fwd(qk, v, g, tables, *, name):
    _, S, _ = qk.shape
    nC = S // RET_CHUNK
    C, DK, DV, H = RET_CHUNK, RET_QK_DIM, RET_V_DIM, RET_HEADS
    cos, sin, dmask, qd, kd, cd = tables
    kscale = DK ** -0.5

    def body(q_ref, k_ref, v_ref, g_ref, cos_ref, sin_ref, dm_ref, qd_ref, kd_ref, cd_ref,
             o_ref, yb_ref, ybt_ref, st_ref, state):
        @pl.when(pl.program_id(0) == 0)
        def _():
            state[...] = jnp.zeros_like(state)

        cs, sn = cos_ref[...], sin_ref[...]
        for h in range(H):
            qs, vs = slice(DK * h, DK * (h + 1)), slice(DV * h, DV * (h + 1))
            Q = _rot(q_ref[:, qs], cs, sn)
            K = _rot(k_ref[:, qs], cs, sn) * kscale
            Qb, Kb, V = Q.astype(BF16), K.astype(BF16), v_ref[:, vs]
            sb = state[h].astype(BF16)
            st_ref[h] = sb
            A = _dot_nt(Qb, Kb) * dm_ref[h]
            o = _dot(A.astype(BF16), V) + _dot((Q * qd_ref[h]).astype(BF16), sb)
            state[h] = state[h] * cd_ref[h] + _dot_tn((K * kd_ref[h]).astype(BF16), V)
            mu = jnp.mean(o, axis=-1, keepdims=True)
            dd = o - mu
            var = jnp.mean(dd * dd, axis=-1, keepdims=True)
            yn = dd * lax.rsqrt(var + GN_EPS)
            gv = g_ref[:, vs]
            yb = gv * jax.nn.sigmoid(gv) * yn
            o_ref[:, vs] = o
            yb_ref[:, vs] = yb.astype(BF16)
            ybt_ref[vs, :] = yb.T.astype(BF16)

    sp = _ret_specs(False, nC)
    return pl.pallas_call(
        body, name=name, grid=(nC,),
        in_specs=[sp["qk"](0), sp["qk"](1), sp["v"], sp["v"], sp["cs"], sp["cs"], sp["dmask"],
                  sp["dec"], sp["dec"], sp["cd"]],
        out_specs=[sp["v"], sp["v"], pl.BlockSpec((H * DV, C), lambda c: (0, c)), sp["st"]],
        out_shape=[jax.ShapeDtypeStruct((S, H * DV), F32), jax.ShapeDtypeStruct((S, H * DV), BF16),
                   jax.ShapeDtypeStruct((H * DV, S), BF16), jax.ShapeDtypeStruct((H, nC, DK, DV), BF16)],
        scratch_shapes=[pltpu.VMEM((H, DK, DV), F32)],
        compiler_params=_cparams(("arbitrary",)))(qk, qk, v, g, cos, sin, dmask, qd, kd, cd)


def _ret_bwd(dyb, qk, v, g, o, states, tables, *, name):
    _, S, _ = qk.shape
    nC = S // RET_CHUNK
    C, DK, DV, H = RET_CHUNK, RET_QK_DIM, RET_V_DIM, RET_HEADS
    cos, sin, dmask, qd, kd, cd = tables
    kscale = DK ** -0.5

    def body(dy_ref, q_ref, k_ref, v_ref, g_ref, o_ref, st_ref, cos_ref, sin_ref, dm_ref, qd_ref, kd_ref,
             cd_ref, dq_ref, dk_ref, dv_ref, dg_ref, dstate):
        @pl.when(pl.program_id(0) == 0)
        def _():
            dstate[...] = jnp.zeros_like(dstate)

        cs, sn = cos_ref[...], sin_ref[...]
        for h in range(H):
            qs, vs = slice(DK * h, DK * (h + 1)), slice(DV * h, DV * (h + 1))
            ov = o_ref[:, vs]
            mu = jnp.mean(ov, axis=-1, keepdims=True)
            dd = ov - mu
            var = jnp.mean(dd * dd, axis=-1, keepdims=True)
            rstd = lax.rsqrt(var + GN_EPS)
            yn = dd * rstd
            gv, dy = g_ref[:, vs], dy_ref[:, vs]
            sg = jax.nn.sigmoid(gv)
            dg_ref[:, vs] = dy * yn * (sg * (1.0 + gv * (1.0 - sg)))
            dyn = dy * (gv * sg)
            dO = rstd * (dyn - jnp.mean(dyn, axis=-1, keepdims=True)
                         - yn * jnp.mean(dyn * yn, axis=-1, keepdims=True))
            dOb = dO.astype(BF16)

            Q = _rot(q_ref[:, qs], cs, sn)
            K = _rot(k_ref[:, qs], cs, sn) * kscale
            Qb, Kb, V = Q.astype(BF16), K.astype(BF16), v_ref[:, vs]
            dm, qd_h, kd_h = dm_ref[h], qd_ref[h], kd_ref[h]
            Sb = st_ref[h]
            dSb = dstate[h].astype(BF16)
            Ab = (_dot_nt(Qb, Kb) * dm).astype(BF16)
            dAb = (_dot_nt(dOb, V) * dm).astype(BF16)
            Qd = (Q * qd_h).astype(BF16)
            Kd = (K * kd_h).astype(BF16)
            dQ = _dot(dAb, Kb) + _dot_nt(dOb, Sb) * qd_h
            dK = _dot_tn(dAb, Qb) + _dot_nt(V, dSb) * kd_h
            dv_ref[:, vs] = _dot_tn(Ab, dOb) + _dot(Kd, dSb)
            dstate[h] = dstate[h] * cd_ref[h] + _dot_tn(Qd, dOb)
            dq_ref[:, qs] = _unrot(dQ, cs, sn)
            dk_ref[:, qs] = _unrot(dK, cs, sn) * kscale

    sp = _ret_specs(True, nC)
    dq, dk, dv, dg = pl.pallas_call(
        body, name=name, grid=(nC,),
        in_specs=[sp["v"], sp["qk"](0), sp["qk"](1), sp["v"], sp["v"], sp["v"], sp["st"], sp["cs"], sp["cs"],
                  sp["dmask"], sp["dec"], sp["dec"], sp["cd"]],
        out_specs=[sp["q"], sp["q"], sp["v"], sp["v"]],
        out_shape=[jax.ShapeDtypeStruct((S, H * DK), F32), jax.ShapeDtypeStruct((S, H * DK), F32),
                   jax.ShapeDtypeStruct((S, H * DV), F32), jax.ShapeDtypeStruct((S, H * DV), F32)],
        scratch_shapes=[pltpu.VMEM((H, DK, DV), F32)],
        compiler_params=_cparams(("arbitrary",)))(dyb, qk, qk, v, g, o, states, cos, sin, dmask, qd, kd, cd)
    return dq, dk, dv, dg


def _layer_fwd(l, x, xb, x_t, W, b_in, biases, ln, tables):
    S = x.shape[0]
    tag = f"l{l}"
    win = W["w_in"]
    c0, c1, c2, c3, c4 = 3 * ATTN_W, 3 * ATTN_W + 2048, 3 * ATTN_W + 4096, 3 * ATTN_W + 6144, IN_COLS
    qkv_a = _mm(xb, win[:, :c0], bias=b_in[:c0], groups=9, lane_chunks=True, name=f"{tag}_in_attn")
    qk_r = _mm(xb, win[:, c0:c1], bias=b_in[c0:c1], groups=2, name=f"{tag}_in_retqk")
    v_r = _mm(xb, win[:, c1:c2], bias=b_in[c1:c2], out_dtype=BF16, name=f"{tag}_in_retv")
    g_r = _mm(xb, win[:, c2:c3], bias=b_in[c2:c3], name=f"{tag}_in_retg")
    gates = _mm(xb, win[:, c3:c4], bias=b_in[c3:c4], groups=2, name=f"{tag}_in_gates")

    os_, ls_ = [], []
    for gi, (_, dil) in enumerate(ATTN_GROUPS):
        o, lse = _attn_fwd(qkv_a, biases[gi], gi, dil, name=f"{tag}_attn_fwd{gi}")
        os_.append(o)
        ls_.append(lse)
    ya_b, ya_t, ya, wts = _combine_fwd(os_, ls_, name=f"{tag}_combine")

    o_r, yb, yb_t, states = _ret_fwd(qk_r, v_r, g_r, tables, name=f"{tag}_ret_fwd")

    pa = _mm(ya_b, W["w_attn_proj"], name=f"{tag}_attn_proj")
    pr = _mm(yb, W["w_ret_proj"], name=f"{tag}_ret_proj")
    merged, merged_t = _merge_fwd(gates, pa, pr, name=f"{tag}_merge")
    mix = _mm(merged, W["w_out"], name=f"{tag}_out_proj")
    h1, x1, x1b, x1_t = _ln_fwd(x, mix, ln["ln1_g"], ln["ln1_b"], name=f"{tag}_ln1")
    uv = _mm(x1b, W["w_gu"], groups=2, name=f"{tag}_ffn_in")
    hh, hh_t = _swiglu_fwd(uv, name=f"{tag}_swiglu")
    f = _mm(hh, W["w_ffn_down"], name=f"{tag}_ffn_down")
    h2, x2, x2b, x2_t = _ln_fwd(x1, f, ln["ln2_g"], ln["ln2_b"], name=f"{tag}_ln2")
    saved = dict(x_t=x_t, qkv_a=qkv_a, qk_r=qk_r, v_r=v_r, g_r=g_r, gates=gates, ls=ls_, ya_t=ya_t, ya=ya,
                 wts=wts, o_r=o_r, yb_t=yb_t, states=states, pa=pa, pr=pr, merged_t=merged_t, h1=h1, x1_t=x1_t,
                 uv=uv, hh_t=hh_t, h2=h2)
    return x2, x2b, x2_t, saved


def _layer_bwd(l, dx2, sv, W, WT, biases, ln, tables):
    S = dx2.shape[0]
    tag = f"l{l}"
    g = {}
    dh2b, res2, g["ln2_g"], g["ln2_b"] = _ln_bwd(dx2, sv["h2"], ln["ln2_g"], name=f"{tag}_ln2_bwd")
    dhh = _mm(dh2b, WT["w_ffn_down"], name=f"{tag}_d_hh")
    g["w_ffn_down"] = _mm(sv["hh_t"], dh2b, name=f"{tag}_dw_down")
    dudv = _swiglu_bwd(dhh, sv["uv"], name=f"{tag}_swiglu_bwd")
    dx1 = _mm(dudv, WT["w_gu"], add=res2, name=f"{tag}_d_x1")
    dwgu = _mm(sv["x1_t"], dudv, name=f"{tag}_dw_gu")
    g["w_ffn_gate"], g["w_ffn_up"] = dwgu[:, :D_FF], dwgu[:, D_FF:]

    dh1b, res1, g["ln1_g"], g["ln1_b"] = _ln_bwd(dx1, sv["h1"], ln["ln1_g"], name=f"{tag}_ln1_bwd")
    dmerged = _mm(dh1b, WT["w_out"], name=f"{tag}_d_merged")
    g["w_out"] = _mm(sv["merged_t"], dh1b, name=f"{tag}_dw_out")
    dpa, dpr, dgates = _merge_bwd(dmerged, sv["gates"], sv["pa"], sv["pr"], name=f"{tag}_merge_bwd")
    dya = _mm(dpa, WT["w_attn_proj"], groups=1, lane_chunks=True, name=f"{tag}_d_ya")
    g["w_attn_proj"] = _mm(sv["ya_t"], dpa, name=f"{tag}_dw_ap")
    dyb = _mm(dpr, WT["w_ret_proj"], name=f"{tag}_d_yb")
    g["w_ret_proj"] = _mm(sv["yb_t"], dpr, name=f"{tag}_dw_rp")

    da, dbs = [], []
    for gi, (_, dil) in enumerate(ATTN_GROUPS):
        dqkv, db = _attn_bwd(sv["qkv_a"], biases[gi], sv["ls"][gi], dya, sv["ya"], sv["wts"][gi], gi, dil,
                             name=f"{tag}_attn_bwd{gi}")
        da.append(dqkv)
        dbs.append(db)
    dq_r, dk_r, dv_r, dg_r = _ret_bwd(dyb, sv["qk_r"], sv["v_r"], sv["g_r"], sv["o_r"], sv["states"], tables,
                                 name=f"{tag}_ret_bwd")
    dz, colsum = _assemble_dz(da, dq_r, dk_r, dv_r, dg_r, dgates, name=f"{tag}_assemble_dz")
    g["b_in"] = colsum.reshape(IN_COLS)
    dx = _mm(dz, WT["w_in"], add=res1, name=f"{tag}_d_x")
    g["w_in"] = _mm(sv["x_t"], dz, name=f"{tag}_dw_in")
    return dx, g, dbs


HBM_SPEC = pl.BlockSpec(memory_space=pltpu.HBM)
OTHER_CHIPS = ((1, 0), (0, 1), (1, 1))


def _flip(v, f):
    return 1 - v if f else v


def _all_gather(shard, *, name):
    R, Wd = shard.shape

    def body(x_ref, out_ref, send_sems, recv_sems, local_sem):
        x, y, c = lax.axis_index("x"), lax.axis_index("y"), lax.axis_index("c")
        me, sibling = (x, y, c), (x, y, 1 - c)
        chips = [(_flip(x, fx), _flip(y, fy)) for fx, fy in OTHER_CHIPS]

        def rows(px, py, pc):
            return out_ref.at[4 * px + 2 * py + pc]

        def copy(k, block, to, src=None):
            return pltpu.make_async_remote_copy(
                src_ref=rows(*block) if src is None else src, dst_ref=rows(*block),
                send_sem=send_sems.at[k], recv_sem=recv_sems.at[k], device_id=to, device_id_type=MESH)

        mine = pltpu.make_async_copy(x_ref, rows(*me), local_sem)
        mine.start()
        first = [copy(0, me, sibling, src=x_ref)]
        first += [copy(1 + j, me, (*chip, c), src=x_ref) for j, chip in enumerate(chips)]
        for cp in first:
            cp.start()
        passed = [copy(4 + j, (*chip, c), sibling) for j, chip in enumerate(chips)]
        for j, chip in enumerate(chips):
            copy(1 + j, (*chip, c), me).wait_recv()
            passed[j].start()
        copy(0, sibling, me).wait_recv()
        for j, chip in enumerate(chips):
            copy(4 + j, (*chip, 1 - c), me).wait_recv()
        for cp in first + passed:
            cp.wait_send()
        mine.wait()

    return pl.pallas_call(
        body, name=name, out_shape=jax.ShapeDtypeStruct((N_DEV, R, Wd), shard.dtype),
        in_specs=[HBM_SPEC], out_specs=HBM_SPEC,
        scratch_shapes=[pltpu.SemaphoreType.DMA((7,)), pltpu.SemaphoreType.DMA((7,)), pltpu.SemaphoreType.DMA],
    )(shard)


def _rs_sibling_exchange(g8, *, name):
    _, _, R, Wd = g8.shape

    def body(g_ref, recv_ref, send_sems, recv_sems):
        x, y, c = lax.axis_index("x"), lax.axis_index("y"), lax.axis_index("c")
        copies = []
        for k in range(4):
            cp = pltpu.make_async_remote_copy(
                src_ref=g_ref.at[k, 1 - c], dst_ref=recv_ref.at[k], send_sem=send_sems.at[k],
                recv_sem=recv_sems.at[k], device_id=(x, y, 1 - c), device_id_type=MESH)
            cp.start()
            copies.append(cp)
        for cp in copies:
            cp.wait()

    return pl.pallas_call(
        body, name=name, out_shape=jax.ShapeDtypeStruct((4, R, Wd), g8.dtype),
        in_specs=[HBM_SPEC], out_specs=HBM_SPEC,
        scratch_shapes=[pltpu.SemaphoreType.DMA((4,)), pltpu.SemaphoreType.DMA((4,))],
    )(g8)


def _rs_chip_sum(g8, recv, core, *, name):
    _, _, R, Wd = g8.shape
    tr = _div_tile(R, 512, 16)

    def body(core_ref, g_ref, r_ref, o_ref):
        o_ref[...] = (g_ref[...] + r_ref[...]).astype(BF16)

    grid_spec = pltpu.PrefetchScalarGridSpec(
        num_scalar_prefetch=1, grid=(4, R // tr),
        in_specs=[pl.BlockSpec((None, None, tr, Wd), lambda k, i, core_ref: (k, core_ref[0], i, 0)),
                  pl.BlockSpec((None, tr, Wd), lambda k, i, core_ref: (k, i, 0))],
        out_specs=pl.BlockSpec((None, tr, Wd), lambda k, i, core_ref: (k, i, 0)))
    return pl.pallas_call(
        body, name=name, grid_spec=grid_spec, out_shape=jax.ShapeDtypeStruct((4, R, Wd), BF16),
        compiler_params=_cparams(("parallel", "parallel")))(core, g8, recv)


def _rs_chip_exchange(p, *, name):
    _, R, Wd = p.shape

    def body(p_ref, out_ref, send_sems, recv_sems, local_sem):
        x, y, c = lax.axis_index("x"), lax.axis_index("y"), lax.axis_index("c")
        my_chip = 2 * x + y
        mine = pltpu.make_async_copy(p_ref.at[my_chip], out_ref.at[my_chip], local_sem)
        mine.start()
        copies = []
        for j, (fx, fy) in enumerate(OTHER_CHIPS):
            px, py = _flip(x, fx), _flip(y, fy)
            cp = pltpu.make_async_remote_copy(
                src_ref=p_ref.at[2 * px + py], dst_ref=out_ref.at[my_chip], send_sem=send_sems.at[j],
                recv_sem=recv_sems.at[j], device_id=(px, py, c), device_id_type=MESH)
            cp.start()
            copies.append(cp)
        for cp in copies:
            cp.wait()
        mine.wait()

    return pl.pallas_call(
        body, name=name, out_shape=jax.ShapeDtypeStruct((4, R, Wd), p.dtype),
        in_specs=[HBM_SPEC], out_specs=HBM_SPEC,
        scratch_shapes=[pltpu.SemaphoreType.DMA((3,)), pltpu.SemaphoreType.DMA((3,)), pltpu.SemaphoreType.DMA],
    )(p)


def _all_reduce_small(v, *, name):
    R, Wd = v.shape

    def body(v_ref, out_ref, slots, send_sems, recv_sems):
        x, y, c = lax.axis_index("x"), lax.axis_index("y"), lax.axis_index("c")
        me = 4 * x + 2 * y + c
        slots[me] = v_ref[...]
        copies = []
        for rel in range(1, N_DEV):
            peer = (_flip(x, rel & 4), _flip(y, rel & 2), _flip(c, rel & 1))
            cp = pltpu.make_async_remote_copy(
                src_ref=v_ref, dst_ref=slots.at[me], send_sem=send_sems.at[rel - 1],
                recv_sem=recv_sems.at[rel - 1], device_id=peer, device_id_type=MESH)
            cp.start()
            copies.append(cp)
        for cp in copies:
            cp.wait()
        acc = slots[0]
        for j in range(1, N_DEV):
            acc = acc + slots[j]
        out_ref[...] = acc

    vm = pl.BlockSpec(memory_space=pltpu.VMEM)
    return pl.pallas_call(
        body, name=name, out_shape=jax.ShapeDtypeStruct((R, Wd), F32),
        in_specs=[vm], out_specs=vm,
        scratch_shapes=[pltpu.VMEM((N_DEV, R, Wd), F32), pltpu.SemaphoreType.DMA((7,)),
                        pltpu.SemaphoreType.DMA((7,))],
    )(v)


def _adam_math(w, g, m, v):
    m2 = ADAM_B1 * m + (1.0 - ADAM_B1) * g
    v2 = ADAM_B2 * v + (1.0 - ADAM_B2) * (g * g)
    m_hat = m2 / (1.0 - ADAM_B1 ** ADAM_STEP)
    v_hat = v2 / (1.0 - ADAM_B2 ** ADAM_STEP)
    delta = -ADAM_LR * (m_hat / (jnp.sqrt(v_hat) + ADAM_EPS) + ADAM_WD * w)
    return delta, m2, v2


def _adam_sharded(parts, w, m, v, *, name):
    R, Wd = w.shape
    tr = _div_tile(R, 512, 16)

    def body(p_ref, w_ref, m_ref, v_ref, g_ref, d_ref, m2_ref, v2_ref):
        g = p_ref[0].astype(F32)
        for s in range(1, 4):
            g = g + p_ref[s].astype(F32)
        delta, m2, v2 = _adam_math(w_ref[...], g, m_ref[...], v_ref[...])
        g_ref[...] = g
        d_ref[...] = delta
        m2_ref[...] = m2
        v2_ref[...] = v2

    return pl.pallas_call(
        body, name=name, grid=(R // tr,),
        in_specs=[pl.BlockSpec((4, tr, Wd), lambda i: (0, i, 0))] + [_row_spec(tr, Wd)] * 3,
        out_specs=[_row_spec(tr, Wd)] * 4, out_shape=[jax.ShapeDtypeStruct((R, Wd), F32)] * 4,
        compiler_params=_cparams(("parallel",)))(parts, w, m, v)


def _adam_small(g, w, m, v, *, name):
    R, Wd = w.shape

    def body(g_ref, w_ref, m_ref, v_ref, d_ref, m2_ref, v2_ref):
        delta, m2, v2 = _adam_math(w_ref[...], g_ref[...], m_ref[...], v_ref[...])
        d_ref[...] = delta
        m2_ref[...] = m2
        v2_ref[...] = v2

    return pl.pallas_call(
        body, name=name, out_shape=[jax.ShapeDtypeStruct((R, Wd), F32)] * 3,
        compiler_params=_cparams())(g, w, m, v)


def _shard_shape(name):
    r, c = FULL_SHAPE[name]
    return (r, c // N_DEV) if name in COL_SHARDED else (r // N_DEV, c)


def _pack_shards(t):
    return jnp.concatenate([t[n][l].reshape(-1, D_MODEL) for l in range(DEPTH) for n in BIG_WEIGHTS], axis=0)


def _unpack_shards(flat):
    out = {n: [] for n in BIG_WEIGHTS}
    off = 0
    for l in range(DEPTH):
        for n in BIG_WEIGHTS:
            r, c = _shard_shape(n)
            rows = r * c // D_MODEL
            out[n].append(flat[off:off + rows].reshape(r, c))
            off += rows
    return {n: jnp.stack(v) for n, v in out.items()}


def _unpack_gathered(gathered):
    layers = []
    off = 0
    for l in range(DEPTH):
        W = {}
        for n in BIG_WEIGHTS:
            r, c = _shard_shape(n)
            rows = r * c // D_MODEL
            blk = gathered[:, off:off + rows].reshape(N_DEV, r, c)
            if n in COL_SHARDED:
                W[n] = jnp.transpose(blk, (1, 0, 2)).reshape(FULL_SHAPE[n])
            else:
                W[n] = blk.reshape(FULL_SHAPE[n])
            off += rows
        layers.append(W)
    return layers


def _pack_full_grads(grads):
    pieces = []
    for l in range(DEPTH):
        for n in BIG_WEIGHTS:
            r, c = _shard_shape(n)
            gfull = grads[l][n]
            if n in COL_SHARDED:
                blk = jnp.transpose(gfull.reshape(r, N_DEV, c), (1, 0, 2))
            else:
                blk = gfull.reshape(N_DEV, r, c)
            pieces.append(blk.reshape(N_DEV, r * c // D_MODEL, D_MODEL))
    return jnp.concatenate(pieces, axis=1)


def _pack_small(t):
    flat = jnp.concatenate([t[n].reshape(-1).astype(F32) for n in SMALL_WEIGHTS])
    return jnp.pad(flat, (0, SMALL_ROWS * LANES - flat.shape[0])).reshape(SMALL_ROWS, LANES)


def _unpack_small(packed):
    flat = packed.reshape(-1)
    out, off = {}, 0
    for n in SMALL_WEIGHTS:
        size = math.prod(SMALL_SHAPE[n])
        out[n] = flat[off:off + size].reshape(SMALL_SHAPE[n])
        off += size
    return out


def _local_step(x, target, rel_bias, b_in, lns, layers_w):
    S = x.shape[0]
    tables = _ret_tables(S)
    biases = [_attn_bias(rel_bias, gi, dil) for gi, (_, dil) in enumerate(ATTN_GROUPS)]
    Ws, WTs = [], []
    for W in layers_w:
        W = dict(W)
        W["w_gu"] = jnp.concatenate([W["w_ffn_gate"], W["w_ffn_up"]], axis=1)
        Ws.append(W)
        WTs.append({n: W[n].T for n in ("w_in", "w_attn_proj", "w_ret_proj", "w_out", "w_gu", "w_ffn_down")})

    h, hb = x, x.astype(BF16)
    h_t = hb.T
    saved = []
    for l in range(DEPTH):
        h, hb, h_t, sv = _layer_fwd(l, h, hb, h_t, Ws[l], b_in[l], biases, lns[l], tables)
        saved.append(sv)
    dy, sq = _loss_fwd_bwd(h, target, name="loss")
    loss_local = 0.5 * sq[0, 0] / D_MODEL

    grads = [None] * DEPTH
    db_tot = None
    dx = dy
    for l in reversed(range(DEPTH)):
        dx, g, dbs = _layer_bwd(l, dx, saved[l], Ws[l], WTs[l], biases, lns[l], tables)
        grads[l] = g
        db_tot = dbs if db_tot is None else [a + b for a, b in zip(db_tot, dbs)]
    small = {"rel_bias": _bias_grad(db_tot, name="bias_grad"),
             "b_in": jnp.stack([grads[l]["b_in"] for l in range(DEPTH)])}
    for n in ("ln1_g", "ln1_b", "ln2_g", "ln2_b"):
        small[n] = jnp.stack([grads[l][n].reshape(D_MODEL) for l in range(DEPTH)])
    return loss_local, dx, grads, small


def kernel(x, rel_bias, w_in, b_in, w_attn_proj, w_ret_proj, w_out, ln1_g, ln1_b, w_ffn_gate, w_ffn_up, w_ffn_down, ln2_g, ln2_b, loss_target, m_rel_bias, m_w_in, m_b_in, m_w_attn_proj, m_w_ret_proj, m_w_out, m_ln1_g, m_ln1_b, m_w_ffn_gate, m_w_ffn_up, m_w_ffn_down, m_ln2_g, m_ln2_b, v_rel_bias, v_w_in, v_b_in, v_w_attn_proj, v_w_ret_proj, v_w_out, v_ln1_g, v_ln1_b, v_w_ffn_gate, v_w_ffn_up, v_w_ffn_down, v_ln2_g, v_ln2_b):
    w = dict(rel_bias=rel_bias, w_in=w_in, b_in=b_in, w_attn_proj=w_attn_proj, w_ret_proj=w_ret_proj, w_out=w_out,
             ln1_g=ln1_g, ln1_b=ln1_b, w_ffn_gate=w_ffn_gate, w_ffn_up=w_ffn_up, w_ffn_down=w_ffn_down,
             ln2_g=ln2_g, ln2_b=ln2_b)
    m = dict(rel_bias=m_rel_bias, w_in=m_w_in, b_in=m_b_in, w_attn_proj=m_w_attn_proj, w_ret_proj=m_w_ret_proj,
             w_out=m_w_out, ln1_g=m_ln1_g, ln1_b=m_ln1_b, w_ffn_gate=m_w_ffn_gate, w_ffn_up=m_w_ffn_up,
             w_ffn_down=m_w_ffn_down, ln2_g=m_ln2_g, ln2_b=m_ln2_b)
    v = dict(rel_bias=v_rel_bias, w_in=v_w_in, b_in=v_b_in, w_attn_proj=v_w_attn_proj, w_ret_proj=v_w_ret_proj,
             w_out=v_w_out, ln1_g=v_ln1_g, ln1_b=v_ln1_b, w_ffn_gate=v_w_ffn_gate, w_ffn_up=v_w_ffn_up,
             w_ffn_down=v_w_ffn_down, ln2_g=v_ln2_g, ln2_b=v_ln2_b)

    w_flat = _pack_shards(w)
    gathered = _all_gather(w_flat.astype(BF16), name="all_gather_weights")
    layers_w = _unpack_gathered(gathered)
    lns = [{n: w[n][l] for n in ("ln1_g", "ln1_b", "ln2_g", "ln2_b")} for l in range(DEPTH)]

    loss_local, grad_x, grads, small = _local_step(x[0], loss_target[0], rel_bias, b_in, lns, layers_w)
    loss = lax.psum(loss_local, ("x", "y", "c"))

    g8 = _pack_full_grads(grads)
    rows = g8.shape[1]
    g8 = g8.reshape(4, 2, rows, D_MODEL)
    from_sibling = _rs_sibling_exchange(g8, name="rs_sibling_exchange")
    core = lax.axis_index("c").astype(jnp.int32).reshape(1)
    chip_parts = _rs_chip_sum(g8, from_sibling, core, name="rs_chip_sum")
    parts = _rs_chip_exchange(chip_parts, name="rs_chip_exchange")
    g_flat, d_flat, m_flat, v_flat = _adam_sharded(parts, w_flat, _pack_shards(m), _pack_shards(v),
                                                   name="adam_sharded")
    big = [_unpack_shards(t) for t in (g_flat, d_flat, m_flat, v_flat)]

    gs = _all_reduce_small(_pack_small(small), name="all_reduce_small")
    ds, ms, vs = _adam_small(gs, _pack_small(w), _pack_small(m), _pack_small(v), name="adam_small")
    sm = [_unpack_small(t) for t in (gs, ds, ms, vs)]

    outs = [loss, grad_x[None]]
    for kind in range(4):
        for n in ALL_WEIGHTS:
            outs.append(big[kind][n] if n in BIG_WEIGHTS else sm[kind][n])
    return tuple(outs)
```

```python
import functools
import math

import numpy as np
import jax
import jax.numpy as jnp
from jax import lax
from jax.experimental import pallas as pl
from jax.experimental.pallas import tpu as pltpu

F32 = jnp.float32
BF16 = jnp.bfloat16
MESH = pl.DeviceIdType.MESH

D_MODEL = 1024
DEPTH = 2
HEAD_DIM = 64
ATTN_GROUPS = ((128, 1), (512, 4), (2048, 16))
HEADS_PER_GROUP = 6
GROUP_WIDTH = HEADS_PER_GROUP * HEAD_DIM
ATTN_BLOCK = 128
NUM_BUCKETS = 32
MAX_DISTANCE = 2048
RET_HEADS = 4
RET_QK_DIM = 256
RET_V_DIM = 512
RET_CHUNK = 128
ROPE_BASE = 10000.0
D_FF = 2816
ALPHA = (2 * DEPTH) ** 0.25
LN_EPS = 1e-5
GN_EPS = 1e-5
ATTN_W = 3 * GROUP_WIDTH
IN_COLS = 3 * ATTN_W + 2 * 1024 + 2 * 2048 + 2 * 1024
ADAM_LR, ADAM_B1, ADAM_B2, ADAM_EPS, ADAM_WD, ADAM_STEP = 0.001, 0.9, 0.999, 1e-08, 0.01, 10
N_DEV = 8
NEG = -1e30
LANES = 128
VMEM_LIMIT = 56 * 1024 * 1024

BIG_WEIGHTS = ("w_in", "w_attn_proj", "w_ret_proj", "w_out", "w_ffn_gate", "w_ffn_up", "w_ffn_down")
COL_SHARDED = ("w_in", "w_attn_proj", "w_ffn_gate", "w_ffn_up")
FULL_SHAPE = {"w_in": (D_MODEL, IN_COLS), "w_attn_proj": (GROUP_WIDTH, D_MODEL), "w_ret_proj": (2048, D_MODEL),
              "w_out": (D_MODEL, D_MODEL), "w_ffn_gate": (D_MODEL, D_FF), "w_ffn_up": (D_MODEL, D_FF),
              "w_ffn_down": (D_FF, D_MODEL)}
SMALL_WEIGHTS = ("rel_bias", "b_in", "ln1_g", "ln1_b", "ln2_g", "ln2_b")
SMALL_SHAPE = {"rel_bias": (NUM_BUCKETS, 18), "b_in": (DEPTH, IN_COLS), "ln1_g": (DEPTH, D_MODEL),
               "ln1_b": (DEPTH, D_MODEL), "ln2_g": (DEPTH, D_MODEL), "ln2_b": (DEPTH, D_MODEL)}
SMALL_ROWS = 256
ALL_WEIGHTS = ("rel_bias", "w_in", "b_in", "w_attn_proj", "w_ret_proj", "w_out", "ln1_g", "ln1_b",
               "w_ffn_gate", "w_ffn_up", "w_ffn_down", "ln2_g", "ln2_b")


def _cparams(sem=None):
    return pltpu.CompilerParams(dimension_semantics=sem, vmem_limit_bytes=VMEM_LIMIT)


def _div_tile(n, cap, unit):
    if n <= cap:
        return n
    best = None
    for t in range(unit, cap + 1, unit):
        if n % t == 0:
            best = t
    assert best is not None, (n, cap, unit)
    return best


def _mm(a, b, *, name, out_dtype=F32, bias=None, add=None, groups=None, lane_chunks=False, transpose_b=False):
    M, K = a.shape
    N, K2 = b.shape if transpose_b else b.shape[::-1]
    assert K == K2 and a.dtype == BF16 and b.dtype == BF16
    tm = _div_tile(M, 1024, 16)
    tn = N // groups if groups else _div_tile(N, 1536, LANES)
    tk = _div_tile(K, 1536, LANES)
    nk = K // tk
    nch = tn // LANES
    has_bias, has_add = bias is not None, add is not None

    def body(*refs):
        a_ref, b_ref = refs[0], refs[1]
        pos = 2
        bias_ref = add_ref = None
        if has_bias:
            bias_ref = refs[pos]
            pos += 1
        if has_add:
            add_ref = refs[pos]
            pos += 1
        o_ref = refs[pos]

        def finish(r):
            if has_bias:
                r = r + bias_ref[...]
            if has_add:
                r = r + add_ref[...]
            if lane_chunks:
                for c in range(nch):
                    o_ref[c] = r[:, c * LANES:(c + 1) * LANES].astype(o_ref.dtype)
            else:
                o_ref[...] = r.astype(o_ref.dtype)

        if transpose_b:
            part = lax.dot_general(a_ref[...], b_ref[...], (((1,), (1,)), ((), ())), preferred_element_type=F32)
        else:
            part = jnp.dot(a_ref[...], b_ref[...], preferred_element_type=F32)
        if nk == 1:
            finish(part)
        else:
            acc_ref = refs[pos + 1]
            k = pl.program_id(2)

            @pl.when(k == 0)
            def _():
                acc_ref[...] = part

            @pl.when(k > 0)
            def _():
                acc_ref[...] += part

            @pl.when(k == nk - 1)
            def _():
                finish(acc_ref[...])

    in_specs = [pl.BlockSpec((tm, tk), lambda i, j, k: (i, k)),
                pl.BlockSpec((tn, tk), lambda i, j, k: (j, k)) if transpose_b
                else pl.BlockSpec((tk, tn), lambda i, j, k: (k, j))]
    args = [a, b]
    if has_bias:
        in_specs.append(pl.BlockSpec((1, tn), lambda i, j, k: (0, j)))
        args.append(bias.reshape(1, N).astype(F32))
    if has_add:
        in_specs.append(pl.BlockSpec((tm, tn), lambda i, j, k: (i, j)))
        args.append(add)
    if lane_chunks:
        assert groups
        out_shape = jax.ShapeDtypeStruct((groups, nch, M, LANES), out_dtype)
        out_spec = pl.BlockSpec((None, nch, tm, LANES), lambda i, j, k: (j, 0, i, 0))
    elif groups:
        out_shape = jax.ShapeDtypeStruct((groups, M, tn), out_dtype)
        out_spec = pl.BlockSpec((None, tm, tn), lambda i, j, k: (j, i, 0))
    else:
        out_shape = jax.ShapeDtypeStruct((M, N), out_dtype)
        out_spec = pl.BlockSpec((tm, tn), lambda i, j, k: (i, j))
    scratch = [pltpu.VMEM((tm, tn), F32)] if nk > 1 else []
    out = pl.pallas_call(
        body, name=name, grid=(M // tm, N // tn, nk), in_specs=in_specs, out_specs=out_spec,
        out_shape=out_shape, scratch_shapes=scratch,
        compiler_params=_cparams(("parallel", "parallel", "arbitrary")))(*args)
    return out.reshape(groups * nch, M, LANES) if lane_chunks else out


def _row_spec(tr, w):
    return pl.BlockSpec((tr, w), lambda i: (i, 0))


def _vec_spec(w):
    return pl.BlockSpec((1, w), lambda i: (0, 0))


def _col_spec(w, tr):
    return pl.BlockSpec((w, tr), lambda i: (0, i))


def _cast_transpose(x, *, name):
    S, W = x.shape
    tr = 512

    def body(x_ref, o_ref, ot_ref):
        v = x_ref[...]
        o_ref[...] = v.astype(BF16)
        ot_ref[...] = v.T.astype(BF16)

    return pl.pallas_call(
        body, name=name, grid=(S // tr,), in_specs=[_row_spec(tr, W)],
        out_specs=[_row_spec(tr, W), _col_spec(W, tr)],
        out_shape=[jax.ShapeDtypeStruct((S, W), BF16), jax.ShapeDtypeStruct((W, S), BF16)],
        compiler_params=_cparams(("parallel",)))(x)


def _ln_fwd(x, sub, g, b, *, name):
    S, W = x.shape
    tr = 512

    def body(x_ref, s_ref, g_ref, b_ref, h_ref, y_ref, yb_ref, ybt_ref):
        h = ALPHA * x_ref[...] + s_ref[...]
        mu = jnp.mean(h, axis=-1, keepdims=True)
        d = h - mu
        var = jnp.mean(d * d, axis=-1, keepdims=True)
        y = d * lax.rsqrt(var + LN_EPS) * g_ref[...] + b_ref[...]
        h_ref[...] = h
        y_ref[...] = y
        yb_ref[...] = y.astype(BF16)
        ybt_ref[...] = y.T.astype(BF16)

    return pl.pallas_call(
        body, name=name, grid=(S // tr,),
        in_specs=[_row_spec(tr, W), _row_spec(tr, W), _vec_spec(W), _vec_spec(W)],
        out_specs=[_row_spec(tr, W)] * 3 + [_col_spec(W, tr)],
        out_shape=[jax.ShapeDtypeStruct((S, W), F32), jax.ShapeDtypeStruct((S, W), F32),
                   jax.ShapeDtypeStruct((S, W), BF16), jax.ShapeDtypeStruct((W, S), BF16)],
        compiler_params=_cparams(("parallel",)))(x, sub, g.reshape(1, W), b.reshape(1, W))


def _ln_bwd(dy, h, g, *, name):
    S, W = dy.shape
    tr = 512

    def body(dy_ref, h_ref, g_ref, dhb_ref, res_ref, dg_ref, db_ref):
        @pl.when(pl.program_id(0) == 0)
        def _():
            dg_ref[...] = jnp.zeros_like(dg_ref)
            db_ref[...] = jnp.zeros_like(db_ref)

        hh = h_ref[...]
        mu = jnp.mean(hh, axis=-1, keepdims=True)
        d = hh - mu
        var = jnp.mean(d * d, axis=-1, keepdims=True)
        rstd = lax.rsqrt(var + LN_EPS)
        xhat = d * rstd
        dyv = dy_ref[...]
        dg_ref[...] += jnp.sum(dyv * xhat, axis=0, keepdims=True)
        db_ref[...] += jnp.sum(dyv, axis=0, keepdims=True)
        dxh = dyv * g_ref[...]
        dh = rstd * (dxh - jnp.mean(dxh, axis=-1, keepdims=True)
                     - xhat * jnp.mean(dxh * xhat, axis=-1, keepdims=True))
        dhb_ref[...] = dh.astype(BF16)
        res_ref[...] = ALPHA * dh

    return pl.pallas_call(
        body, name=name, grid=(S // tr,),
        in_specs=[_row_spec(tr, W), _row_spec(tr, W), _vec_spec(W)],
        out_specs=[_row_spec(tr, W), _row_spec(tr, W), _vec_spec(W), _vec_spec(W)],
        out_shape=[jax.ShapeDtypeStruct((S, W), BF16), jax.ShapeDtypeStruct((S, W), F32),
                   jax.ShapeDtypeStruct((1, W), F32), jax.ShapeDtypeStruct((1, W), F32)],
        compiler_params=_cparams(("arbitrary",)))(dy, h, g.reshape(1, W))


def _loss_fwd_bwd(y, target, *, name):
    S, W = y.shape
    tr = 512

    def body(y_ref, t_ref, dy_ref, acc_ref):
        @pl.when(pl.program_id(0) == 0)
        def _():
            acc_ref[...] = jnp.zeros_like(acc_ref)

        e = y_ref[...] - t_ref[...]
        acc_ref[...] += jnp.sum(jnp.sum(e * e, axis=-1, keepdims=True), axis=0, keepdims=True)
        dy_ref[...] = e * (1.0 / W)

    return pl.pallas_call(
        body, name=name, grid=(S // tr,),
        in_specs=[_row_spec(tr, W), _row_spec(tr, W)],
        out_specs=[_row_spec(tr, W), pl.BlockSpec((1, 1), lambda i: (0, 0))],
        out_shape=[jax.ShapeDtypeStruct((S, W), F32), jax.ShapeDtypeStruct((1, 1), F32)],
        compiler_params=_cparams(("arbitrary",)))(y, target)


def _combine_fwd(os_, ls_, *, name):
    NCH, S, _ = os_[0].shape
    W = NCH * LANES
    tr = 512

    def body(o0, o1, o2, l0, l1, l2, yb_ref, ybt_ref, y_ref, w0_ref, w1_ref, w2_ref):
        for c in range(NCH):
            la, lb, lc = l0[c], l1[c], l2[c]
            m = jnp.maximum(jnp.maximum(la, lb), lc)
            ea, eb, ec = jnp.exp(la - m), jnp.exp(lb - m), jnp.exp(lc - m)
            inv = 1.0 / (ea + eb + ec)
            wa, wb, wc = ea * inv, eb * inv, ec * inv
            y = wa * o0[c] + wb * o1[c] + wc * o2[c]
            y_ref[c] = y
            yb_ref[:, c * LANES:(c + 1) * LANES] = y.astype(BF16)
            ybt_ref[c * LANES:(c + 1) * LANES, :] = y.T.astype(BF16)
            w0_ref[c] = wa
            w1_ref[c] = wb
            w2_ref[c] = wc

    ch = pl.BlockSpec((NCH, tr, LANES), lambda i: (0, i, 0))
    yb, ybt, y, w0, w1, w2 = pl.pallas_call(
        body, name=name, grid=(S // tr,),
        in_specs=[ch] * 6,
        out_specs=[_row_spec(tr, W), _col_spec(W, tr)] + [ch] * 4,
        out_shape=[jax.ShapeDtypeStruct((S, W), BF16), jax.ShapeDtypeStruct((W, S), BF16)]
        + [jax.ShapeDtypeStruct((NCH, S, LANES), F32)] * 4,
        compiler_params=_cparams(("parallel",)))(*os_, *ls_)
    return yb, ybt, y, (w0, w1, w2)


def _merge_fwd(gates, pa, pr, *, name):
    S, W = pa.shape
    tr = 512

    def body(g_ref, pa_ref, pr_ref, o_ref, ot_ref):
        m = jax.nn.sigmoid(g_ref[0]) * pa_ref[...] + jax.nn.sigmoid(g_ref[1]) * pr_ref[...]
        o_ref[...] = m.astype(BF16)
        ot_ref[...] = m.T.astype(BF16)

    return pl.pallas_call(
        body, name=name, grid=(S // tr,),
        in_specs=[pl.BlockSpec((2, tr, W), lambda i: (0, i, 0)), _row_spec(tr, W), _row_spec(tr, W)],
        out_specs=[_row_spec(tr, W), _col_spec(W, tr)],
        out_shape=[jax.ShapeDtypeStruct((S, W), BF16), jax.ShapeDtypeStruct((W, S), BF16)],
        compiler_params=_cparams(("parallel",)))(gates, pa, pr)


def _merge_bwd(dm, gates, pa, pr, *, name):
    S, W = pa.shape
    tr = 256

    def body(dm_ref, g_ref, pa_ref, pr_ref, dpa_ref, dpr_ref, dg_ref):
        dmv = dm_ref[...]
        sa, sb = jax.nn.sigmoid(g_ref[0]), jax.nn.sigmoid(g_ref[1])
        dpa_ref[...] = (dmv * sa).astype(BF16)
        dpr_ref[...] = (dmv * sb).astype(BF16)
        dg_ref[0] = dmv * pa_ref[...] * (sa * (1.0 - sa))
        dg_ref[1] = dmv * pr_ref[...] * (sb * (1.0 - sb))

    g3 = pl.BlockSpec((2, tr, W), lambda i: (0, i, 0))
    return pl.pallas_call(
        body, name=name, grid=(S // tr,),
        in_specs=[_row_spec(tr, W), g3, _row_spec(tr, W), _row_spec(tr, W)],
        out_specs=[_row_spec(tr, W), _row_spec(tr, W), g3],
        out_shape=[jax.ShapeDtypeStruct((S, W), BF16), jax.ShapeDtypeStruct((S, W), BF16),
                   jax.ShapeDtypeStruct((2, S, W), F32)],
        compiler_params=_cparams(("parallel",)))(dm, gates, pa, pr)


def _swiglu_fwd(uv, *, name):
    _, S, W = uv.shape
    tr = 256

    def body(uv_ref, o_ref, ot_ref):
        u = uv_ref[0]
        hh = u * jax.nn.sigmoid(u) * uv_ref[1]
        o_ref[...] = hh.astype(BF16)
        ot_ref[...] = hh.T.astype(BF16)

    return pl.pallas_call(
        body, name=name, grid=(S // tr,),
        in_specs=[pl.BlockSpec((2, tr, W), lambda i: (0, i, 0))],
        out_specs=[_row_spec(tr, W), _col_spec(W, tr)],
        out_shape=[jax.ShapeDtypeStruct((S, W), BF16), jax.ShapeDtypeStruct((W, S), BF16)],
        compiler_params=_cparams(("parallel",)))(uv)


def _swiglu_bwd(dh, uv, *, name):
    _, S, W = uv.shape
    tr = 256

    def body(dh_ref, uv_ref, o_ref):
        u, v, d = uv_ref[0], uv_ref[1], dh_ref[...]
        sg = jax.nn.sigmoid(u)
        o_ref[:, 0:W] = (d * v * (sg * (1.0 + u * (1.0 - sg)))).astype(BF16)
        o_ref[:, W:2 * W] = (d * (u * sg)).astype(BF16)

    return pl.pallas_call(
        body, name=name, grid=(S // tr,),
        in_specs=[_row_spec(tr, W), pl.BlockSpec((2, tr, W), lambda i: (0, i, 0))],
        out_specs=_row_spec(tr, 2 * W), out_shape=jax.ShapeDtypeStruct((S, 2 * W), BF16),
        compiler_params=_cparams(("parallel",)))(dh, uv)


def _assemble_dz(da, dq_r, dk_r, dv_r, dg_r, dgates, *, name):
    S = dv_r.shape[0]
    tr = 128
    GW = GROUP_WIDTH
    NCH = GW // LANES

    def body(*refs):
        a_refs = refs[0:9]
        q_ref, k_ref, v_ref, g_ref, gt_ref, dz_ref, cs_ref = refs[9:]

        @pl.when(pl.program_id(0) == 0)
        def _():
            cs_ref[...] = jnp.zeros_like(cs_ref)

        def put(off, val):
            w = val.shape[-1]
            dz_ref[:, off:off + w] = val.astype(BF16)
            cs_ref[:, off:off + w] += jnp.sum(val, axis=0, keepdims=True)

        for which in range(3):
            for gi in range(3):
                for c in range(NCH):
                    put(which * ATTN_W + gi * GW + c * LANES, a_refs[3 * gi + which][c])
        off = 3 * ATTN_W
        put(off, q_ref[...])
        put(off + 1024, k_ref[...])
        put(off + 2048, v_ref[...])
        put(off + 4096, g_ref[...])
        put(off + 6144, gt_ref[0])
        put(off + 7168, gt_ref[1])

    flat_a = [t for grp in da for t in grp]
    return pl.pallas_call(
        body, name=name, grid=(S // tr,),
        in_specs=[pl.BlockSpec((NCH, tr, LANES), lambda i: (0, i, 0))] * 9 + [_row_spec(tr, 1024), _row_spec(tr, 1024),
                  _row_spec(tr, 2048), _row_spec(tr, 2048), pl.BlockSpec((2, tr, 1024), lambda i: (0, i, 0))],
        out_specs=[_row_spec(tr, IN_COLS), _vec_spec(IN_COLS)],
        out_shape=[jax.ShapeDtypeStruct((S, IN_COLS), BF16), jax.ShapeDtypeStruct((1, IN_COLS), F32)],
        compiler_params=_cparams(("arbitrary",)))(*flat_a, dq_r, dk_r, dv_r, dg_r, dgates)


def _t5_bucket(dist):
    max_exact = NUM_BUCKETS // 2
    large = max_exact + (np.log(np.maximum(dist, max_exact) / max_exact)
                         / np.log(MAX_DISTANCE / max_exact) * (NUM_BUCKETS - max_exact)).astype(np.int32)
    large = np.minimum(large, NUM_BUCKETS - 1)
    return np.where(dist < max_exact, dist, large).astype(np.int32)


def _attn_tables(dilation):
    W = ATTN_BLOCK
    qi = np.arange(W)[:, None]
    kj = np.arange(2 * W)[None, :]
    rel = qi + W - kj
    valid = (rel >= 0) & (rel <= W)
    buckets = _t5_bucket(np.clip(rel, 0, W) * dilation)
    return buckets, valid


def _attn_bias(rel_bias, gi, dilation):
    buckets, valid = _attn_tables(dilation)
    table = rel_bias[:, gi * HEADS_PER_GROUP:(gi + 1) * HEADS_PER_GROUP]
    onehot = (jnp.asarray(buckets.reshape(-1, 1)) == jnp.arange(NUM_BUCKETS)[None, :]).astype(F32)
    bias = jnp.dot(onehot, table.astype(F32), precision=lax.Precision.HIGHEST)
    bias = bias.T.reshape(HEADS_PER_GROUP, ATTN_BLOCK, 2 * ATTN_BLOCK)
    return jnp.where(jnp.asarray(valid)[None], bias, NEG)


def _dot_nt(a, b):
    return lax.dot_general(a, b, (((1,), (1,)), ((), ())), preferred_element_type=F32)


def _dot_tn(a, b):
    return lax.dot_general(a, b, (((0,), (0,)), ((), ())), preferred_element_type=F32)


def _dot(a, b):
    return jnp.dot(a, b, preferred_element_type=F32)


ATTN_RESIDUES_PER_STEP = 4
ATTN_UNITS_AT_ONCE = 4
HEADS_PER_CHUNK = LANES // HEAD_DIM
N_CHUNKS = GROUP_WIDTH // LANES


def _first_block_mask(has_prev):
    col = lax.broadcasted_iota(jnp.int32, (1, 2 * ATTN_BLOCK), 1)
    return jnp.where(jnp.logical_or(has_prev, col >= ATTN_BLOCK), 0.0, NEG).astype(F32)


def _head_lanes(hh):
    return slice(HEAD_DIM * hh, HEAD_DIM * (hh + 1))


def _attn_geometry(S, d):
    rows_per_block = ATTN_BLOCK * d
    rps = min(d, ATTN_RESIDUES_PER_STEP)
    return rows_per_block, S // rows_per_block, rps, d // rps


def _residue_rows(d, rps, rg, rr):
    if d == 1:
        return slice(None)
    return pl.ds(rg * rps + rr, ATTN_BLOCK, stride=d)


def _attn_in_specs(gi, RB, last):
    def spec(which, prev):
        if prev:
            return pl.BlockSpec((None, RB, LANES),
                                lambda j, n, rg: (9 * which + 3 * gi + j, jnp.clip(n - 1, 0, last), 0))
        return pl.BlockSpec((None, RB, LANES), lambda j, n, rg: (9 * which + 3 * gi + j, jnp.minimum(n, last), 0))
    bias = pl.BlockSpec((HEADS_PER_CHUNK, ATTN_BLOCK, 2 * ATTN_BLOCK), lambda j, n, rg: (j, 0, 0))
    return [spec(0, False), spec(1, True), spec(1, False), spec(2, True), spec(2, False), bias]


def _attn_fwd(qkv, bias, gi, d, *, name):
    _, S, _ = qkv.shape
    B = ATTN_BLOCK
    RB, nb, rps, nrg = _attn_geometry(S, d)
    scale = HEAD_DIM ** -0.5
    units = [(rr, hh) for rr in range(rps) for hh in range(HEADS_PER_CHUNK)]

    def body(q_ref, kp_ref, kc_ref, vp_ref, vc_ref, b_ref, o_ref, l_ref):
        n, rg = pl.program_id(1), pl.program_id(2)
        edge = _first_block_mask(n > 0)
        rows = [_residue_rows(d, rps, rg, rr) for rr in range(rps)]
        q = [q_ref[r_, :].astype(BF16) for r_ in rows]
        k2 = [jnp.concatenate([kp_ref[r_, :], kc_ref[r_, :]], axis=0).astype(BF16) for r_ in rows]
        v2 = [jnp.concatenate([vp_ref[r_, :], vc_ref[r_, :]], axis=0).astype(BF16) for r_ in rows]
        o_part, l_part = {}, {}
        for u0 in range(0, len(units), ATTN_UNITS_AT_ONCE):
            us = units[u0:u0 + ATTN_UNITS_AT_ONCE]
            s = [_dot_nt(q[rr][:, _head_lanes(hh)], k2[rr][:, _head_lanes(hh)]) * scale + b_ref[hh] + edge
                 for rr, hh in us]
            m = [jnp.max(x, axis=-1, keepdims=True) for x in s]
            p = [jnp.exp(x - mm) for x, mm in zip(s, m)]
            l = [jnp.sum(x, axis=-1, keepdims=True) for x in p]
            pb = [(x * (1.0 / ll)).astype(BF16) for x, ll in zip(p, l)]
            o = [_dot(x, v2[rr][:, _head_lanes(hh)]) for x, (rr, hh) in zip(pb, us)]
            for u, oo, mm, ll in zip(us, o, m, l):
                o_part[u] = oo
                l_part[u] = jnp.broadcast_to(mm + jnp.log(ll), (B, HEAD_DIM))
        for rr in range(rps):
            o_ref[rows[rr], :] = jnp.concatenate([o_part[(rr, hh)] for hh in range(HEADS_PER_CHUNK)], axis=1)
            l_ref[rows[rr], :] = jnp.concatenate([l_part[(rr, hh)] for hh in range(HEADS_PER_CHUNK)], axis=1)

    out_spec = pl.BlockSpec((None, RB, LANES), lambda j, n, rg: (j, n, 0))
    return pl.pallas_call(
        body, name=name, grid=(N_CHUNKS, nb, nrg),
        in_specs=_attn_in_specs(gi, RB, nb - 1),
        out_specs=[out_spec, out_spec],
        out_shape=[jax.ShapeDtypeStruct((N_CHUNKS, S, LANES), F32)] * 2,
        compiler_params=_cparams(("parallel", "arbitrary", "arbitrary")))(qkv, qkv, qkv, qkv, qkv, bias)


def _attn_bwd(qkv, bias, lse, dya, ya, wts, gi, d, *, name):
    _, S, _ = qkv.shape
    B = ATTN_BLOCK
    RB, nb, rps, nrg = _attn_geometry(S, d)
    scale = HEAD_DIM ** -0.5
    units = [(rr, hh) for rr in range(rps) for hh in range(HEADS_PER_CHUNK)]

    def body(q_ref, kp_ref, kc_ref, vp_ref, vc_ref, b_ref, l_ref, dya_ref, ya_ref, w_ref,
             dq_ref, dk_ref, dv_ref, db_ref, dk_carry, dv_carry):
        n, rg = pl.program_id(1), pl.program_id(2)
        rows = [_residue_rows(d, rps, rg, rr) for rr in range(rps)]

        @pl.when((n == 0) & (rg == 0))
        def _():
            db_ref[...] = jnp.zeros_like(db_ref)
            dk_carry[...] = jnp.zeros_like(dk_carry)
            dv_carry[...] = jnp.zeros_like(dv_carry)

        @pl.when(n < nb)
        def _():
            edge = _first_block_mask(n > 0)
            q = [q_ref[r_, :].astype(BF16) for r_ in rows]
            k2 = [jnp.concatenate([kp_ref[r_, :], kc_ref[r_, :]], axis=0).astype(BF16) for r_ in rows]
            v2 = [jnp.concatenate([vp_ref[r_, :], vc_ref[r_, :]], axis=0).astype(BF16) for r_ in rows]
            lse_c = [l_ref[r_, :] for r_ in rows]
            dy_c = [dya_ref[r_, :] for r_ in rows]
            ya_c = [ya_ref[r_, :] for r_ in rows]
            w_c = [w_ref[r_, :] for r_ in rows]
            ds_sum = [None] * HEADS_PER_CHUNK
            dq_part, dk_part, dv_part = {}, {}, {}
            for u0 in range(0, len(units), ATTN_UNITS_AT_ONCE):
                us = units[u0:u0 + ATTN_UNITS_AT_ONCE]
                hl = [_head_lanes(hh) for _, hh in us]
                qh = [q[rr][:, sl] for (rr, _), sl in zip(us, hl)]
                kh = [k2[rr][:, sl] for (rr, _), sl in zip(us, hl)]
                vh = [v2[rr][:, sl] for (rr, _), sl in zip(us, hl)]
                s = [_dot_nt(a, k) * scale + b_ref[hh] + edge for a, k, (_, hh) in zip(qh, kh, us)]
                p = [jnp.exp(x - lse_c[rr][:, HEAD_DIM * hh:HEAD_DIM * hh + 1]) for x, (rr, hh) in zip(s, us)]
                dy = [dy_c[rr][:, sl] for (rr, _), sl in zip(us, hl)]
                w = [w_c[rr][:, sl] for (rr, _), sl in zip(us, hl)]
                shift = [ww[:, 0:1] * jnp.sum(d_ * ya_c[rr][:, sl], axis=-1, keepdims=True)
                         for ww, d_, (rr, _), sl in zip(w, dy, us, hl)]
                do = [(ww * d_).astype(BF16) for ww, d_ in zip(w, dy)]
                ds = [pp * (_dot_nt(o_, v) - sh) for pp, o_, v, sh in zip(p, do, vh, shift)]
                for x, (_, hh) in zip(ds, us):
                    ds_sum[hh] = x if ds_sum[hh] is None else ds_sum[hh] + x
                dsb = [x.astype(BF16) for x in ds]
                pb = [x.astype(BF16) for x in p]
                for u, x, pp, a, k, o_ in zip(us, dsb, pb, qh, kh, do):
                    dq_part[u] = _dot(x, k) * scale
                    dk_part[u] = _dot_tn(x, a) * scale
                    dv_part[u] = _dot_tn(pp, o_)
            for hh in range(HEADS_PER_CHUNK):
                db_ref[hh] += ds_sum[hh]
            for rr in range(rps):
                r_ = rows[rr]
                dq_ref[r_, :] = jnp.concatenate([dq_part[(rr, hh)] for hh in range(HEADS_PER_CHUNK)], axis=1)
                dk2 = jnp.concatenate([dk_part[(rr, hh)] for hh in range(HEADS_PER_CHUNK)], axis=1)
                dv2 = jnp.concatenate([dv_part[(rr, hh)] for hh in range(HEADS_PER_CHUNK)], axis=1)
                dk_ref[r_, :] = dk_carry[r_, :] + dk2[0:B]
                dv_ref[r_, :] = dv_carry[r_, :] + dv2[0:B]
                dk_carry[r_, :] = dk2[B:2 * B]
                dv_carry[r_, :] = dv2[B:2 * B]

        @pl.when(n == nb)
        def _():
            for r_ in rows:
                dk_ref[r_, :] = dk_carry[r_, :]
                dv_ref[r_, :] = dv_carry[r_, :]

    last = nb - 1
    cur = pl.BlockSpec((None, RB, LANES), lambda j, n, rg: (j, jnp.minimum(n, last), 0))
    lag = pl.BlockSpec((None, RB, LANES), lambda j, n, rg: (j, jnp.maximum(n - 1, 0), 0))
    db_spec = pl.BlockSpec((HEADS_PER_CHUNK, B, 2 * B), lambda j, n, rg: (j, 0, 0))
    dq, dk, dv, db = pl.pallas_call(
        body, name=name, grid=(N_CHUNKS, nb + 1, nrg),
        in_specs=_attn_in_specs(gi, RB, last) + [cur, cur, cur, cur],
        out_specs=[cur, lag, lag, db_spec],
        out_shape=[jax.ShapeDtypeStruct((N_CHUNKS, S, LANES), F32)] * 3
        + [jax.ShapeDtypeStruct((HEADS_PER_GROUP, B, 2 * B), F32)],
        scratch_shapes=[pltpu.VMEM((RB, LANES), F32), pltpu.VMEM((RB, LANES), F32)],
        compiler_params=_cparams(("arbitrary", "arbitrary", "arbitrary")))(
            qkv, qkv, qkv, qkv, qkv, bias, lse, dya, ya, wts)
    return (dq, dk, dv), db


def _bias_grad(dbs, *, name):
    nk = ATTN_BLOCK * 2 * ATTN_BLOCK
    buckets = []
    for (_, dil) in ATTN_GROUPS:
        b, valid = _attn_tables(dil)
        buckets.append(np.where(valid, b, -1).reshape(1, nk))
    bk = jnp.asarray(np.stack(buckets).astype(np.int32))
    flat = [x.reshape(HEADS_PER_GROUP, nk) for x in dbs]

    def body(bk_ref, d0, d1, d2, o_ref):
        ids = lax.broadcasted_iota(jnp.int32, (NUM_BUCKETS, nk), 0)
        for gi, dref in enumerate((d0, d1, d2)):
            onehot = (ids == bk_ref[gi]).astype(F32)
            o_ref[gi] = lax.dot_general(onehot, dref[...], (((1,), (1,)), ((), ())),
                                        preferred_element_type=F32, precision=lax.Precision.HIGHEST)

    out = pl.pallas_call(
        body, name=name,
        out_shape=jax.ShapeDtypeStruct((3, NUM_BUCKETS, HEADS_PER_GROUP), F32),
        compiler_params=_cparams())(bk, *flat)
    return jnp.transpose(out, (1, 0, 2)).reshape(NUM_BUCKETS, 3 * HEADS_PER_GROUP)


def _ret_tables(S):
    half = RET_QK_DIM // 2
    pos = jnp.arange(S, dtype=F32)
    inv_freq = ROPE_BASE ** (-jnp.arange(half, dtype=F32) / half)
    ang = pos[:, None] * inv_freq[None]
    cos, sin = jnp.cos(ang), jnp.sin(ang)
    H, C = RET_HEADS, RET_CHUNK
    log_g = jnp.log(1.0 - 2.0 ** (-5.0 - jnp.arange(H, dtype=F32)))
    n = jnp.arange(C, dtype=F32)
    diff = n[:, None] - n[None, :]
    dmask = jnp.where(diff >= 0, jnp.exp(log_g[:, None, None] * jnp.maximum(diff, 0.0)), 0.0)
    q_dec = jnp.exp(log_g[:, None] * (n + 1.0))
    k_dec = jnp.exp(log_g[:, None] * (C - 1.0 - n))
    chunk_dec = jnp.exp(log_g * C)
    qd = jnp.broadcast_to(q_dec[:, :, None], (H, C, RET_QK_DIM))
    kd = jnp.broadcast_to(k_dec[:, :, None], (H, C, RET_QK_DIM))
    cd = jnp.broadcast_to(chunk_dec[:, None, None], (H, 1, RET_V_DIM))
    return cos, sin, dmask, qd, kd, cd


def _rot(t, cos, sin):
    half = RET_QK_DIM // 2
    t1, t2 = t[:, :half], t[:, half:]
    return jnp.concatenate([t1 * cos - t2 * sin, t1 * sin + t2 * cos], axis=-1)


def _unrot(t, cos, sin):
    half = RET_QK_DIM // 2
    t1, t2 = t[:, :half], t[:, half:]
    return jnp.concatenate([t1 * cos + t2 * sin, t2 * cos - t1 * sin], axis=-1)


def _ret_specs(rev, nC):
    C, DK, DV = RET_CHUNK, RET_QK_DIM, RET_V_DIM
    cidx = (lambda c: nC - 1 - c) if rev else (lambda c: c)
    H = RET_HEADS
    return dict(
        qk=lambda which: pl.BlockSpec((None, C, H * DK), lambda c: (which, cidx(c), 0)),
        q=pl.BlockSpec((C, H * DK), lambda c: (cidx(c), 0)),
        v=pl.BlockSpec((C, H * DV), lambda c: (cidx(c), 0)),
        cs=pl.BlockSpec((C, DK // 2), lambda c: (cidx(c), 0)),
        dmask=pl.BlockSpec((H, C, C), lambda c: (0, 0, 0)),
        dec=pl.BlockSpec((H, C, DK), lambda c: (0, 0, 0)),
        cd=pl.BlockSpec((H, 1, DV), lambda c: (0, 0, 0)),
        st=pl.BlockSpec((H, None, DK, DV), lambda c: (0, cidx(c), 0, 0)),
    )


def _ret_fwd(qk, v, g, tables, *, name):
    _, S, _ = qk.shape
    nC = S // RET_CHUNK
    C, DK, DV, H = RET_CHUNK, RET_QK_DIM, RET_V_DIM, RET_HEADS
    cos, sin, dmask, qd, kd, cd = tables
    kscale = DK ** -0.5

    def body(q_ref, k_ref, v_ref, g_ref, cos_ref, sin_ref, dm_ref, qd_ref, kd_ref, cd_ref,
             o_ref, yb_ref, ybt_ref, st_ref, state):
        @pl.when(pl.program_id(0) == 0)
        def _():
            state[...] = jnp.zeros_like(state)

        cs, sn = cos_ref[...], sin_ref[...]
        for h in range(H):
            qs, vs = slice(DK * h, DK * (h + 1)), slice(DV * h, DV * (h + 1))
            Q = _rot(q_ref[:, qs], cs, sn)
            K = _rot(k_ref[:, qs], cs, sn) * kscale
            Qb, Kb, V = Q.astype(BF16), K.astype(BF16), v_ref[:, vs]
            sb = state[h].astype(BF16)
            st_ref[h] = sb
            A = _dot_nt(Qb, Kb) * dm_ref[h]
            o = _dot(A.astype(BF16), V) + _dot((Q * qd_ref[h]).astype(BF16), sb)
            state[h] = state[h] * cd_ref[h] + _dot_tn((K * kd_ref[h]).astype(BF16), V)
            mu = jnp.mean(o, axis=-1, keepdims=True)
            dd = o - mu
            var = jnp.mean(dd * dd, axis=-1, keepdims=True)
            yn = dd * lax.rsqrt(var + GN_EPS)
            gv = g_ref[:, vs]
            yb = gv * jax.nn.sigmoid(gv) * yn
            o_ref[:, vs] = o
            yb_ref[:, vs] = yb.astype(BF16)
            ybt_ref[vs, :] = yb.T.astype(BF16)

    sp = _ret_specs(False, nC)
    return pl.pallas_call(
        body, name=name, grid=(nC,),
        in_specs=[sp["qk"](0), sp["qk"](1), sp["v"], sp["v"], sp["cs"], sp["cs"], sp["dmask"],
                  sp["dec"], sp["dec"], sp["cd"]],
        out_specs=[sp["v"], sp["v"], pl.BlockSpec((H * DV, C), lambda c: (0, c)), sp["st"]],
        out_shape=[jax.ShapeDtypeStruct((S, H * DV), F32), jax.ShapeDtypeStruct((S, H * DV), BF16),
                   jax.ShapeDtypeStruct((H * DV, S), BF16), jax.ShapeDtypeStruct((H, nC, DK, DV), BF16)],
        scratch_shapes=[pltpu.VMEM((H, DK, DV), F32)],
        compiler_params=_cparams(("arbitrary",)))(qk, qk, v, g, cos, sin, dmask, qd, kd, cd)


def _ret_bwd(dyb, qk, v, g, o, states, tables, *, name):
    _, S, _ = qk.shape
    nC = S // RET_CHUNK
    C, DK, DV, H = RET_CHUNK, RET_QK_DIM, RET_V_DIM, RET_HEADS
    cos, sin, dmask, qd, kd, cd = tables
    kscale = DK ** -0.5

    def body(dy_ref, q_ref, k_ref, v_ref, g_ref, o_ref, st_ref, cos_ref, sin_ref, dm_ref, qd_ref, kd_ref,
             cd_ref, dq_ref, dk_ref, dv_ref, dg_ref, dstate):
        @pl.when(pl.program_id(0) == 0)
        def _():
            dstate[...] = jnp.zeros_like(dstate)

        cs, sn = cos_ref[...], sin_ref[...]
        for h in range(H):
            qs, vs = slice(DK * h, DK * (h + 1)), slice(DV * h, DV * (h + 1))
            ov = o_ref[:, vs]
            mu = jnp.mean(ov, axis=-1, keepdims=True)
            dd = ov - mu
            var = jnp.mean(dd * dd, axis=-1, keepdims=True)
            rstd = lax.rsqrt(var + GN_EPS)
            yn = dd * rstd
            gv, dy = g_ref[:, vs], dy_ref[:, vs]
            sg = jax.nn.sigmoid(gv)
            dg_ref[:, vs] = dy * yn * (sg * (1.0 + gv * (1.0 - sg)))
            dyn = dy * (gv * sg)
            dO = rstd * (dyn - jnp.mean(dyn, axis=-1, keepdims=True)
                         - yn * jnp.mean(dyn * yn, axis=-1, keepdims=True))
            dOb = dO.astype(BF16)

            Q = _rot(q_ref[:, qs], cs, sn)
            K = _rot(k_ref[:, qs], cs, sn) * kscale
            Qb, Kb, V = Q.astype(BF16), K.astype(BF16), v_ref[:, vs]
            dm, qd_h, kd_h = dm_ref[h], qd_ref[h], kd_ref[h]
            Sb = st_ref[h]
            dSb = dstate[h].astype(BF16)
            Ab = (_dot_nt(Qb, Kb) * dm).astype(BF16)
            dAb = (_dot_nt(dOb, V) * dm).astype(BF16)
            Qd = (Q * qd_h).astype(BF16)
            Kd = (K * kd_h).astype(BF16)
            dQ = _dot(dAb, Kb) + _dot_nt(dOb, Sb) * qd_h
            dK = _dot_tn(dAb, Qb) + _dot_nt(V, dSb) * kd_h
            dv_ref[:, vs] = _dot_tn(Ab, dOb) + _dot(Kd, dSb)
            dstate[h] = dstate[h] * cd_ref[h] + _dot_tn(Qd, dOb)
            dq_ref[:, qs] = _unrot(dQ, cs, sn)
            dk_ref[:, qs] = _unrot(dK, cs, sn) * kscale

    sp = _ret_specs(True, nC)
    dq, dk, dv, dg = pl.pallas_call(
        body, name=name, grid=(nC,),
        in_specs=[sp["v"], sp["qk"](0), sp["qk"](1), sp["v"], sp["v"], sp["v"], sp["st"], sp["cs"], sp["cs"],
                  sp["dmask"], sp["dec"], sp["dec"], sp["cd"]],
        out_specs=[sp["q"], sp["q"], sp["v"], sp["v"]],
        out_shape=[jax.ShapeDtypeStruct((S, H * DK), F32), jax.ShapeDtypeStruct((S, H * DK), F32),
                   jax.ShapeDtypeStruct((S, H * DV), F32), jax.ShapeDtypeStruct((S, H * DV), F32)],
        scratch_shapes=[pltpu.VMEM((H, DK, DV), F32)],
        compiler_params=_cparams(("arbitrary",)))(dyb, qk, qk, v, g, o, states, cos, sin, dmask, qd, kd, cd)
    return dq, dk, dv, dg


def _layer_fwd(l, x, xb, x_t, W, b_in, biases, ln, tables):
    S = x.shape[0]
    tag = f"l{l}"
    win = W["w_in"]
    c0, c1, c2, c3, c4 = 3 * ATTN_W, 3 * ATTN_W + 2048, 3 * ATTN_W + 4096, 3 * ATTN_W + 6144, IN_COLS
    qkv_a = _mm(xb, win[:, :c0], bias=b_in[:c0], groups=9, lane_chunks=True, name=f"{tag}_in_attn")
    qk_r = _mm(xb, win[:, c0:c1], bias=b_in[c0:c1], groups=2, name=f"{tag}_in_retqk")
    v_r = _mm(xb, win[:, c1:c2], bias=b_in[c1:c2], out_dtype=BF16, name=f"{tag}_in_retv")
    g_r = _mm(xb, win[:, c2:c3], bias=b_in[c2:c3], name=f"{tag}_in_retg")
    gates = _mm(xb, win[:, c3:c4], bias=b_in[c3:c4], groups=2, name=f"{tag}_in_gates")

    os_, ls_ = [], []
    for gi, (_, dil) in enumerate(ATTN_GROUPS):
        o, lse = _attn_fwd(qkv_a, biases[gi], gi, dil, name=f"{tag}_attn_fwd{gi}")
        os_.append(o)
        ls_.append(lse)
    ya_b, ya_t, ya, wts = _combine_fwd(os_, ls_, name=f"{tag}_combine")

    o_r, yb, yb_t, states = _ret_fwd(qk_r, v_r, g_r, tables, name=f"{tag}_ret_fwd")

    pa = _mm(ya_b, W["w_attn_proj"], name=f"{tag}_attn_proj")
    pr = _mm(yb, W["w_ret_proj"], name=f"{tag}_ret_proj")
    merged, merged_t = _merge_fwd(gates, pa, pr, name=f"{tag}_merge")
    mix = _mm(merged, W["w_out"], name=f"{tag}_out_proj")
    h1, x1, x1b, x1_t = _ln_fwd(x, mix, ln["ln1_g"], ln["ln1_b"], name=f"{tag}_ln1")
    uv = _mm(x1b, W["w_gu"], groups=2, name=f"{tag}_ffn_in")
    hh, hh_t = _swiglu_fwd(uv, name=f"{tag}_swiglu")
    f = _mm(hh, W["w_ffn_down"], name=f"{tag}_ffn_down")
    h2, x2, x2b, x2_t = _ln_fwd(x1, f, ln["ln2_g"], ln["ln2_b"], name=f"{tag}_ln2")
    saved = dict(x_t=x_t, qkv_a=qkv_a, qk_r=qk_r, v_r=v_r, g_r=g_r, gates=gates, ls=ls_, ya_t=ya_t, ya=ya,
                 wts=wts, o_r=o_r, yb_t=yb_t, states=states, pa=pa, pr=pr, merged_t=merged_t, h1=h1, x1_t=x1_t,
                 uv=uv, hh_t=hh_t, h2=h2)
    return x2, x2b, x2_t, saved


def _layer_bwd(l, dx2, sv, W, biases, ln, tables):
    S = dx2.shape[0]
    tag = f"l{l}"
    g = {}
    dh2b, res2, g["ln2_g"], g["ln2_b"] = _ln_bwd(dx2, sv["h2"], ln["ln2_g"], name=f"{tag}_ln2_bwd")
    dhh = _mm(dh2b, W["w_ffn_down"], transpose_b=True, name=f"{tag}_d_hh")
    g["w_ffn_down"] = _mm(sv["hh_t"], dh2b, name=f"{tag}_dw_down")
    dudv = _swiglu_bwd(dhh, sv["uv"], name=f"{tag}_swiglu_bwd")
    dx1 = _mm(dudv, W["w_gu"], transpose_b=True, add=res2, name=f"{tag}_d_x1")
    dwgu = _mm(sv["x1_t"], dudv, name=f"{tag}_dw_gu")
    g["w_ffn_gate"], g["w_ffn_up"] = dwgu[:, :D_FF], dwgu[:, D_FF:]

    dh1b, res1, g["ln1_g"], g["ln1_b"] = _ln_bwd(dx1, sv["h1"], ln["ln1_g"], name=f"{tag}_ln1_bwd")
    dmerged = _mm(dh1b, W["w_out"], transpose_b=True, name=f"{tag}_d_merged")
    g["w_out"] = _mm(sv["merged_t"], dh1b, name=f"{tag}_dw_out")
    dpa, dpr, dgates = _merge_bwd(dmerged, sv["gates"], sv["pa"], sv["pr"], name=f"{tag}_merge_bwd")
    dya = _mm(dpa, W["w_attn_proj"], transpose_b=True, groups=1, lane_chunks=True, name=f"{tag}_d_ya")
    g["w_attn_proj"] = _mm(sv["ya_t"], dpa, name=f"{tag}_dw_ap")
    dyb = _mm(dpr, W["w_ret_proj"], transpose_b=True, name=f"{tag}_d_yb")
    g["w_ret_proj"] = _mm(sv["yb_t"], dpr, name=f"{tag}_dw_rp")

    da, dbs = [], []
    for gi, (_, dil) in enumerate(ATTN_GROUPS):
        dqkv, db = _attn_bwd(sv["qkv_a"], biases[gi], sv["ls"][gi], dya, sv["ya"], sv["wts"][gi], gi, dil,
                             name=f"{tag}_attn_bwd{gi}")
        da.append(dqkv)
        dbs.append(db)
    dq_r, dk_r, dv_r, dg_r = _ret_bwd(dyb, sv["qk_r"], sv["v_r"], sv["g_r"], sv["o_r"], sv["states"], tables,
                                 name=f"{tag}_ret_bwd")
    dz, colsum = _assemble_dz(da, dq_r, dk_r, dv_r, dg_r, dgates, name=f"{tag}_assemble_dz")
    g["b_in"] = colsum.reshape(IN_COLS)
    dx = _mm(dz, W["w_in"], transpose_b=True, add=res1, name=f"{tag}_d_x")
    g["w_in"] = _mm(sv["x_t"], dz, name=f"{tag}_dw_in")
    return dx, g, dbs


HBM_SPEC = pl.BlockSpec(memory_space=pltpu.HBM)
OTHER_CHIPS = ((1, 0), (0, 1), (1, 1))


def _flip(v, f):
    return 1 - v if f else v


def _all_gather(shards, *, name):
    n = len(shards)

    def body(*refs):
        x_refs, out_refs = refs[:n], refs[n:2 * n]
        send_sems, recv_sems, local_sems = refs[2 * n:]
        x, y, c = lax.axis_index("x"), lax.axis_index("y"), lax.axis_index("c")
        me, sibling = (x, y, c), (x, y, 1 - c)
        chips = [(_flip(x, fx), _flip(y, fy)) for fx, fy in OTHER_CHIPS]

        def copy(a, k, block, to, src=None):
            px, py, pc = block
            rows = out_refs[a].at[4 * px + 2 * py + pc]
            return pltpu.make_async_remote_copy(
                src_ref=rows if src is None else src, dst_ref=rows,
                send_sem=send_sems.at[7 * a + k], recv_sem=recv_sems.at[7 * a + k], device_id=to, device_id_type=MESH)

        mine, first, passed = [], [], []
        for a in range(n):
            cp = pltpu.make_async_copy(x_refs[a], out_refs[a].at[4 * x + 2 * y + c], local_sems.at[a])
            cp.start()
            mine.append(cp)
            first.append(copy(a, 0, me, sibling, src=x_refs[a]))
            first += [copy(a, 1 + j, me, (*chip, c), src=x_refs[a]) for j, chip in enumerate(chips)]
        for cp in first:
            cp.start()
        for j, chip in enumerate(chips):
            for a in range(n):
                copy(a, 1 + j, (*chip, c), me).wait_recv()
                cp = copy(a, 4 + j, (*chip, c), sibling)
                cp.start()
                passed.append(cp)
        for a in range(n):
            copy(a, 0, sibling, me).wait_recv()
            for j, chip in enumerate(chips):
                copy(a, 4 + j, (*chip, 1 - c), me).wait_recv()
        for cp in first + passed:
            cp.wait_send()
        for cp in mine:
            cp.wait()

    return pl.pallas_call(
        body, name=name, out_shape=[jax.ShapeDtypeStruct((N_DEV,) + s.shape, s.dtype) for s in shards],
        in_specs=[HBM_SPEC] * n, out_specs=[HBM_SPEC] * n,
        scratch_shapes=[pltpu.SemaphoreType.DMA((7 * n,)), pltpu.SemaphoreType.DMA((7 * n,)),
                        pltpu.SemaphoreType.DMA((n,))],
    )(*shards)


def _rs_sibling_exchange(g8s, *, name):
    n = len(g8s)

    def body(*refs):
        g_refs, recv_refs = refs[:n], refs[n:2 * n]
        send_sems, recv_sems = refs[2 * n:]
        x, y, c = lax.axis_index("x"), lax.axis_index("y"), lax.axis_index("c")
        copies = []
        for a in range(n):
            for k in range(4):
                cp = pltpu.make_async_remote_copy(
                    src_ref=g_refs[a].at[k, 1 - c], dst_ref=recv_refs[a].at[k], send_sem=send_sems.at[4 * a + k],
                    recv_sem=recv_sems.at[4 * a + k], device_id=(x, y, 1 - c), device_id_type=MESH)
                cp.start()
                copies.append(cp)
        for cp in copies:
            cp.wait()

    return pl.pallas_call(
        body, name=name,
        out_shape=[jax.ShapeDtypeStruct((4,) + g.shape[2:], g.dtype) for g in g8s],
        in_specs=[HBM_SPEC] * n, out_specs=[HBM_SPEC] * n,
        scratch_shapes=[pltpu.SemaphoreType.DMA((4 * n,)), pltpu.SemaphoreType.DMA((4 * n,))],
    )(*g8s)


def _rs_chip_sum(g8, recv, core, *, name):
    _, _, R, Wd = g8.shape
    tr = _div_tile(R, 256, 16)

    def body(core_ref, g_ref, r_ref, o_ref):
        o_ref[...] = (g_ref[...] + r_ref[...]).astype(BF16)

    grid_spec = pltpu.PrefetchScalarGridSpec(
        num_scalar_prefetch=1, grid=(4, R // tr),
        in_specs=[pl.BlockSpec((None, None, tr, Wd), lambda k, i, core_ref: (k, core_ref[0], i, 0)),
                  pl.BlockSpec((None, tr, Wd), lambda k, i, core_ref: (k, i, 0))],
        out_specs=pl.BlockSpec((None, tr, Wd), lambda k, i, core_ref: (k, i, 0)))
    return pl.pallas_call(
        body, name=name, grid_spec=grid_spec, out_shape=jax.ShapeDtypeStruct((4, R, Wd), BF16),
        compiler_params=_cparams(("parallel", "parallel")))(core, g8, recv)


def _rs_chip_exchange(ps, *, name):
    n = len(ps)

    def body(*refs):
        p_refs, out_refs = refs[:n], refs[n:2 * n]
        send_sems, recv_sems, local_sems = refs[2 * n:]
        x, y, c = lax.axis_index("x"), lax.axis_index("y"), lax.axis_index("c")
        my_chip = 2 * x + y
        copies = []
        for a in range(n):
            mine = pltpu.make_async_copy(p_refs[a].at[my_chip], out_refs[a].at[my_chip], local_sems.at[a])
            mine.start()
            copies.append(mine)
            for j, (fx, fy) in enumerate(OTHER_CHIPS):
                px, py = _flip(x, fx), _flip(y, fy)
                cp = pltpu.make_async_remote_copy(
                    src_ref=p_refs[a].at[2 * px + py], dst_ref=out_refs[a].at[my_chip],
                    send_sem=send_sems.at[3 * a + j], recv_sem=recv_sems.at[3 * a + j],
                    device_id=(px, py, c), device_id_type=MESH)
                cp.start()
                copies.append(cp)
        for cp in copies:
            cp.wait()

    return pl.pallas_call(
        body, name=name, out_shape=[jax.ShapeDtypeStruct(p.shape, p.dtype) for p in ps],
        in_specs=[HBM_SPEC] * n, out_specs=[HBM_SPEC] * n,
        scratch_shapes=[pltpu.SemaphoreType.DMA((3 * n,)), pltpu.SemaphoreType.DMA((3 * n,)),
                        pltpu.SemaphoreType.DMA((n,))],
    )(*ps)


def _all_reduce_small(v, *, name):
    R, Wd = v.shape

    def body(v_ref, out_ref, slots, send_sems, recv_sems):
        x, y, c = lax.axis_index("x"), lax.axis_index("y"), lax.axis_index("c")
        me = 4 * x + 2 * y + c
        slots[me] = v_ref[...]
        copies = []
        for rel in range(1, N_DEV):
            peer = (_flip(x, rel & 4), _flip(y, rel & 2), _flip(c, rel & 1))
            cp = pltpu.make_async_remote_copy(
                src_ref=v_ref, dst_ref=slots.at[me], send_sem=send_sems.at[rel - 1],
                recv_sem=recv_sems.at[rel - 1], device_id=peer, device_id_type=MESH)
            cp.start()
            copies.append(cp)
        for cp in copies:
            cp.wait()
        acc = slots[0]
        for j in range(1, N_DEV):
            acc = acc + slots[j]
        out_ref[...] = acc

    vm = pl.BlockSpec(memory_space=pltpu.VMEM)
    return pl.pallas_call(
        body, name=name, out_shape=jax.ShapeDtypeStruct((R, Wd), F32),
        in_specs=[vm], out_specs=vm,
        scratch_shapes=[pltpu.VMEM((N_DEV, R, Wd), F32), pltpu.SemaphoreType.DMA((7,)),
                        pltpu.SemaphoreType.DMA((7,))],
    )(v)


def _adam_math(w, g, m, v):
    m2 = ADAM_B1 * m + (1.0 - ADAM_B1) * g
    v2 = ADAM_B2 * v + (1.0 - ADAM_B2) * (g * g)
    m_hat = m2 / (1.0 - ADAM_B1 ** ADAM_STEP)
    v_hat = v2 / (1.0 - ADAM_B2 ** ADAM_STEP)
    delta = -ADAM_LR * (m_hat / (jnp.sqrt(v_hat) + ADAM_EPS) + ADAM_WD * w)
    return delta, m2, v2


def _adam_sharded(parts, w, m, v, *, name):
    _, R, Wd = w.shape
    tr = _div_tile(R, 256, 16)

    def body(p0_ref, p1_ref, w_ref, m_ref, v_ref, g_ref, d_ref, m2_ref, v2_ref):
        def chip_sum(p_ref):
            g = p_ref[0].astype(F32)
            for s in range(1, 4):
                g = g + p_ref[s].astype(F32)
            return g

        g = jnp.where(pl.program_id(0) == 0, chip_sum(p0_ref), chip_sum(p1_ref))
        delta, m2, v2 = _adam_math(w_ref[...], g, m_ref[...], v_ref[...])
        g_ref[...] = g
        d_ref[...] = delta
        m2_ref[...] = m2
        v2_ref[...] = v2

    assert DEPTH == 2
    p_spec = pl.BlockSpec((4, tr, Wd), lambda l, i: (0, i, 0))
    s_spec = pl.BlockSpec((None, tr, Wd), lambda l, i: (l, i, 0))
    return pl.pallas_call(
        body, name=name, grid=(DEPTH, R // tr),
        in_specs=[p_spec, p_spec, s_spec, s_spec, s_spec],
        out_specs=[s_spec] * 4, out_shape=[jax.ShapeDtypeStruct((DEPTH, R, Wd), F32)] * 4,
        compiler_params=_cparams(("parallel", "parallel")))(parts[0], parts[1], w, m, v)


def _adam_small(g, w, m, v, *, name):
    R, Wd = w.shape

    def body(g_ref, w_ref, m_ref, v_ref, d_ref, m2_ref, v2_ref):
        delta, m2, v2 = _adam_math(w_ref[...], g_ref[...], m_ref[...], v_ref[...])
        d_ref[...] = delta
        m2_ref[...] = m2
        v2_ref[...] = v2

    return pl.pallas_call(
        body, name=name, out_shape=[jax.ShapeDtypeStruct((R, Wd), F32)] * 3,
        compiler_params=_cparams())(g, w, m, v)


def _shard_shape(name):
    r, c = FULL_SHAPE[name]
    return (r, c // N_DEV) if name in COL_SHARDED else (r // N_DEV, c)


def _full_from_gathered(name, g):
    if name in COL_SHARDED:
        return jnp.transpose(g, (1, 0, 2)).reshape(FULL_SHAPE[name])
    return g.reshape(FULL_SHAPE[name])


def _dest_major(name, gfull):
    r, c = _shard_shape(name)
    if name in COL_SHARDED:
        blk = jnp.transpose(gfull.reshape(r, N_DEV, c), (1, 0, 2))
    else:
        blk = gfull.reshape(N_DEV, r, c)
    return blk.reshape(4, 2, r, c)


def _pack_small(t):
    flat = jnp.concatenate([t[n].reshape(-1).astype(F32) for n in SMALL_WEIGHTS])
    return jnp.pad(flat, (0, SMALL_ROWS * LANES - flat.shape[0])).reshape(SMALL_ROWS, LANES)


def _unpack_small(packed):
    flat = packed.reshape(-1)
    out, off = {}, 0
    for n in SMALL_WEIGHTS:
        size = math.prod(SMALL_SHAPE[n])
        out[n] = flat[off:off + size].reshape(SMALL_SHAPE[n])
        off += size
    return out


def _local_step(x, target, rel_bias, b_in, lns, layers_w):
    S = x.shape[0]
    tables = _ret_tables(S)
    biases = [_attn_bias(rel_bias, gi, dil) for gi, (_, dil) in enumerate(ATTN_GROUPS)]
    Ws = []
    for W in layers_w:
        W = dict(W)
        W["w_gu"] = jnp.concatenate([W["w_ffn_gate"], W["w_ffn_up"]], axis=1)
        Ws.append(W)

    h = x
    hb, h_t = _cast_transpose(x, name="cast_x")
    saved = []
    for l in range(DEPTH):
        h, hb, h_t, sv = _layer_fwd(l, h, hb, h_t, Ws[l], b_in[l], biases, lns[l], tables)
        saved.append(sv)
    dy, sq = _loss_fwd_bwd(h, target, name="loss")
    loss_local = 0.5 * sq[0, 0] / D_MODEL

    grads = [None] * DEPTH
    db_tot = None
    dx = dy
    for l in reversed(range(DEPTH)):
        dx, g, dbs = _layer_bwd(l, dx, saved[l], Ws[l], biases, lns[l], tables)
        grads[l] = g
        db_tot = dbs if db_tot is None else [a + b for a, b in zip(db_tot, dbs)]
    small = {"rel_bias": _bias_grad(db_tot, name="bias_grad"),
             "b_in": jnp.stack([grads[l]["b_in"] for l in range(DEPTH)])}
    for n in ("ln1_g", "ln1_b", "ln2_g", "ln2_b"):
        small[n] = jnp.stack([grads[l][n].reshape(D_MODEL) for l in range(DEPTH)])
    return loss_local, dx, grads, small


def kernel(x, rel_bias, w_in, b_in, w_attn_proj, w_ret_proj, w_out, ln1_g, ln1_b, w_ffn_gate, w_ffn_up, w_ffn_down, ln2_g, ln2_b, loss_target, m_rel_bias, m_w_in, m_b_in, m_w_attn_proj, m_w_ret_proj, m_w_out, m_ln1_g, m_ln1_b, m_w_ffn_gate, m_w_ffn_up, m_w_ffn_down, m_ln2_g, m_ln2_b, v_rel_bias, v_w_in, v_b_in, v_w_attn_proj, v_w_ret_proj, v_w_out, v_ln1_g, v_ln1_b, v_w_ffn_gate, v_w_ffn_up, v_w_ffn_down, v_ln2_g, v_ln2_b):
    w = dict(rel_bias=rel_bias, w_in=w_in, b_in=b_in, w_attn_proj=w_attn_proj, w_ret_proj=w_ret_proj, w_out=w_out,
             ln1_g=ln1_g, ln1_b=ln1_b, w_ffn_gate=w_ffn_gate, w_ffn_up=w_ffn_up, w_ffn_down=w_ffn_down,
             ln2_g=ln2_g, ln2_b=ln2_b)
    m = dict(rel_bias=m_rel_bias, w_in=m_w_in, b_in=m_b_in, w_attn_proj=m_w_attn_proj, w_ret_proj=m_w_ret_proj,
             w_out=m_w_out, ln1_g=m_ln1_g, ln1_b=m_ln1_b, w_ffn_gate=m_w_ffn_gate, w_ffn_up=m_w_ffn_up,
             w_ffn_down=m_w_ffn_down, ln2_g=m_ln2_g, ln2_b=m_ln2_b)
    v = dict(rel_bias=v_rel_bias, w_in=v_w_in, b_in=v_b_in, w_attn_proj=v_w_attn_proj, w_ret_proj=v_w_ret_proj,
             w_out=v_w_out, ln1_g=v_ln1_g, ln1_b=v_ln1_b, w_ffn_gate=v_w_ffn_gate, w_ffn_up=v_w_ffn_up,
             w_ffn_down=v_w_ffn_down, ln2_g=v_ln2_g, ln2_b=v_ln2_b)

    layers_w = []
    for l in range(DEPTH):
        gathered = _all_gather([w[n][l].astype(BF16) for n in BIG_WEIGHTS], name=f"all_gather_l{l}")
        layers_w.append({n: _full_from_gathered(n, g) for n, g in zip(BIG_WEIGHTS, gathered)})
    lns = [{n: w[n][l] for n in ("ln1_g", "ln1_b", "ln2_g", "ln2_b")} for l in range(DEPTH)]

    loss_local, grad_x, grads, small = _local_step(x[0], loss_target[0], rel_bias, b_in, lns, layers_w)
    loss = lax.psum(loss_local, ("x", "y", "c"))

    core = lax.axis_index("c").astype(jnp.int32).reshape(1)
    parts = []
    for l in range(DEPTH):
        g8 = [_dest_major(n, grads[l][n]) for n in BIG_WEIGHTS]
        from_sibling = _rs_sibling_exchange(g8, name=f"rs_sibling_exchange_l{l}")
        chip_parts = [_rs_chip_sum(a, b, core, name=f"rs_chip_sum_l{l}_{n}")
                      for n, a, b in zip(BIG_WEIGHTS, g8, from_sibling)]
        parts.append(_rs_chip_exchange(chip_parts, name=f"rs_chip_exchange_l{l}"))
    big = [{} for _ in range(4)]
    for i, n in enumerate(BIG_WEIGHTS):
        res = _adam_sharded([parts[l][i] for l in range(DEPTH)], w[n], m[n], v[n], name=f"adam_{n}")
        for kind in range(4):
            big[kind][n] = res[kind]

    gs = _all_reduce_small(_pack_small(small), name="all_reduce_small")
    ds, ms, vs = _adam_small(gs, _pack_small(w), _pack_small(m), _pack_small(v), name="adam_small")
    sm = [_unpack_small(t) for t in (gs, ds, ms, vs)]

    outs = [loss, grad_x[None]]
    for kind in range(4):
        for n in ALL_WEIGHTS:
            outs.append(big[kind][n] if n in BIG_WEIGHTS else sm[kind][n])
    return tuple(outs)
```

```python
import functools
import math

import numpy as np
import jax
import jax.numpy as jnp
from jax import lax
from jax.experimental import pallas as pl
from jax.experimental.pallas import tpu as pltpu

F32 = jnp.float32
BF16 = jnp.bfloat16
MESH = pl.DeviceIdType.MESH

D_MODEL = 1024
DEPTH = 2
HEAD_DIM = 64
ATTN_GROUPS = ((128, 1), (512, 4), (2048, 16))
HEADS_PER_GROUP = 6
GROUP_WIDTH = HEADS_PER_GROUP * HEAD_DIM
ATTN_BLOCK = 128
NUM_BUCKETS = 32
MAX_DISTANCE = 2048
RET_HEADS = 4
RET_QK_DIM = 256
RET_V_DIM = 512
RET_CHUNK = 128
RET_T_CHUNKS = 4
ROPE_BASE = 10000.0
D_FF = 2816
ALPHA = (2 * DEPTH) ** 0.25
LN_EPS = 1e-5
GN_EPS = 1e-5
ATTN_W = 3 * GROUP_WIDTH
IN_COLS = 3 * ATTN_W + 2 * 1024 + 2 * 2048 + 2 * 1024
ADAM_LR, ADAM_B1, ADAM_B2, ADAM_EPS, ADAM_WD, ADAM_STEP = 0.001, 0.9, 0.999, 1e-08, 0.01, 10
N_DEV = 8
NEG = -1e30
LANES = 128
VMEM_LIMIT = 56 * 1024 * 1024
MM_TILE_CAP = 1664

BIG_WEIGHTS = ("w_in", "w_attn_proj", "w_ret_proj", "w_out", "w_ffn_gate", "w_ffn_up", "w_ffn_down")
COL_SHARDED = ("w_in", "w_attn_proj", "w_ffn_gate", "w_ffn_up")
FULL_SHAPE = {"w_in": (D_MODEL, IN_COLS), "w_attn_proj": (GROUP_WIDTH, D_MODEL), "w_ret_proj": (2048, D_MODEL),
              "w_out": (D_MODEL, D_MODEL), "w_ffn_gate": (D_MODEL, D_FF), "w_ffn_up": (D_MODEL, D_FF),
              "w_ffn_down": (D_FF, D_MODEL)}
SMALL_WEIGHTS = ("rel_bias", "b_in", "ln1_g", "ln1_b", "ln2_g", "ln2_b")
SMALL_SHAPE = {"rel_bias": (NUM_BUCKETS, 18), "b_in": (DEPTH, IN_COLS), "ln1_g": (DEPTH, D_MODEL),
               "ln1_b": (DEPTH, D_MODEL), "ln2_g": (DEPTH, D_MODEL), "ln2_b": (DEPTH, D_MODEL)}
SMALL_ROWS = 256
ALL_WEIGHTS = ("rel_bias", "w_in", "b_in", "w_attn_proj", "w_ret_proj", "w_out", "ln1_g", "ln1_b",
               "w_ffn_gate", "w_ffn_up", "w_ffn_down", "ln2_g", "ln2_b")


def _cparams(sem=None):
    return pltpu.CompilerParams(dimension_semantics=sem, vmem_limit_bytes=VMEM_LIMIT)


def _div_tile(n, cap, unit):
    if n <= cap:
        return n
    best = None
    for t in range(unit, cap + 1, unit):
        if n % t == 0:
            best = t
    assert best is not None, (n, cap, unit)
    return best


def _mm(a, b, *, name, out_dtype=F32, bias=None, add=None, groups=None, lane_chunks=False, transpose_b=False):
    M, K = a.shape
    N, K2 = b.shape if transpose_b else b.shape[::-1]
    assert K == K2 and a.dtype == BF16 and b.dtype == BF16
    tm = _div_tile(M, 1024, 16)
    tn = N // groups if groups else _div_tile(N, MM_TILE_CAP, LANES)
    tk = _div_tile(K, MM_TILE_CAP, LANES)
    nk = K // tk
    nch = tn // LANES
    has_bias, has_add = bias is not None, add is not None

    def body(*refs):
        a_ref, b_ref = refs[0], refs[1]
        pos = 2
        bias_ref = add_ref = None
        if has_bias:
            bias_ref = refs[pos]
            pos += 1
        if has_add:
            add_ref = refs[pos]
            pos += 1
        o_ref = refs[pos]

        def finish(r):
            if has_bias:
                r = r + bias_ref[...]
            if has_add:
                r = r + add_ref[...]
            if lane_chunks:
                for c in range(nch):
                    o_ref[c] = r[:, c * LANES:(c + 1) * LANES].astype(o_ref.dtype)
            else:
                o_ref[...] = r.astype(o_ref.dtype)

        def product():
            if transpose_b:
                return lax.dot_general(a_ref[...], b_ref[...], (((1,), (1,)), ((), ())), preferred_element_type=F32)
            return jnp.dot(a_ref[...], b_ref[...], preferred_element_type=F32)

        if nk == 1:
            finish(product())
        else:
            acc_ref = refs[pos + 1]
            k = pl.program_id(2)

            @pl.when(k == 0)
            def _():
                acc_ref[...] = jnp.zeros_like(acc_ref)

            acc_ref[...] += product()

            @pl.when(k == nk - 1)
            def _():
                finish(acc_ref[...])

    in_specs = [pl.BlockSpec((tm, tk), lambda i, j, k: (i, k)),
                pl.BlockSpec((tn, tk), lambda i, j, k: (j, k)) if transpose_b
                else pl.BlockSpec((tk, tn), lambda i, j, k: (k, j))]
    args = [a, b]
    if has_bias:
        in_specs.append(pl.BlockSpec((1, tn), lambda i, j, k: (0, j)))
        args.append(bias.reshape(1, N).astype(F32))
    if has_add:
        in_specs.append(pl.BlockSpec((tm, tn), lambda i, j, k: (i, j)))
        args.append(add)
    if lane_chunks:
        assert groups
        out_shape = jax.ShapeDtypeStruct((groups, nch, M, LANES), out_dtype)
        out_spec = pl.BlockSpec((None, nch, tm, LANES), lambda i, j, k: (j, 0, i, 0))
    elif groups:
        out_shape = jax.ShapeDtypeStruct((groups, M, tn), out_dtype)
        out_spec = pl.BlockSpec((None, tm, tn), lambda i, j, k: (j, i, 0))
    else:
        out_shape = jax.ShapeDtypeStruct((M, N), out_dtype)
        out_spec = pl.BlockSpec((tm, tn), lambda i, j, k: (i, j))
    scratch = [pltpu.VMEM((tm, tn), F32)] if nk > 1 else []
    out = pl.pallas_call(
        body, name=name, grid=(M // tm, N // tn, nk), in_specs=in_specs, out_specs=out_spec,
        out_shape=out_shape, scratch_shapes=scratch,
        compiler_params=_cparams(("parallel", "parallel", "arbitrary")))(*args)
    return out.reshape(groups * nch, M, LANES) if lane_chunks else out


def _row_spec(tr, w):
    return pl.BlockSpec((tr, w), lambda i: (i, 0))


def _vec_spec(w):
    return pl.BlockSpec((1, w), lambda i: (0, 0))


def _col_spec(w, tr):
    return pl.BlockSpec((w, tr), lambda i: (0, i))


def _cast_transpose(x, *, name):
    S, W = x.shape
    tr = 512

    def body(x_ref, o_ref, ot_ref):
        v = x_ref[...]
        o_ref[...] = v.astype(BF16)
        ot_ref[...] = v.T.astype(BF16)

    return pl.pallas_call(
        body, name=name, grid=(S // tr,), in_specs=[_row_spec(tr, W)],
        out_specs=[_row_spec(tr, W), _col_spec(W, tr)],
        out_shape=[jax.ShapeDtypeStruct((S, W), BF16), jax.ShapeDtypeStruct((W, S), BF16)],
        compiler_params=_cparams(("parallel",)))(x)


def _ln_fwd(x, sub, g, b, *, name):
    S, W = x.shape
    tr = 512

    def body(x_ref, s_ref, g_ref, b_ref, h_ref, y_ref, yb_ref, ybt_ref):
        h = ALPHA * x_ref[...] + s_ref[...]
        mu = jnp.mean(h, axis=-1, keepdims=True)
        d = h - mu
        var = jnp.mean(d * d, axis=-1, keepdims=True)
        y = d * lax.rsqrt(var + LN_EPS) * g_ref[...] + b_ref[...]
        h_ref[...] = h
        y_ref[...] = y
        yb_ref[...] = y.astype(BF16)
        ybt_ref[...] = y.T.astype(BF16)

    return pl.pallas_call(
        body, name=name, grid=(S // tr,),
        in_specs=[_row_spec(tr, W), _row_spec(tr, W), _vec_spec(W), _vec_spec(W)],
        out_specs=[_row_spec(tr, W)] * 3 + [_col_spec(W, tr)],
        out_shape=[jax.ShapeDtypeStruct((S, W), F32), jax.ShapeDtypeStruct((S, W), F32),
                   jax.ShapeDtypeStruct((S, W), BF16), jax.ShapeDtypeStruct((W, S), BF16)],
        compiler_params=_cparams(("parallel",)))(x, sub, g.reshape(1, W), b.reshape(1, W))


def _ln_bwd(dy, h, g, *, name):
    S, W = dy.shape
    tr = 512

    def body(dy_ref, h_ref, g_ref, dhb_ref, res_ref, dg_ref, db_ref):
        @pl.when(pl.program_id(0) == 0)
        def _():
            dg_ref[...] = jnp.zeros_like(dg_ref)
            db_ref[...] = jnp.zeros_like(db_ref)

        hh = h_ref[...]
        mu = jnp.mean(hh, axis=-1, keepdims=True)
        d = hh - mu
        var = jnp.mean(d * d, axis=-1, keepdims=True)
        rstd = lax.rsqrt(var + LN_EPS)
        xhat = d * rstd
        dyv = dy_ref[...]
        dg_ref[...] += jnp.sum(dyv * xhat, axis=0, keepdims=True)
        db_ref[...] += jnp.sum(dyv, axis=0, keepdims=True)
        dxh = dyv * g_ref[...]
        dh = rstd * (dxh - jnp.mean(dxh, axis=-1, keepdims=True)
                     - xhat * jnp.mean(dxh * xhat, axis=-1, keepdims=True))
        dhb_ref[...] = dh.astype(BF16)
        res_ref[...] = ALPHA * dh

    return pl.pallas_call(
        body, name=name, grid=(S // tr,),
        in_specs=[_row_spec(tr, W), _row_spec(tr, W), _vec_spec(W)],
        out_specs=[_row_spec(tr, W), _row_spec(tr, W), _vec_spec(W), _vec_spec(W)],
        out_shape=[jax.ShapeDtypeStruct((S, W), BF16), jax.ShapeDtypeStruct((S, W), F32),
                   jax.ShapeDtypeStruct((1, W), F32), jax.ShapeDtypeStruct((1, W), F32)],
        compiler_params=_cparams(("arbitrary",)))(dy, h, g.reshape(1, W))


def _loss_fwd_bwd(y, target, *, name):
    S, W = y.shape
    tr = 512

    def body(y_ref, t_ref, dy_ref, acc_ref):
        @pl.when(pl.program_id(0) == 0)
        def _():
            acc_ref[...] = jnp.zeros_like(acc_ref)

        e = y_ref[...] - t_ref[...]
        acc_ref[...] += jnp.sum(jnp.sum(e * e, axis=-1, keepdims=True), axis=0, keepdims=True)
        dy_ref[...] = e * (1.0 / W)

    return pl.pallas_call(
        body, name=name, grid=(S // tr,),
        in_specs=[_row_spec(tr, W), _row_spec(tr, W)],
        out_specs=[_row_spec(tr, W), pl.BlockSpec((1, 1), lambda i: (0, 0))],
        out_shape=[jax.ShapeDtypeStruct((S, W), F32), jax.ShapeDtypeStruct((1, 1), F32)],
        compiler_params=_cparams(("arbitrary",)))(y, target)


def _combine_fwd(os_, ls_, *, name):
    NCH, S, _ = os_[0].shape
    W = NCH * LANES
    tr = 512

    def body(o0, o1, o2, l0, l1, l2, yb_ref, ybt_ref, y_ref, w0_ref, w1_ref, w2_ref):
        for c in range(NCH):
            la, lb, lc = l0[c], l1[c], l2[c]
            m = jnp.maximum(jnp.maximum(la, lb), lc)
            ea, eb, ec = jnp.exp(la - m), jnp.exp(lb - m), jnp.exp(lc - m)
            inv = 1.0 / (ea + eb + ec)
            wa, wb, wc = ea * inv, eb * inv, ec * inv
            y = wa * o0[c] + wb * o1[c] + wc * o2[c]
            y_ref[c] = y
            yb_ref[:, c * LANES:(c + 1) * LANES] = y.astype(BF16)
            ybt_ref[c * LANES:(c + 1) * LANES, :] = y.T.astype(BF16)
            w0_ref[c] = wa
            w1_ref[c] = wb
            w2_ref[c] = wc

    ch = pl.BlockSpec((NCH, tr, LANES), lambda i: (0, i, 0))
    yb, ybt, y, w0, w1, w2 = pl.pallas_call(
        body, name=name, grid=(S // tr,),
        in_specs=[ch] * 6,
        out_specs=[_row_spec(tr, W), _col_spec(W, tr)] + [ch] * 4,
        out_shape=[jax.ShapeDtypeStruct((S, W), BF16), jax.ShapeDtypeStruct((W, S), BF16)]
        + [jax.ShapeDtypeStruct((NCH, S, LANES), F32)] * 4,
        compiler_params=_cparams(("parallel",)))(*os_, *ls_)
    return yb, ybt, y, (w0, w1, w2)


def _merge_fwd(gates, pa, pr, *, name):
    S, W = pa.shape
    tr = 512

    def body(g_ref, pa_ref, pr_ref, o_ref, ot_ref):
        m = jax.nn.sigmoid(g_ref[0]) * pa_ref[...] + jax.nn.sigmoid(g_ref[1]) * pr_ref[...]
        o_ref[...] = m.astype(BF16)
        ot_ref[...] = m.T.astype(BF16)

    return pl.pallas_call(
        body, name=name, grid=(S // tr,),
        in_specs=[pl.BlockSpec((2, tr, W), lambda i: (0, i, 0)), _row_spec(tr, W), _row_spec(tr, W)],
        out_specs=[_row_spec(tr, W), _col_spec(W, tr)],
        out_shape=[jax.ShapeDtypeStruct((S, W), BF16), jax.ShapeDtypeStruct((W, S), BF16)],
        compiler_params=_cparams(("parallel",)))(gates, pa, pr)


def _merge_bwd(dm, gates, pa, pr, *, name):
    S, W = pa.shape
    tr = 256

    def body(dm_ref, g_ref, pa_ref, pr_ref, dpa_ref, dpr_ref, dg_ref):
        dmv = dm_ref[...]
        sa, sb = jax.nn.sigmoid(g_ref[0]), jax.nn.sigmoid(g_ref[1])
        dpa_ref[...] = (dmv * sa).astype(BF16)
        dpr_ref[...] = (dmv * sb).astype(BF16)
        dg_ref[0] = (dmv * pa_ref[...] * (sa * (1.0 - sa))).astype(BF16)
        dg_ref[1] = (dmv * pr_ref[...] * (sb * (1.0 - sb))).astype(BF16)

    g3 = pl.BlockSpec((2, tr, W), lambda i: (0, i, 0))
    return pl.pallas_call(
        body, name=name, grid=(S // tr,),
        in_specs=[_row_spec(tr, W), g3, _row_spec(tr, W), _row_spec(tr, W)],
        out_specs=[_row_spec(tr, W), _row_spec(tr, W), g3],
        out_shape=[jax.ShapeDtypeStruct((S, W), BF16), jax.ShapeDtypeStruct((S, W), BF16),
                   jax.ShapeDtypeStruct((2, S, W), BF16)],
        compiler_params=_cparams(("parallel",)))(dm, gates, pa, pr)


def _swiglu_fwd(uv, *, name):
    _, S, W = uv.shape
    tr = 256

    def body(uv_ref, o_ref, ot_ref):
        u = uv_ref[0]
        hh = u * jax.nn.sigmoid(u) * uv_ref[1]
        o_ref[...] = hh.astype(BF16)
        ot_ref[...] = hh.T.astype(BF16)

    return pl.pallas_call(
        body, name=name, grid=(S // tr,),
        in_specs=[pl.BlockSpec((2, tr, W), lambda i: (0, i, 0))],
        out_specs=[_row_spec(tr, W), _col_spec(W, tr)],
        out_shape=[jax.ShapeDtypeStruct((S, W), BF16), jax.ShapeDtypeStruct((W, S), BF16)],
        compiler_params=_cparams(("parallel",)))(uv)


def _swiglu_bwd(dh, uv, *, name):
    _, S, W = uv.shape
    tr = 256

    def body(dh_ref, uv_ref, o_ref):
        u, v, d = uv_ref[0], uv_ref[1], dh_ref[...]
        sg = jax.nn.sigmoid(u)
        o_ref[:, 0:W] = (d * v * (sg * (1.0 + u * (1.0 - sg)))).astype(BF16)
        o_ref[:, W:2 * W] = (d * (u * sg)).astype(BF16)

    return pl.pallas_call(
        body, name=name, grid=(S // tr,),
        in_specs=[_row_spec(tr, W), pl.BlockSpec((2, tr, W), lambda i: (0, i, 0))],
        out_specs=_row_spec(tr, 2 * W), out_shape=jax.ShapeDtypeStruct((S, 2 * W), BF16),
        compiler_params=_cparams(("parallel",)))(dh, uv)


def _assemble_dz(da, dq_r, dk_r, dv_r, dg_r, dgates, *, name):
    S = dv_r.shape[0]
    tr = 256
    GW = GROUP_WIDTH
    NCH = GW // LANES

    def body(*refs):
        a_refs = refs[0:9]
        q_ref, k_ref, v_ref, g_ref, gt_ref, dz_ref, cs_ref = refs[9:]

        @pl.when(pl.program_id(0) == 0)
        def _():
            cs_ref[...] = jnp.zeros_like(cs_ref)

        def put(off, val):
            w = val.shape[-1]
            dz_ref[:, off:off + w] = val.astype(BF16)
            cs_ref[:, off:off + w] += jnp.sum(val.astype(F32), axis=0, keepdims=True)

        for which in range(3):
            for gi in range(3):
                for c in range(NCH):
                    put(which * ATTN_W + gi * GW + c * LANES, a_refs[3 * gi + which][c])
        off = 3 * ATTN_W
        put(off, q_ref[...])
        put(off + 1024, k_ref[...])
        put(off + 2048, v_ref[...])
        put(off + 4096, g_ref[...])
        put(off + 6144, gt_ref[0])
        put(off + 7168, gt_ref[1])

    flat_a = [t for grp in da for t in grp]
    return pl.pallas_call(
        body, name=name, grid=(S // tr,),
        in_specs=[pl.BlockSpec((NCH, tr, LANES), lambda i: (0, i, 0))] * 9 + [_row_spec(tr, 1024), _row_spec(tr, 1024),
                  _row_spec(tr, 2048), _row_spec(tr, 2048), pl.BlockSpec((2, tr, 1024), lambda i: (0, i, 0))],
        out_specs=[_row_spec(tr, IN_COLS), _vec_spec(IN_COLS)],
        out_shape=[jax.ShapeDtypeStruct((S, IN_COLS), BF16), jax.ShapeDtypeStruct((1, IN_COLS), F32)],
        compiler_params=_cparams(("arbitrary",)))(*flat_a, dq_r, dk_r, dv_r, dg_r, dgates)


def _t5_bucket(dist):
    max_exact = NUM_BUCKETS // 2
    large = max_exact + (np.log(np.maximum(dist, max_exact) / max_exact)
                         / np.log(MAX_DISTANCE / max_exact) * (NUM_BUCKETS - max_exact)).astype(np.int32)
    large = np.minimum(large, NUM_BUCKETS - 1)
    return np.where(dist < max_exact, dist, large).astype(np.int32)


def _attn_tables(dilation):
    W = ATTN_BLOCK
    qi = np.arange(W)[:, None]
    kj = np.arange(2 * W)[None, :]
    rel = qi + W - kj
    valid = (rel >= 0) & (rel <= W)
    buckets = _t5_bucket(np.clip(rel, 0, W) * dilation)
    return buckets, valid


def _attn_bias(rel_bias, gi, dilation):
    buckets, valid = _attn_tables(dilation)
    table = rel_bias[:, gi * HEADS_PER_GROUP:(gi + 1) * HEADS_PER_GROUP]
    onehot = (jnp.asarray(buckets.reshape(-1, 1)) == jnp.arange(NUM_BUCKETS)[None, :]).astype(F32)
    bias = jnp.dot(onehot, table.astype(F32), precision=lax.Precision.HIGHEST)
    bias = bias.T.reshape(HEADS_PER_GROUP, ATTN_BLOCK, 2 * ATTN_BLOCK)
    return jnp.where(jnp.asarray(valid)[None], bias, NEG)


def _dot_nt(a, b):
    return lax.dot_general(a, b, (((1,), (1,)), ((), ())), preferred_element_type=F32)


def _dot_tn(a, b):
    return lax.dot_general(a, b, (((0,), (0,)), ((), ())), preferred_element_type=F32)


def _dot(a, b):
    return jnp.dot(a, b, preferred_element_type=F32)


ATTN_RESIDUES_PER_STEP = 4
ATTN_UNITS_AT_ONCE = 8
HEADS_PER_CHUNK = LANES // HEAD_DIM
N_CHUNKS = GROUP_WIDTH // LANES


def _first_block_mask(has_prev):
    col = lax.broadcasted_iota(jnp.int32, (1, 2 * ATTN_BLOCK), 1)
    return jnp.where(jnp.logical_or(has_prev, col >= ATTN_BLOCK), 0.0, NEG).astype(F32)


def _head_lanes(hh):
    return slice(HEAD_DIM * hh, HEAD_DIM * (hh + 1))


def _attn_geometry(S, d):
    rows_per_block = ATTN_BLOCK * d
    rps = min(d, ATTN_RESIDUES_PER_STEP)
    return rows_per_block, S // rows_per_block, rps, d // rps


def _residue_rows(d, rps, rg, rr):
    if d == 1:
        return slice(None)
    return pl.ds(rg * rps + rr, ATTN_BLOCK, stride=d)


def _attn_in_specs(gi, RB, last):
    def spec(which, prev):
        if prev:
            return pl.BlockSpec((None, RB, LANES),
                                lambda j, n, rg: (9 * which + 3 * gi + j, jnp.clip(n - 1, 0, last), 0))
        return pl.BlockSpec((None, RB, LANES), lambda j, n, rg: (9 * which + 3 * gi + j, jnp.minimum(n, last), 0))
    bias = pl.BlockSpec((HEADS_PER_CHUNK, ATTN_BLOCK, 2 * ATTN_BLOCK), lambda j, n, rg: (j, 0, 0))
    return [spec(0, False), spec(1, True), spec(1, False), spec(2, True), spec(2, False), bias]


def _attn_fwd(qkv, bias, gi, d, *, name):
    _, S, _ = qkv.shape
    B = ATTN_BLOCK
    RB, nb, rps, nrg = _attn_geometry(S, d)
    scale = HEAD_DIM ** -0.5
    units = [(rr, hh) for rr in range(rps) for hh in range(HEADS_PER_CHUNK)]

    def body(q_ref, kp_ref, kc_ref, vp_ref, vc_ref, b_ref, o_ref, l_ref):
        n, rg = pl.program_id(1), pl.program_id(2)
        edge = _first_block_mask(n > 0)
        rows = [_residue_rows(d, rps, rg, rr) for rr in range(rps)]
        q = [q_ref[r_, :].astype(BF16) for r_ in rows]
        k2 = [jnp.concatenate([kp_ref[r_, :], kc_ref[r_, :]], axis=0).astype(BF16) for r_ in rows]
        v2 = [jnp.concatenate([vp_ref[r_, :], vc_ref[r_, :]], axis=0).astype(BF16) for r_ in rows]
        o_part, l_part = {}, {}
        for u0 in range(0, len(units), ATTN_UNITS_AT_ONCE):
            us = units[u0:u0 + ATTN_UNITS_AT_ONCE]
            s = [_dot_nt(q[rr][:, _head_lanes(hh)], k2[rr][:, _head_lanes(hh)]) * scale + b_ref[hh] + edge
                 for rr, hh in us]
            m = [jnp.max(x, axis=-1, keepdims=True) for x in s]
            p = [jnp.exp(x - mm) for x, mm in zip(s, m)]
            l = [jnp.sum(x, axis=-1, keepdims=True) for x in p]
            pb = [(x * (1.0 / ll)).astype(BF16) for x, ll in zip(p, l)]
            o = [_dot(x, v2[rr][:, _head_lanes(hh)]) for x, (rr, hh) in zip(pb, us)]
            for u, oo, mm, ll in zip(us, o, m, l):
                o_part[u] = oo
                l_part[u] = jnp.broadcast_to(mm + jnp.log(ll), (B, HEAD_DIM))
        for rr in range(rps):
            o_ref[rows[rr], :] = jnp.concatenate([o_part[(rr, hh)] for hh in range(HEADS_PER_CHUNK)], axis=1)
            l_ref[rows[rr], :] = jnp.concatenate([l_part[(rr, hh)] for hh in range(HEADS_PER_CHUNK)], axis=1)

    out_spec = pl.BlockSpec((None, RB, LANES), lambda j, n, rg: (j, n, 0))
    return pl.pallas_call(
        body, name=name, grid=(N_CHUNKS, nb, nrg),
        in_specs=_attn_in_specs(gi, RB, nb - 1),
        out_specs=[out_spec, out_spec],
        out_shape=[jax.ShapeDtypeStruct((N_CHUNKS, S, LANES), F32)] * 2,
        compiler_params=_cparams(("parallel", "arbitrary", "arbitrary")))(qkv, qkv, qkv, qkv, qkv, bias)


def _attn_bwd(qkv, bias, lse, dya, ya, wts, gi, d, *, name):
    _, S, _ = qkv.shape
    B = ATTN_BLOCK
    RB, nb, rps, nrg = _attn_geometry(S, d)
    scale = HEAD_DIM ** -0.5
    units = [(rr, hh) for rr in range(rps) for hh in range(HEADS_PER_CHUNK)]

    def body(q_ref, kp_ref, kc_ref, vp_ref, vc_ref, b_ref, l_ref, dya_ref, ya_ref, w_ref,
             dq_ref, dk_ref, dv_ref, db_ref, dk_carry, dv_carry):
        n, rg = pl.program_id(1), pl.program_id(2)
        rows = [_residue_rows(d, rps, rg, rr) for rr in range(rps)]

        @pl.when((n == 0) & (rg == 0))
        def _():
            db_ref[...] = jnp.zeros_like(db_ref)
            dk_carry[...] = jnp.zeros_like(dk_carry)
            dv_carry[...] = jnp.zeros_like(dv_carry)

        @pl.when(n < nb)
        def _():
            edge = _first_block_mask(n > 0)
            q = [q_ref[r_, :].astype(BF16) for r_ in rows]
            k2 = [jnp.concatenate([kp_ref[r_, :], kc_ref[r_, :]], axis=0).astype(BF16) for r_ in rows]
            v2 = [jnp.concatenate([vp_ref[r_, :], vc_ref[r_, :]], axis=0).astype(BF16) for r_ in rows]
            lse_c = [l_ref[r_, :] for r_ in rows]
            dy_c = [dya_ref[r_, :] for r_ in rows]
            ya_c = [ya_ref[r_, :] for r_ in rows]
            w_c = [w_ref[r_, :] for r_ in rows]
            ds_sum = [None] * HEADS_PER_CHUNK
            dq_part, dk_part, dv_part = {}, {}, {}
            for u0 in range(0, len(units), ATTN_UNITS_AT_ONCE):
                us = units[u0:u0 + ATTN_UNITS_AT_ONCE]
                hl = [_head_lanes(hh) for _, hh in us]
                qh = [q[rr][:, sl] for (rr, _), sl in zip(us, hl)]
                kh = [k2[rr][:, sl] for (rr, _), sl in zip(us, hl)]
                vh = [v2[rr][:, sl] for (rr, _), sl in zip(us, hl)]
                s = [_dot_nt(a, k) * scale + b_ref[hh] + edge for a, k, (_, hh) in zip(qh, kh, us)]
                p = [jnp.exp(x - lse_c[rr][:, HEAD_DIM * hh:HEAD_DIM * hh + 1]) for x, (rr, hh) in zip(s, us)]
                dy = [dy_c[rr][:, sl] for (rr, _), sl in zip(us, hl)]
                w = [w_c[rr][:, sl] for (rr, _), sl in zip(us, hl)]
                shift = [ww[:, 0:1] * jnp.sum(d_ * ya_c[rr][:, sl], axis=-1, keepdims=True)
                         for ww, d_, (rr, _), sl in zip(w, dy, us, hl)]
                do = [(ww * d_).astype(BF16) for ww, d_ in zip(w, dy)]
                ds = [pp * (_dot_nt(o_, v) - sh) for pp, o_, v, sh in zip(p, do, vh, shift)]
                for x, (_, hh) in zip(ds, us):
                    ds_sum[hh] = x if ds_sum[hh] is None else ds_sum[hh] + x
                dsb = [x.astype(BF16) for x in ds]
                pb = [x.astype(BF16) for x in p]
                for u, x, pp, a, k, o_ in zip(us, dsb, pb, qh, kh, do):
                    dq_part[u] = _dot(x, k) * scale
                    dk_part[u] = _dot_tn(x, a) * scale
                    dv_part[u] = _dot_tn(pp, o_)
            for hh in range(HEADS_PER_CHUNK):
                db_ref[hh] += ds_sum[hh]
            for rr in range(rps):
                r_ = rows[rr]
                dq_ref[r_, :] = jnp.concatenate([dq_part[(rr, hh)] for hh in range(HEADS_PER_CHUNK)], axis=1)
                dk2 = jnp.concatenate([dk_part[(rr, hh)] for hh in range(HEADS_PER_CHUNK)], axis=1)
                dv2 = jnp.concatenate([dv_part[(rr, hh)] for hh in range(HEADS_PER_CHUNK)], axis=1)
                dk_ref[r_, :] = dk_carry[r_, :] + dk2[0:B]
                dv_ref[r_, :] = dv_carry[r_, :] + dv2[0:B]
                dk_carry[r_, :] = dk2[B:2 * B]
                dv_carry[r_, :] = dv2[B:2 * B]

        @pl.when(n == nb)
        def _():
            for r_ in rows:
                dk_ref[r_, :] = dk_carry[r_, :]
                dv_ref[r_, :] = dv_carry[r_, :]

    last = nb - 1
    cur = pl.BlockSpec((None, RB, LANES), lambda j, n, rg: (j, jnp.minimum(n, last), 0))
    lag = pl.BlockSpec((None, RB, LANES), lambda j, n, rg: (j, jnp.maximum(n - 1, 0), 0))
    db_spec = pl.BlockSpec((HEADS_PER_CHUNK, B, 2 * B), lambda j, n, rg: (j, 0, 0))
    dq, dk, dv, db = pl.pallas_call(
        body, name=name, grid=(N_CHUNKS, nb + 1, nrg),
        in_specs=_attn_in_specs(gi, RB, last) + [cur, cur, cur, cur],
        out_specs=[cur, lag, lag, db_spec],
        out_shape=[jax.ShapeDtypeStruct((N_CHUNKS, S, LANES), F32)] * 3
        + [jax.ShapeDtypeStruct((HEADS_PER_GROUP, B, 2 * B), F32)],
        scratch_shapes=[pltpu.VMEM((RB, LANES), F32), pltpu.VMEM((RB, LANES), F32)],
        compiler_params=_cparams(("arbitrary", "arbitrary", "arbitrary")))(
            qkv, qkv, qkv, qkv, qkv, bias, lse, dya, ya, wts)
    return (dq, dk, dv), db


def _bias_grad(dbs, *, name):
    nk = ATTN_BLOCK * 2 * ATTN_BLOCK
    buckets = []
    for (_, dil) in ATTN_GROUPS:
        b, valid = _attn_tables(dil)
        buckets.append(np.where(valid, b, -1).reshape(1, nk))
    bk = jnp.asarray(np.stack(buckets).astype(np.int32))
    flat = [x.reshape(HEADS_PER_GROUP, nk) for x in dbs]

    def body(bk_ref, d0, d1, d2, o_ref):
        ids = lax.broadcasted_iota(jnp.int32, (NUM_BUCKETS, nk), 0)
        for gi, dref in enumerate((d0, d1, d2)):
            onehot = (ids == bk_ref[gi]).astype(F32)
            o_ref[gi] = lax.dot_general(onehot, dref[...], (((1,), (1,)), ((), ())),
                                        preferred_element_type=F32, precision=lax.Precision.HIGHEST)

    out = pl.pallas_call(
        body, name=name,
        out_shape=jax.ShapeDtypeStruct((3, NUM_BUCKETS, HEADS_PER_GROUP), F32),
        compiler_params=_cparams())(bk, *flat)
    return jnp.transpose(out, (1, 0, 2)).reshape(NUM_BUCKETS, 3 * HEADS_PER_GROUP)


def _ret_tables(S):
    half = RET_QK_DIM // 2
    pos = jnp.arange(S, dtype=F32)
    inv_freq = ROPE_BASE ** (-jnp.arange(half, dtype=F32) / half)
    ang = pos[:, None] * inv_freq[None]
    cos, sin = jnp.cos(ang), jnp.sin(ang)
    H, C = RET_HEADS, RET_CHUNK
    log_g = jnp.log(1.0 - 2.0 ** (-5.0 - jnp.arange(H, dtype=F32)))
    n = jnp.arange(C, dtype=F32)
    diff = n[:, None] - n[None, :]
    dmask = jnp.where(diff >= 0, jnp.exp(log_g[:, None, None] * jnp.maximum(diff, 0.0)), 0.0)
    q_dec = jnp.exp(log_g[:, None] * (n + 1.0))
    k_dec = jnp.exp(log_g[:, None] * (C - 1.0 - n))
    chunk_dec = jnp.exp(log_g * C)
    qd = jnp.broadcast_to(q_dec[:, :, None], (H, C, RET_QK_DIM))
    kd = jnp.broadcast_to(k_dec[:, :, None], (H, C, RET_QK_DIM))
    cd = jnp.broadcast_to(chunk_dec[:, None, None], (H, 1, RET_V_DIM))
    return cos, sin, dmask, qd, kd, cd


def _rot(t, cos, sin):
    half = RET_QK_DIM // 2
    t1, t2 = t[:, :half], t[:, half:]
    return jnp.concatenate([t1 * cos - t2 * sin, t1 * sin + t2 * cos], axis=-1)


def _unrot(t, cos, sin):
    half = RET_QK_DIM // 2
    t1, t2 = t[:, :half], t[:, half:]
    return jnp.concatenate([t1 * cos + t2 * sin, t2 * cos - t1 * sin], axis=-1)


def _ret_specs(rev, nC):
    C, DK, DV = RET_CHUNK, RET_QK_DIM, RET_V_DIM
    cidx = (lambda c: nC - 1 - c) if rev else (lambda c: c)
    H = RET_HEADS
    return dict(
        qk=lambda which: pl.BlockSpec((None, C, H * DK), lambda c: (which, cidx(c), 0)),
        q=pl.BlockSpec((C, H * DK), lambda c: (cidx(c), 0)),
        v=pl.BlockSpec((C, H * DV), lambda c: (cidx(c), 0)),
        cs=pl.BlockSpec((C, DK // 2), lambda c: (cidx(c), 0)),
        dmask=pl.BlockSpec((H, C, C), lambda c: (0, 0, 0)),
        dec=pl.BlockSpec((H, C, DK), lambda c: (0, 0, 0)),
        cd=pl.BlockSpec((H, 1, DV), lambda c: (0, 0, 0)),
        st=pl.BlockSpec((H, None, DK, DV), lambda c: (0, cidx(c), 0, 0)),
    )


def _ret_fwd(qk, v, g, tables, *, name):
    _, S, _ = qk.shape
    nC = S // RET_CHUNK
    C, DK, DV, H = RET_CHUNK, RET_QK_DIM, RET_V_DIM, RET_HEADS
    cos, sin, dmask, qd, kd, cd = tables
    kscale = DK ** -0.5

    def body(q_ref, k_ref, v_ref, g_ref, cos_ref, sin_ref, dm_ref, qd_ref, kd_ref, cd_ref,
             o_ref, yb_ref, ybt_ref, st_ref, state):
        @pl.when(pl.program_id(0) == 0)
        def _():
            state[...] = jnp.zeros_like(state)

        tcol = pl.multiple_of((pl.program_id(0) % RET_T_CHUNKS) * C, C)
        cs, sn = cos_ref[...], sin_ref[...]
        for h in range(H):
            qs, vs = slice(DK * h, DK * (h + 1)), slice(DV * h, DV * (h + 1))
            Q = _rot(q_ref[:, qs], cs, sn)
            K = _rot(k_ref[:, qs], cs, sn) * kscale
            Qb, Kb, V = Q.astype(BF16), K.astype(BF16), v_ref[:, vs]
            sb = state[h].astype(BF16)
            st_ref[h] = sb
            A = _dot_nt(Qb, Kb) * dm_ref[h]
            o = _dot(A.astype(BF16), V) + _dot((Q * qd_ref[h]).astype(BF16), sb)
            state[h] = state[h] * cd_ref[h] + _dot_tn((K * kd_ref[h]).astype(BF16), V)
            mu = jnp.mean(o, axis=-1, keepdims=True)
            dd = o - mu
            var = jnp.mean(dd * dd, axis=-1, keepdims=True)
            yn = dd * lax.rsqrt(var + GN_EPS)
            gv = g_ref[:, vs]
            yb = gv * jax.nn.sigmoid(gv) * yn
            o_ref[:, vs] = o
            yb_ref[:, vs] = yb.astype(BF16)
            ybt_ref[vs, pl.ds(tcol, C)] = yb.T.astype(BF16)

    sp = _ret_specs(False, nC)
    return pl.pallas_call(
        body, name=name, grid=(nC,),
        in_specs=[sp["qk"](0), sp["qk"](1), sp["v"], sp["v"], sp["cs"], sp["cs"], sp["dmask"],
                  sp["dec"], sp["dec"], sp["cd"]],
        out_specs=[sp["v"], sp["v"], pl.BlockSpec((H * DV, RET_T_CHUNKS * C), lambda c: (0, c // RET_T_CHUNKS)),
                   sp["st"]],
        out_shape=[jax.ShapeDtypeStruct((S, H * DV), F32), jax.ShapeDtypeStruct((S, H * DV), BF16),
                   jax.ShapeDtypeStruct((H * DV, S), BF16), jax.ShapeDtypeStruct((H, nC, DK, DV), BF16)],
        scratch_shapes=[pltpu.VMEM((H, DK, DV), F32)],
        compiler_params=_cparams(("arbitrary",)))(qk, qk, v, g, cos, sin, dmask, qd, kd, cd)


def _ret_bwd(dyb, qk, v, g, o, states, tables, *, name):
    _, S, _ = qk.shape
    nC = S // RET_CHUNK
    C, DK, DV, H = RET_CHUNK, RET_QK_DIM, RET_V_DIM, RET_HEADS
    cos, sin, dmask, qd, kd, cd = tables
    kscale = DK ** -0.5

    def body(dy_ref, q_ref, k_ref, v_ref, g_ref, o_ref, st_ref, cos_ref, sin_ref, dm_ref, qd_ref, kd_ref,
             cd_ref, dq_ref, dk_ref, dv_ref, dg_ref, dstate):
        @pl.when(pl.program_id(0) == 0)
        def _():
            dstate[...] = jnp.zeros_like(dstate)

        cs, sn = cos_ref[...], sin_ref[...]
        for h in range(H):
            qs, vs = slice(DK * h, DK * (h + 1)), slice(DV * h, DV * (h + 1))
            ov = o_ref[:, vs]
            mu = jnp.mean(ov, axis=-1, keepdims=True)
            dd = ov - mu
            var = jnp.mean(dd * dd, axis=-1, keepdims=True)
            rstd = lax.rsqrt(var + GN_EPS)
            yn = dd * rstd
            gv, dy = g_ref[:, vs], dy_ref[:, vs]
            sg = jax.nn.sigmoid(gv)
            dg_ref[:, vs] = (dy * yn * (sg * (1.0 + gv * (1.0 - sg)))).astype(BF16)
            dyn = dy * (gv * sg)
            dO = rstd * (dyn - jnp.mean(dyn, axis=-1, keepdims=True)
                         - yn * jnp.mean(dyn * yn, axis=-1, keepdims=True))
            dOb = dO.astype(BF16)

            Q = _rot(q_ref[:, qs], cs, sn)
            K = _rot(k_ref[:, qs], cs, sn) * kscale
            Qb, Kb, V = Q.astype(BF16), K.astype(BF16), v_ref[:, vs]
            dm, qd_h, kd_h = dm_ref[h], qd_ref[h], kd_ref[h]
            Sb = st_ref[h]
            dSb = dstate[h].astype(BF16)
            Ab = (_dot_nt(Qb, Kb) * dm).astype(BF16)
            dAb = (_dot_nt(dOb, V) * dm).astype(BF16)
            Qd = (Q * qd_h).astype(BF16)
            Kd = (K * kd_h).astype(BF16)
            dQ = _dot(dAb, Kb) + _dot_nt(dOb, Sb) * qd_h
            dK = _dot_tn(dAb, Qb) + _dot_nt(V, dSb) * kd_h
            dv_ref[:, vs] = (_dot_tn(Ab, dOb) + _dot(Kd, dSb)).astype(BF16)
            dstate[h] = dstate[h] * cd_ref[h] + _dot_tn(Qd, dOb)
            dq_ref[:, qs] = _unrot(dQ, cs, sn).astype(BF16)
            dk_ref[:, qs] = (_unrot(dK, cs, sn) * kscale).astype(BF16)

    sp = _ret_specs(True, nC)
    dq, dk, dv, dg = pl.pallas_call(
        body, name=name, grid=(nC,),
        in_specs=[sp["v"], sp["qk"](0), sp["qk"](1), sp["v"], sp["v"], sp["v"], sp["st"], sp["cs"], sp["cs"],
                  sp["dmask"], sp["dec"], sp["dec"], sp["cd"]],
        out_specs=[sp["q"], sp["q"], sp["v"], sp["v"]],
        out_shape=[jax.ShapeDtypeStruct((S, H * DK), BF16), jax.ShapeDtypeStruct((S, H * DK), BF16),
                   jax.ShapeDtypeStruct((S, H * DV), BF16), jax.ShapeDtypeStruct((S, H * DV), BF16)],
        scratch_shapes=[pltpu.VMEM((H, DK, DV), F32)],
        compiler_params=_cparams(("arbitrary",)))(dyb, qk, qk, v, g, o, states, cos, sin, dmask, qd, kd, cd)
    return dq, dk, dv, dg


def _layer_fwd(l, x, xb, x_t, W, b_in, biases, ln, tables):
    S = x.shape[0]
    tag = f"l{l}"
    win = W["w_in"]
    c0, c1, c2, c3, c4 = 3 * ATTN_W, 3 * ATTN_W + 2048, 3 * ATTN_W + 4096, 3 * ATTN_W + 6144, IN_COLS
    qkv_a = _mm(xb, win[:, :c0], bias=b_in[:c0], groups=3, lane_chunks=True, name=f"{tag}_in_attn")
    qk_r = _mm(xb, win[:, c0:c1], bias=b_in[c0:c1], groups=2, name=f"{tag}_in_retqk")
    v_r = _mm(xb, win[:, c1:c2], bias=b_in[c1:c2], out_dtype=BF16, name=f"{tag}_in_retv")
    g_r = _mm(xb, win[:, c2:c3], bias=b_in[c2:c3], name=f"{tag}_in_retg")
    gates = _mm(xb, win[:, c3:c4], bias=b_in[c3:c4], groups=2, name=f"{tag}_in_gates")

    os_, ls_ = [], []
    for gi, (_, dil) in enumerate(ATTN_GROUPS):
        o, lse = _attn_fwd(qkv_a, biases[gi], gi, dil, name=f"{tag}_attn_fwd{gi}")
        os_.append(o)
        ls_.append(lse)
    ya_b, ya_t, ya, wts = _combine_fwd(os_, ls_, name=f"{tag}_combine")

    o_r, yb, yb_t, states = _ret_fwd(qk_r, v_r, g_r, tables, name=f"{tag}_ret_fwd")

    pa = _mm(ya_b, W["w_attn_proj"], name=f"{tag}_attn_proj")
    pr = _mm(yb, W["w_ret_proj"], name=f"{tag}_ret_proj")
    merged, merged_t = _merge_fwd(gates, pa, pr, name=f"{tag}_merge")
    mix = _mm(merged, W["w_out"], name=f"{tag}_out_proj")
    h1, x1, x1b, x1_t = _ln_fwd(x, mix, ln["ln1_g"], ln["ln1_b"], name=f"{tag}_ln1")
    uv = _mm(x1b, W["w_gu"], groups=2, name=f"{tag}_ffn_in")
    hh, hh_t = _swiglu_fwd(uv, name=f"{tag}_swiglu")
    f = _mm(hh, W["w_ffn_down"], name=f"{tag}_ffn_down")
    h2, x2, x2b, x2_t = _ln_fwd(x1, f, ln["ln2_g"], ln["ln2_b"], name=f"{tag}_ln2")
    saved = dict(x_t=x_t, qkv_a=qkv_a, qk_r=qk_r, v_r=v_r, g_r=g_r, gates=gates, ls=ls_, ya_t=ya_t, ya=ya,
                 wts=wts, o_r=o_r, yb_t=yb_t, states=states, pa=pa, pr=pr, merged_t=merged_t, h1=h1, x1_t=x1_t,
                 uv=uv, hh_t=hh_t, h2=h2)
    return x2, x2b, x2_t, saved


def _layer_bwd(l, dx2, sv, W, biases, ln, tables):
    S = dx2.shape[0]
    tag = f"l{l}"
    g = {}
    dh2b, res2, g["ln2_g"], g["ln2_b"] = _ln_bwd(dx2, sv["h2"], ln["ln2_g"], name=f"{tag}_ln2_bwd")
    dhh = _mm(dh2b, W["w_ffn_down"], transpose_b=True, name=f"{tag}_d_hh")
    g["w_ffn_down"] = _mm(sv["hh_t"], dh2b, name=f"{tag}_dw_down")
    dudv = _swiglu_bwd(dhh, sv["uv"], name=f"{tag}_swiglu_bwd")
    dx1 = _mm(dudv, W["w_gu"], transpose_b=True, add=res2, name=f"{tag}_d_x1")
    dwgu = _mm(sv["x1_t"], dudv, name=f"{tag}_dw_gu")
    g["w_ffn_gate"], g["w_ffn_up"] = dwgu[:, :D_FF], dwgu[:, D_FF:]

    dh1b, res1, g["ln1_g"], g["ln1_b"] = _ln_bwd(dx1, sv["h1"], ln["ln1_g"], name=f"{tag}_ln1_bwd")
    dmerged = _mm(dh1b, W["w_out"], transpose_b=True, name=f"{tag}_d_merged")
    g["w_out"] = _mm(sv["merged_t"], dh1b, name=f"{tag}_dw_out")
    dpa, dpr, dgates = _merge_bwd(dmerged, sv["gates"], sv["pa"], sv["pr"], name=f"{tag}_merge_bwd")
    dya = _mm(dpa, W["w_attn_proj"], transpose_b=True, groups=1, lane_chunks=True, name=f"{tag}_d_ya")
    g["w_attn_proj"] = _mm(sv["ya_t"], dpa, name=f"{tag}_dw_ap")
    dyb = _mm(dpr, W["w_ret_proj"], transpose_b=True, name=f"{tag}_d_yb")
    g["w_ret_proj"] = _mm(sv["yb_t"], dpr, name=f"{tag}_dw_rp")

    da, dbs = [], []
    for gi, (_, dil) in enumerate(ATTN_GROUPS):
        dqkv, db = _attn_bwd(sv["qkv_a"], biases[gi], sv["ls"][gi], dya, sv["ya"], sv["wts"][gi], gi, dil,
                             name=f"{tag}_attn_bwd{gi}")
        da.append(dqkv)
        dbs.append(db)
    dq_r, dk_r, dv_r, dg_r = _ret_bwd(dyb, sv["qk_r"], sv["v_r"], sv["g_r"], sv["o_r"], sv["states"], tables,
                                 name=f"{tag}_ret_bwd")
    dz, colsum = _assemble_dz(da, dq_r, dk_r, dv_r, dg_r, dgates, name=f"{tag}_assemble_dz")
    g["b_in"] = colsum.reshape(IN_COLS)
    dx = _mm(dz, W["w_in"], transpose_b=True, add=res1, name=f"{tag}_d_x")
    g["w_in"] = _mm(sv["x_t"], dz, name=f"{tag}_dw_in")
    return dx, g, dbs


HBM_SPEC = pl.BlockSpec(memory_space=pltpu.HBM)
OTHER_CHIPS = ((1, 0), (0, 1), (1, 1))


def _flip(v, f):
    return 1 - v if f else v


def _all_gather(shards, *, name):
    n = len(shards)

    def body(*refs):
        x_refs, out_refs = refs[:n], refs[n:2 * n]
        send_sems, recv_sems, local_sems = refs[2 * n:]
        x, y, c = lax.axis_index("x"), lax.axis_index("y"), lax.axis_index("c")
        me, sibling = (x, y, c), (x, y, 1 - c)
        chips = [(_flip(x, fx), _flip(y, fy)) for fx, fy in OTHER_CHIPS]

        def copy(a, k, block, to, src=None):
            px, py, pc = block
            rows = out_refs[a].at[4 * px + 2 * py + pc]
            return pltpu.make_async_remote_copy(
                src_ref=rows if src is None else src, dst_ref=rows,
                send_sem=send_sems.at[7 * a + k], recv_sem=recv_sems.at[7 * a + k], device_id=to, device_id_type=MESH)

        mine, first, passed = [], [], []
        for a in range(n):
            cp = pltpu.make_async_copy(x_refs[a], out_refs[a].at[4 * x + 2 * y + c], local_sems.at[a])
            cp.start()
            mine.append(cp)
            first.append(copy(a, 0, me, sibling, src=x_refs[a]))
            first += [copy(a, 1 + j, me, (*chip, c), src=x_refs[a]) for j, chip in enumerate(chips)]
        for cp in first:
            cp.start()
        for j, chip in enumerate(chips):
            for a in range(n):
                copy(a, 1 + j, (*chip, c), me).wait_recv()
                cp = copy(a, 4 + j, (*chip, c), sibling)
                cp.start()
                passed.append(cp)
        for a in range(n):
            copy(a, 0, sibling, me).wait_recv()
            for j, chip in enumerate(chips):
                copy(a, 4 + j, (*chip, 1 - c), me).wait_recv()
        for cp in first + passed:
            cp.wait_send()
        for cp in mine:
            cp.wait()

    return pl.pallas_call(
        body, name=name, out_shape=[jax.ShapeDtypeStruct((N_DEV,) + s.shape, s.dtype) for s in shards],
        in_specs=[HBM_SPEC] * n, out_specs=[HBM_SPEC] * n,
        scratch_shapes=[pltpu.SemaphoreType.DMA((7 * n,)), pltpu.SemaphoreType.DMA((7 * n,)),
                        pltpu.SemaphoreType.DMA((n,))],
    )(*shards)


def _rs_sibling_exchange(g8s, *, name):
    n = len(g8s)

    def body(*refs):
        g_refs, recv_refs = refs[:n], refs[n:2 * n]
        send_sems, recv_sems = refs[2 * n:]
        x, y, c = lax.axis_index("x"), lax.axis_index("y"), lax.axis_index("c")
        copies = []
        for a in range(n):
            for k in range(4):
                cp = pltpu.make_async_remote_copy(
                    src_ref=g_refs[a].at[k, 1 - c], dst_ref=recv_refs[a].at[k], send_sem=send_sems.at[4 * a + k],
                    recv_sem=recv_sems.at[4 * a + k], device_id=(x, y, 1 - c), device_id_type=MESH)
                cp.start()
                copies.append(cp)
        for cp in copies:
            cp.wait()

    return pl.pallas_call(
        body, name=name,
        out_shape=[jax.ShapeDtypeStruct((4,) + g.shape[2:], g.dtype) for g in g8s],
        in_specs=[HBM_SPEC] * n, out_specs=[HBM_SPEC] * n,
        scratch_shapes=[pltpu.SemaphoreType.DMA((4 * n,)), pltpu.SemaphoreType.DMA((4 * n,))],
    )(*g8s)


def _rs_chip_sum(g8, recv, core, *, name):
    _, _, R, Wd = g8.shape
    tr = _div_tile(R, 256, 16)

    def body(core_ref, g_ref, r_ref, o_ref):
        o_ref[...] = (g_ref[...] + r_ref[...]).astype(BF16)

    grid_spec = pltpu.PrefetchScalarGridSpec(
        num_scalar_prefetch=1, grid=(4, R // tr),
        in_specs=[pl.BlockSpec((None, None, tr, Wd), lambda k, i, core_ref: (k, core_ref[0], i, 0)),
                  pl.BlockSpec((None, tr, Wd), lambda k, i, core_ref: (k, i, 0))],
        out_specs=pl.BlockSpec((None, tr, Wd), lambda k, i, core_ref: (k, i, 0)))
    return pl.pallas_call(
        body, name=name, grid_spec=grid_spec, out_shape=jax.ShapeDtypeStruct((4, R, Wd), BF16),
        compiler_params=_cparams(("parallel", "parallel")))(core, g8, recv)


def _rs_chip_exchange(ps, *, name):
    n = len(ps)

    def body(*refs):
        p_refs, out_refs = refs[:n], refs[n:2 * n]
        send_sems, recv_sems, local_sems = refs[2 * n:]
        x, y, c = lax.axis_index("x"), lax.axis_index("y"), lax.axis_index("c")
        my_chip = 2 * x + y
        copies = []
        for a in range(n):
            mine = pltpu.make_async_copy(p_refs[a].at[my_chip], out_refs[a].at[my_chip], local_sems.at[a])
            mine.start()
            copies.append(mine)
            for j, (fx, fy) in enumerate(OTHER_CHIPS):
                px, py = _flip(x, fx), _flip(y, fy)
                cp = pltpu.make_async_remote_copy(
                    src_ref=p_refs[a].at[2 * px + py], dst_ref=out_refs[a].at[my_chip],
                    send_sem=send_sems.at[3 * a + j], recv_sem=recv_sems.at[3 * a + j],
                    device_id=(px, py, c), device_id_type=MESH)
                cp.start()
                copies.append(cp)
        for cp in copies:
            cp.wait()

    return pl.pallas_call(
        body, name=name, out_shape=[jax.ShapeDtypeStruct(p.shape, p.dtype) for p in ps],
        in_specs=[HBM_SPEC] * n, out_specs=[HBM_SPEC] * n,
        scratch_shapes=[pltpu.SemaphoreType.DMA((3 * n,)), pltpu.SemaphoreType.DMA((3 * n,)),
                        pltpu.SemaphoreType.DMA((n,))],
    )(*ps)


def _all_reduce_small(v, *, name):
    R, Wd = v.shape

    def body(v_ref, out_ref, slots, send_sems, recv_sems):
        x, y, c = lax.axis_index("x"), lax.axis_index("y"), lax.axis_index("c")
        me = 4 * x + 2 * y + c
        slots[me] = v_ref[...]
        copies = []
        for rel in range(1, N_DEV):
            peer = (_flip(x, rel & 4), _flip(y, rel & 2), _flip(c, rel & 1))
            cp = pltpu.make_async_remote_copy(
                src_ref=v_ref, dst_ref=slots.at[me], send_sem=send_sems.at[rel - 1],
                recv_sem=recv_sems.at[rel - 1], device_id=peer, device_id_type=MESH)
            cp.start()
            copies.append(cp)
        for cp in copies:
            cp.wait()
        acc = slots[0]
        for j in range(1, N_DEV):
            acc = acc + slots[j]
        out_ref[...] = acc

    vm = pl.BlockSpec(memory_space=pltpu.VMEM)
    return pl.pallas_call(
        body, name=name, out_shape=jax.ShapeDtypeStruct((R, Wd), F32),
        in_specs=[vm], out_specs=vm,
        scratch_shapes=[pltpu.VMEM((N_DEV, R, Wd), F32), pltpu.SemaphoreType.DMA((7,)),
                        pltpu.SemaphoreType.DMA((7,))],
    )(v)


def _adam_math(w, g, m, v):
    m2 = ADAM_B1 * m + (1.0 - ADAM_B1) * g
    v2 = ADAM_B2 * v + (1.0 - ADAM_B2) * (g * g)
    m_hat = m2 / (1.0 - ADAM_B1 ** ADAM_STEP)
    v_hat = v2 / (1.0 - ADAM_B2 ** ADAM_STEP)
    delta = -ADAM_LR * (m_hat / (jnp.sqrt(v_hat) + ADAM_EPS) + ADAM_WD * w)
    return delta, m2, v2


def _adam_sharded(parts, w, m, v, *, name):
    _, R, Wd = w.shape
    tr = _div_tile(R, 256, 16)

    def body(p0_ref, p1_ref, w_ref, m_ref, v_ref, g_ref, d_ref, m2_ref, v2_ref):
        def chip_sum(p_ref):
            g = p_ref[0].astype(F32)
            for s in range(1, 4):
                g = g + p_ref[s].astype(F32)
            return g

        g = jnp.where(pl.program_id(0) == 0, chip_sum(p0_ref), chip_sum(p1_ref))
        delta, m2, v2 = _adam_math(w_ref[...], g, m_ref[...], v_ref[...])
        g_ref[...] = g
        d_ref[...] = delta
        m2_ref[...] = m2
        v2_ref[...] = v2

    assert DEPTH == 2
    p_spec = pl.BlockSpec((4, tr, Wd), lambda l, i: (0, i, 0))
    s_spec = pl.BlockSpec((None, tr, Wd), lambda l, i: (l, i, 0))
    return pl.pallas_call(
        body, name=name, grid=(DEPTH, R // tr),
        in_specs=[p_spec, p_spec, s_spec, s_spec, s_spec],
        out_specs=[s_spec] * 4, out_shape=[jax.ShapeDtypeStruct((DEPTH, R, Wd), F32)] * 4,
        compiler_params=_cparams(("parallel", "parallel")))(parts[0], parts[1], w, m, v)


def _adam_small(g, w, m, v, *, name):
    R, Wd = w.shape

    def body(g_ref, w_ref, m_ref, v_ref, d_ref, m2_ref, v2_ref):
        delta, m2, v2 = _adam_math(w_ref[...], g_ref[...], m_ref[...], v_ref[...])
        d_ref[...] = delta
        m2_ref[...] = m2
        v2_ref[...] = v2

    return pl.pallas_call(
        body, name=name, out_shape=[jax.ShapeDtypeStruct((R, Wd), F32)] * 3,
        compiler_params=_cparams())(g, w, m, v)


def _shard_shape(name):
    r, c = FULL_SHAPE[name]
    return (r, c // N_DEV) if name in COL_SHARDED else (r // N_DEV, c)


def _full_from_gathered(name, g):
    if name in COL_SHARDED:
        return jnp.transpose(g, (1, 0, 2)).reshape(FULL_SHAPE[name])
    return g.reshape(FULL_SHAPE[name])


def _dest_major(name, gfull):
    r, c = _shard_shape(name)
    if name in COL_SHARDED:
        blk = jnp.transpose(gfull.reshape(r, N_DEV, c), (1, 0, 2))
    else:
        blk = gfull.reshape(N_DEV, r, c)
    return blk.reshape(4, 2, r, c)


def _pack_small(t):
    flat = jnp.concatenate([t[n].reshape(-1).astype(F32) for n in SMALL_WEIGHTS])
    return jnp.pad(flat, (0, SMALL_ROWS * LANES - flat.shape[0])).reshape(SMALL_ROWS, LANES)


def _unpack_small(packed):
    flat = packed.reshape(-1)
    out, off = {}, 0
    for n in SMALL_WEIGHTS:
        size = math.prod(SMALL_SHAPE[n])
        out[n] = flat[off:off + size].reshape(SMALL_SHAPE[n])
        off += size
    return out


def _local_step(x, target, rel_bias, b_in, lns, layers_w):
    S = x.shape[0]
    tables = _ret_tables(S)
    biases = [_attn_bias(rel_bias, gi, dil) for gi, (_, dil) in enumerate(ATTN_GROUPS)]
    Ws = []
    for W in layers_w:
        W = dict(W)
        W["w_gu"] = jnp.concatenate([W["w_ffn_gate"], W["w_ffn_up"]], axis=1)
        Ws.append(W)

    h = x
    hb, h_t = _cast_transpose(x, name="cast_x")
    saved = []
    for l in range(DEPTH):
        h, hb, h_t, sv = _layer_fwd(l, h, hb, h_t, Ws[l], b_in[l], biases, lns[l], tables)
        saved.append(sv)
    dy, sq = _loss_fwd_bwd(h, target, name="loss")
    loss_local = 0.5 * sq[0, 0] / D_MODEL

    grads = [None] * DEPTH
    db_tot = None
    dx = dy
    for l in reversed(range(DEPTH)):
        dx, g, dbs = _layer_bwd(l, dx, saved[l], Ws[l], biases, lns[l], tables)
        grads[l] = g
        db_tot = dbs if db_tot is None else [a + b for a, b in zip(db_tot, dbs)]
    small = {"rel_bias": _bias_grad(db_tot, name="bias_grad"),
             "b_in": jnp.stack([grads[l]["b_in"] for l in range(DEPTH)])}
    for n in ("ln1_g", "ln1_b", "ln2_g", "ln2_b"):
        small[n] = jnp.stack([grads[l][n].reshape(D_MODEL) for l in range(DEPTH)])
    return loss_local, dx, grads, small


def kernel(x, rel_bias, w_in, b_in, w_attn_proj, w_ret_proj, w_out, ln1_g, ln1_b, w_ffn_gate, w_ffn_up, w_ffn_down, ln2_g, ln2_b, loss_target, m_rel_bias, m_w_in, m_b_in, m_w_attn_proj, m_w_ret_proj, m_w_out, m_ln1_g, m_ln1_b, m_w_ffn_gate, m_w_ffn_up, m_w_ffn_down, m_ln2_g, m_ln2_b, v_rel_bias, v_w_in, v_b_in, v_w_attn_proj, v_w_ret_proj, v_w_out, v_ln1_g, v_ln1_b, v_w_ffn_gate, v_w_ffn_up, v_w_ffn_down, v_ln2_g, v_ln2_b):
    w = dict(rel_bias=rel_bias, w_in=w_in, b_in=b_in, w_attn_proj=w_attn_proj, w_ret_proj=w_ret_proj, w_out=w_out,
             ln1_g=ln1_g, ln1_b=ln1_b, w_ffn_gate=w_ffn_gate, w_ffn_up=w_ffn_up, w_ffn_down=w_ffn_down,
             ln2_g=ln2_g, ln2_b=ln2_b)
    m = dict(rel_bias=m_rel_bias, w_in=m_w_in, b_in=m_b_in, w_attn_proj=m_w_attn_proj, w_ret_proj=m_w_ret_proj,
             w_out=m_w_out, ln1_g=m_ln1_g, ln1_b=m_ln1_b, w_ffn_gate=m_w_ffn_gate, w_ffn_up=m_w_ffn_up,
             w_ffn_down=m_w_ffn_down, ln2_g=m_ln2_g, ln2_b=m_ln2_b)
    v = dict(rel_bias=v_rel_bias, w_in=v_w_in, b_in=v_b_in, w_attn_proj=v_w_attn_proj, w_ret_proj=v_w_ret_proj,
             w_out=v_w_out, ln1_g=v_ln1_g, ln1_b=v_ln1_b, w_ffn_gate=v_w_ffn_gate, w_ffn_up=v_w_ffn_up,
             w_ffn_down=v_w_ffn_down, ln2_g=v_ln2_g, ln2_b=v_ln2_b)

    layers_w = []
    for l in range(DEPTH):
        gathered = _all_gather([w[n][l].astype(BF16) for n in BIG_WEIGHTS], name=f"all_gather_l{l}")
        layers_w.append({n: _full_from_gathered(n, g) for n, g in zip(BIG_WEIGHTS, gathered)})
    lns = [{n: w[n][l] for n in ("ln1_g", "ln1_b", "ln2_g", "ln2_b")} for l in range(DEPTH)]

    loss_local, grad_x, grads, small = _local_step(x[0], loss_target[0], rel_bias, b_in, lns, layers_w)
    loss = lax.psum(loss_local, ("x", "y", "c"))

    core = lax.axis_index("c").astype(jnp.int32).reshape(1)
    parts = []
    for l in range(DEPTH):
        g8 = [_dest_major(n, grads[l][n]) for n in BIG_WEIGHTS]
        from_sibling = _rs_sibling_exchange(g8, name=f"rs_sibling_exchange_l{l}")
        chip_parts = [_rs_chip_sum(a, b, core, name=f"rs_chip_sum_l{l}_{n}")
                      for n, a, b in zip(BIG_WEIGHTS, g8, from_sibling)]
        parts.append(_rs_chip_exchange(chip_parts, name=f"rs_chip_exchange_l{l}"))
    big = [{} for _ in range(4)]
    for i, n in enumerate(BIG_WEIGHTS):
        res = _adam_sharded([parts[l][i] for l in range(DEPTH)], w[n], m[n], v[n], name=f"adam_{n}")
        for kind in range(4):
            big[kind][n] = res[kind]

    gs = _all_reduce_small(_pack_small(small), name="all_reduce_small")
    ds, ms, vs = _adam_small(gs, _pack_small(w), _pack_small(m), _pack_small(v), name="adam_small")
    sm = [_unpack_small(t) for t in (gs, ds, ms, vs)]

    outs = [loss, grad_x[None]]
    for kind in range(4):
        for n in ALL_WEIGHTS:
            outs.append(big[kind][n] if n in BIG_WEIGHTS else sm[kind][n])
    return tuple(outs)
```

```python
import functools
import math

import numpy as np
import jax
import jax.numpy as jnp
from jax import lax
from jax.experimental import pallas as pl
from jax.experimental.pallas import tpu as pltpu

F32 = jnp.float32
BF16 = jnp.bfloat16
MESH = pl.DeviceIdType.MESH

D_MODEL = 1024
DEPTH = 2
HEAD_DIM = 64
ATTN_GROUPS = ((128, 1), (512, 4), (2048, 16))
HEADS_PER_GROUP = 6
GROUP_WIDTH = HEADS_PER_GROUP * HEAD_DIM
ATTN_BLOCK = 128
NUM_BUCKETS = 32
MAX_DISTANCE = 2048
RET_HEADS = 4
RET_QK_DIM = 256
RET_V_DIM = 512
RET_CHUNK = 128
RET_T_CHUNKS = 4
ROPE_BASE = 10000.0
D_FF = 2816
ALPHA = (2 * DEPTH) ** 0.25
LN_EPS = 1e-5
GN_EPS = 1e-5
ATTN_W = 3 * GROUP_WIDTH
IN_COLS = 3 * ATTN_W + 2 * 1024 + 2 * 2048 + 2 * 1024
ADAM_LR, ADAM_B1, ADAM_B2, ADAM_EPS, ADAM_WD, ADAM_STEP = 0.001, 0.9, 0.999, 1e-08, 0.01, 10
N_DEV = 8
NEG = -1e30
LANES = 128
VMEM_LIMIT = 56 * 1024 * 1024
MM_TILE_CAP = 1664

BIG_WEIGHTS = ("w_in", "w_attn_proj", "w_ret_proj", "w_out", "w_ffn_gate", "w_ffn_up", "w_ffn_down")
COL_SHARDED = ("w_in", "w_attn_proj", "w_ffn_gate", "w_ffn_up")
FULL_SHAPE = {"w_in": (D_MODEL, IN_COLS), "w_attn_proj": (GROUP_WIDTH, D_MODEL), "w_ret_proj": (2048, D_MODEL),
              "w_out": (D_MODEL, D_MODEL), "w_ffn_gate": (D_MODEL, D_FF), "w_ffn_up": (D_MODEL, D_FF),
              "w_ffn_down": (D_FF, D_MODEL)}
SMALL_WEIGHTS = ("rel_bias", "b_in", "ln1_g", "ln1_b", "ln2_g", "ln2_b")
SMALL_SHAPE = {"rel_bias": (NUM_BUCKETS, 18), "b_in": (DEPTH, IN_COLS), "ln1_g": (DEPTH, D_MODEL),
               "ln1_b": (DEPTH, D_MODEL), "ln2_g": (DEPTH, D_MODEL), "ln2_b": (DEPTH, D_MODEL)}
SMALL_ROWS = 256
ALL_WEIGHTS = ("rel_bias", "w_in", "b_in", "w_attn_proj", "w_ret_proj", "w_out", "ln1_g", "ln1_b",
               "w_ffn_gate", "w_ffn_up", "w_ffn_down", "ln2_g", "ln2_b")


def _cparams(sem=None):
    return pltpu.CompilerParams(dimension_semantics=sem, vmem_limit_bytes=VMEM_LIMIT)


def _div_tile(n, cap, unit):
    if n <= cap:
        return n
    best = None
    for t in range(unit, cap + 1, unit):
        if n % t == 0:
            best = t
    assert best is not None, (n, cap, unit)
    return best


def _mm(a, b, *, name, out_dtype=F32, bias=None, add=None, groups=None, lane_chunks=False, transpose_b=False):
    M, K = a.shape
    N, K2 = b.shape if transpose_b else b.shape[::-1]
    assert K == K2 and a.dtype == BF16 and b.dtype == BF16
    tm = _div_tile(M, 1024, 16)
    tn = N // groups if groups else _div_tile(N, MM_TILE_CAP, LANES)
    tk = _div_tile(K, MM_TILE_CAP, LANES)
    nk = K // tk
    nch = tn // LANES
    has_bias, has_add = bias is not None, add is not None

    def body(*refs):
        a_ref, b_ref = refs[0], refs[1]
        pos = 2
        bias_ref = add_ref = None
        if has_bias:
            bias_ref = refs[pos]
            pos += 1
        if has_add:
            add_ref = refs[pos]
            pos += 1
        o_ref = refs[pos]

        def finish(r):
            if has_bias:
                r = r + bias_ref[...]
            if has_add:
                r = r + add_ref[...]
            if lane_chunks:
                for c in range(nch):
                    o_ref[c] = r[:, c * LANES:(c + 1) * LANES].astype(o_ref.dtype)
            else:
                o_ref[...] = r.astype(o_ref.dtype)

        def product():
            if transpose_b:
                return lax.dot_general(a_ref[...], b_ref[...], (((1,), (1,)), ((), ())), preferred_element_type=F32)
            return jnp.dot(a_ref[...], b_ref[...], preferred_element_type=F32)

        if nk == 1:
            finish(product())
        else:
            acc_ref = refs[pos + 1]
            k = pl.program_id(2)

            @pl.when(k == 0)
            def _():
                acc_ref[...] = jnp.zeros_like(acc_ref)

            acc_ref[...] += product()

            @pl.when(k == nk - 1)
            def _():
                finish(acc_ref[...])

    in_specs = [pl.BlockSpec((tm, tk), lambda i, j, k: (i, k)),
                pl.BlockSpec((tn, tk), lambda i, j, k: (j, k)) if transpose_b
                else pl.BlockSpec((tk, tn), lambda i, j, k: (k, j))]
    args = [a, b]
    if has_bias:
        in_specs.append(pl.BlockSpec((1, tn), lambda i, j, k: (0, j)))
        args.append(bias.reshape(1, N).astype(F32))
    if has_add:
        in_specs.append(pl.BlockSpec((tm, tn), lambda i, j, k: (i, j)))
        args.append(add)
    if lane_chunks:
        assert groups
        out_shape = jax.ShapeDtypeStruct((groups, nch, M, LANES), out_dtype)
        out_spec = pl.BlockSpec((None, nch, tm, LANES), lambda i, j, k: (j, 0, i, 0))
    elif groups:
        out_shape = jax.ShapeDtypeStruct((groups, M, tn), out_dtype)
        out_spec = pl.BlockSpec((None, tm, tn), lambda i, j, k: (j, i, 0))
    else:
        out_shape = jax.ShapeDtypeStruct((M, N), out_dtype)
        out_spec = pl.BlockSpec((tm, tn), lambda i, j, k: (i, j))
    scratch = [pltpu.VMEM((tm, tn), F32)] if nk > 1 else []
    out = pl.pallas_call(
        body, name=name, grid=(M // tm, N // tn, nk), in_specs=in_specs, out_specs=out_spec,
        out_shape=out_shape, scratch_shapes=scratch,
        compiler_params=_cparams(("parallel", "parallel", "arbitrary")))(*args)
    return out.reshape(groups * nch, M, LANES) if lane_chunks else out


def _row_spec(tr, w):
    return pl.BlockSpec((tr, w), lambda i: (i, 0))


def _vec_spec(w):
    return pl.BlockSpec((1, w), lambda i: (0, 0))


def _col_spec(w, tr):
    return pl.BlockSpec((w, tr), lambda i: (0, i))


def _cast_transpose(x, *, name):
    S, W = x.shape
    tr = 512

    def body(x_ref, o_ref, ot_ref):
        v = x_ref[...]
        o_ref[...] = v.astype(BF16)
        ot_ref[...] = v.T.astype(BF16)

    return pl.pallas_call(
        body, name=name, grid=(S // tr,), in_specs=[_row_spec(tr, W)],
        out_specs=[_row_spec(tr, W), _col_spec(W, tr)],
        out_shape=[jax.ShapeDtypeStruct((S, W), BF16), jax.ShapeDtypeStruct((W, S), BF16)],
        compiler_params=_cparams(("parallel",)))(x)


def _ln_fwd(x, sub, g, b, *, name):
    S, W = x.shape
    tr = 512

    def body(x_ref, s_ref, g_ref, b_ref, h_ref, y_ref, yb_ref, ybt_ref):
        h = ALPHA * x_ref[...] + s_ref[...]
        mu = jnp.mean(h, axis=-1, keepdims=True)
        d = h - mu
        var = jnp.mean(d * d, axis=-1, keepdims=True)
        y = d * lax.rsqrt(var + LN_EPS) * g_ref[...] + b_ref[...]
        h_ref[...] = h
        y_ref[...] = y
        yb_ref[...] = y.astype(BF16)
        ybt_ref[...] = y.T.astype(BF16)

    return pl.pallas_call(
        body, name=name, grid=(S // tr,),
        in_specs=[_row_spec(tr, W), _row_spec(tr, W), _vec_spec(W), _vec_spec(W)],
        out_specs=[_row_spec(tr, W)] * 3 + [_col_spec(W, tr)],
        out_shape=[jax.ShapeDtypeStruct((S, W), F32), jax.ShapeDtypeStruct((S, W), F32),
                   jax.ShapeDtypeStruct((S, W), BF16), jax.ShapeDtypeStruct((W, S), BF16)],
        compiler_params=_cparams(("parallel",)))(x, sub, g.reshape(1, W), b.reshape(1, W))


def _ln_bwd(dy, h, g, *, name):
    S, W = dy.shape
    tr = 512

    def body(dy_ref, h_ref, g_ref, dhb_ref, res_ref, dg_ref, db_ref):
        @pl.when(pl.program_id(0) == 0)
        def _():
            dg_ref[...] = jnp.zeros_like(dg_ref)
            db_ref[...] = jnp.zeros_like(db_ref)

        hh = h_ref[...]
        mu = jnp.mean(hh, axis=-1, keepdims=True)
        d = hh - mu
        var = jnp.mean(d * d, axis=-1, keepdims=True)
        rstd = lax.rsqrt(var + LN_EPS)
        xhat = d * rstd
        dyv = dy_ref[...]
        dg_ref[...] += jnp.sum(dyv * xhat, axis=0, keepdims=True)
        db_ref[...] += jnp.sum(dyv, axis=0, keepdims=True)
        dxh = dyv * g_ref[...]
        dh = rstd * (dxh - jnp.mean(dxh, axis=-1, keepdims=True)
                     - xhat * jnp.mean(dxh * xhat, axis=-1, keepdims=True))
        dhb_ref[...] = dh.astype(BF16)
        res_ref[...] = ALPHA * dh

    return pl.pallas_call(
        body, name=name, grid=(S // tr,),
        in_specs=[_row_spec(tr, W), _row_spec(tr, W), _vec_spec(W)],
        out_specs=[_row_spec(tr, W), _row_spec(tr, W), _vec_spec(W), _vec_spec(W)],
        out_shape=[jax.ShapeDtypeStruct((S, W), BF16), jax.ShapeDtypeStruct((S, W), F32),
                   jax.ShapeDtypeStruct((1, W), F32), jax.ShapeDtypeStruct((1, W), F32)],
        compiler_params=_cparams(("arbitrary",)))(dy, h, g.reshape(1, W))


def _loss_fwd_bwd(y, target, *, name):
    S, W = y.shape
    tr = 512

    def body(y_ref, t_ref, dy_ref, acc_ref):
        @pl.when(pl.program_id(0) == 0)
        def _():
            acc_ref[...] = jnp.zeros_like(acc_ref)

        e = y_ref[...] - t_ref[...]
        acc_ref[...] += jnp.sum(jnp.sum(e * e, axis=-1, keepdims=True), axis=0, keepdims=True)
        dy_ref[...] = e * (1.0 / W)

    return pl.pallas_call(
        body, name=name, grid=(S // tr,),
        in_specs=[_row_spec(tr, W), _row_spec(tr, W)],
        out_specs=[_row_spec(tr, W), pl.BlockSpec((1, 1), lambda i: (0, 0))],
        out_shape=[jax.ShapeDtypeStruct((S, W), F32), jax.ShapeDtypeStruct((1, 1), F32)],
        compiler_params=_cparams(("arbitrary",)))(y, target)


def _combine_fwd(os_, ls_, *, name):
    NCH, S, _ = os_[0].shape
    W = NCH * LANES
    tr = 512

    def body(o0, o1, o2, l0, l1, l2, yb_ref, ybt_ref, y_ref, w0_ref, w1_ref, w2_ref):
        for c in range(NCH):
            la, lb, lc = l0[c], l1[c], l2[c]
            m = jnp.maximum(jnp.maximum(la, lb), lc)
            ea, eb, ec = jnp.exp(la - m), jnp.exp(lb - m), jnp.exp(lc - m)
            inv = 1.0 / (ea + eb + ec)
            wa, wb, wc = ea * inv, eb * inv, ec * inv
            y = wa * o0[c] + wb * o1[c] + wc * o2[c]
            y_ref[c] = y
            yb_ref[:, c * LANES:(c + 1) * LANES] = y.astype(BF16)
            ybt_ref[c * LANES:(c + 1) * LANES, :] = y.T.astype(BF16)
            w0_ref[c] = wa
            w1_ref[c] = wb
            w2_ref[c] = wc

    ch = pl.BlockSpec((NCH, tr, LANES), lambda i: (0, i, 0))
    yb, ybt, y, w0, w1, w2 = pl.pallas_call(
        body, name=name, grid=(S // tr,),
        in_specs=[ch] * 6,
        out_specs=[_row_spec(tr, W), _col_spec(W, tr)] + [ch] * 4,
        out_shape=[jax.ShapeDtypeStruct((S, W), BF16), jax.ShapeDtypeStruct((W, S), BF16)]
        + [jax.ShapeDtypeStruct((NCH, S, LANES), F32)] * 4,
        compiler_params=_cparams(("parallel",)))(*os_, *ls_)
    return yb, ybt, y, (w0, w1, w2)


def _merge_fwd(gates, pa, pr, *, name):
    S, W = pa.shape
    tr = 512

    def body(g_ref, pa_ref, pr_ref, o_ref, ot_ref):
        m = jax.nn.sigmoid(g_ref[0]) * pa_ref[...] + jax.nn.sigmoid(g_ref[1]) * pr_ref[...]
        o_ref[...] = m.astype(BF16)
        ot_ref[...] = m.T.astype(BF16)

    return pl.pallas_call(
        body, name=name, grid=(S // tr,),
        in_specs=[pl.BlockSpec((2, tr, W), lambda i: (0, i, 0)), _row_spec(tr, W), _row_spec(tr, W)],
        out_specs=[_row_spec(tr, W), _col_spec(W, tr)],
        out_shape=[jax.ShapeDtypeStruct((S, W), BF16), jax.ShapeDtypeStruct((W, S), BF16)],
        compiler_params=_cparams(("parallel",)))(gates, pa, pr)


def _merge_bwd(dm, gates, pa, pr, *, name):
    S, W = pa.shape
    tr = 256

    def body(dm_ref, g_ref, pa_ref, pr_ref, dpa_ref, dpr_ref, dg_ref):
        dmv = dm_ref[...]
        sa, sb = jax.nn.sigmoid(g_ref[0]), jax.nn.sigmoid(g_ref[1])
        dpa_ref[...] = (dmv * sa).astype(BF16)
        dpr_ref[...] = (dmv * sb).astype(BF16)
        dg_ref[0] = (dmv * pa_ref[...] * (sa * (1.0 - sa))).astype(BF16)
        dg_ref[1] = (dmv * pr_ref[...] * (sb * (1.0 - sb))).astype(BF16)

    g3 = pl.BlockSpec((2, tr, W), lambda i: (0, i, 0))
    return pl.pallas_call(
        body, name=name, grid=(S // tr,),
        in_specs=[_row_spec(tr, W), g3, _row_spec(tr, W), _row_spec(tr, W)],
        out_specs=[_row_spec(tr, W), _row_spec(tr, W), g3],
        out_shape=[jax.ShapeDtypeStruct((S, W), BF16), jax.ShapeDtypeStruct((S, W), BF16),
                   jax.ShapeDtypeStruct((2, S, W), BF16)],
        compiler_params=_cparams(("parallel",)))(dm, gates, pa, pr)


def _swiglu_fwd(uv, *, name):
    _, S, W = uv.shape
    tr = 256

    def body(uv_ref, o_ref, ot_ref):
        u = uv_ref[0]
        hh = u * jax.nn.sigmoid(u) * uv_ref[1]
        o_ref[...] = hh.astype(BF16)
        ot_ref[...] = hh.T.astype(BF16)

    return pl.pallas_call(
        body, name=name, grid=(S // tr,),
        in_specs=[pl.BlockSpec((2, tr, W), lambda i: (0, i, 0))],
        out_specs=[_row_spec(tr, W), _col_spec(W, tr)],
        out_shape=[jax.ShapeDtypeStruct((S, W), BF16), jax.ShapeDtypeStruct((W, S), BF16)],
        compiler_params=_cparams(("parallel",)))(uv)


def _swiglu_bwd(dh, uv, *, name):
    _, S, W = uv.shape
    tr = 256

    def body(dh_ref, uv_ref, o_ref):
        u, v, d = uv_ref[0], uv_ref[1], dh_ref[...]
        sg = jax.nn.sigmoid(u)
        o_ref[:, 0:W] = (d * v * (sg * (1.0 + u * (1.0 - sg)))).astype(BF16)
        o_ref[:, W:2 * W] = (d * (u * sg)).astype(BF16)

    return pl.pallas_call(
        body, name=name, grid=(S // tr,),
        in_specs=[_row_spec(tr, W), pl.BlockSpec((2, tr, W), lambda i: (0, i, 0))],
        out_specs=_row_spec(tr, 2 * W), out_shape=jax.ShapeDtypeStruct((S, 2 * W), BF16),
        compiler_params=_cparams(("parallel",)))(dh, uv)


def _assemble_dz(da, dq_r, dk_r, dv_r, dg_r, dgates, *, name):
    S = dv_r.shape[0]
    tr = 256
    GW = GROUP_WIDTH
    NCH = GW // LANES

    def body(*refs):
        a_refs = refs[0:9]
        q_ref, k_ref, v_ref, g_ref, gt_ref, dz_ref, cs_ref = refs[9:]

        @pl.when(pl.program_id(0) == 0)
        def _():
            cs_ref[...] = jnp.zeros_like(cs_ref)

        def put(off, val):
            w = val.shape[-1]
            dz_ref[:, off:off + w] = val.astype(BF16)
            cs_ref[:, off:off + w] += jnp.sum(val.astype(F32), axis=0, keepdims=True)

        for which in range(3):
            for gi in range(3):
                for c in range(NCH):
                    put(which * ATTN_W + gi * GW + c * LANES, a_refs[3 * gi + which][c])
        off = 3 * ATTN_W
        put(off, q_ref[...])
        put(off + 1024, k_ref[...])
        put(off + 2048, v_ref[...])
        put(off + 4096, g_ref[...])
        put(off + 6144, gt_ref[0])
        put(off + 7168, gt_ref[1])

    flat_a = [t for grp in da for t in grp]
    return pl.pallas_call(
        body, name=name, grid=(S // tr,),
        in_specs=[pl.BlockSpec((NCH, tr, LANES), lambda i: (0, i, 0))] * 9 + [_row_spec(tr, 1024), _row_spec(tr, 1024),
                  _row_spec(tr, 2048), _row_spec(tr, 2048), pl.BlockSpec((2, tr, 1024), lambda i: (0, i, 0))],
        out_specs=[_row_spec(tr, IN_COLS), _vec_spec(IN_COLS)],
        out_shape=[jax.ShapeDtypeStruct((S, IN_COLS), BF16), jax.ShapeDtypeStruct((1, IN_COLS), F32)],
        compiler_params=_cparams(("arbitrary",)))(*flat_a, dq_r, dk_r, dv_r, dg_r, dgates)


def _t5_bucket(dist):
    max_exact = NUM_BUCKETS // 2
    large = max_exact + (np.log(np.maximum(dist, max_exact) / max_exact)
                         / np.log(MAX_DISTANCE / max_exact) * (NUM_BUCKETS - max_exact)).astype(np.int32)
    large = np.minimum(large, NUM_BUCKETS - 1)
    return np.where(dist < max_exact, dist, large).astype(np.int32)


def _attn_tables(dilation):
    W = ATTN_BLOCK
    qi = np.arange(W)[:, None]
    kj = np.arange(2 * W)[None, :]
    rel = qi + W - kj
    valid = (rel >= 0) & (rel <= W)
    buckets = _t5_bucket(np.clip(rel, 0, W) * dilation)
    return buckets, valid


def _attn_bias(rel_bias, gi, dilation):
    buckets, valid = _attn_tables(dilation)
    table = rel_bias[:, gi * HEADS_PER_GROUP:(gi + 1) * HEADS_PER_GROUP]
    onehot = (jnp.asarray(buckets.reshape(-1, 1)) == jnp.arange(NUM_BUCKETS)[None, :]).astype(F32)
    bias = jnp.dot(onehot, table.astype(F32), precision=lax.Precision.HIGHEST)
    bias = bias.T.reshape(HEADS_PER_GROUP, ATTN_BLOCK, 2 * ATTN_BLOCK)
    return jnp.where(jnp.asarray(valid)[None], bias, NEG)


def _dot_nt(a, b):
    return lax.dot_general(a, b, (((1,), (1,)), ((), ())), preferred_element_type=F32)


def _dot_tn(a, b):
    return lax.dot_general(a, b, (((0,), (0,)), ((), ())), preferred_element_type=F32)


def _dot(a, b):
    return jnp.dot(a, b, preferred_element_type=F32)


ATTN_RESIDUES_PER_STEP = 4
ATTN_UNITS_AT_ONCE = 8
HEADS_PER_CHUNK = LANES // HEAD_DIM
N_CHUNKS = GROUP_WIDTH // LANES


def _first_block_mask(has_prev):
    col = lax.broadcasted_iota(jnp.int32, (1, 2 * ATTN_BLOCK), 1)
    return jnp.where(jnp.logical_or(has_prev, col >= ATTN_BLOCK), 0.0, NEG).astype(F32)


def _head_lanes(hh):
    return slice(HEAD_DIM * hh, HEAD_DIM * (hh + 1))


def _attn_geometry(S, d):
    rows_per_block = ATTN_BLOCK * d
    rps = min(d, ATTN_RESIDUES_PER_STEP)
    return rows_per_block, S // rows_per_block, rps, d // rps


def _residue_rows(d, rps, rg, rr):
    if d == 1:
        return slice(None)
    return pl.ds(rg * rps + rr, ATTN_BLOCK, stride=d)


def _attn_in_specs(gi, RB, last):
    def spec(which, prev):
        if prev:
            return pl.BlockSpec((None, RB, LANES),
                                lambda j, n, rg: (9 * which + 3 * gi + j, jnp.clip(n - 1, 0, last), 0))
        return pl.BlockSpec((None, RB, LANES), lambda j, n, rg: (9 * which + 3 * gi + j, jnp.minimum(n, last), 0))
    bias = pl.BlockSpec((HEADS_PER_CHUNK, ATTN_BLOCK, 2 * ATTN_BLOCK), lambda j, n, rg: (j, 0, 0))
    return [spec(0, False), spec(1, True), spec(1, False), spec(2, True), spec(2, False), bias]


def _attn_fwd(qkv, bias, gi, d, *, name):
    _, S, _ = qkv.shape
    B = ATTN_BLOCK
    RB, nb, rps, nrg = _attn_geometry(S, d)
    scale = HEAD_DIM ** -0.5
    units = [(rr, hh) for rr in range(rps) for hh in range(HEADS_PER_CHUNK)]

    def body(q_ref, kp_ref, kc_ref, vp_ref, vc_ref, b_ref, o_ref, l_ref):
        n, rg = pl.program_id(1), pl.program_id(2)
        edge = _first_block_mask(n > 0)
        rows = [_residue_rows(d, rps, rg, rr) for rr in range(rps)]
        q = [q_ref[r_, :].astype(BF16) for r_ in rows]
        k2 = [jnp.concatenate([kp_ref[r_, :], kc_ref[r_, :]], axis=0).astype(BF16) for r_ in rows]
        v2 = [jnp.concatenate([vp_ref[r_, :], vc_ref[r_, :]], axis=0).astype(BF16) for r_ in rows]
        o_part, l_part = {}, {}
        for u0 in range(0, len(units), ATTN_UNITS_AT_ONCE):
            us = units[u0:u0 + ATTN_UNITS_AT_ONCE]
            s = [_dot_nt(q[rr][:, _head_lanes(hh)], k2[rr][:, _head_lanes(hh)]) * scale + b_ref[hh] + edge
                 for rr, hh in us]
            m = [jnp.max(x, axis=-1, keepdims=True) for x in s]
            p = [jnp.exp(x - mm) for x, mm in zip(s, m)]
            l = [jnp.sum(x, axis=-1, keepdims=True) for x in p]
            pb = [(x * (1.0 / ll)).astype(BF16) for x, ll in zip(p, l)]
            o = [_dot(x, v2[rr][:, _head_lanes(hh)]) for x, (rr, hh) in zip(pb, us)]
            for u, oo, mm, ll in zip(us, o, m, l):
                o_part[u] = oo
                l_part[u] = jnp.broadcast_to(mm + jnp.log(ll), (B, HEAD_DIM))
        for rr in range(rps):
            o_ref[rows[rr], :] = jnp.concatenate([o_part[(rr, hh)] for hh in range(HEADS_PER_CHUNK)], axis=1)
            l_ref[rows[rr], :] = jnp.concatenate([l_part[(rr, hh)] for hh in range(HEADS_PER_CHUNK)], axis=1)

    out_spec = pl.BlockSpec((None, RB, LANES), lambda j, n, rg: (j, n, 0))
    return pl.pallas_call(
        body, name=name, grid=(N_CHUNKS, nb, nrg),
        in_specs=_attn_in_specs(gi, RB, nb - 1),
        out_specs=[out_spec, out_spec],
        out_shape=[jax.ShapeDtypeStruct((N_CHUNKS, S, LANES), F32)] * 2,
        compiler_params=_cparams(("parallel", "arbitrary", "arbitrary")))(qkv, qkv, qkv, qkv, qkv, bias)


def _attn_bwd(qkv, bias, lse, dya, ya, wts, gi, d, *, name):
    _, S, _ = qkv.shape
    B = ATTN_BLOCK
    RB, nb, rps, nrg = _attn_geometry(S, d)
    scale = HEAD_DIM ** -0.5
    units = [(rr, hh) for rr in range(rps) for hh in range(HEADS_PER_CHUNK)]

    def body(q_ref, kp_ref, kc_ref, vp_ref, vc_ref, b_ref, l_ref, dya_ref, ya_ref, w_ref,
             dq_ref, dk_ref, dv_ref, db_ref, dk_carry, dv_carry):
        n, rg = pl.program_id(1), pl.program_id(2)
        rows = [_residue_rows(d, rps, rg, rr) for rr in range(rps)]

        @pl.when((n == 0) & (rg == 0))
        def _():
            db_ref[...] = jnp.zeros_like(db_ref)
            dk_carry[...] = jnp.zeros_like(dk_carry)
            dv_carry[...] = jnp.zeros_like(dv_carry)

        @pl.when(n < nb)
        def _():
            edge = _first_block_mask(n > 0)
            q = [q_ref[r_, :].astype(BF16) for r_ in rows]
            k2 = [jnp.concatenate([kp_ref[r_, :], kc_ref[r_, :]], axis=0).astype(BF16) for r_ in rows]
            v2 = [jnp.concatenate([vp_ref[r_, :], vc_ref[r_, :]], axis=0).astype(BF16) for r_ in rows]
            lse_c = [l_ref[r_, :] for r_ in rows]
            dy_c = [dya_ref[r_, :] for r_ in rows]
            ya_c = [ya_ref[r_, :] for r_ in rows]
            w_c = [w_ref[r_, :] for r_ in rows]
            ds_sum = [None] * HEADS_PER_CHUNK
            dq_part, dk_part, dv_part = {}, {}, {}
            for u0 in range(0, len(units), ATTN_UNITS_AT_ONCE):
                us = units[u0:u0 + ATTN_UNITS_AT_ONCE]
                hl = [_head_lanes(hh) for _, hh in us]
                qh = [q[rr][:, sl] for (rr, _), sl in zip(us, hl)]
                kh = [k2[rr][:, sl] for (rr, _), sl in zip(us, hl)]
                vh = [v2[rr][:, sl] for (rr, _), sl in zip(us, hl)]
                s = [_dot_nt(a, k) * scale + b_ref[hh] + edge for a, k, (_, hh) in zip(qh, kh, us)]
                p = [jnp.exp(x - lse_c[rr][:, HEAD_DIM * hh:HEAD_DIM * hh + 1]) for x, (rr, hh) in zip(s, us)]
                dy = [dy_c[rr][:, sl] for (rr, _), sl in zip(us, hl)]
                w = [w_c[rr][:, sl] for (rr, _), sl in zip(us, hl)]
                shift = [ww[:, 0:1] * jnp.sum(d_ * ya_c[rr][:, sl], axis=-1, keepdims=True)
                         for ww, d_, (rr, _), sl in zip(w, dy, us, hl)]
                do = [(ww * d_).astype(BF16) for ww, d_ in zip(w, dy)]
                ds = [pp * (_dot_nt(o_, v) - sh) for pp, o_, v, sh in zip(p, do, vh, shift)]
                for x, (_, hh) in zip(ds, us):
                    ds_sum[hh] = x if ds_sum[hh] is None else ds_sum[hh] + x
                dsb = [x.astype(BF16) for x in ds]
                pb = [x.astype(BF16) for x in p]
                for u, x, pp, a, k, o_ in zip(us, dsb, pb, qh, kh, do):
                    dq_part[u] = _dot(x, k) * scale
                    dk_part[u] = _dot_tn(x, a) * scale
                    dv_part[u] = _dot_tn(pp, o_)
            for hh in range(HEADS_PER_CHUNK):
                db_ref[hh] += ds_sum[hh]
            for rr in range(rps):
                r_ = rows[rr]
                dq_ref[r_, :] = jnp.concatenate([dq_part[(rr, hh)] for hh in range(HEADS_PER_CHUNK)], axis=1)
                dk2 = jnp.concatenate([dk_part[(rr, hh)] for hh in range(HEADS_PER_CHUNK)], axis=1)
                dv2 = jnp.concatenate([dv_part[(rr, hh)] for hh in range(HEADS_PER_CHUNK)], axis=1)
                dk_ref[r_, :] = dk_carry[r_, :] + dk2[0:B]
                dv_ref[r_, :] = dv_carry[r_, :] + dv2[0:B]
                dk_carry[r_, :] = dk2[B:2 * B]
                dv_carry[r_, :] = dv2[B:2 * B]

        @pl.when(n == nb)
        def _():
            for r_ in rows:
                dk_ref[r_, :] = dk_carry[r_, :]
                dv_ref[r_, :] = dv_carry[r_, :]

    last = nb - 1
    cur = pl.BlockSpec((None, RB, LANES), lambda j, n, rg: (j, jnp.minimum(n, last), 0))
    lag = pl.BlockSpec((None, RB, LANES), lambda j, n, rg: (j, jnp.maximum(n - 1, 0), 0))
    db_spec = pl.BlockSpec((HEADS_PER_CHUNK, B, 2 * B), lambda j, n, rg: (j, 0, 0))
    dq, dk, dv, db = pl.pallas_call(
        body, name=name, grid=(N_CHUNKS, nb + 1, nrg),
        in_specs=_attn_in_specs(gi, RB, last) + [cur, cur, cur, cur],
        out_specs=[cur, lag, lag, db_spec],
        out_shape=[jax.ShapeDtypeStruct((N_CHUNKS, S, LANES), F32)] * 3
        + [jax.ShapeDtypeStruct((HEADS_PER_GROUP, B, 2 * B), F32)],
        scratch_shapes=[pltpu.VMEM((RB, LANES), F32), pltpu.VMEM((RB, LANES), F32)],
        compiler_params=_cparams(("arbitrary", "arbitrary", "arbitrary")))(
            qkv, qkv, qkv, qkv, qkv, bias, lse, dya, ya, wts)
    return (dq, dk, dv), db


def _bias_grad(dbs, *, name):
    nk = ATTN_BLOCK * 2 * ATTN_BLOCK
    buckets = []
    for (_, dil) in ATTN_GROUPS:
        b, valid = _attn_tables(dil)
        buckets.append(np.where(valid, b, -1).reshape(1, nk))
    bk = jnp.asarray(np.stack(buckets).astype(np.int32))
    flat = [x.reshape(HEADS_PER_GROUP, nk) for x in dbs]

    def body(bk_ref, d0, d1, d2, o_ref):
        ids = lax.broadcasted_iota(jnp.int32, (NUM_BUCKETS, nk), 0)
        for gi, dref in enumerate((d0, d1, d2)):
            onehot = (ids == bk_ref[gi]).astype(F32)
            o_ref[gi] = lax.dot_general(onehot, dref[...], (((1,), (1,)), ((), ())),
                                        preferred_element_type=F32, precision=lax.Precision.HIGHEST)

    out = pl.pallas_call(
        body, name=name,
        out_shape=jax.ShapeDtypeStruct((3, NUM_BUCKETS, HEADS_PER_GROUP), F32),
        compiler_params=_cparams())(bk, *flat)
    return jnp.transpose(out, (1, 0, 2)).reshape(NUM_BUCKETS, 3 * HEADS_PER_GROUP)


def _ret_tables(S):
    half = RET_QK_DIM // 2
    pos = jnp.arange(S, dtype=F32)
    inv_freq = ROPE_BASE ** (-jnp.arange(half, dtype=F32) / half)
    ang = pos[:, None] * inv_freq[None]
    cos, sin = jnp.cos(ang), jnp.sin(ang)
    H, C = RET_HEADS, RET_CHUNK
    log_g = jnp.log(1.0 - 2.0 ** (-5.0 - jnp.arange(H, dtype=F32)))
    n = jnp.arange(C, dtype=F32)
    diff = n[:, None] - n[None, :]
    dmask = jnp.where(diff >= 0, jnp.exp(log_g[:, None, None] * jnp.maximum(diff, 0.0)), 0.0)
    q_dec = jnp.exp(log_g[:, None] * (n + 1.0))
    k_dec = jnp.exp(log_g[:, None] * (C - 1.0 - n))
    chunk_dec = jnp.exp(log_g * C)
    qd = jnp.broadcast_to(q_dec[:, :, None], (H, C, RET_QK_DIM))
    kd = jnp.broadcast_to(k_dec[:, :, None], (H, C, RET_QK_DIM))
    cd = jnp.broadcast_to(chunk_dec[:, None, None], (H, 1, RET_V_DIM))
    return cos, sin, dmask, qd, kd, cd


def _rot(t, cos, sin):
    half = RET_QK_DIM // 2
    t1, t2 = t[:, :half], t[:, half:]
    return jnp.concatenate([t1 * cos - t2 * sin, t1 * sin + t2 * cos], axis=-1)


def _unrot(t, cos, sin):
    half = RET_QK_DIM // 2
    t1, t2 = t[:, :half], t[:, half:]
    return jnp.concatenate([t1 * cos + t2 * sin, t2 * cos - t1 * sin], axis=-1)


def _ret_specs(rev, nC):
    C, DK, DV = RET_CHUNK, RET_QK_DIM, RET_V_DIM
    cidx = (lambda c: nC - 1 - c) if rev else (lambda c: c)
    H = RET_HEADS
    return dict(
        qk=lambda which: pl.BlockSpec((None, C, H * DK), lambda c: (which, cidx(c), 0)),
        q=pl.BlockSpec((C, H * DK), lambda c: (cidx(c), 0)),
        v=pl.BlockSpec((C, H * DV), lambda c: (cidx(c), 0)),
        cs=pl.BlockSpec((C, DK // 2), lambda c: (cidx(c), 0)),
        dmask=pl.BlockSpec((H, C, C), lambda c: (0, 0, 0)),
        dec=pl.BlockSpec((H, C, DK), lambda c: (0, 0, 0)),
        cd=pl.BlockSpec((H, 1, DV), lambda c: (0, 0, 0)),
        st=pl.BlockSpec((H, None, DK, DV), lambda c: (0, cidx(c), 0, 0)),
    )


def _ret_fwd(qk, v, g, tables, *, name):
    _, S, _ = qk.shape
    nC = S // RET_CHUNK
    C, DK, DV, H = RET_CHUNK, RET_QK_DIM, RET_V_DIM, RET_HEADS
    cos, sin, dmask, qd, kd, cd = tables
    kscale = DK ** -0.5

    def body(q_ref, k_ref, v_ref, g_ref, cos_ref, sin_ref, dm_ref, qd_ref, kd_ref, cd_ref,
             o_ref, yb_ref, ybt_ref, st_ref, state):
        @pl.when(pl.program_id(0) == 0)
        def _():
            state[...] = jnp.zeros_like(state)

        tcol = pl.multiple_of((pl.program_id(0) % RET_T_CHUNKS) * C, C)
        cs, sn = cos_ref[...], sin_ref[...]
        for h in range(H):
            qs, vs = slice(DK * h, DK * (h + 1)), slice(DV * h, DV * (h + 1))
            Q = _rot(q_ref[:, qs], cs, sn)
            K = _rot(k_ref[:, qs], cs, sn) * kscale
            Qb, Kb, V = Q.astype(BF16), K.astype(BF16), v_ref[:, vs]
            sb = state[h].astype(BF16)
            st_ref[h] = sb
            A = _dot_nt(Qb, Kb) * dm_ref[h]
            o = _dot(A.astype(BF16), V) + _dot((Q * qd_ref[h]).astype(BF16), sb)
            state[h] = state[h] * cd_ref[h] + _dot_tn((K * kd_ref[h]).astype(BF16), V)
            mu = jnp.mean(o, axis=-1, keepdims=True)
            dd = o - mu
            var = jnp.mean(dd * dd, axis=-1, keepdims=True)
            yn = dd * lax.rsqrt(var + GN_EPS)
            gv = g_ref[:, vs]
            yb = gv * jax.nn.sigmoid(gv) * yn
            o_ref[:, vs] = o
            yb_ref[:, vs] = yb.astype(BF16)
            ybt_ref[vs, pl.ds(tcol, C)] = yb.T.astype(BF16)

    sp = _ret_specs(False, nC)
    return pl.pallas_call(
        body, name=name, grid=(nC,),
        in_specs=[sp["qk"](0), sp["qk"](1), sp["v"], sp["v"], sp["cs"], sp["cs"], sp["dmask"],
                  sp["dec"], sp["dec"], sp["cd"]],
        out_specs=[sp["v"], sp["v"], pl.BlockSpec((H * DV, RET_T_CHUNKS * C), lambda c: (0, c // RET_T_CHUNKS)),
                   sp["st"]],
        out_shape=[jax.ShapeDtypeStruct((S, H * DV), F32), jax.ShapeDtypeStruct((S, H * DV), BF16),
                   jax.ShapeDtypeStruct((H * DV, S), BF16), jax.ShapeDtypeStruct((H, nC, DK, DV), BF16)],
        scratch_shapes=[pltpu.VMEM((H, DK, DV), F32)],
        compiler_params=_cparams(("arbitrary",)))(qk, qk, v, g, cos, sin, dmask, qd, kd, cd)


def _ret_bwd(dyb, qk, v, g, o, states, tables, *, name):
    _, S, _ = qk.shape
    nC = S // RET_CHUNK
    C, DK, DV, H = RET_CHUNK, RET_QK_DIM, RET_V_DIM, RET_HEADS
    cos, sin, dmask, qd, kd, cd = tables
    kscale = DK ** -0.5

    def body(dy_ref, q_ref, k_ref, v_ref, g_ref, o_ref, st_ref, cos_ref, sin_ref, dm_ref, qd_ref, kd_ref,
             cd_ref, dq_ref, dk_ref, dv_ref, dg_ref, dstate):
        @pl.when(pl.program_id(0) == 0)
        def _():
            dstate[...] = jnp.zeros_like(dstate)

        cs, sn = cos_ref[...], sin_ref[...]
        for h in range(H):
            qs, vs = slice(DK * h, DK * (h + 1)), slice(DV * h, DV * (h + 1))
            ov = o_ref[:, vs]
            mu = jnp.mean(ov, axis=-1, keepdims=True)
            dd = ov - mu
            var = jnp.mean(dd * dd, axis=-1, keepdims=True)
            rstd = lax.rsqrt(var + GN_EPS)
            yn = dd * rstd
            gv, dy = g_ref[:, vs], dy_ref[:, vs]
            sg = jax.nn.sigmoid(gv)
            dg_ref[:, vs] = (dy * yn * (sg * (1.0 + gv * (1.0 - sg)))).astype(BF16)
            dyn = dy * (gv * sg)
            dO = rstd * (dyn - jnp.mean(dyn, axis=-1, keepdims=True)
                         - yn * jnp.mean(dyn * yn, axis=-1, keepdims=True))
            dOb = dO.astype(BF16)

            Q = _rot(q_ref[:, qs], cs, sn)
            K = _rot(k_ref[:, qs], cs, sn) * kscale
            Qb, Kb, V = Q.astype(BF16), K.astype(BF16), v_ref[:, vs]
            dm, qd_h, kd_h = dm_ref[h], qd_ref[h], kd_ref[h]
            Sb = st_ref[h]
            dSb = dstate[h].astype(BF16)
            Ab = (_dot_nt(Qb, Kb) * dm).astype(BF16)
            dAb = (_dot_nt(dOb, V) * dm).astype(BF16)
            Qd = (Q * qd_h).astype(BF16)
            Kd = (K * kd_h).astype(BF16)
            dQ = _dot(dAb, Kb) + _dot_nt(dOb, Sb) * qd_h
            dK = _dot_tn(dAb, Qb) + _dot_nt(V, dSb) * kd_h
            dv_ref[:, vs] = (_dot_tn(Ab, dOb) + _dot(Kd, dSb)).astype(BF16)
            dstate[h] = dstate[h] * cd_ref[h] + _dot_tn(Qd, dOb)
            dq_ref[:, qs] = _unrot(dQ, cs, sn).astype(BF16)
            dk_ref[:, qs] = (_unrot(dK, cs, sn) * kscale).astype(BF16)

    sp = _ret_specs(True, nC)
    dq, dk, dv, dg = pl.pallas_call(
        body, name=name, grid=(nC,),
        in_specs=[sp["v"], sp["qk"](0), sp["qk"](1), sp["v"], sp["v"], sp["v"], sp["st"], sp["cs"], sp["cs"],
                  sp["dmask"], sp["dec"], sp["dec"], sp["cd"]],
        out_specs=[sp["q"], sp["q"], sp["v"], sp["v"]],
        out_shape=[jax.ShapeDtypeStruct((S, H * DK), BF16), jax.ShapeDtypeStruct((S, H * DK), BF16),
                   jax.ShapeDtypeStruct((S, H * DV), BF16), jax.ShapeDtypeStruct((S, H * DV), BF16)],
        scratch_shapes=[pltpu.VMEM((H, DK, DV), F32)],
        compiler_params=_cparams(("arbitrary",)))(dyb, qk, qk, v, g, o, states, cos, sin, dmask, qd, kd, cd)
    return dq, dk, dv, dg


def _layer_fwd(l, x, xb, x_t, W, b_in, biases, ln, tables):
    S = x.shape[0]
    tag = f"l{l}"
    win = W["w_in"]
    c0, c1, c2, c3, c4 = 3 * ATTN_W, 3 * ATTN_W + 2048, 3 * ATTN_W + 4096, 3 * ATTN_W + 6144, IN_COLS
    qkv_a = _mm(xb, win[:, :c0], bias=b_in[:c0], groups=3, lane_chunks=True, name=f"{tag}_in_attn")
    qk_r = _mm(xb, win[:, c0:c1], bias=b_in[c0:c1], groups=2, name=f"{tag}_in_retqk")
    v_r = _mm(xb, win[:, c1:c2], bias=b_in[c1:c2], out_dtype=BF16, name=f"{tag}_in_retv")
    g_r = _mm(xb, win[:, c2:c3], bias=b_in[c2:c3], name=f"{tag}_in_retg")
    gates = _mm(xb, win[:, c3:c4], bias=b_in[c3:c4], groups=2, name=f"{tag}_in_gates")

    os_, ls_ = [], []
    for gi, (_, dil) in enumerate(ATTN_GROUPS):
        o, lse = _attn_fwd(qkv_a, biases[gi], gi, dil, name=f"{tag}_attn_fwd{gi}")
        os_.append(o)
        ls_.append(lse)
    ya_b, ya_t, ya, wts = _combine_fwd(os_, ls_, name=f"{tag}_combine")

    o_r, yb, yb_t, states = _ret_fwd(qk_r, v_r, g_r, tables, name=f"{tag}_ret_fwd")

    pa = _mm(ya_b, W["w_attn_proj"], name=f"{tag}_attn_proj")
    pr = _mm(yb, W["w_ret_proj"], name=f"{tag}_ret_proj")
    merged, merged_t = _merge_fwd(gates, pa, pr, name=f"{tag}_merge")
    mix = _mm(merged, W["w_out"], name=f"{tag}_out_proj")
    h1, x1, x1b, x1_t = _ln_fwd(x, mix, ln["ln1_g"], ln["ln1_b"], name=f"{tag}_ln1")
    uv = _mm(x1b, W["w_gu"], groups=2, name=f"{tag}_ffn_in")
    hh, hh_t = _swiglu_fwd(uv, name=f"{tag}_swiglu")
    f = _mm(hh, W["w_ffn_down"], name=f"{tag}_ffn_down")
    h2, x2, x2b, x2_t = _ln_fwd(x1, f, ln["ln2_g"], ln["ln2_b"], name=f"{tag}_ln2")
    saved = dict(x_t=x_t, qkv_a=qkv_a, qk_r=qk_r, v_r=v_r, g_r=g_r, gates=gates, ls=ls_, ya_t=ya_t, ya=ya,
                 wts=wts, o_r=o_r, yb_t=yb_t, states=states, pa=pa, pr=pr, merged_t=merged_t, h1=h1, x1_t=x1_t,
                 uv=uv, hh_t=hh_t, h2=h2)
    return x2, x2b, x2_t, saved


def _layer_bwd(l, dx2, sv, W, biases, ln, tables):
    S = dx2.shape[0]
    tag = f"l{l}"
    g = {}
    dh2b, res2, g["ln2_g"], g["ln2_b"] = _ln_bwd(dx2, sv["h2"], ln["ln2_g"], name=f"{tag}_ln2_bwd")
    dhh = _mm(dh2b, W["w_ffn_down"], transpose_b=True, name=f"{tag}_d_hh")
    g["w_ffn_down"] = _mm(sv["hh_t"], dh2b, name=f"{tag}_dw_down")
    dudv = _swiglu_bwd(dhh, sv["uv"], name=f"{tag}_swiglu_bwd")
    dx1 = _mm(dudv, W["w_gu"], transpose_b=True, add=res2, name=f"{tag}_d_x1")
    dwgu = _mm(sv["x1_t"], dudv, name=f"{tag}_dw_gu")
    g["w_ffn_gate"], g["w_ffn_up"] = dwgu[:, :D_FF], dwgu[:, D_FF:]

    dh1b, res1, g["ln1_g"], g["ln1_b"] = _ln_bwd(dx1, sv["h1"], ln["ln1_g"], name=f"{tag}_ln1_bwd")
    dmerged = _mm(dh1b, W["w_out"], transpose_b=True, name=f"{tag}_d_merged")
    g["w_out"] = _mm(sv["merged_t"], dh1b, name=f"{tag}_dw_out")
    dpa, dpr, dgates = _merge_bwd(dmerged, sv["gates"], sv["pa"], sv["pr"], name=f"{tag}_merge_bwd")
    dya = _mm(dpa, W["w_attn_proj"], transpose_b=True, groups=1, lane_chunks=True, name=f"{tag}_d_ya")
    g["w_attn_proj"] = _mm(sv["ya_t"], dpa, name=f"{tag}_dw_ap")
    dyb = _mm(dpr, W["w_ret_proj"], transpose_b=True, name=f"{tag}_d_yb")
    g["w_ret_proj"] = _mm(sv["yb_t"], dpr, name=f"{tag}_dw_rp")

    da, dbs = [], []
    for gi, (_, dil) in enumerate(ATTN_GROUPS):
        dqkv, db = _attn_bwd(sv["qkv_a"], biases[gi], sv["ls"][gi], dya, sv["ya"], sv["wts"][gi], gi, dil,
                             name=f"{tag}_attn_bwd{gi}")
        da.append(dqkv)
        dbs.append(db)
    dq_r, dk_r, dv_r, dg_r = _ret_bwd(dyb, sv["qk_r"], sv["v_r"], sv["g_r"], sv["o_r"], sv["states"], tables,
                                 name=f"{tag}_ret_bwd")
    dz, colsum = _assemble_dz(da, dq_r, dk_r, dv_r, dg_r, dgates, name=f"{tag}_assemble_dz")
    g["b_in"] = colsum.reshape(IN_COLS)
    dx = _mm(dz, W["w_in"], transpose_b=True, add=res1, name=f"{tag}_d_x")
    g["w_in"] = _mm(sv["x_t"], dz, name=f"{tag}_dw_in")
    return dx, g, dbs


HBM_SPEC = pl.BlockSpec(memory_space=pltpu.HBM)
OTHER_CHIPS = ((1, 0), (0, 1), (1, 1))


def _flip(v, f):
    return 1 - v if f else v


def _all_gather(shards, *, name):
    n = len(shards)

    def body(*refs):
        x_refs, out_refs = refs[:n], refs[n:2 * n]
        send_sems, recv_sems, local_sems = refs[2 * n:]
        x, y, c = lax.axis_index("x"), lax.axis_index("y"), lax.axis_index("c")
        me, sibling = (x, y, c), (x, y, 1 - c)
        chips = [(_flip(x, fx), _flip(y, fy)) for fx, fy in OTHER_CHIPS]

        def copy(a, k, block, to, src=None):
            px, py, pc = block
            rows = out_refs[a].at[4 * px + 2 * py + pc]
            return pltpu.make_async_remote_copy(
                src_ref=rows if src is None else src, dst_ref=rows,
                send_sem=send_sems.at[7 * a + k], recv_sem=recv_sems.at[7 * a + k], device_id=to, device_id_type=MESH)

        mine, first, passed = [], [], []
        for a in range(n):
            cp = pltpu.make_async_copy(x_refs[a], out_refs[a].at[4 * x + 2 * y + c], local_sems.at[a])
            cp.start()
            mine.append(cp)
            first.append(copy(a, 0, me, sibling, src=x_refs[a]))
            first += [copy(a, 1 + j, me, (*chip, c), src=x_refs[a]) for j, chip in enumerate(chips)]
        for cp in first:
            cp.start()
        for j, chip in enumerate(chips):
            for a in range(n):
                copy(a, 1 + j, (*chip, c), me).wait_recv()
                cp = copy(a, 4 + j, (*chip, c), sibling)
                cp.start()
                passed.append(cp)
        for a in range(n):
            copy(a, 0, sibling, me).wait_recv()
            for j, chip in enumerate(chips):
                copy(a, 4 + j, (*chip, 1 - c), me).wait_recv()
        for cp in first + passed:
            cp.wait_send()
        for cp in mine:
            cp.wait()

    return pl.pallas_call(
        body, name=name, out_shape=[jax.ShapeDtypeStruct((N_DEV,) + s.shape, s.dtype) for s in shards],
        in_specs=[HBM_SPEC] * n, out_specs=[HBM_SPEC] * n,
        scratch_shapes=[pltpu.SemaphoreType.DMA((7 * n,)), pltpu.SemaphoreType.DMA((7 * n,)),
                        pltpu.SemaphoreType.DMA((n,))],
    )(*shards)


def _rs_sibling_exchange(g8s, *, name):
    n = len(g8s)

    def body(*refs):
        g_refs, recv_refs = refs[:n], refs[n:2 * n]
        send_sems, recv_sems = refs[2 * n:]
        x, y, c = lax.axis_index("x"), lax.axis_index("y"), lax.axis_index("c")
        copies = []
        for a in range(n):
            for k in range(4):
                cp = pltpu.make_async_remote_copy(
                    src_ref=g_refs[a].at[k, 1 - c], dst_ref=recv_refs[a].at[k], send_sem=send_sems.at[4 * a + k],
                    recv_sem=recv_sems.at[4 * a + k], device_id=(x, y, 1 - c), device_id_type=MESH)
                cp.start()
                copies.append(cp)
        for cp in copies:
            cp.wait()

    return pl.pallas_call(
        body, name=name,
        out_shape=[jax.ShapeDtypeStruct((4,) + g.shape[2:], g.dtype) for g in g8s],
        in_specs=[HBM_SPEC] * n, out_specs=[HBM_SPEC] * n,
        scratch_shapes=[pltpu.SemaphoreType.DMA((4 * n,)), pltpu.SemaphoreType.DMA((4 * n,))],
    )(*g8s)


def _rs_chip_sum(g8, recv, core, *, name):
    _, _, R, Wd = g8.shape
    tr = _div_tile(R, 256, 16)

    def body(core_ref, g_ref, r_ref, o_ref):
        o_ref[...] = (g_ref[...] + r_ref[...]).astype(BF16)

    grid_spec = pltpu.PrefetchScalarGridSpec(
        num_scalar_prefetch=1, grid=(4, R // tr),
        in_specs=[pl.BlockSpec((None, None, tr, Wd), lambda k, i, core_ref: (k, core_ref[0], i, 0)),
                  pl.BlockSpec((None, tr, Wd), lambda k, i, core_ref: (k, i, 0))],
        out_specs=pl.BlockSpec((None, tr, Wd), lambda k, i, core_ref: (k, i, 0)))
    return pl.pallas_call(
        body, name=name, grid_spec=grid_spec, out_shape=jax.ShapeDtypeStruct((4, R, Wd), BF16),
        compiler_params=_cparams(("parallel", "parallel")))(core, g8, recv)


def _rs_chip_exchange(ps, *, name):
    n = len(ps)

    def body(*refs):
        p_refs, out_refs = refs[:n], refs[n:2 * n]
        send_sems, recv_sems, local_sems = refs[2 * n:]
        x, y, c = lax.axis_index("x"), lax.axis_index("y"), lax.axis_index("c")
        my_chip = 2 * x + y
        copies = []
        for a in range(n):
            mine = pltpu.make_async_copy(p_refs[a].at[my_chip], out_refs[a].at[my_chip], local_sems.at[a])
            mine.start()
            copies.append(mine)
            for j, (fx, fy) in enumerate(OTHER_CHIPS):
                px, py = _flip(x, fx), _flip(y, fy)
                cp = pltpu.make_async_remote_copy(
                    src_ref=p_refs[a].at[2 * px + py], dst_ref=out_refs[a].at[my_chip],
                    send_sem=send_sems.at[3 * a + j], recv_sem=recv_sems.at[3 * a + j],
                    device_id=(px, py, c), device_id_type=MESH)
                cp.start()
                copies.append(cp)
        for cp in copies:
            cp.wait()

    return pl.pallas_call(
        body, name=name, out_shape=[jax.ShapeDtypeStruct(p.shape, p.dtype) for p in ps],
        in_specs=[HBM_SPEC] * n, out_specs=[HBM_SPEC] * n,
        scratch_shapes=[pltpu.SemaphoreType.DMA((3 * n,)), pltpu.SemaphoreType.DMA((3 * n,)),
                        pltpu.SemaphoreType.DMA((n,))],
    )(*ps)


SEM_SPEC = pl.BlockSpec(memory_space=pltpu.SEMAPHORE)
DATAFLOW = pltpu.SideEffectType.DATAFLOW_SIDE_EFFECTING


def _direct_copies(src_refs, land_refs, send_sems, recv_sems, per_peer):
    x, y, c = lax.axis_index("x"), lax.axis_index("y"), lax.axis_index("c")
    me = 4 * x + 2 * y + c
    copies = []
    for a, (s, l) in enumerate(zip(src_refs, land_refs)):
        for rel in range(1, N_DEV):
            px, py, pc = _flip(x, rel & 4), _flip(y, rel & 2), _flip(c, rel & 1)
            copies.append(pltpu.make_async_remote_copy(
                src_ref=s.at[4 * px + 2 * py + pc] if per_peer else s, dst_ref=l.at[me],
                send_sem=send_sems.at[7 * a + rel - 1], recv_sem=recv_sems.at[7 * a + rel - 1],
                device_id=(px, py, pc), device_id_type=MESH))
    return copies


def _exchange_start(srcs, per_peer, *, name):
    n = len(srcs)
    lands = [lax.empty((N_DEV,) + (s.shape[1:] if per_peer else s.shape), s.dtype) for s in srcs]
    operands = [pltpu.with_memory_space_constraint(t, pltpu.HBM) for t in list(srcs) + lands]

    def body(*refs):
        src_refs, land_refs = refs[:n], refs[n:2 * n]
        send_sems, recv_sems = refs[2 * n], refs[2 * n + 1]
        token = refs[-1]
        for cp in _direct_copies(src_refs, land_refs, send_sems, recv_sems, per_peer):
            cp.start()
        token[...] = jnp.zeros_like(token)

    return pl.pallas_call(
        body, name=name,
        out_shape=(pltpu.SemaphoreType.DMA((7 * n,)), pltpu.SemaphoreType.DMA((7 * n,)),
                   *[pltpu.HBM(t.shape, t.dtype) for t in operands], jax.ShapeDtypeStruct((8, LANES), F32)),
        in_specs=[HBM_SPEC] * (2 * n),
        out_specs=(SEM_SPEC, SEM_SPEC, *[HBM_SPEC] * (2 * n), pl.BlockSpec(memory_space=pltpu.VMEM)),
        input_output_aliases={i: 2 + i for i in range(2 * n)},
        compiler_params=pltpu.CompilerParams(has_side_effects=DATAFLOW))(*operands)


def _exchange_wait(started, after, per_peer, *, name):
    n = (len(started) - 3) // 2
    send_sems, recv_sems = started[0], started[1]
    thru = list(started[2:2 + 2 * n])

    def body(*refs):
        src_refs, land_refs = refs[:n], refs[n:2 * n]
        send_s, recv_s = refs[2 * n], refs[2 * n + 1]
        for cp in _direct_copies(src_refs, land_refs, send_s, recv_s, per_peer):
            cp.wait_send()
            cp.wait_recv()

    outs = pl.pallas_call(
        body, name=name, out_shape=tuple(pltpu.HBM(t.shape, t.dtype) for t in thru),
        in_specs=[HBM_SPEC] * (2 * n) + [SEM_SPEC, SEM_SPEC, pl.BlockSpec(memory_space=pl.ANY)],
        out_specs=[HBM_SPEC] * (2 * n), input_output_aliases={i: i for i in range(2 * n)},
        compiler_params=pltpu.CompilerParams(has_side_effects=DATAFLOW))(*thru, send_sems, recv_sems, after)
    return list(outs[n:])


def _all_reduce_small(v, *, name):
    R, Wd = v.shape

    def body(v_ref, out_ref, slots, send_sems, recv_sems):
        x, y, c = lax.axis_index("x"), lax.axis_index("y"), lax.axis_index("c")
        me = 4 * x + 2 * y + c
        slots[me] = v_ref[...]
        copies = []
        for rel in range(1, N_DEV):
            peer = (_flip(x, rel & 4), _flip(y, rel & 2), _flip(c, rel & 1))
            cp = pltpu.make_async_remote_copy(
                src_ref=v_ref, dst_ref=slots.at[me], send_sem=send_sems.at[rel - 1],
                recv_sem=recv_sems.at[rel - 1], device_id=peer, device_id_type=MESH)
            cp.start()
            copies.append(cp)
        for cp in copies:
            cp.wait()
        acc = slots[0]
        for j in range(1, N_DEV):
            acc = acc + slots[j]
        out_ref[...] = acc

    vm = pl.BlockSpec(memory_space=pltpu.VMEM)
    return pl.pallas_call(
        body, name=name, out_shape=jax.ShapeDtypeStruct((R, Wd), F32),
        in_specs=[vm], out_specs=vm,
        scratch_shapes=[pltpu.VMEM((N_DEV, R, Wd), F32), pltpu.SemaphoreType.DMA((7,)),
                        pltpu.SemaphoreType.DMA((7,))],
    )(v)


def _adam_math(w, g, m, v):
    m2 = ADAM_B1 * m + (1.0 - ADAM_B1) * g
    v2 = ADAM_B2 * v + (1.0 - ADAM_B2) * (g * g)
    m_hat = m2 / (1.0 - ADAM_B1 ** ADAM_STEP)
    v_hat = v2 / (1.0 - ADAM_B2 ** ADAM_STEP)
    delta = -ADAM_LR * (m_hat / (jnp.sqrt(v_hat) + ADAM_EPS) + ADAM_WD * w)
    return delta, m2, v2


def _adam_sharded(parts, w, m, v, *, name):
    _, R, Wd = w.shape
    tr = _div_tile(R, 256, 16)

    def body(p0_ref, p1_ref, w_ref, m_ref, v_ref, g_ref, d_ref, m2_ref, v2_ref):
        def slot_sum(p_ref):
            g = p_ref[0].astype(F32)
            for s in range(1, p_ref.shape[0]):
                g = g + p_ref[s].astype(F32)
            return g

        g = jnp.where(pl.program_id(0) == 0, slot_sum(p0_ref), slot_sum(p1_ref))
        delta, m2, v2 = _adam_math(w_ref[...], g, m_ref[...], v_ref[...])
        g_ref[...] = g
        d_ref[...] = delta
        m2_ref[...] = m2
        v2_ref[...] = v2

    assert DEPTH == 2
    p_specs = [pl.BlockSpec((p.shape[0], tr, Wd), lambda l, i: (0, i, 0)) for p in parts]
    s_spec = pl.BlockSpec((None, tr, Wd), lambda l, i: (l, i, 0))
    return pl.pallas_call(
        body, name=name, grid=(DEPTH, R // tr),
        in_specs=p_specs + [s_spec, s_spec, s_spec],
        out_specs=[s_spec] * 4, out_shape=[jax.ShapeDtypeStruct((DEPTH, R, Wd), F32)] * 4,
        compiler_params=_cparams(("parallel", "parallel")))(parts[0], parts[1], w, m, v)


def _adam_small(g, w, m, v, *, name):
    R, Wd = w.shape

    def body(g_ref, w_ref, m_ref, v_ref, d_ref, m2_ref, v2_ref):
        delta, m2, v2 = _adam_math(w_ref[...], g_ref[...], m_ref[...], v_ref[...])
        d_ref[...] = delta
        m2_ref[...] = m2
        v2_ref[...] = v2

    return pl.pallas_call(
        body, name=name, out_shape=[jax.ShapeDtypeStruct((R, Wd), F32)] * 3,
        compiler_params=_cparams())(g, w, m, v)


def _shard_shape(name):
    r, c = FULL_SHAPE[name]
    return (r, c // N_DEV) if name in COL_SHARDED else (r // N_DEV, c)


def _full_from_gathered(name, g):
    if name in COL_SHARDED:
        return jnp.transpose(g, (1, 0, 2)).reshape(FULL_SHAPE[name])
    return g.reshape(FULL_SHAPE[name])


def _dest_major(name, gfull):
    r, c = _shard_shape(name)
    if name in COL_SHARDED:
        blk = jnp.transpose(gfull.reshape(r, N_DEV, c), (1, 0, 2))
    else:
        blk = gfull.reshape(N_DEV, r, c)
    return blk.reshape(4, 2, r, c)


def _pack_small(t):
    flat = jnp.concatenate([t[n].reshape(-1).astype(F32) for n in SMALL_WEIGHTS])
    return jnp.pad(flat, (0, SMALL_ROWS * LANES - flat.shape[0])).reshape(SMALL_ROWS, LANES)


def _unpack_small(packed):
    flat = packed.reshape(-1)
    out, off = {}, 0
    for n in SMALL_WEIGHTS:
        size = math.prod(SMALL_SHAPE[n])
        out[n] = flat[off:off + size].reshape(SMALL_SHAPE[n])
        off += size
    return out


def _after(value, token):
    return lax.optimization_barrier((value, token))[0]


def _local_step(x, target, rel_bias, b_in, lns, weights_of, grads_done=None):
    S = x.shape[0]
    tables = _ret_tables(S)
    biases = [_attn_bias(rel_bias, gi, dil) for gi, (_, dil) in enumerate(ATTN_GROUPS)]

    h = x
    hb, h_t = _cast_transpose(x, name="cast_x")
    saved, Ws = [], []
    for l in range(DEPTH):
        W = dict(weights_of(l, h))
        W["w_gu"] = jnp.concatenate([W["w_ffn_gate"], W["w_ffn_up"]], axis=1)
        Ws.append(W)
        h, hb, h_t, sv = _layer_fwd(l, h, hb, h_t, W, b_in[l], biases, lns[l], tables)
        saved.append(sv)
    dy, sq = _loss_fwd_bwd(h, target, name="loss")
    loss_local = 0.5 * sq[0, 0] / D_MODEL

    grads = [None] * DEPTH
    db_tot = None
    dx = dy
    token = None
    for l in reversed(range(DEPTH)):
        ln = dict(lns[l])
        if token is not None:
            ln["ln2_g"] = _after(ln["ln2_g"], token)
        dx, g, dbs = _layer_bwd(l, dx, saved[l], Ws[l], biases, ln, tables)
        grads[l] = g
        token = grads_done(l, g) if grads_done is not None else None
        db_tot = dbs if db_tot is None else [a + b for a, b in zip(db_tot, dbs)]
    small = {"rel_bias": _bias_grad(db_tot, name="bias_grad"),
             "b_in": jnp.stack([grads[l]["b_in"] for l in range(DEPTH)])}
    for n in ("ln1_g", "ln1_b", "ln2_g", "ln2_b"):
        small[n] = jnp.stack([grads[l][n].reshape(D_MODEL) for l in range(DEPTH)])
    return loss_local, dx, grads, small


def kernel(x, rel_bias, w_in, b_in, w_attn_proj, w_ret_proj, w_out, ln1_g, ln1_b, w_ffn_gate, w_ffn_up, w_ffn_down, ln2_g, ln2_b, loss_target, m_rel_bias, m_w_in, m_b_in, m_w_attn_proj, m_w_ret_proj, m_w_out, m_ln1_g, m_ln1_b, m_w_ffn_gate, m_w_ffn_up, m_w_ffn_down, m_ln2_g, m_ln2_b, v_rel_bias, v_w_in, v_b_in, v_w_attn_proj, v_w_ret_proj, v_w_out, v_ln1_g, v_ln1_b, v_w_ffn_gate, v_w_ffn_up, v_w_ffn_down, v_ln2_g, v_ln2_b):
    w = dict(rel_bias=rel_bias, w_in=w_in, b_in=b_in, w_attn_proj=w_attn_proj, w_ret_proj=w_ret_proj, w_out=w_out,
             ln1_g=ln1_g, ln1_b=ln1_b, w_ffn_gate=w_ffn_gate, w_ffn_up=w_ffn_up, w_ffn_down=w_ffn_down,
             ln2_g=ln2_g, ln2_b=ln2_b)
    m = dict(rel_bias=m_rel_bias, w_in=m_w_in, b_in=m_b_in, w_attn_proj=m_w_attn_proj, w_ret_proj=m_w_ret_proj,
             w_out=m_w_out, ln1_g=m_ln1_g, ln1_b=m_ln1_b, w_ffn_gate=m_w_ffn_gate, w_ffn_up=m_w_ffn_up,
             w_ffn_down=m_w_ffn_down, ln2_g=m_ln2_g, ln2_b=m_ln2_b)
    v = dict(rel_bias=v_rel_bias, w_in=v_w_in, b_in=v_b_in, w_attn_proj=v_w_attn_proj, w_ret_proj=v_w_ret_proj,
             w_out=v_w_out, ln1_g=v_ln1_g, ln1_b=v_ln1_b, w_ffn_gate=v_w_ffn_gate, w_ffn_up=v_w_ffn_up,
             w_ffn_down=v_w_ffn_down, ln2_g=v_ln2_g, ln2_b=v_ln2_b)

    assert DEPTH == 2
    me = 4 * lax.axis_index("x") + 2 * lax.axis_index("y") + lax.axis_index("c")
    core = lax.axis_index("c").astype(jnp.int32).reshape(1)

    def own_slot(lands, blocks):
        return [lax.dynamic_update_index_in_dim(land, blk, me, 0) for land, blk in zip(lands, blocks)]

    shards = [[w[n][l].astype(BF16) for n in BIG_WEIGHTS] for l in range(DEPTH)]
    gathered0 = _all_gather(shards[0], name="all_gather_l0")
    ag1 = _exchange_start(_after(shards[1], gathered0), False, name="all_gather_l1_start")
    b_in_fwd = [_after(b_in[0], ag1[-1]), b_in[1]]

    def weights_of(l, after):
        if l == 0:
            gathered = gathered0
        else:
            gathered = own_slot(_exchange_wait(ag1, after, False, name="all_gather_l1_wait"), shards[1])
        return {n: _full_from_gathered(n, g) for n, g in zip(BIG_WEIGHTS, gathered)}

    rs1 = {}

    def grads_done(l, g):
        if l != 1:
            return None
        rs1["blocks"] = [_dest_major(n, g[n]).reshape((N_DEV,) + _shard_shape(n)).astype(BF16) for n in BIG_WEIGHTS]
        rs1["started"] = _exchange_start(rs1["blocks"], True, name="rs_l1_start")
        return rs1["started"][-1]

    lns = [{n: w[n][l] for n in ("ln1_g", "ln1_b", "ln2_g", "ln2_b")} for l in range(DEPTH)]
    loss_local, grad_x, grads, small = _local_step(x[0], loss_target[0], rel_bias, b_in_fwd, lns, weights_of,
                                                   grads_done)
    loss = lax.psum(loss_local, ("x", "y", "c"))

    lands1 = _exchange_wait(rs1["started"], grad_x, True, name="rs_l1_wait")
    parts1 = own_slot(lands1, [lax.dynamic_index_in_dim(b, me, 0, keepdims=False) for b in rs1["blocks"]])

    g8 = _after([_dest_major(n, grads[0][n]) for n in BIG_WEIGHTS], lands1)
    from_sibling = _rs_sibling_exchange(g8, name="rs_sibling_exchange_l0")
    chip_parts = [_rs_chip_sum(a, b, core, name=f"rs_chip_sum_l0_{n}") for n, a, b in zip(BIG_WEIGHTS, g8, from_sibling)]
    parts0 = _rs_chip_exchange(chip_parts, name="rs_chip_exchange_l0")
    parts = [parts0, parts1]
    big = [{} for _ in range(4)]
    for i, n in enumerate(BIG_WEIGHTS):
        res = _adam_sharded([parts[l][i] for l in range(DEPTH)], w[n], m[n], v[n], name=f"adam_{n}")
        for kind in range(4):
            big[kind][n] = res[kind]

    gs = _all_reduce_small(_pack_small(small), name="all_reduce_small")
    ds, ms, vs = _adam_small(gs, _pack_small(w), _pack_small(m), _pack_small(v), name="adam_small")
    sm = [_unpack_small(t) for t in (gs, ds, ms, vs)]

    outs = [loss, grad_x[None]]
    for kind in range(4):
        for n in ALL_WEIGHTS:
            outs.append(big[kind][n] if n in BIG_WEIGHTS else sm[kind][n])
    return tuple(outs)
```

```python
import functools
import math

import numpy as np
import jax
import jax.numpy as jnp
from jax import lax
from jax.experimental import pallas as pl
from jax.experimental.pallas import tpu as pltpu

F32 = jnp.float32
BF16 = jnp.bfloat16
MESH = pl.DeviceIdType.MESH

D_MODEL = 1024
DEPTH = 2
HEAD_DIM = 64
ATTN_GROUPS = ((128, 1), (512, 4), (2048, 16))
HEADS_PER_GROUP = 6
GROUP_WIDTH = HEADS_PER_GROUP * HEAD_DIM
ATTN_BLOCK = 128
NUM_BUCKETS = 32
MAX_DISTANCE = 2048
RET_HEADS = 4
RET_QK_DIM = 256
RET_V_DIM = 512
RET_CHUNK = 128
RET_T_CHUNKS = 4
ROPE_BASE = 10000.0
D_FF = 2816
ALPHA = (2 * DEPTH) ** 0.25
LN_EPS = 1e-5
GN_EPS = 1e-5
ATTN_W = 3 * GROUP_WIDTH
IN_COLS = 3 * ATTN_W + 2 * 1024 + 2 * 2048 + 2 * 1024
ADAM_LR, ADAM_B1, ADAM_B2, ADAM_EPS, ADAM_WD, ADAM_STEP = 0.001, 0.9, 0.999, 1e-08, 0.01, 10
N_DEV = 8
NEG = -1e30
LANES = 128
VMEM_LIMIT = 56 * 1024 * 1024
MM_TILE_CAP = 1664

BIG_WEIGHTS = ("w_in", "w_attn_proj", "w_ret_proj", "w_out", "w_ffn_gate", "w_ffn_up", "w_ffn_down")
COL_SHARDED = ("w_in", "w_attn_proj", "w_ffn_gate", "w_ffn_up")
FULL_SHAPE = {"w_in": (D_MODEL, IN_COLS), "w_attn_proj": (GROUP_WIDTH, D_MODEL), "w_ret_proj": (2048, D_MODEL),
              "w_out": (D_MODEL, D_MODEL), "w_ffn_gate": (D_MODEL, D_FF), "w_ffn_up": (D_MODEL, D_FF),
              "w_ffn_down": (D_FF, D_MODEL)}
SMALL_WEIGHTS = ("rel_bias", "b_in", "ln1_g", "ln1_b", "ln2_g", "ln2_b")
SMALL_SHAPE = {"rel_bias": (NUM_BUCKETS, 18), "b_in": (DEPTH, IN_COLS), "ln1_g": (DEPTH, D_MODEL),
               "ln1_b": (DEPTH, D_MODEL), "ln2_g": (DEPTH, D_MODEL), "ln2_b": (DEPTH, D_MODEL)}
SMALL_ROWS = 256
ALL_WEIGHTS = ("rel_bias", "w_in", "b_in", "w_attn_proj", "w_ret_proj", "w_out", "ln1_g", "ln1_b",
               "w_ffn_gate", "w_ffn_up", "w_ffn_down", "ln2_g", "ln2_b")


def _cparams(sem=None):
    return pltpu.CompilerParams(dimension_semantics=sem, vmem_limit_bytes=VMEM_LIMIT)


def _div_tile(n, cap, unit):
    if n <= cap:
        return n
    best = None
    for t in range(unit, cap + 1, unit):
        if n % t == 0:
            best = t
    assert best is not None, (n, cap, unit)
    return best


def _mm(a, b, *, name, out_dtype=F32, bias=None, add=None, groups=None, lane_chunks=False, transpose_b=False):
    M, K = a.shape
    N, K2 = b.shape if transpose_b else b.shape[::-1]
    assert K == K2 and a.dtype == BF16 and b.dtype == BF16
    tm = _div_tile(M, 1024, 16)
    tn = N // groups if groups else _div_tile(N, MM_TILE_CAP, LANES)
    tk = _div_tile(K, MM_TILE_CAP, LANES)
    nk = K // tk
    nch = tn // LANES
    has_bias, has_add = bias is not None, add is not None

    def body(*refs):
        a_ref, b_ref = refs[0], refs[1]
        pos = 2
        bias_ref = add_ref = None
        if has_bias:
            bias_ref = refs[pos]
            pos += 1
        if has_add:
            add_ref = refs[pos]
            pos += 1
        o_ref = refs[pos]

        def finish(r):
            if has_bias:
                r = r + bias_ref[...]
            if has_add:
                r = r + add_ref[...]
            if lane_chunks:
                for c in range(nch):
                    o_ref[c] = r[:, c * LANES:(c + 1) * LANES].astype(o_ref.dtype)
            else:
                o_ref[...] = r.astype(o_ref.dtype)

        def product():
            if transpose_b:
                return lax.dot_general(a_ref[...], b_ref[...], (((1,), (1,)), ((), ())), preferred_element_type=F32)
            return jnp.dot(a_ref[...], b_ref[...], preferred_element_type=F32)

        if nk == 1:
            finish(product())
        else:
            acc_ref = refs[pos + 1]
            k = pl.program_id(2)

            @pl.when(k == 0)
            def _():
                acc_ref[...] = jnp.zeros_like(acc_ref)

            acc_ref[...] += product()

            @pl.when(k == nk - 1)
            def _():
                finish(acc_ref[...])

    in_specs = [pl.BlockSpec((tm, tk), lambda i, j, k: (i, k)),
                pl.BlockSpec((tn, tk), lambda i, j, k: (j, k)) if transpose_b
                else pl.BlockSpec((tk, tn), lambda i, j, k: (k, j))]
    args = [a, b]
    if has_bias:
        in_specs.append(pl.BlockSpec((1, tn), lambda i, j, k: (0, j)))
        args.append(bias.reshape(1, N).astype(F32))
    if has_add:
        in_specs.append(pl.BlockSpec((tm, tn), lambda i, j, k: (i, j)))
        args.append(add)
    if lane_chunks:
        assert groups
        out_shape = jax.ShapeDtypeStruct((groups, nch, M, LANES), out_dtype)
        out_spec = pl.BlockSpec((None, nch, tm, LANES), lambda i, j, k: (j, 0, i, 0))
    elif groups:
        out_shape = jax.ShapeDtypeStruct((groups, M, tn), out_dtype)
        out_spec = pl.BlockSpec((None, tm, tn), lambda i, j, k: (j, i, 0))
    else:
        out_shape = jax.ShapeDtypeStruct((M, N), out_dtype)
        out_spec = pl.BlockSpec((tm, tn), lambda i, j, k: (i, j))
    scratch = [pltpu.VMEM((tm, tn), F32)] if nk > 1 else []
    out = pl.pallas_call(
        body, name=name, grid=(M // tm, N // tn, nk), in_specs=in_specs, out_specs=out_spec,
        out_shape=out_shape, scratch_shapes=scratch,
        compiler_params=_cparams(("parallel", "parallel", "arbitrary")))(*args)
    return out.reshape(groups * nch, M, LANES) if lane_chunks else out


def _row_spec(tr, w):
    return pl.BlockSpec((tr, w), lambda i: (i, 0))


def _vec_spec(w):
    return pl.BlockSpec((1, w), lambda i: (0, 0))


def _col_spec(w, tr):
    return pl.BlockSpec((w, tr), lambda i: (0, i))


def _cast_transpose(x, *, name):
    S, W = x.shape
    tr = 512

    def body(x_ref, o_ref, ot_ref):
        v = x_ref[...]
        o_ref[...] = v.astype(BF16)
        ot_ref[...] = v.T.astype(BF16)

    return pl.pallas_call(
        body, name=name, grid=(S // tr,), in_specs=[_row_spec(tr, W)],
        out_specs=[_row_spec(tr, W), _col_spec(W, tr)],
        out_shape=[jax.ShapeDtypeStruct((S, W), BF16), jax.ShapeDtypeStruct((W, S), BF16)],
        compiler_params=_cparams(("parallel",)))(x)


def _ln_fwd(x, sub, g, b, *, name):
    S, W = x.shape
    tr = 512

    def body(x_ref, s_ref, g_ref, b_ref, h_ref, y_ref, yb_ref, ybt_ref):
        h = ALPHA * x_ref[...] + s_ref[...]
        mu = jnp.mean(h, axis=-1, keepdims=True)
        d = h - mu
        var = jnp.mean(d * d, axis=-1, keepdims=True)
        y = d * lax.rsqrt(var + LN_EPS) * g_ref[...] + b_ref[...]
        h_ref[...] = h
        y_ref[...] = y
        yb_ref[...] = y.astype(BF16)
        ybt_ref[...] = y.T.astype(BF16)

    return pl.pallas_call(
        body, name=name, grid=(S // tr,),
        in_specs=[_row_spec(tr, W), _row_spec(tr, W), _vec_spec(W), _vec_spec(W)],
        out_specs=[_row_spec(tr, W)] * 3 + [_col_spec(W, tr)],
        out_shape=[jax.ShapeDtypeStruct((S, W), F32), jax.ShapeDtypeStruct((S, W), F32),
                   jax.ShapeDtypeStruct((S, W), BF16), jax.ShapeDtypeStruct((W, S), BF16)],
        compiler_params=_cparams(("parallel",)))(x, sub, g.reshape(1, W), b.reshape(1, W))


def _ln_bwd(dy, h, g, *, name):
    S, W = dy.shape
    tr = 512

    def body(dy_ref, h_ref, g_ref, dhb_ref, res_ref, dg_ref, db_ref):
        @pl.when(pl.program_id(0) == 0)
        def _():
            dg_ref[...] = jnp.zeros_like(dg_ref)
            db_ref[...] = jnp.zeros_like(db_ref)

        hh = h_ref[...]
        mu = jnp.mean(hh, axis=-1, keepdims=True)
        d = hh - mu
        var = jnp.mean(d * d, axis=-1, keepdims=True)
        rstd = lax.rsqrt(var + LN_EPS)
        xhat = d * rstd
        dyv = dy_ref[...]
        dg_ref[...] += jnp.sum(dyv * xhat, axis=0, keepdims=True)
        db_ref[...] += jnp.sum(dyv, axis=0, keepdims=True)
        dxh = dyv * g_ref[...]
        dh = rstd * (dxh - jnp.mean(dxh, axis=-1, keepdims=True)
                     - xhat * jnp.mean(dxh * xhat, axis=-1, keepdims=True))
        dhb_ref[...] = dh.astype(BF16)
        res_ref[...] = ALPHA * dh

    return pl.pallas_call(
        body, name=name, grid=(S // tr,),
        in_specs=[_row_spec(tr, W), _row_spec(tr, W), _vec_spec(W)],
        out_specs=[_row_spec(tr, W), _row_spec(tr, W), _vec_spec(W), _vec_spec(W)],
        out_shape=[jax.ShapeDtypeStruct((S, W), BF16), jax.ShapeDtypeStruct((S, W), F32),
                   jax.ShapeDtypeStruct((1, W), F32), jax.ShapeDtypeStruct((1, W), F32)],
        compiler_params=_cparams(("arbitrary",)))(dy, h, g.reshape(1, W))


def _loss_fwd_bwd(y, target, *, name):
    S, W = y.shape
    tr = 512

    def body(y_ref, t_ref, dy_ref, acc_ref):
        @pl.when(pl.program_id(0) == 0)
        def _():
            acc_ref[...] = jnp.zeros_like(acc_ref)

        e = y_ref[...] - t_ref[...]
        acc_ref[...] += jnp.sum(jnp.sum(e * e, axis=-1, keepdims=True), axis=0, keepdims=True)
        dy_ref[...] = e * (1.0 / W)

    return pl.pallas_call(
        body, name=name, grid=(S // tr,),
        in_specs=[_row_spec(tr, W), _row_spec(tr, W)],
        out_specs=[_row_spec(tr, W), pl.BlockSpec((1, 1), lambda i: (0, 0))],
        out_shape=[jax.ShapeDtypeStruct((S, W), F32), jax.ShapeDtypeStruct((1, 1), F32)],
        compiler_params=_cparams(("arbitrary",)))(y, target)


def _combine_fwd(os_, ls_, *, name):
    NCH, S, _ = os_[0].shape
    W = NCH * LANES
    tr = 512

    def body(o0, o1, o2, l0, l1, l2, yb_ref, ybt_ref, y_ref, w0_ref, w1_ref, w2_ref):
        for c in range(NCH):
            la, lb, lc = l0[c], l1[c], l2[c]
            m = jnp.maximum(jnp.maximum(la, lb), lc)
            ea, eb, ec = jnp.exp(la - m), jnp.exp(lb - m), jnp.exp(lc - m)
            inv = 1.0 / (ea + eb + ec)
            wa, wb, wc = ea * inv, eb * inv, ec * inv
            y = wa * o0[c] + wb * o1[c] + wc * o2[c]
            y_ref[c] = y
            yb_ref[:, c * LANES:(c + 1) * LANES] = y.astype(BF16)
            ybt_ref[c * LANES:(c + 1) * LANES, :] = y.T.astype(BF16)
            w0_ref[c] = wa
            w1_ref[c] = wb
            w2_ref[c] = wc

    ch = pl.BlockSpec((NCH, tr, LANES), lambda i: (0, i, 0))
    yb, ybt, y, w0, w1, w2 = pl.pallas_call(
        body, name=name, grid=(S // tr,),
        in_specs=[ch] * 6,
        out_specs=[_row_spec(tr, W), _col_spec(W, tr)] + [ch] * 4,
        out_shape=[jax.ShapeDtypeStruct((S, W), BF16), jax.ShapeDtypeStruct((W, S), BF16)]
        + [jax.ShapeDtypeStruct((NCH, S, LANES), F32)] * 4,
        compiler_params=_cparams(("parallel",)))(*os_, *ls_)
    return yb, ybt, y, (w0, w1, w2)


def _merge_fwd(gates, pa, pr, *, name):
    S, W = pa.shape
    tr = 512

    def body(g_ref, pa_ref, pr_ref, o_ref, ot_ref):
        m = jax.nn.sigmoid(g_ref[0]) * pa_ref[...] + jax.nn.sigmoid(g_ref[1]) * pr_ref[...]
        o_ref[...] = m.astype(BF16)
        ot_ref[...] = m.T.astype(BF16)

    return pl.pallas_call(
        body, name=name, grid=(S // tr,),
        in_specs=[pl.BlockSpec((2, tr, W), lambda i: (0, i, 0)), _row_spec(tr, W), _row_spec(tr, W)],
        out_specs=[_row_spec(tr, W), _col_spec(W, tr)],
        out_shape=[jax.ShapeDtypeStruct((S, W), BF16), jax.ShapeDtypeStruct((W, S), BF16)],
        compiler_params=_cparams(("parallel",)))(gates, pa, pr)


def _merge_bwd(dm, gates, pa, pr, *, name):
    S, W = pa.shape
    tr = 256

    def body(dm_ref, g_ref, pa_ref, pr_ref, dpa_ref, dpr_ref, dg_ref):
        dmv = dm_ref[...]
        sa, sb = jax.nn.sigmoid(g_ref[0]), jax.nn.sigmoid(g_ref[1])
        dpa_ref[...] = (dmv * sa).astype(BF16)
        dpr_ref[...] = (dmv * sb).astype(BF16)
        dg_ref[0] = (dmv * pa_ref[...] * (sa * (1.0 - sa))).astype(BF16)
        dg_ref[1] = (dmv * pr_ref[...] * (sb * (1.0 - sb))).astype(BF16)

    g3 = pl.BlockSpec((2, tr, W), lambda i: (0, i, 0))
    return pl.pallas_call(
        body, name=name, grid=(S // tr,),
        in_specs=[_row_spec(tr, W), g3, _row_spec(tr, W), _row_spec(tr, W)],
        out_specs=[_row_spec(tr, W), _row_spec(tr, W), g3],
        out_shape=[jax.ShapeDtypeStruct((S, W), BF16), jax.ShapeDtypeStruct((S, W), BF16),
                   jax.ShapeDtypeStruct((2, S, W), BF16)],
        compiler_params=_cparams(("parallel",)))(dm, gates, pa, pr)


def _swiglu_fwd(uv, *, name):
    _, S, W = uv.shape
    tr = 256

    def body(uv_ref, o_ref, ot_ref):
        u = uv_ref[0]
        hh = u * jax.nn.sigmoid(u) * uv_ref[1]
        o_ref[...] = hh.astype(BF16)
        ot_ref[...] = hh.T.astype(BF16)

    return pl.pallas_call(
        body, name=name, grid=(S // tr,),
        in_specs=[pl.BlockSpec((2, tr, W), lambda i: (0, i, 0))],
        out_specs=[_row_spec(tr, W), _col_spec(W, tr)],
        out_shape=[jax.ShapeDtypeStruct((S, W), BF16), jax.ShapeDtypeStruct((W, S), BF16)],
        compiler_params=_cparams(("parallel",)))(uv)


def _swiglu_bwd(dh, uv, *, name):
    _, S, W = uv.shape
    tr = 256

    def body(dh_ref, uv_ref, o_ref):
        u, v, d = uv_ref[0], uv_ref[1], dh_ref[...]
        sg = jax.nn.sigmoid(u)
        o_ref[:, 0:W] = (d * v * (sg * (1.0 + u * (1.0 - sg)))).astype(BF16)
        o_ref[:, W:2 * W] = (d * (u * sg)).astype(BF16)

    return pl.pallas_call(
        body, name=name, grid=(S // tr,),
        in_specs=[_row_spec(tr, W), pl.BlockSpec((2, tr, W), lambda i: (0, i, 0))],
        out_specs=_row_spec(tr, 2 * W), out_shape=jax.ShapeDtypeStruct((S, 2 * W), BF16),
        compiler_params=_cparams(("parallel",)))(dh, uv)


def _assemble_dz(da, dq_r, dk_r, dv_r, dg_r, dgates, *, name):
    S = dv_r.shape[0]
    tr = 256
    GW = GROUP_WIDTH
    NCH = GW // LANES

    def body(*refs):
        a_refs = refs[0:9]
        q_ref, k_ref, v_ref, g_ref, gt_ref, dz_ref, cs_ref = refs[9:]

        @pl.when(pl.program_id(0) == 0)
        def _():
            cs_ref[...] = jnp.zeros_like(cs_ref)

        def put(off, val):
            w = val.shape[-1]
            dz_ref[:, off:off + w] = val.astype(BF16)
            cs_ref[:, off:off + w] += jnp.sum(val.astype(F32), axis=0, keepdims=True)

        for which in range(3):
            for gi in range(3):
                for c in range(NCH):
                    put(which * ATTN_W + gi * GW + c * LANES, a_refs[3 * gi + which][c])
        off = 3 * ATTN_W
        put(off, q_ref[...])
        put(off + 1024, k_ref[...])
        put(off + 2048, v_ref[...])
        put(off + 4096, g_ref[...])
        put(off + 6144, gt_ref[0])
        put(off + 7168, gt_ref[1])

    flat_a = [t for grp in da for t in grp]
    return pl.pallas_call(
        body, name=name, grid=(S // tr,),
        in_specs=[pl.BlockSpec((NCH, tr, LANES), lambda i: (0, i, 0))] * 9 + [_row_spec(tr, 1024), _row_spec(tr, 1024),
                  _row_spec(tr, 2048), _row_spec(tr, 2048), pl.BlockSpec((2, tr, 1024), lambda i: (0, i, 0))],
        out_specs=[_row_spec(tr, IN_COLS), _vec_spec(IN_COLS)],
        out_shape=[jax.ShapeDtypeStruct((S, IN_COLS), BF16), jax.ShapeDtypeStruct((1, IN_COLS), F32)],
        compiler_params=_cparams(("arbitrary",)))(*flat_a, dq_r, dk_r, dv_r, dg_r, dgates)


def _t5_bucket(dist):
    max_exact = NUM_BUCKETS // 2
    large = max_exact + (np.log(np.maximum(dist, max_exact) / max_exact)
                         / np.log(MAX_DISTANCE / max_exact) * (NUM_BUCKETS - max_exact)).astype(np.int32)
    large = np.minimum(large, NUM_BUCKETS - 1)
    return np.where(dist < max_exact, dist, large).astype(np.int32)


def _attn_tables(dilation):
    W = ATTN_BLOCK
    qi = np.arange(W)[:, None]
    kj = np.arange(2 * W)[None, :]
    rel = qi + W - kj
    valid = (rel >= 0) & (rel <= W)
    buckets = _t5_bucket(np.clip(rel, 0, W) * dilation)
    return buckets, valid


def _attn_bias(rel_bias, gi, dilation):
    buckets, valid = _attn_tables(dilation)
    table = rel_bias[:, gi * HEADS_PER_GROUP:(gi + 1) * HEADS_PER_GROUP]
    onehot = (jnp.asarray(buckets.reshape(-1, 1)) == jnp.arange(NUM_BUCKETS)[None, :]).astype(F32)
    bias = jnp.dot(onehot, table.astype(F32), precision=lax.Precision.HIGHEST)
    bias = bias.T.reshape(HEADS_PER_GROUP, ATTN_BLOCK, 2 * ATTN_BLOCK)
    return jnp.where(jnp.asarray(valid)[None], bias, NEG)


def _dot_nt(a, b):
    return lax.dot_general(a, b, (((1,), (1,)), ((), ())), preferred_element_type=F32)


def _dot_tn(a, b):
    return lax.dot_general(a, b, (((0,), (0,)), ((), ())), preferred_element_type=F32)


def _dot(a, b):
    return jnp.dot(a, b, preferred_element_type=F32)


ATTN_RESIDUES_PER_STEP = 4
ATTN_UNITS_AT_ONCE = 8
HEADS_PER_CHUNK = LANES // HEAD_DIM
N_CHUNKS = GROUP_WIDTH // LANES


def _first_block_mask(has_prev):
    col = lax.broadcasted_iota(jnp.int32, (1, 2 * ATTN_BLOCK), 1)
    return jnp.where(jnp.logical_or(has_prev, col >= ATTN_BLOCK), 0.0, NEG).astype(F32)


def _head_lanes(hh):
    return slice(HEAD_DIM * hh, HEAD_DIM * (hh + 1))


def _attn_geometry(S, d):
    rows_per_block = ATTN_BLOCK * d
    rps = min(d, ATTN_RESIDUES_PER_STEP)
    return rows_per_block, S // rows_per_block, rps, d // rps


def _residue_rows(d, rps, rg, rr):
    if d == 1:
        return slice(None)
    return pl.ds(rg * rps + rr, ATTN_BLOCK, stride=d)


def _attn_in_specs(gi, RB, last):
    def spec(which, prev):
        if prev:
            return pl.BlockSpec((None, RB, LANES),
                                lambda j, n, rg: (9 * which + 3 * gi + j, jnp.clip(n - 1, 0, last), 0))
        return pl.BlockSpec((None, RB, LANES), lambda j, n, rg: (9 * which + 3 * gi + j, jnp.minimum(n, last), 0))
    bias = pl.BlockSpec((HEADS_PER_CHUNK, ATTN_BLOCK, 2 * ATTN_BLOCK), lambda j, n, rg: (j, 0, 0))
    return [spec(0, False), spec(1, True), spec(1, False), spec(2, True), spec(2, False), bias]


def _attn_fwd(qkv, bias, gi, d, *, name):
    _, S, _ = qkv.shape
    B = ATTN_BLOCK
    RB, nb, rps, nrg = _attn_geometry(S, d)
    scale = HEAD_DIM ** -0.5
    units = [(rr, hh) for rr in range(rps) for hh in range(HEADS_PER_CHUNK)]

    def body(q_ref, kp_ref, kc_ref, vp_ref, vc_ref, b_ref, o_ref, l_ref):
        n, rg = pl.program_id(1), pl.program_id(2)
        edge = _first_block_mask(n > 0)
        rows = [_residue_rows(d, rps, rg, rr) for rr in range(rps)]
        q = [q_ref[r_, :].astype(BF16) for r_ in rows]
        k2 = [jnp.concatenate([kp_ref[r_, :], kc_ref[r_, :]], axis=0).astype(BF16) for r_ in rows]
        v2 = [jnp.concatenate([vp_ref[r_, :], vc_ref[r_, :]], axis=0).astype(BF16) for r_ in rows]
        o_part, l_part = {}, {}
        for u0 in range(0, len(units), ATTN_UNITS_AT_ONCE):
            us = units[u0:u0 + ATTN_UNITS_AT_ONCE]
            s = [_dot_nt(q[rr][:, _head_lanes(hh)], k2[rr][:, _head_lanes(hh)]) * scale + b_ref[hh] + edge
                 for rr, hh in us]
            m = [jnp.max(x, axis=-1, keepdims=True) for x in s]
            p = [jnp.exp(x - mm) for x, mm in zip(s, m)]
            l = [jnp.sum(x, axis=-1, keepdims=True) for x in p]
            pb = [(x * (1.0 / ll)).astype(BF16) for x, ll in zip(p, l)]
            o = [_dot(x, v2[rr][:, _head_lanes(hh)]) for x, (rr, hh) in zip(pb, us)]
            for u, oo, mm, ll in zip(us, o, m, l):
                o_part[u] = oo
                l_part[u] = jnp.broadcast_to(mm + jnp.log(ll), (B, HEAD_DIM))
        for rr in range(rps):
            o_ref[rows[rr], :] = jnp.concatenate([o_part[(rr, hh)] for hh in range(HEADS_PER_CHUNK)], axis=1)
            l_ref[rows[rr], :] = jnp.concatenate([l_part[(rr, hh)] for hh in range(HEADS_PER_CHUNK)], axis=1)

    out_spec = pl.BlockSpec((None, RB, LANES), lambda j, n, rg: (j, n, 0))
    return pl.pallas_call(
        body, name=name, grid=(N_CHUNKS, nb, nrg),
        in_specs=_attn_in_specs(gi, RB, nb - 1),
        out_specs=[out_spec, out_spec],
        out_shape=[jax.ShapeDtypeStruct((N_CHUNKS, S, LANES), F32)] * 2,
        compiler_params=_cparams(("parallel", "arbitrary", "arbitrary")))(qkv, qkv, qkv, qkv, qkv, bias)


def _attn_bwd(qkv, bias, lse, dya, ya, wts, gi, d, *, name):
    _, S, _ = qkv.shape
    B = ATTN_BLOCK
    RB, nb, rps, nrg = _attn_geometry(S, d)
    scale = HEAD_DIM ** -0.5
    units = [(rr, hh) for rr in range(rps) for hh in range(HEADS_PER_CHUNK)]

    def body(q_ref, kp_ref, kc_ref, vp_ref, vc_ref, b_ref, l_ref, dya_ref, ya_ref, w_ref,
             dq_ref, dk_ref, dv_ref, db_ref, dk_carry, dv_carry):
        n, rg = pl.program_id(1), pl.program_id(2)
        rows = [_residue_rows(d, rps, rg, rr) for rr in range(rps)]

        @pl.when((n == 0) & (rg == 0))
        def _():
            db_ref[...] = jnp.zeros_like(db_ref)
            dk_carry[...] = jnp.zeros_like(dk_carry)
            dv_carry[...] = jnp.zeros_like(dv_carry)

        @pl.when(n < nb)
        def _():
            edge = _first_block_mask(n > 0)
            q = [q_ref[r_, :].astype(BF16) for r_ in rows]
            k2 = [jnp.concatenate([kp_ref[r_, :], kc_ref[r_, :]], axis=0).astype(BF16) for r_ in rows]
            v2 = [jnp.concatenate([vp_ref[r_, :], vc_ref[r_, :]], axis=0).astype(BF16) for r_ in rows]
            lse_c = [l_ref[r_, :] for r_ in rows]
            dy_c = [dya_ref[r_, :] for r_ in rows]
            ya_c = [ya_ref[r_, :] for r_ in rows]
            w_c = [w_ref[r_, :] for r_ in rows]
            ds_sum = [None] * HEADS_PER_CHUNK
            dq_part, dk_part, dv_part = {}, {}, {}
            for u0 in range(0, len(units), ATTN_UNITS_AT_ONCE):
                us = units[u0:u0 + ATTN_UNITS_AT_ONCE]
                hl = [_head_lanes(hh) for _, hh in us]
                qh = [q[rr][:, sl] for (rr, _), sl in zip(us, hl)]
                kh = [k2[rr][:, sl] for (rr, _), sl in zip(us, hl)]
                vh = [v2[rr][:, sl] for (rr, _), sl in zip(us, hl)]
                s = [_dot_nt(a, k) * scale + b_ref[hh] + edge for a, k, (_, hh) in zip(qh, kh, us)]
                p = [jnp.exp(x - lse_c[rr][:, HEAD_DIM * hh:HEAD_DIM * hh + 1]) for x, (rr, hh) in zip(s, us)]
                dy = [dy_c[rr][:, sl] for (rr, _), sl in zip(us, hl)]
                w = [w_c[rr][:, sl] for (rr, _), sl in zip(us, hl)]
                shift = [ww[:, 0:1] * jnp.sum(d_ * ya_c[rr][:, sl], axis=-1, keepdims=True)
                         for ww, d_, (rr, _), sl in zip(w, dy, us, hl)]
                do = [(ww * d_).astype(BF16) for ww, d_ in zip(w, dy)]
                ds = [pp * (_dot_nt(o_, v) - sh) for pp, o_, v, sh in zip(p, do, vh, shift)]
                for x, (_, hh) in zip(ds, us):
                    ds_sum[hh] = x if ds_sum[hh] is None else ds_sum[hh] + x
                dsb = [x.astype(BF16) for x in ds]
                pb = [x.astype(BF16) for x in p]
                for u, x, pp, a, k, o_ in zip(us, dsb, pb, qh, kh, do):
                    dq_part[u] = _dot(x, k) * scale
                    dk_part[u] = _dot_tn(x, a) * scale
                    dv_part[u] = _dot_tn(pp, o_)
            for hh in range(HEADS_PER_CHUNK):
                db_ref[hh] += ds_sum[hh]
            for rr in range(rps):
                r_ = rows[rr]
                dq_ref[r_, :] = jnp.concatenate([dq_part[(rr, hh)] for hh in range(HEADS_PER_CHUNK)], axis=1)
                dk2 = jnp.concatenate([dk_part[(rr, hh)] for hh in range(HEADS_PER_CHUNK)], axis=1)
                dv2 = jnp.concatenate([dv_part[(rr, hh)] for hh in range(HEADS_PER_CHUNK)], axis=1)
                dk_ref[r_, :] = dk_carry[r_, :] + dk2[0:B]
                dv_ref[r_, :] = dv_carry[r_, :] + dv2[0:B]
                dk_carry[r_, :] = dk2[B:2 * B]
                dv_carry[r_, :] = dv2[B:2 * B]

        @pl.when(n == nb)
        def _():
            for r_ in rows:
                dk_ref[r_, :] = dk_carry[r_, :]
                dv_ref[r_, :] = dv_carry[r_, :]

    last = nb - 1
    cur = pl.BlockSpec((None, RB, LANES), lambda j, n, rg: (j, jnp.minimum(n, last), 0))
    lag = pl.BlockSpec((None, RB, LANES), lambda j, n, rg: (j, jnp.maximum(n - 1, 0), 0))
    db_spec = pl.BlockSpec((HEADS_PER_CHUNK, B, 2 * B), lambda j, n, rg: (j, 0, 0))
    dq, dk, dv, db = pl.pallas_call(
        body, name=name, grid=(N_CHUNKS, nb + 1, nrg),
        in_specs=_attn_in_specs(gi, RB, last) + [cur, cur, cur, cur],
        out_specs=[cur, lag, lag, db_spec],
        out_shape=[jax.ShapeDtypeStruct((N_CHUNKS, S, LANES), F32)] * 3
        + [jax.ShapeDtypeStruct((HEADS_PER_GROUP, B, 2 * B), F32)],
        scratch_shapes=[pltpu.VMEM((RB, LANES), F32), pltpu.VMEM((RB, LANES), F32)],
        compiler_params=_cparams(("arbitrary", "arbitrary", "arbitrary")))(
            qkv, qkv, qkv, qkv, qkv, bias, lse, dya, ya, wts)
    return (dq, dk, dv), db


def _bias_grad(dbs, *, name):
    nk = ATTN_BLOCK * 2 * ATTN_BLOCK
    buckets = []
    for (_, dil) in ATTN_GROUPS:
        b, valid = _attn_tables(dil)
        buckets.append(np.where(valid, b, -1).reshape(1, nk))
    bk = jnp.asarray(np.stack(buckets).astype(np.int32))
    flat = [x.reshape(HEADS_PER_GROUP, nk) for x in dbs]

    def body(bk_ref, d0, d1, d2, o_ref):
        ids = lax.broadcasted_iota(jnp.int32, (NUM_BUCKETS, nk), 0)
        for gi, dref in enumerate((d0, d1, d2)):
            onehot = (ids == bk_ref[gi]).astype(F32)
            o_ref[gi] = lax.dot_general(onehot, dref[...], (((1,), (1,)), ((), ())),
                                        preferred_element_type=F32, precision=lax.Precision.HIGHEST)

    out = pl.pallas_call(
        body, name=name,
        out_shape=jax.ShapeDtypeStruct((3, NUM_BUCKETS, HEADS_PER_GROUP), F32),
        compiler_params=_cparams())(bk, *flat)
    return jnp.transpose(out, (1, 0, 2)).reshape(NUM_BUCKETS, 3 * HEADS_PER_GROUP)


def _ret_tables(S):
    half = RET_QK_DIM // 2
    pos = jnp.arange(S, dtype=F32)
    inv_freq = ROPE_BASE ** (-jnp.arange(half, dtype=F32) / half)
    ang = pos[:, None] * inv_freq[None]
    cos, sin = jnp.cos(ang), jnp.sin(ang)
    H, C = RET_HEADS, RET_CHUNK
    log_g = jnp.log(1.0 - 2.0 ** (-5.0 - jnp.arange(H, dtype=F32)))
    n = jnp.arange(C, dtype=F32)
    diff = n[:, None] - n[None, :]
    dmask = jnp.where(diff >= 0, jnp.exp(log_g[:, None, None] * jnp.maximum(diff, 0.0)), 0.0)
    q_dec = jnp.exp(log_g[:, None] * (n + 1.0))
    k_dec = jnp.exp(log_g[:, None] * (C - 1.0 - n))
    chunk_dec = jnp.exp(log_g * C)
    qd = jnp.broadcast_to(q_dec[:, :, None], (H, C, RET_QK_DIM))
    kd = jnp.broadcast_to(k_dec[:, :, None], (H, C, RET_QK_DIM))
    cd = jnp.broadcast_to(chunk_dec[:, None, None], (H, 1, RET_V_DIM))
    return cos, sin, dmask, qd, kd, cd


def _rot(t, cos, sin):
    half = RET_QK_DIM // 2
    t1, t2 = t[:, :half], t[:, half:]
    return jnp.concatenate([t1 * cos - t2 * sin, t1 * sin + t2 * cos], axis=-1)


def _unrot(t, cos, sin):
    half = RET_QK_DIM // 2
    t1, t2 = t[:, :half], t[:, half:]
    return jnp.concatenate([t1 * cos + t2 * sin, t2 * cos - t1 * sin], axis=-1)


def _ret_specs(rev, nC):
    C, DK, DV = RET_CHUNK, RET_QK_DIM, RET_V_DIM
    cidx = (lambda c: nC - 1 - c) if rev else (lambda c: c)
    H = RET_HEADS
    return dict(
        qk=lambda which: pl.BlockSpec((None, C, H * DK), lambda c: (which, cidx(c), 0)),
        q=pl.BlockSpec((C, H * DK), lambda c: (cidx(c), 0)),
        v=pl.BlockSpec((C, H * DV), lambda c: (cidx(c), 0)),
        cs=pl.BlockSpec((C, DK // 2), lambda c: (cidx(c), 0)),
        dmask=pl.BlockSpec((H, C, C), lambda c: (0, 0, 0)),
        dec=pl.BlockSpec((H, C, DK), lambda c: (0, 0, 0)),
        cd=pl.BlockSpec((H, 1, DV), lambda c: (0, 0, 0)),
        st=pl.BlockSpec((H, None, DK, DV), lambda c: (0, cidx(c), 0, 0)),
    )


def _ret_fwd(qk, v, g, tables, *, name):
    _, S, _ = qk.shape
    nC = S // RET_CHUNK
    C, DK, DV, H = RET_CHUNK, RET_QK_DIM, RET_V_DIM, RET_HEADS
    cos, sin, dmask, qd, kd, cd = tables
    kscale = DK ** -0.5

    def body(q_ref, k_ref, v_ref, g_ref, cos_ref, sin_ref, dm_ref, qd_ref, kd_ref, cd_ref,
             o_ref, yb_ref, ybt_ref, st_ref, state):
        @pl.when(pl.program_id(0) == 0)
        def _():
            state[...] = jnp.zeros_like(state)

        tcol = pl.multiple_of((pl.program_id(0) % RET_T_CHUNKS) * C, C)
        cs, sn = cos_ref[...], sin_ref[...]
        for h in range(H):
            qs, vs = slice(DK * h, DK * (h + 1)), slice(DV * h, DV * (h + 1))
            Q = _rot(q_ref[:, qs], cs, sn)
            K = _rot(k_ref[:, qs], cs, sn) * kscale
            Qb, Kb, V = Q.astype(BF16), K.astype(BF16), v_ref[:, vs]
            sb = state[h].astype(BF16)
            st_ref[h] = sb
            A = _dot_nt(Qb, Kb) * dm_ref[h]
            o = _dot(A.astype(BF16), V) + _dot((Q * qd_ref[h]).astype(BF16), sb)
            state[h] = state[h] * cd_ref[h] + _dot_tn((K * kd_ref[h]).astype(BF16), V)
            mu = jnp.mean(o, axis=-1, keepdims=True)
            dd = o - mu
            var = jnp.mean(dd * dd, axis=-1, keepdims=True)
            yn = dd * lax.rsqrt(var + GN_EPS)
            gv = g_ref[:, vs]
            yb = gv * jax.nn.sigmoid(gv) * yn
            o_ref[:, vs] = o
            yb_ref[:, vs] = yb.astype(BF16)
            ybt_ref[vs, pl.ds(tcol, C)] = yb.T.astype(BF16)

    sp = _ret_specs(False, nC)
    return pl.pallas_call(
        body, name=name, grid=(nC,),
        in_specs=[sp["qk"](0), sp["qk"](1), sp["v"], sp["v"], sp["cs"], sp["cs"], sp["dmask"],
                  sp["dec"], sp["dec"], sp["cd"]],
        out_specs=[sp["v"], sp["v"], pl.BlockSpec((H * DV, RET_T_CHUNKS * C), lambda c: (0, c // RET_T_CHUNKS)),
                   sp["st"]],
        out_shape=[jax.ShapeDtypeStruct((S, H * DV), F32), jax.ShapeDtypeStruct((S, H * DV), BF16),
                   jax.ShapeDtypeStruct((H * DV, S), BF16), jax.ShapeDtypeStruct((H, nC, DK, DV), BF16)],
        scratch_shapes=[pltpu.VMEM((H, DK, DV), F32)],
        compiler_params=_cparams(("arbitrary",)))(qk, qk, v, g, cos, sin, dmask, qd, kd, cd)


def _ret_bwd(dyb, qk, v, g, o, states, tables, *, name):
    _, S, _ = qk.shape
    nC = S // RET_CHUNK
    C, DK, DV, H = RET_CHUNK, RET_QK_DIM, RET_V_DIM, RET_HEADS
    cos, sin, dmask, qd, kd, cd = tables
    kscale = DK ** -0.5

    def body(dy_ref, q_ref, k_ref, v_ref, g_ref, o_ref, st_ref, cos_ref, sin_ref, dm_ref, qd_ref, kd_ref,
             cd_ref, dq_ref, dk_ref, dv_ref, dg_ref, dstate):
        @pl.when(pl.program_id(0) == 0)
        def _():
            dstate[...] = jnp.zeros_like(dstate)

        cs, sn = cos_ref[...], sin_ref[...]
        for h in range(H):
            qs, vs = slice(DK * h, DK * (h + 1)), slice(DV * h, DV * (h + 1))
            ov = o_ref[:, vs]
            mu = jnp.mean(ov, axis=-1, keepdims=True)
            dd = ov - mu
            var = jnp.mean(dd * dd, axis=-1, keepdims=True)
            rstd = lax.rsqrt(var + GN_EPS)
            yn = dd * rstd
            gv, dy = g_ref[:, vs], dy_ref[:, vs]
            sg = jax.nn.sigmoid(gv)
            dg_ref[:, vs] = (dy * yn * (sg * (1.0 + gv * (1.0 - sg)))).astype(BF16)
            dyn = dy * (gv * sg)
            dO = rstd * (dyn - jnp.mean(dyn, axis=-1, keepdims=True)
                         - yn * jnp.mean(dyn * yn, axis=-1, keepdims=True))
            dOb = dO.astype(BF16)

            Q = _rot(q_ref[:, qs], cs, sn)
            K = _rot(k_ref[:, qs], cs, sn) * kscale
            Qb, Kb, V = Q.astype(BF16), K.astype(BF16), v_ref[:, vs]
            dm, qd_h, kd_h = dm_ref[h], qd_ref[h], kd_ref[h]
            Sb = st_ref[h]
            dSb = dstate[h].astype(BF16)
            Ab = (_dot_nt(Qb, Kb) * dm).astype(BF16)
            dAb = (_dot_nt(dOb, V) * dm).astype(BF16)
            Qd = (Q * qd_h).astype(BF16)
            Kd = (K * kd_h).astype(BF16)
            dQ = _dot(dAb, Kb) + _dot_nt(dOb, Sb) * qd_h
            dK = _dot_tn(dAb, Qb) + _dot_nt(V, dSb) * kd_h
            dv_ref[:, vs] = (_dot_tn(Ab, dOb) + _dot(Kd, dSb)).astype(BF16)
            dstate[h] = dstate[h] * cd_ref[h] + _dot_tn(Qd, dOb)
            dq_ref[:, qs] = _unrot(dQ, cs, sn).astype(BF16)
            dk_ref[:, qs] = (_unrot(dK, cs, sn) * kscale).astype(BF16)

    sp = _ret_specs(True, nC)
    dq, dk, dv, dg = pl.pallas_call(
        body, name=name, grid=(nC,),
        in_specs=[sp["v"], sp["qk"](0), sp["qk"](1), sp["v"], sp["v"], sp["v"], sp["st"], sp["cs"], sp["cs"],
                  sp["dmask"], sp["dec"], sp["dec"], sp["cd"]],
        out_specs=[sp["q"], sp["q"], sp["v"], sp["v"]],
        out_shape=[jax.ShapeDtypeStruct((S, H * DK), BF16), jax.ShapeDtypeStruct((S, H * DK), BF16),
                   jax.ShapeDtypeStruct((S, H * DV), BF16), jax.ShapeDtypeStruct((S, H * DV), BF16)],
        scratch_shapes=[pltpu.VMEM((H, DK, DV), F32)],
        compiler_params=_cparams(("arbitrary",)))(dyb, qk, qk, v, g, o, states, cos, sin, dmask, qd, kd, cd)
    return dq, dk, dv, dg


def _layer_fwd(l, x, xb, x_t, weights_of, b_in, biases, ln, tables):
    S = x.shape[0]
    tag = f"l{l}"
    W = dict(weights_of(l, "in", x))
    win = W["w_in"]
    c0, c1, c2, c3, c4 = 3 * ATTN_W, 3 * ATTN_W + 2048, 3 * ATTN_W + 4096, 3 * ATTN_W + 6144, IN_COLS
    qkv_a = _mm(xb, win[:, :c0], bias=b_in[:c0], groups=3, lane_chunks=True, name=f"{tag}_in_attn")
    qk_r = _mm(xb, win[:, c0:c1], bias=b_in[c0:c1], groups=2, name=f"{tag}_in_retqk")
    v_r = _mm(xb, win[:, c1:c2], bias=b_in[c1:c2], out_dtype=BF16, name=f"{tag}_in_retv")
    g_r = _mm(xb, win[:, c2:c3], bias=b_in[c2:c3], name=f"{tag}_in_retg")
    gates = _mm(xb, win[:, c3:c4], bias=b_in[c3:c4], groups=2, name=f"{tag}_in_gates")

    os_, ls_ = [], []
    for gi, (_, dil) in enumerate(ATTN_GROUPS):
        o, lse = _attn_fwd(qkv_a, biases[gi], gi, dil, name=f"{tag}_attn_fwd{gi}")
        os_.append(o)
        ls_.append(lse)
    ya_b, ya_t, ya, wts = _combine_fwd(os_, ls_, name=f"{tag}_combine")

    o_r, yb, yb_t, states = _ret_fwd(qk_r, v_r, g_r, tables, name=f"{tag}_ret_fwd")

    W.update(weights_of(l, "rest", yb))
    W["w_gu"] = jnp.concatenate([W["w_ffn_gate"], W["w_ffn_up"]], axis=1)
    pa = _mm(ya_b, W["w_attn_proj"], name=f"{tag}_attn_proj")
    pr = _mm(yb, W["w_ret_proj"], name=f"{tag}_ret_proj")
    merged, merged_t = _merge_fwd(gates, pa, pr, name=f"{tag}_merge")
    mix = _mm(merged, W["w_out"], name=f"{tag}_out_proj")
    h1, x1, x1b, x1_t = _ln_fwd(x, mix, ln["ln1_g"], ln["ln1_b"], name=f"{tag}_ln1")
    uv = _mm(x1b, W["w_gu"], groups=2, name=f"{tag}_ffn_in")
    hh, hh_t = _swiglu_fwd(uv, name=f"{tag}_swiglu")
    f = _mm(hh, W["w_ffn_down"], name=f"{tag}_ffn_down")
    h2, x2, x2b, x2_t = _ln_fwd(x1, f, ln["ln2_g"], ln["ln2_b"], name=f"{tag}_ln2")
    saved = dict(x_t=x_t, qkv_a=qkv_a, qk_r=qk_r, v_r=v_r, g_r=g_r, gates=gates, ls=ls_, ya_t=ya_t, ya=ya,
                 wts=wts, o_r=o_r, yb_t=yb_t, states=states, pa=pa, pr=pr, merged_t=merged_t, h1=h1, x1_t=x1_t,
                 uv=uv, hh_t=hh_t, h2=h2)
    return x2, x2b, x2_t, saved, W


WEIGHT_GROUPS = {"in": ("w_in",), "proj": ("w_attn_proj", "w_ret_proj", "w_out"),
                 "ffn": ("w_ffn_gate", "w_ffn_up", "w_ffn_down")}


def _behind(value, token):
    return value if token is None else value + token[0, 0]


def _layer_bwd(l, dx2, sv, W, biases, ln, tables, token, on_grads):
    S = dx2.shape[0]
    tag = f"l{l}"
    g = {}

    def done(group):
        return None if on_grads is None else on_grads(l, group, {n: g[n] for n in WEIGHT_GROUPS[group]})

    dh2b, res2, g["ln2_g"], g["ln2_b"] = _ln_bwd(dx2, sv["h2"], _behind(ln["ln2_g"], token), name=f"{tag}_ln2_bwd")
    dhh = _mm(dh2b, W["w_ffn_down"], transpose_b=True, name=f"{tag}_d_hh")
    g["w_ffn_down"] = _mm(sv["hh_t"], dh2b, name=f"{tag}_dw_down")
    dudv = _swiglu_bwd(dhh, sv["uv"], name=f"{tag}_swiglu_bwd")
    dx1 = _mm(dudv, W["w_gu"], transpose_b=True, add=res2, name=f"{tag}_d_x1")
    dwgu = _mm(sv["x1_t"], dudv, name=f"{tag}_dw_gu")
    g["w_ffn_gate"], g["w_ffn_up"] = dwgu[:, :D_FF], dwgu[:, D_FF:]
    token = done("ffn")

    dh1b, res1, g["ln1_g"], g["ln1_b"] = _ln_bwd(dx1, sv["h1"], _behind(ln["ln1_g"], token), name=f"{tag}_ln1_bwd")
    dmerged = _mm(dh1b, W["w_out"], transpose_b=True, name=f"{tag}_d_merged")
    g["w_out"] = _mm(sv["merged_t"], dh1b, name=f"{tag}_dw_out")
    dpa, dpr, dgates = _merge_bwd(dmerged, sv["gates"], sv["pa"], sv["pr"], name=f"{tag}_merge_bwd")
    dya = _mm(dpa, W["w_attn_proj"], transpose_b=True, groups=1, lane_chunks=True, name=f"{tag}_d_ya")
    g["w_attn_proj"] = _mm(sv["ya_t"], dpa, name=f"{tag}_dw_ap")
    dyb = _mm(dpr, W["w_ret_proj"], transpose_b=True, name=f"{tag}_d_yb")
    g["w_ret_proj"] = _mm(sv["yb_t"], dpr, name=f"{tag}_dw_rp")
    token = done("proj")
    tables = tables[:-1] + (_behind(tables[-1], token),)

    da, dbs = [], []
    for gi, (_, dil) in enumerate(ATTN_GROUPS):
        dqkv, db = _attn_bwd(sv["qkv_a"], biases[gi], sv["ls"][gi], dya, sv["ya"], sv["wts"][gi], gi, dil,
                             name=f"{tag}_attn_bwd{gi}")
        da.append(dqkv)
        dbs.append(db)
    dq_r, dk_r, dv_r, dg_r = _ret_bwd(dyb, sv["qk_r"], sv["v_r"], sv["g_r"], sv["o_r"], sv["states"], tables,
                                 name=f"{tag}_ret_bwd")
    dz, colsum = _assemble_dz(da, dq_r, dk_r, dv_r, dg_r, dgates, name=f"{tag}_assemble_dz")
    g["b_in"] = colsum.reshape(IN_COLS)
    dx = _mm(dz, W["w_in"], transpose_b=True, add=res1, name=f"{tag}_d_x")
    g["w_in"] = _mm(sv["x_t"], dz, name=f"{tag}_dw_in")
    return dx, g, dbs, done("in")


HBM_SPEC = pl.BlockSpec(memory_space=pltpu.HBM)
OTHER_CHIPS = ((1, 0), (0, 1), (1, 1))


def _flip(v, f):
    return 1 - v if f else v


def _all_gather(shards, *, name):
    n = len(shards)

    def body(*refs):
        x_refs, out_refs = refs[:n], refs[n:2 * n]
        send_sems, recv_sems, local_sems = refs[2 * n:]
        x, y, c = lax.axis_index("x"), lax.axis_index("y"), lax.axis_index("c")
        me, sibling = (x, y, c), (x, y, 1 - c)
        chips = [(_flip(x, fx), _flip(y, fy)) for fx, fy in OTHER_CHIPS]

        def copy(a, k, block, to, src=None):
            px, py, pc = block
            rows = out_refs[a].at[4 * px + 2 * py + pc]
            return pltpu.make_async_remote_copy(
                src_ref=rows if src is None else src, dst_ref=rows,
                send_sem=send_sems.at[7 * a + k], recv_sem=recv_sems.at[7 * a + k], device_id=to, device_id_type=MESH)

        mine, first, passed = [], [], []
        for a in range(n):
            cp = pltpu.make_async_copy(x_refs[a], out_refs[a].at[4 * x + 2 * y + c], local_sems.at[a])
            cp.start()
            mine.append(cp)
            first.append(copy(a, 0, me, sibling, src=x_refs[a]))
            first += [copy(a, 1 + j, me, (*chip, c), src=x_refs[a]) for j, chip in enumerate(chips)]
        for cp in first:
            cp.start()
        for j, chip in enumerate(chips):
            for a in range(n):
                copy(a, 1 + j, (*chip, c), me).wait_recv()
                cp = copy(a, 4 + j, (*chip, c), sibling)
                cp.start()
                passed.append(cp)
        for a in range(n):
            copy(a, 0, sibling, me).wait_recv()
            for j, chip in enumerate(chips):
                copy(a, 4 + j, (*chip, 1 - c), me).wait_recv()
        for cp in first + passed:
            cp.wait_send()
        for cp in mine:
            cp.wait()

    return pl.pallas_call(
        body, name=name, out_shape=[jax.ShapeDtypeStruct((N_DEV,) + s.shape, s.dtype) for s in shards],
        in_specs=[HBM_SPEC] * n, out_specs=[HBM_SPEC] * n,
        scratch_shapes=[pltpu.SemaphoreType.DMA((7 * n,)), pltpu.SemaphoreType.DMA((7 * n,)),
                        pltpu.SemaphoreType.DMA((n,))],
    )(*shards)


def _rs_sibling_exchange(g8s, *, name):
    n = len(g8s)

    def body(*refs):
        g_refs, recv_refs = refs[:n], refs[n:2 * n]
        send_sems, recv_sems = refs[2 * n:]
        x, y, c = lax.axis_index("x"), lax.axis_index("y"), lax.axis_index("c")
        copies = []
        for a in range(n):
            for k in range(4):
                cp = pltpu.make_async_remote_copy(
                    src_ref=g_refs[a].at[k, 1 - c], dst_ref=recv_refs[a].at[k], send_sem=send_sems.at[4 * a + k],
                    recv_sem=recv_sems.at[4 * a + k], device_id=(x, y, 1 - c), device_id_type=MESH)
                cp.start()
                copies.append(cp)
        for cp in copies:
            cp.wait()

    return pl.pallas_call(
        body, name=name,
        out_shape=[jax.ShapeDtypeStruct((4,) + g.shape[2:], g.dtype) for g in g8s],
        in_specs=[HBM_SPEC] * n, out_specs=[HBM_SPEC] * n,
        scratch_shapes=[pltpu.SemaphoreType.DMA((4 * n,)), pltpu.SemaphoreType.DMA((4 * n,))],
    )(*g8s)


def _rs_chip_sum(g8, recv, core, *, name):
    _, _, R, Wd = g8.shape
    tr = _div_tile(R, 256, 16)

    def body(core_ref, g_ref, r_ref, o_ref):
        o_ref[...] = (g_ref[...] + r_ref[...]).astype(BF16)

    grid_spec = pltpu.PrefetchScalarGridSpec(
        num_scalar_prefetch=1, grid=(4, R // tr),
        in_specs=[pl.BlockSpec((None, None, tr, Wd), lambda k, i, core_ref: (k, core_ref[0], i, 0)),
                  pl.BlockSpec((None, tr, Wd), lambda k, i, core_ref: (k, i, 0))],
        out_specs=pl.BlockSpec((None, tr, Wd), lambda k, i, core_ref: (k, i, 0)))
    return pl.pallas_call(
        body, name=name, grid_spec=grid_spec, out_shape=jax.ShapeDtypeStruct((4, R, Wd), BF16),
        compiler_params=_cparams(("parallel", "parallel")))(core, g8, recv)


def _rs_chip_exchange(ps, *, name):
    n = len(ps)

    def body(*refs):
        p_refs, out_refs = refs[:n], refs[n:2 * n]
        send_sems, recv_sems, local_sems = refs[2 * n:]
        x, y, c = lax.axis_index("x"), lax.axis_index("y"), lax.axis_index("c")
        my_chip = 2 * x + y
        copies = []
        for a in range(n):
            mine = pltpu.make_async_copy(p_refs[a].at[my_chip], out_refs[a].at[my_chip], local_sems.at[a])
            mine.start()
            copies.append(mine)
            for j, (fx, fy) in enumerate(OTHER_CHIPS):
                px, py = _flip(x, fx), _flip(y, fy)
                cp = pltpu.make_async_remote_copy(
                    src_ref=p_refs[a].at[2 * px + py], dst_ref=out_refs[a].at[my_chip],
                    send_sem=send_sems.at[3 * a + j], recv_sem=recv_sems.at[3 * a + j],
                    device_id=(px, py, c), device_id_type=MESH)
                cp.start()
                copies.append(cp)
        for cp in copies:
            cp.wait()

    return pl.pallas_call(
        body, name=name, out_shape=[jax.ShapeDtypeStruct(p.shape, p.dtype) for p in ps],
        in_specs=[HBM_SPEC] * n, out_specs=[HBM_SPEC] * n,
        scratch_shapes=[pltpu.SemaphoreType.DMA((3 * n,)), pltpu.SemaphoreType.DMA((3 * n,)),
                        pltpu.SemaphoreType.DMA((n,))],
    )(*ps)


SEM_SPEC = pl.BlockSpec(memory_space=pltpu.SEMAPHORE)
DATAFLOW = pltpu.SideEffectType.DATAFLOW_SIDE_EFFECTING


def _direct_copies(src_refs, land_refs, send_sems, recv_sems, per_peer):
    x, y, c = lax.axis_index("x"), lax.axis_index("y"), lax.axis_index("c")
    me = 4 * x + 2 * y + c
    copies = []
    for a, (s, l) in enumerate(zip(src_refs, land_refs)):
        for rel in range(1, N_DEV):
            px, py, pc = _flip(x, rel & 4), _flip(y, rel & 2), _flip(c, rel & 1)
            copies.append(pltpu.make_async_remote_copy(
                src_ref=s.at[4 * px + 2 * py + pc] if per_peer else s, dst_ref=l.at[me],
                send_sem=send_sems.at[7 * a + rel - 1], recv_sem=recv_sems.at[7 * a + rel - 1],
                device_id=(px, py, pc), device_id_type=MESH))
    return copies


def _exchange_start(srcs, per_peer, *, name):
    n = len(srcs)
    lands = [lax.empty((N_DEV,) + (s.shape[1:] if per_peer else s.shape), s.dtype) for s in srcs]
    operands = [pltpu.with_memory_space_constraint(t, pltpu.HBM) for t in list(srcs) + lands]

    def body(*refs):
        src_refs, land_refs = refs[:n], refs[n:2 * n]
        send_sems, recv_sems = refs[2 * n], refs[2 * n + 1]
        token = refs[-1]
        for cp in _direct_copies(src_refs, land_refs, send_sems, recv_sems, per_peer):
            cp.start()
        token[...] = jnp.zeros_like(token)

    return pl.pallas_call(
        body, name=name,
        out_shape=(pltpu.SemaphoreType.DMA((7 * n,)), pltpu.SemaphoreType.DMA((7 * n,)),
                   *[pltpu.HBM(t.shape, t.dtype) for t in operands], jax.ShapeDtypeStruct((8, LANES), F32)),
        in_specs=[HBM_SPEC] * (2 * n),
        out_specs=(SEM_SPEC, SEM_SPEC, *[HBM_SPEC] * (2 * n), pl.BlockSpec(memory_space=pltpu.VMEM)),
        input_output_aliases={i: 2 + i for i in range(2 * n)},
        compiler_params=pltpu.CompilerParams(has_side_effects=DATAFLOW))(*operands)


def _exchange_wait(started, after, per_peer, *, name):
    n = (len(started) - 3) // 2
    send_sems, recv_sems = started[0], started[1]
    thru = list(started[2:2 + 2 * n])

    def body(*refs):
        src_refs, land_refs = refs[:n], refs[n:2 * n]
        send_s, recv_s = refs[2 * n], refs[2 * n + 1]
        for cp in _direct_copies(src_refs, land_refs, send_s, recv_s, per_peer):
            cp.wait_send()
            cp.wait_recv()

    outs = pl.pallas_call(
        body, name=name, out_shape=tuple(pltpu.HBM(t.shape, t.dtype) for t in thru),
        in_specs=[HBM_SPEC] * (2 * n) + [SEM_SPEC, SEM_SPEC, pl.BlockSpec(memory_space=pl.ANY)],
        out_specs=[HBM_SPEC] * (2 * n), input_output_aliases={i: i for i in range(2 * n)},
        compiler_params=pltpu.CompilerParams(has_side_effects=DATAFLOW))(*thru, send_sems, recv_sems, after)
    return list(outs[n:])


def _all_reduce_small(v, *, name):
    R, Wd = v.shape

    def body(v_ref, out_ref, slots, send_sems, recv_sems):
        x, y, c = lax.axis_index("x"), lax.axis_index("y"), lax.axis_index("c")
        me = 4 * x + 2 * y + c
        slots[me] = v_ref[...]
        copies = []
        for rel in range(1, N_DEV):
            peer = (_flip(x, rel & 4), _flip(y, rel & 2), _flip(c, rel & 1))
            cp = pltpu.make_async_remote_copy(
                src_ref=v_ref, dst_ref=slots.at[me], send_sem=send_sems.at[rel - 1],
                recv_sem=recv_sems.at[rel - 1], device_id=peer, device_id_type=MESH)
            cp.start()
            copies.append(cp)
        for cp in copies:
            cp.wait()
        acc = slots[0]
        for j in range(1, N_DEV):
            acc = acc + slots[j]
        out_ref[...] = acc

    vm = pl.BlockSpec(memory_space=pltpu.VMEM)
    return pl.pallas_call(
        body, name=name, out_shape=jax.ShapeDtypeStruct((R, Wd), F32),
        in_specs=[vm], out_specs=vm,
        scratch_shapes=[pltpu.VMEM((N_DEV, R, Wd), F32), pltpu.SemaphoreType.DMA((7,)),
                        pltpu.SemaphoreType.DMA((7,))],
    )(v)


def _adam_math(w, g, m, v):
    m2 = ADAM_B1 * m + (1.0 - ADAM_B1) * g
    v2 = ADAM_B2 * v + (1.0 - ADAM_B2) * (g * g)
    m_hat = m2 / (1.0 - ADAM_B1 ** ADAM_STEP)
    v_hat = v2 / (1.0 - ADAM_B2 ** ADAM_STEP)
    delta = -ADAM_LR * (m_hat / (jnp.sqrt(v_hat) + ADAM_EPS) + ADAM_WD * w)
    return delta, m2, v2


def _adam_sharded(parts, w, m, v, *, name):
    _, R, Wd = w.shape
    tr = _div_tile(R, 256, 16)

    def body(p0_ref, p1_ref, w_ref, m_ref, v_ref, g_ref, d_ref, m2_ref, v2_ref):
        def slot_sum(p_ref):
            g = p_ref[0].astype(F32)
            for s in range(1, p_ref.shape[0]):
                g = g + p_ref[s].astype(F32)
            return g

        g = jnp.where(pl.program_id(0) == 0, slot_sum(p0_ref), slot_sum(p1_ref))
        delta, m2, v2 = _adam_math(w_ref[...], g, m_ref[...], v_ref[...])
        g_ref[...] = g
        d_ref[...] = delta
        m2_ref[...] = m2
        v2_ref[...] = v2

    assert DEPTH == 2
    p_specs = [pl.BlockSpec((p.shape[0], tr, Wd), lambda l, i: (0, i, 0)) for p in parts]
    s_spec = pl.BlockSpec((None, tr, Wd), lambda l, i: (l, i, 0))
    return pl.pallas_call(
        body, name=name, grid=(DEPTH, R // tr),
        in_specs=p_specs + [s_spec, s_spec, s_spec],
        out_specs=[s_spec] * 4, out_shape=[jax.ShapeDtypeStruct((DEPTH, R, Wd), F32)] * 4,
        compiler_params=_cparams(("parallel", "parallel")))(parts[0], parts[1], w, m, v)


def _adam_small(g, w, m, v, *, name):
    R, Wd = w.shape

    def body(g_ref, w_ref, m_ref, v_ref, d_ref, m2_ref, v2_ref):
        delta, m2, v2 = _adam_math(w_ref[...], g_ref[...], m_ref[...], v_ref[...])
        d_ref[...] = delta
        m2_ref[...] = m2
        v2_ref[...] = v2

    return pl.pallas_call(
        body, name=name, out_shape=[jax.ShapeDtypeStruct((R, Wd), F32)] * 3,
        compiler_params=_cparams())(g, w, m, v)


def _shard_shape(name):
    r, c = FULL_SHAPE[name]
    return (r, c // N_DEV) if name in COL_SHARDED else (r // N_DEV, c)


def _full_from_gathered(name, g):
    if name in COL_SHARDED:
        return jnp.transpose(g, (1, 0, 2)).reshape(FULL_SHAPE[name])
    return g.reshape(FULL_SHAPE[name])


def _dest_major(name, gfull):
    r, c = _shard_shape(name)
    if name in COL_SHARDED:
        blk = jnp.transpose(gfull.reshape(r, N_DEV, c), (1, 0, 2))
    else:
        blk = gfull.reshape(N_DEV, r, c)
    return blk.reshape(4, 2, r, c)


def _pack_small(t):
    flat = jnp.concatenate([t[n].reshape(-1).astype(F32) for n in SMALL_WEIGHTS])
    return jnp.pad(flat, (0, SMALL_ROWS * LANES - flat.shape[0])).reshape(SMALL_ROWS, LANES)


def _unpack_small(packed):
    flat = packed.reshape(-1)
    out, off = {}, 0
    for n in SMALL_WEIGHTS:
        size = math.prod(SMALL_SHAPE[n])
        out[n] = flat[off:off + size].reshape(SMALL_SHAPE[n])
        off += size
    return out


def _after(value, token):
    return lax.optimization_barrier((value, token))[0]


def _local_step(x, target, rel_bias, b_in, lns, weights_of, on_grads=None):
    S = x.shape[0]
    tables = _ret_tables(S)
    biases = [_attn_bias(rel_bias, gi, dil) for gi, (_, dil) in enumerate(ATTN_GROUPS)]

    h = x
    hb, h_t = _cast_transpose(x, name="cast_x")
    saved, Ws = [], []
    for l in range(DEPTH):
        h, hb, h_t, sv, W = _layer_fwd(l, h, hb, h_t, weights_of, b_in[l], biases, lns[l], tables)
        saved.append(sv)
        Ws.append(W)
    dy, sq = _loss_fwd_bwd(h, target, name="loss")
    loss_local = 0.5 * sq[0, 0] / D_MODEL

    grads = [None] * DEPTH
    db_tot = None
    dx = dy
    token = None
    for l in reversed(range(DEPTH)):
        dx, g, dbs, token = _layer_bwd(l, dx, saved[l], Ws[l], biases, lns[l], tables, token, on_grads)
        grads[l] = g
        db_tot = dbs if db_tot is None else [a + b for a, b in zip(db_tot, dbs)]
    small = {"rel_bias": _bias_grad(db_tot, name="bias_grad"),
             "b_in": jnp.stack([grads[l]["b_in"] for l in range(DEPTH)])}
    for n in ("ln1_g", "ln1_b", "ln2_g", "ln2_b"):
        small[n] = jnp.stack([grads[l][n].reshape(D_MODEL) for l in range(DEPTH)])
    return loss_local, dx, grads, small


def kernel(x, rel_bias, w_in, b_in, w_attn_proj, w_ret_proj, w_out, ln1_g, ln1_b, w_ffn_gate, w_ffn_up, w_ffn_down, ln2_g, ln2_b, loss_target, m_rel_bias, m_w_in, m_b_in, m_w_attn_proj, m_w_ret_proj, m_w_out, m_ln1_g, m_ln1_b, m_w_ffn_gate, m_w_ffn_up, m_w_ffn_down, m_ln2_g, m_ln2_b, v_rel_bias, v_w_in, v_b_in, v_w_attn_proj, v_w_ret_proj, v_w_out, v_ln1_g, v_ln1_b, v_w_ffn_gate, v_w_ffn_up, v_w_ffn_down, v_ln2_g, v_ln2_b):
    w = dict(rel_bias=rel_bias, w_in=w_in, b_in=b_in, w_attn_proj=w_attn_proj, w_ret_proj=w_ret_proj, w_out=w_out,
             ln1_g=ln1_g, ln1_b=ln1_b, w_ffn_gate=w_ffn_gate, w_ffn_up=w_ffn_up, w_ffn_down=w_ffn_down,
             ln2_g=ln2_g, ln2_b=ln2_b)
    m = dict(rel_bias=m_rel_bias, w_in=m_w_in, b_in=m_b_in, w_attn_proj=m_w_attn_proj, w_ret_proj=m_w_ret_proj,
             w_out=m_w_out, ln1_g=m_ln1_g, ln1_b=m_ln1_b, w_ffn_gate=m_w_ffn_gate, w_ffn_up=m_w_ffn_up,
             w_ffn_down=m_w_ffn_down, ln2_g=m_ln2_g, ln2_b=m_ln2_b)
    v = dict(rel_bias=v_rel_bias, w_in=v_w_in, b_in=v_b_in, w_attn_proj=v_w_attn_proj, w_ret_proj=v_w_ret_proj,
             w_out=v_w_out, ln1_g=v_ln1_g, ln1_b=v_ln1_b, w_ffn_gate=v_w_ffn_gate, w_ffn_up=v_w_ffn_up,
             w_ffn_down=v_w_ffn_down, ln2_g=v_ln2_g, ln2_b=v_ln2_b)

    assert DEPTH == 2
    me = 4 * lax.axis_index("x") + 2 * lax.axis_index("y") + lax.axis_index("c")
    core = lax.axis_index("c").astype(jnp.int32).reshape(1)

    def own_slot(lands, blocks):
        return [lax.dynamic_update_index_in_dim(land, blk, me, 0) for land, blk in zip(lands, blocks)]

    shard = {(l, n): w[n][l].astype(BF16) for l in range(DEPTH) for n in BIG_WEIGHTS}
    rest = WEIGHT_GROUPS["proj"] + WEIGHT_GROUPS["ffn"]
    (w_in0,) = _all_gather([shard[0, "w_in"]], name="all_gather_l0_in")
    gathers = {0: (rest, _exchange_start(_after([shard[0, n] for n in rest], w_in0), False,
                                         name="all_gather_l0_rest_start"))}
    first_token = gathers[0][1][-1][0, 0].astype(BF16)
    gathers[1] = (BIG_WEIGHTS, _exchange_start([shard[1, n] + first_token for n in BIG_WEIGHTS], False,
                                               name="all_gather_l1_start"))
    b_in_fwd = [_behind(b_in[0], gathers[1][1][-1]), b_in[1]]
    arrived = {}

    def weights_of(l, group, after):
        if (l, group) == (0, "in"):
            return {"w_in": _full_from_gathered("w_in", w_in0)}
        if l not in arrived:
            names, started = gathers[l]
            lands = _exchange_wait(started, after, False, name=f"all_gather_l{l}_wait")
            full = own_slot(lands, [shard[l, n] for n in names])
            arrived[l] = {n: _full_from_gathered(n, g) for n, g in zip(names, full)}
        names = WEIGHT_GROUPS["in"] if group == "in" else rest
        return {n: arrived[l][n] for n in names}

    scatters = {}

    def on_grads(l, group, gd):
        if (l, group) == (0, "in"):
            return None
        names = WEIGHT_GROUPS[group]
        blocks = [_dest_major(n, gd[n]).reshape((N_DEV,) + _shard_shape(n)).astype(BF16) for n in names]
        scatters[l, group] = (names, blocks, _exchange_start(blocks, True, name=f"rs_l{l}_{group}_start"))
        return scatters[l, group][2][-1]

    lns = [{n: w[n][l] for n in ("ln1_g", "ln1_b", "ln2_g", "ln2_b")} for l in range(DEPTH)]
    loss_local, grad_x, grads, small = _local_step(x[0], loss_target[0], rel_bias, b_in_fwd, lns, weights_of,
                                                   on_grads)
    loss = lax.psum(loss_local, ("x", "y", "c"))

    g8 = [_dest_major("w_in", grads[0]["w_in"])]
    from_sibling = _rs_sibling_exchange(g8, name="rs_sibling_exchange_l0_in")
    chip_parts = [_rs_chip_sum(g8[0], from_sibling[0], core, name="rs_chip_sum_l0_in")]
    parts = {(0, "w_in"): _rs_chip_exchange(chip_parts, name="rs_chip_exchange_l0_in")[0]}
    for (l, group), (names, blocks, started) in scatters.items():
        lands = _exchange_wait(started, parts[0, "w_in"], True, name=f"rs_l{l}_{group}_wait")
        own = [lax.dynamic_index_in_dim(b, me, 0, keepdims=False) for b in blocks]
        for n, p in zip(names, own_slot(lands, own)):
            parts[l, n] = p
    big = [{} for _ in range(4)]
    for n in BIG_WEIGHTS:
        res = _adam_sharded([parts[l, n] for l in range(DEPTH)], w[n], m[n], v[n], name=f"adam_{n}")
        for kind in range(4):
            big[kind][n] = res[kind]

    gs = _all_reduce_small(_pack_small(small), name="all_reduce_small")
    ds, ms, vs = _adam_small(gs, _pack_small(w), _pack_small(m), _pack_small(v), name="adam_small")
    sm = [_unpack_small(t) for t in (gs, ds, ms, vs)]

    outs = [loss, grad_x[None]]
    for kind in range(4):
        for n in ALL_WEIGHTS:
            outs.append(big[kind][n] if n in BIG_WEIGHTS else sm[kind][n])
    return tuple(outs)
```

```python
import functools
import math

import numpy as np
import jax
import jax.numpy as jnp
from jax import lax
from jax.experimental import pallas as pl
from jax.experimental.pallas import tpu as pltpu

F32 = jnp.float32
BF16 = jnp.bfloat16
MESH = pl.DeviceIdType.MESH

D_MODEL = 1024
DEPTH = 2
HEAD_DIM = 64
ATTN_GROUPS = ((128, 1), (512, 4), (2048, 16))
HEADS_PER_GROUP = 6
GROUP_WIDTH = HEADS_PER_GROUP * HEAD_DIM
ATTN_BLOCK = 128
NUM_BUCKETS = 32
MAX_DISTANCE = 2048
RET_HEADS = 4
RET_QK_DIM = 256
RET_V_DIM = 512
RET_CHUNK = 128
RET_T_CHUNKS = 4
ROPE_BASE = 10000.0
D_FF = 2816
ALPHA = (2 * DEPTH) ** 0.25
LN_EPS = 1e-5
GN_EPS = 1e-5
ATTN_W = 3 * GROUP_WIDTH
IN_COLS = 3 * ATTN_W + 2 * 1024 + 2 * 2048 + 2 * 1024
ADAM_LR, ADAM_B1, ADAM_B2, ADAM_EPS, ADAM_WD, ADAM_STEP = 0.001, 0.9, 0.999, 1e-08, 0.01, 10
N_DEV = 8
NEG = -1e30
LANES = 128
VMEM_LIMIT = 56 * 1024 * 1024
MM_TILE_CAP = 1664
MM_VMEM_BUDGET = 44 * 1024 * 1024

BIG_WEIGHTS = ("w_in", "w_attn_proj", "w_ret_proj", "w_out", "w_ffn_gate", "w_ffn_up", "w_ffn_down")
COL_SHARDED = ("w_in", "w_attn_proj", "w_ffn_gate", "w_ffn_up")
FULL_SHAPE = {"w_in": (D_MODEL, IN_COLS), "w_attn_proj": (GROUP_WIDTH, D_MODEL), "w_ret_proj": (2048, D_MODEL),
              "w_out": (D_MODEL, D_MODEL), "w_ffn_gate": (D_MODEL, D_FF), "w_ffn_up": (D_MODEL, D_FF),
              "w_ffn_down": (D_FF, D_MODEL)}
SMALL_WEIGHTS = ("rel_bias", "b_in", "ln1_g", "ln1_b", "ln2_g", "ln2_b")
SMALL_SHAPE = {"rel_bias": (NUM_BUCKETS, 18), "b_in": (DEPTH, IN_COLS), "ln1_g": (DEPTH, D_MODEL),
               "ln1_b": (DEPTH, D_MODEL), "ln2_g": (DEPTH, D_MODEL), "ln2_b": (DEPTH, D_MODEL)}
SMALL_ROWS = 256
ALL_WEIGHTS = ("rel_bias", "w_in", "b_in", "w_attn_proj", "w_ret_proj", "w_out", "ln1_g", "ln1_b",
               "w_ffn_gate", "w_ffn_up", "w_ffn_down", "ln2_g", "ln2_b")


def _cparams(sem=None):
    return pltpu.CompilerParams(dimension_semantics=sem, vmem_limit_bytes=VMEM_LIMIT)


def _div_tile(n, cap, unit):
    if n <= cap:
        return n
    best = None
    for t in range(unit, cap + 1, unit):
        if n % t == 0:
            best = t
    assert best is not None, (n, cap, unit)
    return best


def _mm(a, b, *, name, out_dtype=F32, bias=None, add=None, groups=None, lane_chunks=False, transpose_b=False):
    M, K = a.shape
    N, K2 = b.shape if transpose_b else b.shape[::-1]
    assert K == K2 and a.dtype == BF16 and b.dtype == BF16
    has_bias, has_add = bias is not None, add is not None
    tm = _div_tile(M, 1024, 16)
    tn = N // groups if groups else _div_tile(N, MM_TILE_CAP, LANES)
    out_bytes = jnp.dtype(out_dtype).itemsize

    def vmem_bytes(tk):
        return 2 * (2 * tm * tk + 2 * tk * tn + out_bytes * tm * tn + (4 * tm * tn if has_add else 0))

    tk = max(t for t in range(LANES, K + 1, LANES) if K % t == 0 and (t == LANES or vmem_bytes(t) <= MM_VMEM_BUDGET))
    nk = K // tk
    nch = tn // LANES
    assert nk == 1 or (out_dtype == F32 and not lane_chunks and not has_bias)

    def body(*refs):
        a_ref, b_ref = refs[0], refs[1]
        pos = 2
        bias_ref = add_ref = None
        if has_bias:
            bias_ref = refs[pos]
            pos += 1
        if has_add:
            add_ref = refs[pos]
            pos += 1
        o_ref = refs[pos]

        def finish(r):
            if has_bias:
                r = r + bias_ref[...]
            if has_add:
                r = r + add_ref[...]
            if lane_chunks:
                for c in range(nch):
                    o_ref[c] = r[:, c * LANES:(c + 1) * LANES].astype(o_ref.dtype)
            else:
                o_ref[...] = r.astype(o_ref.dtype)

        def product():
            if transpose_b:
                return lax.dot_general(a_ref[...], b_ref[...], (((1,), (1,)), ((), ())), preferred_element_type=F32)
            return jnp.dot(a_ref[...], b_ref[...], preferred_element_type=F32)

        if nk == 1:
            finish(product())
        else:
            @pl.when(pl.program_id(2) == 0)
            def _():
                o_ref[...] = add_ref[...] if has_add else jnp.zeros_like(o_ref)

            o_ref[...] += product()

    in_specs = [pl.BlockSpec((tm, tk), lambda i, j, k: (i, k)),
                pl.BlockSpec((tn, tk), lambda i, j, k: (j, k)) if transpose_b
                else pl.BlockSpec((tk, tn), lambda i, j, k: (k, j))]
    args = [a, b]
    if has_bias:
        in_specs.append(pl.BlockSpec((1, tn), lambda i, j, k: (0, j)))
        args.append(bias.reshape(1, N).astype(F32))
    if has_add:
        in_specs.append(pl.BlockSpec((tm, tn), lambda i, j, k: (i, j)))
        args.append(add)
    if lane_chunks:
        assert groups
        out_shape = jax.ShapeDtypeStruct((groups, nch, M, LANES), out_dtype)
        out_spec = pl.BlockSpec((None, nch, tm, LANES), lambda i, j, k: (j, 0, i, 0))
    elif groups:
        out_shape = jax.ShapeDtypeStruct((groups, M, tn), out_dtype)
        out_spec = pl.BlockSpec((None, tm, tn), lambda i, j, k: (j, i, 0))
    else:
        out_shape = jax.ShapeDtypeStruct((M, N), out_dtype)
        out_spec = pl.BlockSpec((tm, tn), lambda i, j, k: (i, j))
    out = pl.pallas_call(
        body, name=name, grid=(M // tm, N // tn, nk), in_specs=in_specs, out_specs=out_spec,
        out_shape=out_shape,
        compiler_params=_cparams(("parallel", "parallel", "arbitrary")))(*args)
    return out.reshape(groups * nch, M, LANES) if lane_chunks else out


def _row_spec(tr, w):
    return pl.BlockSpec((tr, w), lambda i: (i, 0))


def _vec_spec(w):
    return pl.BlockSpec((1, w), lambda i: (0, 0))


def _col_spec(w, tr):
    return pl.BlockSpec((w, tr), lambda i: (0, i))


def _cast_transpose(x, *, name):
    S, W = x.shape
    tr = 512

    def body(x_ref, o_ref, ot_ref):
        v = x_ref[...]
        o_ref[...] = v.astype(BF16)
        ot_ref[...] = v.T.astype(BF16)

    return pl.pallas_call(
        body, name=name, grid=(S // tr,), in_specs=[_row_spec(tr, W)],
        out_specs=[_row_spec(tr, W), _col_spec(W, tr)],
        out_shape=[jax.ShapeDtypeStruct((S, W), BF16), jax.ShapeDtypeStruct((W, S), BF16)],
        compiler_params=_cparams(("parallel",)))(x)


def _ln_fwd(x, sub, g, b, *, name):
    S, W = x.shape
    tr = 512

    def body(x_ref, s_ref, g_ref, b_ref, h_ref, y_ref, yb_ref, ybt_ref):
        h = ALPHA * x_ref[...] + s_ref[...]
        mu = jnp.mean(h, axis=-1, keepdims=True)
        d = h - mu
        var = jnp.mean(d * d, axis=-1, keepdims=True)
        y = d * lax.rsqrt(var + LN_EPS) * g_ref[...] + b_ref[...]
        h_ref[...] = h
        y_ref[...] = y
        yb_ref[...] = y.astype(BF16)
        ybt_ref[...] = y.T.astype(BF16)

    return pl.pallas_call(
        body, name=name, grid=(S // tr,),
        in_specs=[_row_spec(tr, W), _row_spec(tr, W), _vec_spec(W), _vec_spec(W)],
        out_specs=[_row_spec(tr, W)] * 3 + [_col_spec(W, tr)],
        out_shape=[jax.ShapeDtypeStruct((S, W), F32), jax.ShapeDtypeStruct((S, W), F32),
                   jax.ShapeDtypeStruct((S, W), BF16), jax.ShapeDtypeStruct((W, S), BF16)],
        compiler_params=_cparams(("parallel",)))(x, sub, g.reshape(1, W), b.reshape(1, W))


def _ln_bwd(dy, h, g, *, name):
    S, W = dy.shape
    tr = 512

    def body(dy_ref, h_ref, g_ref, dhb_ref, res_ref, dg_ref, db_ref):
        @pl.when(pl.program_id(0) == 0)
        def _():
            dg_ref[...] = jnp.zeros_like(dg_ref)
            db_ref[...] = jnp.zeros_like(db_ref)

        hh = h_ref[...]
        mu = jnp.mean(hh, axis=-1, keepdims=True)
        d = hh - mu
        var = jnp.mean(d * d, axis=-1, keepdims=True)
        rstd = lax.rsqrt(var + LN_EPS)
        xhat = d * rstd
        dyv = dy_ref[...]
        dg_ref[...] += jnp.sum(dyv * xhat, axis=0, keepdims=True)
        db_ref[...] += jnp.sum(dyv, axis=0, keepdims=True)
        dxh = dyv * g_ref[...]
        dh = rstd * (dxh - jnp.mean(dxh, axis=-1, keepdims=True)
                     - xhat * jnp.mean(dxh * xhat, axis=-1, keepdims=True))
        dhb_ref[...] = dh.astype(BF16)
        res_ref[...] = ALPHA * dh

    return pl.pallas_call(
        body, name=name, grid=(S // tr,),
        in_specs=[_row_spec(tr, W), _row_spec(tr, W), _vec_spec(W)],
        out_specs=[_row_spec(tr, W), _row_spec(tr, W), _vec_spec(W), _vec_spec(W)],
        out_shape=[jax.ShapeDtypeStruct((S, W), BF16), jax.ShapeDtypeStruct((S, W), F32),
                   jax.ShapeDtypeStruct((1, W), F32), jax.ShapeDtypeStruct((1, W), F32)],
        compiler_params=_cparams(("arbitrary",)))(dy, h, g.reshape(1, W))


def _loss_fwd_bwd(y, target, *, name):
    S, W = y.shape
    tr = 512

    def body(y_ref, t_ref, dy_ref, acc_ref):
        @pl.when(pl.program_id(0) == 0)
        def _():
            acc_ref[...] = jnp.zeros_like(acc_ref)

        e = y_ref[...] - t_ref[...]
        acc_ref[...] += jnp.sum(jnp.sum(e * e, axis=-1, keepdims=True), axis=0, keepdims=True)
        dy_ref[...] = e * (1.0 / W)

    return pl.pallas_call(
        body, name=name, grid=(S // tr,),
        in_specs=[_row_spec(tr, W), _row_spec(tr, W)],
        out_specs=[_row_spec(tr, W), pl.BlockSpec((1, 1), lambda i: (0, 0))],
        out_shape=[jax.ShapeDtypeStruct((S, W), F32), jax.ShapeDtypeStruct((1, 1), F32)],
        compiler_params=_cparams(("arbitrary",)))(y, target)


def _combine_fwd(os_, ls_, *, name):
    NCH, S, _ = os_[0].shape
    W = NCH * LANES
    tr = 512

    def body(o0, o1, o2, l0, l1, l2, yb_ref, ybt_ref, y_ref, w0_ref, w1_ref, w2_ref):
        for c in range(NCH):
            la, lb, lc = l0[c], l1[c], l2[c]
            m = jnp.maximum(jnp.maximum(la, lb), lc)
            ea, eb, ec = jnp.exp(la - m), jnp.exp(lb - m), jnp.exp(lc - m)
            inv = 1.0 / (ea + eb + ec)
            wa, wb, wc = ea * inv, eb * inv, ec * inv
            y = wa * o0[c] + wb * o1[c] + wc * o2[c]
            y_ref[c] = y
            yb_ref[:, c * LANES:(c + 1) * LANES] = y.astype(BF16)
            ybt_ref[c * LANES:(c + 1) * LANES, :] = y.T.astype(BF16)
            w0_ref[c] = wa
            w1_ref[c] = wb
            w2_ref[c] = wc

    ch = pl.BlockSpec((NCH, tr, LANES), lambda i: (0, i, 0))
    yb, ybt, y, w0, w1, w2 = pl.pallas_call(
        body, name=name, grid=(S // tr,),
        in_specs=[ch] * 6,
        out_specs=[_row_spec(tr, W), _col_spec(W, tr)] + [ch] * 4,
        out_shape=[jax.ShapeDtypeStruct((S, W), BF16), jax.ShapeDtypeStruct((W, S), BF16)]
        + [jax.ShapeDtypeStruct((NCH, S, LANES), F32)] * 4,
        compiler_params=_cparams(("parallel",)))(*os_, *ls_)
    return yb, ybt, y, (w0, w1, w2)


def _merge_fwd(gates, pa, pr, *, name):
    S, W = pa.shape
    tr = 512

    def body(g_ref, pa_ref, pr_ref, o_ref, ot_ref):
        m = jax.nn.sigmoid(g_ref[0]) * pa_ref[...] + jax.nn.sigmoid(g_ref[1]) * pr_ref[...]
        o_ref[...] = m.astype(BF16)
        ot_ref[...] = m.T.astype(BF16)

    return pl.pallas_call(
        body, name=name, grid=(S // tr,),
        in_specs=[pl.BlockSpec((2, tr, W), lambda i: (0, i, 0)), _row_spec(tr, W), _row_spec(tr, W)],
        out_specs=[_row_spec(tr, W), _col_spec(W, tr)],
        out_shape=[jax.ShapeDtypeStruct((S, W), BF16), jax.ShapeDtypeStruct((W, S), BF16)],
        compiler_params=_cparams(("parallel",)))(gates, pa, pr)


def _merge_bwd(dm, gates, pa, pr, *, name):
    S, W = pa.shape
    tr = 256

    def body(dm_ref, g_ref, pa_ref, pr_ref, dpa_ref, dpr_ref, dg_ref):
        dmv = dm_ref[...]
        sa, sb = jax.nn.sigmoid(g_ref[0]), jax.nn.sigmoid(g_ref[1])
        dpa_ref[...] = (dmv * sa).astype(BF16)
        dpr_ref[...] = (dmv * sb).astype(BF16)
        dg_ref[0] = (dmv * pa_ref[...] * (sa * (1.0 - sa))).astype(BF16)
        dg_ref[1] = (dmv * pr_ref[...] * (sb * (1.0 - sb))).astype(BF16)

    g3 = pl.BlockSpec((2, tr, W), lambda i: (0, i, 0))
    return pl.pallas_call(
        body, name=name, grid=(S // tr,),
        in_specs=[_row_spec(tr, W), g3, _row_spec(tr, W), _row_spec(tr, W)],
        out_specs=[_row_spec(tr, W), _row_spec(tr, W), g3],
        out_shape=[jax.ShapeDtypeStruct((S, W), BF16), jax.ShapeDtypeStruct((S, W), BF16),
                   jax.ShapeDtypeStruct((2, S, W), BF16)],
        compiler_params=_cparams(("parallel",)))(dm, gates, pa, pr)


def _swiglu_fwd(uv, *, name):
    _, S, W = uv.shape
    tr = 256

    def body(uv_ref, o_ref, ot_ref):
        u = uv_ref[0]
        hh = u * jax.nn.sigmoid(u) * uv_ref[1]
        o_ref[...] = hh.astype(BF16)
        ot_ref[...] = hh.T.astype(BF16)

    return pl.pallas_call(
        body, name=name, grid=(S // tr,),
        in_specs=[pl.BlockSpec((2, tr, W), lambda i: (0, i, 0))],
        out_specs=[_row_spec(tr, W), _col_spec(W, tr)],
        out_shape=[jax.ShapeDtypeStruct((S, W), BF16), jax.ShapeDtypeStruct((W, S), BF16)],
        compiler_params=_cparams(("parallel",)))(uv)


def _swiglu_bwd(dh, uv, *, name):
    _, S, W = uv.shape
    tr = 256

    def body(dh_ref, uv_ref, o_ref):
        u, v, d = uv_ref[0], uv_ref[1], dh_ref[...]
        sg = jax.nn.sigmoid(u)
        o_ref[:, 0:W] = (d * v * (sg * (1.0 + u * (1.0 - sg)))).astype(BF16)
        o_ref[:, W:2 * W] = (d * (u * sg)).astype(BF16)

    return pl.pallas_call(
        body, name=name, grid=(S // tr,),
        in_specs=[_row_spec(tr, W), pl.BlockSpec((2, tr, W), lambda i: (0, i, 0))],
        out_specs=_row_spec(tr, 2 * W), out_shape=jax.ShapeDtypeStruct((S, 2 * W), BF16),
        compiler_params=_cparams(("parallel",)))(dh, uv)


def _assemble_dz(da, dq_r, dk_r, dv_r, dg_r, dgates, *, name):
    S = dv_r.shape[0]
    tr = 256
    GW = GROUP_WIDTH
    NCH = GW // LANES

    def body(*refs):
        a_refs = refs[0:9]
        q_ref, k_ref, v_ref, g_ref, gt_ref, dz_ref, cs_ref = refs[9:]

        @pl.when(pl.program_id(0) == 0)
        def _():
            cs_ref[...] = jnp.zeros_like(cs_ref)

        def put(off, val):
            w = val.shape[-1]
            dz_ref[:, off:off + w] = val.astype(BF16)
            cs_ref[:, off:off + w] += jnp.sum(val.astype(F32), axis=0, keepdims=True)

        for which in range(3):
            for gi in range(3):
                for c in range(NCH):
                    put(which * ATTN_W + gi * GW + c * LANES, a_refs[3 * gi + which][c])
        off = 3 * ATTN_W
        put(off, q_ref[...])
        put(off + 1024, k_ref[...])
        put(off + 2048, v_ref[...])
        put(off + 4096, g_ref[...])
        put(off + 6144, gt_ref[0])
        put(off + 7168, gt_ref[1])

    flat_a = [t for grp in da for t in grp]
    return pl.pallas_call(
        body, name=name, grid=(S // tr,),
        in_specs=[pl.BlockSpec((NCH, tr, LANES), lambda i: (0, i, 0))] * 9 + [_row_spec(tr, 1024), _row_spec(tr, 1024),
                  _row_spec(tr, 2048), _row_spec(tr, 2048), pl.BlockSpec((2, tr, 1024), lambda i: (0, i, 0))],
        out_specs=[_row_spec(tr, IN_COLS), _vec_spec(IN_COLS)],
        out_shape=[jax.ShapeDtypeStruct((S, IN_COLS), BF16), jax.ShapeDtypeStruct((1, IN_COLS), F32)],
        compiler_params=_cparams(("arbitrary",)))(*flat_a, dq_r, dk_r, dv_r, dg_r, dgates)


def _t5_bucket(dist):
    max_exact = NUM_BUCKETS // 2
    large = max_exact + (np.log(np.maximum(dist, max_exact) / max_exact)
                         / np.log(MAX_DISTANCE / max_exact) * (NUM_BUCKETS - max_exact)).astype(np.int32)
    large = np.minimum(large, NUM_BUCKETS - 1)
    return np.where(dist < max_exact, dist, large).astype(np.int32)


def _attn_tables(dilation):
    W = ATTN_BLOCK
    qi = np.arange(W)[:, None]
    kj = np.arange(2 * W)[None, :]
    rel = qi + W - kj
    valid = (rel >= 0) & (rel <= W)
    buckets = _t5_bucket(np.clip(rel, 0, W) * dilation)
    return buckets, valid


def _attn_bias(rel_bias, gi, dilation):
    buckets, valid = _attn_tables(dilation)
    table = rel_bias[:, gi * HEADS_PER_GROUP:(gi + 1) * HEADS_PER_GROUP]
    onehot = (jnp.asarray(buckets.reshape(-1, 1)) == jnp.arange(NUM_BUCKETS)[None, :]).astype(F32)
    bias = jnp.dot(onehot, table.astype(F32), precision=lax.Precision.HIGHEST)
    bias = bias.T.reshape(HEADS_PER_GROUP, ATTN_BLOCK, 2 * ATTN_BLOCK)
    return jnp.where(jnp.asarray(valid)[None], bias, NEG)


def _dot_nt(a, b):
    return lax.dot_general(a, b, (((1,), (1,)), ((), ())), preferred_element_type=F32)


def _dot_tn(a, b):
    return lax.dot_general(a, b, (((0,), (0,)), ((), ())), preferred_element_type=F32)


def _dot(a, b):
    return jnp.dot(a, b, preferred_element_type=F32)


ATTN_RESIDUES_PER_STEP = 4
ATTN_UNITS_AT_ONCE = 8
HEADS_PER_CHUNK = LANES // HEAD_DIM
N_CHUNKS = GROUP_WIDTH // LANES


def _first_block_mask(has_prev):
    col = lax.broadcasted_iota(jnp.int32, (1, 2 * ATTN_BLOCK), 1)
    return jnp.where(jnp.logical_or(has_prev, col >= ATTN_BLOCK), 0.0, NEG).astype(F32)


def _head_lanes(hh):
    return slice(HEAD_DIM * hh, HEAD_DIM * (hh + 1))


def _attn_geometry(S, d):
    rps = ATTN_RESIDUES_PER_STEP if d == 1 else min(d, ATTN_RESIDUES_PER_STEP)
    rows_per_block = ATTN_BLOCK * (rps if d == 1 else d)
    return rows_per_block, S // rows_per_block, rps, 1 if d == 1 else d // rps


def _unit_rows(d, rps, rg, u):
    B = ATTN_BLOCK
    if d == 1:
        return pl.ds(B * u, B), pl.ds(B * (u - 1 if u else rps - 1), B), u == 0
    rows = pl.ds(rg * rps + u, B, stride=d)
    return rows, rows, True


def _attn_in_specs(gi, RB, last):
    def spec(which, prev):
        if prev:
            return pl.BlockSpec((None, RB, LANES),
                                lambda j, n, rg: (9 * which + 3 * gi + j, jnp.clip(n - 1, 0, last), 0))
        return pl.BlockSpec((None, RB, LANES), lambda j, n, rg: (9 * which + 3 * gi + j, jnp.minimum(n, last), 0))
    bias = pl.BlockSpec((HEADS_PER_CHUNK, ATTN_BLOCK, 2 * ATTN_BLOCK), lambda j, n, rg: (j, 0, 0))
    return [spec(0, False), spec(1, True), spec(1, False), spec(2, True), spec(2, False), bias]


def _attn_fwd(qkv, bias, gi, d, *, name):
    _, S, _ = qkv.shape
    B = ATTN_BLOCK
    RB, nb, rps, nrg = _attn_geometry(S, d)
    scale = HEAD_DIM ** -0.5
    units = [(rr, hh) for rr in range(rps) for hh in range(HEADS_PER_CHUNK)]

    def body(q_ref, kp_ref, kc_ref, vp_ref, vc_ref, b_ref, o_ref, l_ref):
        n, rg = pl.program_id(1), pl.program_id(2)
        first = _first_block_mask(n > 0)
        ur = [_unit_rows(d, rps, rg, u) for u in range(rps)]
        rows = [r_ for r_, _, _ in ur]
        edge = [first if in_prev else 0.0 for _, _, in_prev in ur]
        q = [q_ref[r_, :].astype(BF16) for r_ in rows]
        k2 = [jnp.concatenate([(kp_ref if in_prev else kc_ref)[pr, :], kc_ref[r_, :]], axis=0).astype(BF16)
              for r_, pr, in_prev in ur]
        v2 = [jnp.concatenate([(vp_ref if in_prev else vc_ref)[pr, :], vc_ref[r_, :]], axis=0).astype(BF16)
              for r_, pr, in_prev in ur]
        o_part, l_part = {}, {}
        for u0 in range(0, len(units), ATTN_UNITS_AT_ONCE):
            us = units[u0:u0 + ATTN_UNITS_AT_ONCE]
            s = [_dot_nt(q[rr][:, _head_lanes(hh)], k2[rr][:, _head_lanes(hh)]) * scale + b_ref[hh] + edge[rr]
                 for rr, hh in us]
            m = [jnp.max(x, axis=-1, keepdims=True) for x in s]
            p = [jnp.exp(x - mm) for x, mm in zip(s, m)]
            l = [jnp.sum(x, axis=-1, keepdims=True) for x in p]
            pb = [(x * (1.0 / ll)).astype(BF16) for x, ll in zip(p, l)]
            o = [_dot(x, v2[rr][:, _head_lanes(hh)]) for x, (rr, hh) in zip(pb, us)]
            for u, oo, mm, ll in zip(us, o, m, l):
                o_part[u] = oo
                l_part[u] = jnp.broadcast_to(mm + jnp.log(ll), (B, HEAD_DIM))
        for rr in range(rps):
            o_ref[rows[rr], :] = jnp.concatenate([o_part[(rr, hh)] for hh in range(HEADS_PER_CHUNK)], axis=1)
            l_ref[rows[rr], :] = jnp.concatenate([l_part[(rr, hh)] for hh in range(HEADS_PER_CHUNK)], axis=1)

    out_spec = pl.BlockSpec((None, RB, LANES), lambda j, n, rg: (j, n, 0))
    return pl.pallas_call(
        body, name=name, grid=(N_CHUNKS, nb, nrg),
        in_specs=_attn_in_specs(gi, RB, nb - 1),
        out_specs=[out_spec, out_spec],
        out_shape=[jax.ShapeDtypeStruct((N_CHUNKS, S, LANES), F32)] * 2,
        compiler_params=_cparams(("parallel", "arbitrary", "arbitrary")))(qkv, qkv, qkv, qkv, qkv, bias)


def _attn_bwd(qkv, bias, lse, dya, ya, wts, gi, d, *, name):
    _, S, _ = qkv.shape
    B = ATTN_BLOCK
    RB, nb, rps, nrg = _attn_geometry(S, d)
    scale = HEAD_DIM ** -0.5
    units = [(rr, hh) for rr in range(rps) for hh in range(HEADS_PER_CHUNK)]

    def body(q_ref, kp_ref, kc_ref, vp_ref, vc_ref, b_ref, l_ref, dya_ref, ya_ref, w_ref,
             dq_ref, dk_ref, dv_ref, db_ref, dk_carry, dv_carry):
        n, rg = pl.program_id(1), pl.program_id(2)
        ur = [_unit_rows(d, rps, rg, u) for u in range(rps)]
        rows = [r_ for r_, _, _ in ur]

        @pl.when((n == 0) & (rg == 0))
        def _():
            db_ref[...] = jnp.zeros_like(db_ref)
            dk_carry[...] = jnp.zeros_like(dk_carry)
            dv_carry[...] = jnp.zeros_like(dv_carry)

        @pl.when(n < nb)
        def _():
            first = _first_block_mask(n > 0)
            edge = [first if in_prev else 0.0 for _, _, in_prev in ur]
            q = [q_ref[r_, :].astype(BF16) for r_ in rows]
            k2 = [jnp.concatenate([(kp_ref if in_prev else kc_ref)[pr, :], kc_ref[r_, :]], axis=0).astype(BF16)
                  for r_, pr, in_prev in ur]
            v2 = [jnp.concatenate([(vp_ref if in_prev else vc_ref)[pr, :], vc_ref[r_, :]], axis=0).astype(BF16)
                  for r_, pr, in_prev in ur]
            lse_c = [l_ref[r_, :] for r_ in rows]
            dy_c = [dya_ref[r_, :] for r_ in rows]
            ya_c = [ya_ref[r_, :] for r_ in rows]
            w_c = [w_ref[r_, :] for r_ in rows]
            ds_sum = [None] * HEADS_PER_CHUNK
            dq_part, dk_part, dv_part = {}, {}, {}
            for u0 in range(0, len(units), ATTN_UNITS_AT_ONCE):
                us = units[u0:u0 + ATTN_UNITS_AT_ONCE]
                hl = [_head_lanes(hh) for _, hh in us]
                qh = [q[rr][:, sl] for (rr, _), sl in zip(us, hl)]
                kh = [k2[rr][:, sl] for (rr, _), sl in zip(us, hl)]
                vh = [v2[rr][:, sl] for (rr, _), sl in zip(us, hl)]
                s = [_dot_nt(a, k) * scale + b_ref[hh] + edge[rr] for a, k, (rr, hh) in zip(qh, kh, us)]
                p = [jnp.exp(x - lse_c[rr][:, HEAD_DIM * hh:HEAD_DIM * hh + 1]) for x, (rr, hh) in zip(s, us)]
                dy = [dy_c[rr][:, sl] for (rr, _), sl in zip(us, hl)]
                w = [w_c[rr][:, sl] for (rr, _), sl in zip(us, hl)]
                shift = [ww[:, 0:1] * jnp.sum(d_ * ya_c[rr][:, sl], axis=-1, keepdims=True)
                         for ww, d_, (rr, _), sl in zip(w, dy, us, hl)]
                do = [(ww * d_).astype(BF16) for ww, d_ in zip(w, dy)]
                ds = [pp * (_dot_nt(o_, v) - sh) for pp, o_, v, sh in zip(p, do, vh, shift)]
                for x, (_, hh) in zip(ds, us):
                    ds_sum[hh] = x if ds_sum[hh] is None else ds_sum[hh] + x
                dsb = [x.astype(BF16) for x in ds]
                pb = [x.astype(BF16) for x in p]
                for u, x, pp, a, k, o_ in zip(us, dsb, pb, qh, kh, do):
                    dq_part[u] = _dot(x, k) * scale
                    dk_part[u] = _dot_tn(x, a) * scale
                    dv_part[u] = _dot_tn(pp, o_)
            for hh in range(HEADS_PER_CHUNK):
                db_ref[hh] += ds_sum[hh]
            dk2, dv2 = [], []
            for rr in range(rps):
                dq_ref[rows[rr], :] = jnp.concatenate([dq_part[(rr, hh)] for hh in range(HEADS_PER_CHUNK)], axis=1)
                dk2.append(jnp.concatenate([dk_part[(rr, hh)] for hh in range(HEADS_PER_CHUNK)], axis=1))
                dv2.append(jnp.concatenate([dv_part[(rr, hh)] for hh in range(HEADS_PER_CHUNK)], axis=1))
            for out_ref, carry, d2 in ((dk_ref, dk_carry, dk2), (dv_ref, dv_carry, dv2)):
                if d == 1:
                    out_ref[...] = carry[...]
                    out_ref[ur[0][1], :] += d2[0][0:B]
                    for rr in range(rps):
                        nxt = d2[rr + 1][0:B] if rr + 1 < rps else 0.0
                        carry[rows[rr], :] = d2[rr][B:2 * B] + nxt
                else:
                    for rr in range(rps):
                        out_ref[rows[rr], :] = carry[rows[rr], :] + d2[rr][0:B]
                        carry[rows[rr], :] = d2[rr][B:2 * B]

        @pl.when(n == nb)
        def _():
            for r_ in rows:
                dk_ref[r_, :] = dk_carry[r_, :]
                dv_ref[r_, :] = dv_carry[r_, :]

    last = nb - 1
    cur = pl.BlockSpec((None, RB, LANES), lambda j, n, rg: (j, jnp.minimum(n, last), 0))
    lag = pl.BlockSpec((None, RB, LANES), lambda j, n, rg: (j, jnp.maximum(n - 1, 0), 0))
    db_spec = pl.BlockSpec((HEADS_PER_CHUNK, B, 2 * B), lambda j, n, rg: (j, 0, 0))
    dq, dk, dv, db = pl.pallas_call(
        body, name=name, grid=(N_CHUNKS, nb + 1, nrg),
        in_specs=_attn_in_specs(gi, RB, last) + [cur, cur, cur, cur],
        out_specs=[cur, lag, lag, db_spec],
        out_shape=[jax.ShapeDtypeStruct((N_CHUNKS, S, LANES), F32)] * 3
        + [jax.ShapeDtypeStruct((HEADS_PER_GROUP, B, 2 * B), F32)],
        scratch_shapes=[pltpu.VMEM((RB, LANES), F32), pltpu.VMEM((RB, LANES), F32)],
        compiler_params=_cparams(("arbitrary", "arbitrary", "arbitrary")))(
            qkv, qkv, qkv, qkv, qkv, bias, lse, dya, ya, wts)
    return (dq, dk, dv), db


def _bias_grad(dbs, *, name):
    nk = ATTN_BLOCK * 2 * ATTN_BLOCK
    buckets = []
    for (_, dil) in ATTN_GROUPS:
        b, valid = _attn_tables(dil)
        buckets.append(np.where(valid, b, -1).reshape(1, nk))
    bk = jnp.asarray(np.stack(buckets).astype(np.int32))
    flat = [x.reshape(HEADS_PER_GROUP, nk) for x in dbs]

    def body(bk_ref, d0, d1, d2, o_ref):
        ids = lax.broadcasted_iota(jnp.int32, (NUM_BUCKETS, nk), 0)
        for gi, dref in enumerate((d0, d1, d2)):
            onehot = (ids == bk_ref[gi]).astype(F32)
            o_ref[gi] = lax.dot_general(onehot, dref[...], (((1,), (1,)), ((), ())),
                                        preferred_element_type=F32, precision=lax.Precision.HIGHEST)

    out = pl.pallas_call(
        body, name=name,
        out_shape=jax.ShapeDtypeStruct((3, NUM_BUCKETS, HEADS_PER_GROUP), F32),
        compiler_params=_cparams())(bk, *flat)
    return jnp.transpose(out, (1, 0, 2)).reshape(NUM_BUCKETS, 3 * HEADS_PER_GROUP)


def _ret_tables(S):
    half = RET_QK_DIM // 2
    pos = jnp.arange(S, dtype=F32)
    inv_freq = ROPE_BASE ** (-jnp.arange(half, dtype=F32) / half)
    ang = pos[:, None] * inv_freq[None]
    cos, sin = jnp.cos(ang), jnp.sin(ang)
    H, C = RET_HEADS, RET_CHUNK
    log_g = jnp.log(1.0 - 2.0 ** (-5.0 - jnp.arange(H, dtype=F32)))
    n = jnp.arange(C, dtype=F32)
    diff = n[:, None] - n[None, :]
    dmask = jnp.where(diff >= 0, jnp.exp(log_g[:, None, None] * jnp.maximum(diff, 0.0)), 0.0)
    q_dec = jnp.exp(log_g[:, None] * (n + 1.0))
    k_dec = jnp.exp(log_g[:, None] * (C - 1.0 - n))
    chunk_dec = jnp.exp(log_g * C)
    qd = jnp.broadcast_to(q_dec[:, :, None], (H, C, RET_QK_DIM))
    kd = jnp.broadcast_to(k_dec[:, :, None], (H, C, RET_QK_DIM))
    cd = jnp.broadcast_to(chunk_dec[:, None, None], (H, 1, RET_V_DIM))
    return cos, sin, dmask, qd, kd, cd


def _rot(t, cos, sin):
    half = RET_QK_DIM // 2
    t1, t2 = t[:, :half], t[:, half:]
    return jnp.concatenate([t1 * cos - t2 * sin, t1 * sin + t2 * cos], axis=-1)


def _unrot(t, cos, sin):
    half = RET_QK_DIM // 2
    t1, t2 = t[:, :half], t[:, half:]
    return jnp.concatenate([t1 * cos + t2 * sin, t2 * cos - t1 * sin], axis=-1)


def _ret_specs(rev, nC):
    C, DK, DV = RET_CHUNK, RET_QK_DIM, RET_V_DIM
    cidx = (lambda c: nC - 1 - c) if rev else (lambda c: c)
    H = RET_HEADS
    return dict(
        qk=lambda which: pl.BlockSpec((None, C, H * DK), lambda c: (which, cidx(c), 0)),
        q=pl.BlockSpec((C, H * DK), lambda c: (cidx(c), 0)),
        v=pl.BlockSpec((C, H * DV), lambda c: (cidx(c), 0)),
        cs=pl.BlockSpec((C, DK // 2), lambda c: (cidx(c), 0)),
        dmask=pl.BlockSpec((H, C, C), lambda c: (0, 0, 0)),
        dec=pl.BlockSpec((H, C, DK), lambda c: (0, 0, 0)),
        cd=pl.BlockSpec((H, 1, DV), lambda c: (0, 0, 0)),
        st=pl.BlockSpec((H, None, DK, DV), lambda c: (0, cidx(c), 0, 0)),
    )


def _ret_fwd(qk, v, g, tables, *, name):
    _, S, _ = qk.shape
    nC = S // RET_CHUNK
    C, DK, DV, H = RET_CHUNK, RET_QK_DIM, RET_V_DIM, RET_HEADS
    cos, sin, dmask, qd, kd, cd = tables
    kscale = DK ** -0.5

    def body(q_ref, k_ref, v_ref, g_ref, cos_ref, sin_ref, dm_ref, qd_ref, kd_ref, cd_ref,
             o_ref, yb_ref, ybt_ref, st_ref, state):
        @pl.when(pl.program_id(0) == 0)
        def _():
            state[...] = jnp.zeros_like(state)

        tcol = pl.multiple_of((pl.program_id(0) % RET_T_CHUNKS) * C, C)
        cs, sn = cos_ref[...], sin_ref[...]
        for h in range(H):
            qs, vs = slice(DK * h, DK * (h + 1)), slice(DV * h, DV * (h + 1))
            Q = _rot(q_ref[:, qs], cs, sn)
            K = _rot(k_ref[:, qs], cs, sn) * kscale
            Qb, Kb, V = Q.astype(BF16), K.astype(BF16), v_ref[:, vs]
            sb = state[h].astype(BF16)
            st_ref[h] = sb
            A = _dot_nt(Qb, Kb) * dm_ref[h]
            o = _dot(A.astype(BF16), V) + _dot((Q * qd_ref[h]).astype(BF16), sb)
            state[h] = state[h] * cd_ref[h] + _dot_tn((K * kd_ref[h]).astype(BF16), V)
            mu = jnp.mean(o, axis=-1, keepdims=True)
            dd = o - mu
            var = jnp.mean(dd * dd, axis=-1, keepdims=True)
            yn = dd * lax.rsqrt(var + GN_EPS)
            gv = g_ref[:, vs]
            yb = gv * jax.nn.sigmoid(gv) * yn
            o_ref[:, vs] = o
            yb_ref[:, vs] = yb.astype(BF16)
            ybt_ref[vs, pl.ds(tcol, C)] = yb.T.astype(BF16)

    sp = _ret_specs(False, nC)
    return pl.pallas_call(
        body, name=name, grid=(nC,),
        in_specs=[sp["qk"](0), sp["qk"](1), sp["v"], sp["v"], sp["cs"], sp["cs"], sp["dmask"],
                  sp["dec"], sp["dec"], sp["cd"]],
        out_specs=[sp["v"], sp["v"], pl.BlockSpec((H * DV, RET_T_CHUNKS * C), lambda c: (0, c // RET_T_CHUNKS)),
                   sp["st"]],
        out_shape=[jax.ShapeDtypeStruct((S, H * DV), F32), jax.ShapeDtypeStruct((S, H * DV), BF16),
                   jax.ShapeDtypeStruct((H * DV, S), BF16), jax.ShapeDtypeStruct((H, nC, DK, DV), BF16)],
        scratch_shapes=[pltpu.VMEM((H, DK, DV), F32)],
        compiler_params=_cparams(("arbitrary",)))(qk, qk, v, g, cos, sin, dmask, qd, kd, cd)


def _ret_bwd(dyb, qk, v, g, o, states, tables, *, name):
    _, S, _ = qk.shape
    nC = S // RET_CHUNK
    C, DK, DV, H = RET_CHUNK, RET_QK_DIM, RET_V_DIM, RET_HEADS
    cos, sin, dmask, qd, kd, cd = tables
    kscale = DK ** -0.5

    def body(dy_ref, q_ref, k_ref, v_ref, g_ref, o_ref, st_ref, cos_ref, sin_ref, dm_ref, qd_ref, kd_ref,
             cd_ref, dq_ref, dk_ref, dv_ref, dg_ref, dstate):
        @pl.when(pl.program_id(0) == 0)
        def _():
            dstate[...] = jnp.zeros_like(dstate)

        cs, sn = cos_ref[...], sin_ref[...]
        for h in range(H):
            qs, vs = slice(DK * h, DK * (h + 1)), slice(DV * h, DV * (h + 1))
            ov = o_ref[:, vs]
            mu = jnp.mean(ov, axis=-1, keepdims=True)
            dd = ov - mu
            var = jnp.mean(dd * dd, axis=-1, keepdims=True)
            rstd = lax.rsqrt(var + GN_EPS)
            yn = dd * rstd
            gv, dy = g_ref[:, vs], dy_ref[:, vs]
            sg = jax.nn.sigmoid(gv)
            dg_ref[:, vs] = (dy * yn * (sg * (1.0 + gv * (1.0 - sg)))).astype(BF16)
            dyn = dy * (gv * sg)
            dO = rstd * (dyn - jnp.mean(dyn, axis=-1, keepdims=True)
                         - yn * jnp.mean(dyn * yn, axis=-1, keepdims=True))
            dOb = dO.astype(BF16)

            Q = _rot(q_ref[:, qs], cs, sn)
            K = _rot(k_ref[:, qs], cs, sn) * kscale
            Qb, Kb, V = Q.astype(BF16), K.astype(BF16), v_ref[:, vs]
            dm, qd_h, kd_h = dm_ref[h], qd_ref[h], kd_ref[h]
            Sb = st_ref[h]
            dSb = dstate[h].astype(BF16)
            Ab = (_dot_nt(Qb, Kb) * dm).astype(BF16)
            dAb = (_dot_nt(dOb, V) * dm).astype(BF16)
            Qd = (Q * qd_h).astype(BF16)
            Kd = (K * kd_h).astype(BF16)
            dQ = _dot(dAb, Kb) + _dot_nt(dOb, Sb) * qd_h
            dK = _dot_tn(dAb, Qb) + _dot_nt(V, dSb) * kd_h
            dv_ref[:, vs] = (_dot_tn(Ab, dOb) + _dot(Kd, dSb)).astype(BF16)
            dstate[h] = dstate[h] * cd_ref[h] + _dot_tn(Qd, dOb)
            dq_ref[:, qs] = _unrot(dQ, cs, sn).astype(BF16)
            dk_ref[:, qs] = (_unrot(dK, cs, sn) * kscale).astype(BF16)

    sp = _ret_specs(True, nC)
    dq, dk, dv, dg = pl.pallas_call(
        body, name=name, grid=(nC,),
        in_specs=[sp["v"], sp["qk"](0), sp["qk"](1), sp["v"], sp["v"], sp["v"], sp["st"], sp["cs"], sp["cs"],
                  sp["dmask"], sp["dec"], sp["dec"], sp["cd"]],
        out_specs=[sp["q"], sp["q"], sp["v"], sp["v"]],
        out_shape=[jax.ShapeDtypeStruct((S, H * DK), BF16), jax.ShapeDtypeStruct((S, H * DK), BF16),
                   jax.ShapeDtypeStruct((S, H * DV), BF16), jax.ShapeDtypeStruct((S, H * DV), BF16)],
        scratch_shapes=[pltpu.VMEM((H, DK, DV), F32)],
        compiler_params=_cparams(("arbitrary",)))(dyb, qk, qk, v, g, o, states, cos, sin, dmask, qd, kd, cd)
    return dq, dk, dv, dg


def _layer_fwd(l, x, xb, x_t, weights_of, b_in, biases, ln, tables):
    S = x.shape[0]
    tag = f"l{l}"
    W = dict(weights_of(l, "in", x))
    win = W["w_in"]
    c0, c1, c2, c3, c4 = 3 * ATTN_W, 3 * ATTN_W + 2048, 3 * ATTN_W + 4096, 3 * ATTN_W + 6144, IN_COLS
    qkv_a = _mm(xb, win[:, :c0], bias=b_in[:c0], groups=3, lane_chunks=True, name=f"{tag}_in_attn")
    qk_r = _mm(xb, win[:, c0:c1], bias=b_in[c0:c1], groups=2, name=f"{tag}_in_retqk")
    v_r = _mm(xb, win[:, c1:c2], bias=b_in[c1:c2], out_dtype=BF16, name=f"{tag}_in_retv")
    g_r = _mm(xb, win[:, c2:c3], bias=b_in[c2:c3], name=f"{tag}_in_retg")
    gates = _mm(xb, win[:, c3:c4], bias=b_in[c3:c4], groups=2, name=f"{tag}_in_gates")

    os_, ls_ = [], []
    for gi, (_, dil) in enumerate(ATTN_GROUPS):
        o, lse = _attn_fwd(qkv_a, biases[gi], gi, dil, name=f"{tag}_attn_fwd{gi}")
        os_.append(o)
        ls_.append(lse)
    ya_b, ya_t, ya, wts = _combine_fwd(os_, ls_, name=f"{tag}_combine")

    o_r, yb, yb_t, states = _ret_fwd(qk_r, v_r, g_r, tables, name=f"{tag}_ret_fwd")

    W.update(weights_of(l, "rest", yb))
    W["w_gu"] = jnp.concatenate([W["w_ffn_gate"], W["w_ffn_up"]], axis=1)
    pa = _mm(ya_b, W["w_attn_proj"], name=f"{tag}_attn_proj")
    pr = _mm(yb, W["w_ret_proj"], name=f"{tag}_ret_proj")
    merged, merged_t = _merge_fwd(gates, pa, pr, name=f"{tag}_merge")
    mix = _mm(merged, W["w_out"], name=f"{tag}_out_proj")
    h1, x1, x1b, x1_t = _ln_fwd(x, mix, ln["ln1_g"], ln["ln1_b"], name=f"{tag}_ln1")
    uv = _mm(x1b, W["w_gu"], groups=2, name=f"{tag}_ffn_in")
    hh, hh_t = _swiglu_fwd(uv, name=f"{tag}_swiglu")
    f = _mm(hh, W["w_ffn_down"], name=f"{tag}_ffn_down")
    h2, x2, x2b, x2_t = _ln_fwd(x1, f, ln["ln2_g"], ln["ln2_b"], name=f"{tag}_ln2")
    saved = dict(x_t=x_t, qkv_a=qkv_a, qk_r=qk_r, v_r=v_r, g_r=g_r, gates=gates, ls=ls_, ya_t=ya_t, ya=ya,
                 wts=wts, o_r=o_r, yb_t=yb_t, states=states, pa=pa, pr=pr, merged_t=merged_t, h1=h1, x1_t=x1_t,
                 uv=uv, hh_t=hh_t, h2=h2)
    return x2, x2b, x2_t, saved, W


WEIGHT_GROUPS = {"in": ("w_in",), "proj": ("w_attn_proj", "w_ret_proj", "w_out"),
                 "ffn": ("w_ffn_gate", "w_ffn_up", "w_ffn_down")}


def _behind(value, token):
    return value if token is None else value + token[0, 0]


def _layer_bwd(l, dx2, sv, W, biases, ln, tables, token, on_grads):
    S = dx2.shape[0]
    tag = f"l{l}"
    g = {}

    def done(group):
        return None if on_grads is None else on_grads(l, group, {n: g[n] for n in WEIGHT_GROUPS[group]})

    dh2b, res2, g["ln2_g"], g["ln2_b"] = _ln_bwd(dx2, sv["h2"], _behind(ln["ln2_g"], token), name=f"{tag}_ln2_bwd")
    dhh = _mm(dh2b, W["w_ffn_down"], transpose_b=True, name=f"{tag}_d_hh")
    g["w_ffn_down"] = _mm(sv["hh_t"], dh2b, name=f"{tag}_dw_down")
    dudv = _swiglu_bwd(dhh, sv["uv"], name=f"{tag}_swiglu_bwd")
    dx1 = _mm(dudv, W["w_gu"], transpose_b=True, add=res2, name=f"{tag}_d_x1")
    dwgu = _mm(sv["x1_t"], dudv, name=f"{tag}_dw_gu")
    g["w_ffn_gate"], g["w_ffn_up"] = dwgu[:, :D_FF], dwgu[:, D_FF:]
    token = done("ffn")

    dh1b, res1, g["ln1_g"], g["ln1_b"] = _ln_bwd(dx1, sv["h1"], _behind(ln["ln1_g"], token), name=f"{tag}_ln1_bwd")
    dmerged = _mm(dh1b, W["w_out"], transpose_b=True, name=f"{tag}_d_merged")
    g["w_out"] = _mm(sv["merged_t"], dh1b, name=f"{tag}_dw_out")
    dpa, dpr, dgates = _merge_bwd(dmerged, sv["gates"], sv["pa"], sv["pr"], name=f"{tag}_merge_bwd")
    dya = _mm(dpa, W["w_attn_proj"], transpose_b=True, groups=1, lane_chunks=True, name=f"{tag}_d_ya")
    g["w_attn_proj"] = _mm(sv["ya_t"], dpa, name=f"{tag}_dw_ap")
    dyb = _mm(dpr, W["w_ret_proj"], transpose_b=True, name=f"{tag}_d_yb")
    g["w_ret_proj"] = _mm(sv["yb_t"], dpr, name=f"{tag}_dw_rp")
    token = done("proj")
    tables = tables[:-1] + (_behind(tables[-1], token),)

    da, dbs = [], []
    for gi, (_, dil) in enumerate(ATTN_GROUPS):
        dqkv, db = _attn_bwd(sv["qkv_a"], biases[gi], sv["ls"][gi], dya, sv["ya"], sv["wts"][gi], gi, dil,
                             name=f"{tag}_attn_bwd{gi}")
        da.append(dqkv)
        dbs.append(db)
    dq_r, dk_r, dv_r, dg_r = _ret_bwd(dyb, sv["qk_r"], sv["v_r"], sv["g_r"], sv["o_r"], sv["states"], tables,
                                 name=f"{tag}_ret_bwd")
    dz, colsum = _assemble_dz(da, dq_r, dk_r, dv_r, dg_r, dgates, name=f"{tag}_assemble_dz")
    g["b_in"] = colsum.reshape(IN_COLS)
    dx = _mm(dz, W["w_in"], transpose_b=True, add=res1, name=f"{tag}_d_x")
    g["w_in"] = _mm(sv["x_t"], dz, name=f"{tag}_dw_in")
    return dx, g, dbs, done("in")


HBM_SPEC = pl.BlockSpec(memory_space=pltpu.HBM)
OTHER_CHIPS = ((1, 0), (0, 1), (1, 1))


def _flip(v, f):
    return 1 - v if f else v


def _all_gather(shards, *, name):
    n = len(shards)

    def body(*refs):
        x_refs, out_refs = refs[:n], refs[n:2 * n]
        send_sems, recv_sems, local_sems = refs[2 * n:]
        x, y, c = lax.axis_index("x"), lax.axis_index("y"), lax.axis_index("c")
        me, sibling = (x, y, c), (x, y, 1 - c)
        chips = [(_flip(x, fx), _flip(y, fy)) for fx, fy in OTHER_CHIPS]

        def copy(a, k, block, to, src=None):
            px, py, pc = block
            rows = out_refs[a].at[4 * px + 2 * py + pc]
            return pltpu.make_async_remote_copy(
                src_ref=rows if src is None else src, dst_ref=rows,
                send_sem=send_sems.at[7 * a + k], recv_sem=recv_sems.at[7 * a + k], device_id=to, device_id_type=MESH)

        mine, first, passed = [], [], []
        for a in range(n):
            cp = pltpu.make_async_copy(x_refs[a], out_refs[a].at[4 * x + 2 * y + c], local_sems.at[a])
            cp.start()
            mine.append(cp)
            first.append(copy(a, 0, me, sibling, src=x_refs[a]))
            first += [copy(a, 1 + j, me, (*chip, c), src=x_refs[a]) for j, chip in enumerate(chips)]
        for cp in first:
            cp.start()
        for j, chip in enumerate(chips):
            for a in range(n):
                copy(a, 1 + j, (*chip, c), me).wait_recv()
                cp = copy(a, 4 + j, (*chip, c), sibling)
                cp.start()
                passed.append(cp)
        for a in range(n):
            copy(a, 0, sibling, me).wait_recv()
            for j, chip in enumerate(chips):
                copy(a, 4 + j, (*chip, 1 - c), me).wait_recv()
        for cp in first + passed:
            cp.wait_send()
        for cp in mine:
            cp.wait()

    return pl.pallas_call(
        body, name=name, out_shape=[jax.ShapeDtypeStruct((N_DEV,) + s.shape, s.dtype) for s in shards],
        in_specs=[HBM_SPEC] * n, out_specs=[HBM_SPEC] * n,
        scratch_shapes=[pltpu.SemaphoreType.DMA((7 * n,)), pltpu.SemaphoreType.DMA((7 * n,)),
                        pltpu.SemaphoreType.DMA((n,))],
    )(*shards)


def _rs_sibling_exchange(g8s, *, name):
    n = len(g8s)

    def body(*refs):
        g_refs, recv_refs = refs[:n], refs[n:2 * n]
        send_sems, recv_sems = refs[2 * n:]
        x, y, c = lax.axis_index("x"), lax.axis_index("y"), lax.axis_index("c")
        copies = []
        for a in range(n):
            for k in range(4):
                cp = pltpu.make_async_remote_copy(
                    src_ref=g_refs[a].at[k, 1 - c], dst_ref=recv_refs[a].at[k], send_sem=send_sems.at[4 * a + k],
                    recv_sem=recv_sems.at[4 * a + k], device_id=(x, y, 1 - c), device_id_type=MESH)
                cp.start()
                copies.append(cp)
        for cp in copies:
            cp.wait()

    return pl.pallas_call(
        body, name=name,
        out_shape=[jax.ShapeDtypeStruct((4,) + g.shape[2:], g.dtype) for g in g8s],
        in_specs=[HBM_SPEC] * n, out_specs=[HBM_SPEC] * n,
        scratch_shapes=[pltpu.SemaphoreType.DMA((4 * n,)), pltpu.SemaphoreType.DMA((4 * n,))],
    )(*g8s)


def _rs_chip_sum(g8, recv, core, *, name):
    _, _, R, Wd = g8.shape
    tr = _div_tile(R, 256, 16)

    def body(core_ref, g_ref, r_ref, o_ref):
        o_ref[...] = (g_ref[...] + r_ref[...]).astype(BF16)

    grid_spec = pltpu.PrefetchScalarGridSpec(
        num_scalar_prefetch=1, grid=(4, R // tr),
        in_specs=[pl.BlockSpec((None, None, tr, Wd), lambda k, i, core_ref: (k, core_ref[0], i, 0)),
                  pl.BlockSpec((None, tr, Wd), lambda k, i, core_ref: (k, i, 0))],
        out_specs=pl.BlockSpec((None, tr, Wd), lambda k, i, core_ref: (k, i, 0)))
    return pl.pallas_call(
        body, name=name, grid_spec=grid_spec, out_shape=jax.ShapeDtypeStruct((4, R, Wd), BF16),
        compiler_params=_cparams(("parallel", "parallel")))(core, g8, recv)


def _rs_chip_exchange(ps, *, name):
    n = len(ps)

    def body(*refs):
        p_refs, out_refs = refs[:n], refs[n:2 * n]
        send_sems, recv_sems, local_sems = refs[2 * n:]
        x, y, c = lax.axis_index("x"), lax.axis_index("y"), lax.axis_index("c")
        my_chip = 2 * x + y
        copies = []
        for a in range(n):
            mine = pltpu.make_async_copy(p_refs[a].at[my_chip], out_refs[a].at[my_chip], local_sems.at[a])
            mine.start()
            copies.append(mine)
            for j, (fx, fy) in enumerate(OTHER_CHIPS):
                px, py = _flip(x, fx), _flip(y, fy)
                cp = pltpu.make_async_remote_copy(
                    src_ref=p_refs[a].at[2 * px + py], dst_ref=out_refs[a].at[my_chip],
                    send_sem=send_sems.at[3 * a + j], recv_sem=recv_sems.at[3 * a + j],
                    device_id=(px, py, c), device_id_type=MESH)
                cp.start()
                copies.append(cp)
        for cp in copies:
            cp.wait()

    return pl.pallas_call(
        body, name=name, out_shape=[jax.ShapeDtypeStruct(p.shape, p.dtype) for p in ps],
        in_specs=[HBM_SPEC] * n, out_specs=[HBM_SPEC] * n,
        scratch_shapes=[pltpu.SemaphoreType.DMA((3 * n,)), pltpu.SemaphoreType.DMA((3 * n,)),
                        pltpu.SemaphoreType.DMA((n,))],
    )(*ps)


SEM_SPEC = pl.BlockSpec(memory_space=pltpu.SEMAPHORE)
DATAFLOW = pltpu.SideEffectType.DATAFLOW_SIDE_EFFECTING


def _direct_copies(src_refs, land_refs, send_sems, recv_sems, per_peer):
    x, y, c = lax.axis_index("x"), lax.axis_index("y"), lax.axis_index("c")
    me = 4 * x + 2 * y + c
    copies = []
    for a, (s, l) in enumerate(zip(src_refs, land_refs)):
        for rel in range(1, N_DEV):
            px, py, pc = _flip(x, rel & 4), _flip(y, rel & 2), _flip(c, rel & 1)
            copies.append(pltpu.make_async_remote_copy(
                src_ref=s.at[4 * px + 2 * py + pc] if per_peer else s, dst_ref=l.at[me],
                send_sem=send_sems.at[7 * a + rel - 1], recv_sem=recv_sems.at[7 * a + rel - 1],
                device_id=(px, py, pc), device_id_type=MESH))
    return copies


def _exchange_start(srcs, per_peer, *, name):
    n = len(srcs)
    lands = [lax.empty((N_DEV,) + (s.shape[1:] if per_peer else s.shape), s.dtype) for s in srcs]
    operands = [pltpu.with_memory_space_constraint(t, pltpu.HBM) for t in list(srcs) + lands]

    def body(*refs):
        src_refs, land_refs = refs[:n], refs[n:2 * n]
        send_sems, recv_sems = refs[2 * n], refs[2 * n + 1]
        token = refs[-1]
        for cp in _direct_copies(src_refs, land_refs, send_sems, recv_sems, per_peer):
            cp.start()
        token[...] = jnp.zeros_like(token)

    return pl.pallas_call(
        body, name=name,
        out_shape=(pltpu.SemaphoreType.DMA((7 * n,)), pltpu.SemaphoreType.DMA((7 * n,)),
                   *[pltpu.HBM(t.shape, t.dtype) for t in operands], jax.ShapeDtypeStruct((8, LANES), F32)),
        in_specs=[HBM_SPEC] * (2 * n),
        out_specs=(SEM_SPEC, SEM_SPEC, *[HBM_SPEC] * (2 * n), pl.BlockSpec(memory_space=pltpu.VMEM)),
        input_output_aliases={i: 2 + i for i in range(2 * n)},
        compiler_params=pltpu.CompilerParams(has_side_effects=DATAFLOW))(*operands)


def _exchange_wait(started, after, per_peer, *, name):
    n = (len(started) - 3) // 2
    send_sems, recv_sems = started[0], started[1]
    thru = list(started[2:2 + 2 * n])

    def body(*refs):
        src_refs, land_refs = refs[:n], refs[n:2 * n]
        send_s, recv_s = refs[2 * n], refs[2 * n + 1]
        for cp in _direct_copies(src_refs, land_refs, send_s, recv_s, per_peer):
            cp.wait_send()
            cp.wait_recv()

    outs = pl.pallas_call(
        body, name=name, out_shape=tuple(pltpu.HBM(t.shape, t.dtype) for t in thru),
        in_specs=[HBM_SPEC] * (2 * n) + [SEM_SPEC, SEM_SPEC, pl.BlockSpec(memory_space=pl.ANY)],
        out_specs=[HBM_SPEC] * (2 * n), input_output_aliases={i: i for i in range(2 * n)},
        compiler_params=pltpu.CompilerParams(has_side_effects=DATAFLOW))(*thru, send_sems, recv_sems, after)
    return list(outs[n:])


def _all_reduce_small(v, *, name):
    R, Wd = v.shape

    def body(v_ref, out_ref, slots, send_sems, recv_sems):
        x, y, c = lax.axis_index("x"), lax.axis_index("y"), lax.axis_index("c")
        me = 4 * x + 2 * y + c
        slots[me] = v_ref[...]
        copies = []
        for rel in range(1, N_DEV):
            peer = (_flip(x, rel & 4), _flip(y, rel & 2), _flip(c, rel & 1))
            cp = pltpu.make_async_remote_copy(
                src_ref=v_ref, dst_ref=slots.at[me], send_sem=send_sems.at[rel - 1],
                recv_sem=recv_sems.at[rel - 1], device_id=peer, device_id_type=MESH)
            cp.start()
            copies.append(cp)
        for cp in copies:
            cp.wait()
        acc = slots[0]
        for j in range(1, N_DEV):
            acc = acc + slots[j]
        out_ref[...] = acc

    vm = pl.BlockSpec(memory_space=pltpu.VMEM)
    return pl.pallas_call(
        body, name=name, out_shape=jax.ShapeDtypeStruct((R, Wd), F32),
        in_specs=[vm], out_specs=vm,
        scratch_shapes=[pltpu.VMEM((N_DEV, R, Wd), F32), pltpu.SemaphoreType.DMA((7,)),
                        pltpu.SemaphoreType.DMA((7,))],
    )(v)


def _adam_math(w, g, m, v):
    m2 = ADAM_B1 * m + (1.0 - ADAM_B1) * g
    v2 = ADAM_B2 * v + (1.0 - ADAM_B2) * (g * g)
    m_hat = m2 / (1.0 - ADAM_B1 ** ADAM_STEP)
    v_hat = v2 / (1.0 - ADAM_B2 ** ADAM_STEP)
    delta = -ADAM_LR * (m_hat / (jnp.sqrt(v_hat) + ADAM_EPS) + ADAM_WD * w)
    return delta, m2, v2


def _adam_sharded(parts, w, m, v, *, name):
    _, R, Wd = w.shape
    tr = _div_tile(R, 256, 16)

    def body(p0_ref, p1_ref, w_ref, m_ref, v_ref, g_ref, d_ref, m2_ref, v2_ref):
        def slot_sum(p_ref):
            g = p_ref[0].astype(F32)
            for s in range(1, p_ref.shape[0]):
                g = g + p_ref[s].astype(F32)
            return g

        g = jnp.where(pl.program_id(0) == 0, slot_sum(p0_ref), slot_sum(p1_ref))
        delta, m2, v2 = _adam_math(w_ref[...], g, m_ref[...], v_ref[...])
        g_ref[...] = g
        d_ref[...] = delta
        m2_ref[...] = m2
        v2_ref[...] = v2

    assert DEPTH == 2
    p_specs = [pl.BlockSpec((p.shape[0], tr, Wd), lambda l, i: (0, i, 0)) for p in parts]
    s_spec = pl.BlockSpec((None, tr, Wd), lambda l, i: (l, i, 0))
    return pl.pallas_call(
        body, name=name, grid=(DEPTH, R // tr),
        in_specs=p_specs + [s_spec, s_spec, s_spec],
        out_specs=[s_spec] * 4, out_shape=[jax.ShapeDtypeStruct((DEPTH, R, Wd), F32)] * 4,
        compiler_params=_cparams(("parallel", "parallel")))(parts[0], parts[1], w, m, v)


def _adam_small(g, w, m, v, *, name):
    R, Wd = w.shape

    def body(g_ref, w_ref, m_ref, v_ref, d_ref, m2_ref, v2_ref):
        delta, m2, v2 = _adam_math(w_ref[...], g_ref[...], m_ref[...], v_ref[...])
        d_ref[...] = delta
        m2_ref[...] = m2
        v2_ref[...] = v2

    return pl.pallas_call(
        body, name=name, out_shape=[jax.ShapeDtypeStruct((R, Wd), F32)] * 3,
        compiler_params=_cparams())(g, w, m, v)


def _shard_shape(name):
    r, c = FULL_SHAPE[name]
    return (r, c // N_DEV) if name in COL_SHARDED else (r // N_DEV, c)


def _full_from_gathered(name, g):
    if name in COL_SHARDED:
        return jnp.transpose(g, (1, 0, 2)).reshape(FULL_SHAPE[name])
    return g.reshape(FULL_SHAPE[name])


def _dest_major(name, gfull):
    r, c = _shard_shape(name)
    if name in COL_SHARDED:
        blk = jnp.transpose(gfull.reshape(r, N_DEV, c), (1, 0, 2))
    else:
        blk = gfull.reshape(N_DEV, r, c)
    return blk.reshape(4, 2, r, c)


def _pack_small(t):
    flat = jnp.concatenate([t[n].reshape(-1).astype(F32) for n in SMALL_WEIGHTS])
    return jnp.pad(flat, (0, SMALL_ROWS * LANES - flat.shape[0])).reshape(SMALL_ROWS, LANES)


def _unpack_small(packed):
    flat = packed.reshape(-1)
    out, off = {}, 0
    for n in SMALL_WEIGHTS:
        size = math.prod(SMALL_SHAPE[n])
        out[n] = flat[off:off + size].reshape(SMALL_SHAPE[n])
        off += size
    return out


def _after(value, token):
    return lax.optimization_barrier((value, token))[0]


def _local_step(x, target, rel_bias, b_in, lns, weights_of, on_grads=None):
    S = x.shape[0]
    tables = _ret_tables(S)
    biases = [_attn_bias(rel_bias, gi, dil) for gi, (_, dil) in enumerate(ATTN_GROUPS)]

    h = x
    hb, h_t = _cast_transpose(x, name="cast_x")
    saved, Ws = [], []
    for l in range(DEPTH):
        h, hb, h_t, sv, W = _layer_fwd(l, h, hb, h_t, weights_of, b_in[l], biases, lns[l], tables)
        saved.append(sv)
        Ws.append(W)
    dy, sq = _loss_fwd_bwd(h, target, name="loss")
    loss_local = 0.5 * sq[0, 0] / D_MODEL

    grads = [None] * DEPTH
    db_tot = None
    dx = dy
    token = None
    for l in reversed(range(DEPTH)):
        dx, g, dbs, token = _layer_bwd(l, dx, saved[l], Ws[l], biases, lns[l], tables, token, on_grads)
        grads[l] = g
        db_tot = dbs if db_tot is None else [a + b for a, b in zip(db_tot, dbs)]
    small = {"rel_bias": _bias_grad(db_tot, name="bias_grad"),
             "b_in": jnp.stack([grads[l]["b_in"] for l in range(DEPTH)])}
    for n in ("ln1_g", "ln1_b", "ln2_g", "ln2_b"):
        small[n] = jnp.stack([grads[l][n].reshape(D_MODEL) for l in range(DEPTH)])
    return loss_local, dx, grads, small


def kernel(x, rel_bias, w_in, b_in, w_attn_proj, w_ret_proj, w_out, ln1_g, ln1_b, w_ffn_gate, w_ffn_up, w_ffn_down, ln2_g, ln2_b, loss_target, m_rel_bias, m_w_in, m_b_in, m_w_attn_proj, m_w_ret_proj, m_w_out, m_ln1_g, m_ln1_b, m_w_ffn_gate, m_w_ffn_up, m_w_ffn_down, m_ln2_g, m_ln2_b, v_rel_bias, v_w_in, v_b_in, v_w_attn_proj, v_w_ret_proj, v_w_out, v_ln1_g, v_ln1_b, v_w_ffn_gate, v_w_ffn_up, v_w_ffn_down, v_ln2_g, v_ln2_b):
    w = dict(rel_bias=rel_bias, w_in=w_in, b_in=b_in, w_attn_proj=w_attn_proj, w_ret_proj=w_ret_proj, w_out=w_out,
             ln1_g=ln1_g, ln1_b=ln1_b, w_ffn_gate=w_ffn_gate, w_ffn_up=w_ffn_up, w_ffn_down=w_ffn_down,
             ln2_g=ln2_g, ln2_b=ln2_b)
    m = dict(rel_bias=m_rel_bias, w_in=m_w_in, b_in=m_b_in, w_attn_proj=m_w_attn_proj, w_ret_proj=m_w_ret_proj,
             w_out=m_w_out, ln1_g=m_ln1_g, ln1_b=m_ln1_b, w_ffn_gate=m_w_ffn_gate, w_ffn_up=m_w_ffn_up,
             w_ffn_down=m_w_ffn_down, ln2_g=m_ln2_g, ln2_b=m_ln2_b)
    v = dict(rel_bias=v_rel_bias, w_in=v_w_in, b_in=v_b_in, w_attn_proj=v_w_attn_proj, w_ret_proj=v_w_ret_proj,
             w_out=v_w_out, ln1_g=v_ln1_g, ln1_b=v_ln1_b, w_ffn_gate=v_w_ffn_gate, w_ffn_up=v_w_ffn_up,
             w_ffn_down=v_w_ffn_down, ln2_g=v_ln2_g, ln2_b=v_ln2_b)

    assert DEPTH == 2
    me = 4 * lax.axis_index("x") + 2 * lax.axis_index("y") + lax.axis_index("c")
    core = lax.axis_index("c").astype(jnp.int32).reshape(1)

    def own_slot(lands, blocks):
        return [lax.dynamic_update_index_in_dim(land, blk, me, 0) for land, blk in zip(lands, blocks)]

    shard = {(l, n): w[n][l].astype(BF16) for l in range(DEPTH) for n in BIG_WEIGHTS}
    rest = WEIGHT_GROUPS["proj"] + WEIGHT_GROUPS["ffn"]
    (w_in0,) = _all_gather([shard[0, "w_in"]], name="all_gather_l0_in")
    gathers = {0: (rest, _exchange_start(_after([shard[0, n] for n in rest], w_in0), False,
                                         name="all_gather_l0_rest_start"))}
    first_token = gathers[0][1][-1][0, 0].astype(BF16)
    gathers[1] = (BIG_WEIGHTS, _exchange_start([shard[1, n] + first_token for n in BIG_WEIGHTS], False,
                                               name="all_gather_l1_start"))
    b_in_fwd = [_behind(b_in[0], gathers[1][1][-1]), b_in[1]]
    arrived = {}

    def weights_of(l, group, after):
        if (l, group) == (0, "in"):
            return {"w_in": _full_from_gathered("w_in", w_in0)}
        if l not in arrived:
            names, started = gathers[l]
            lands = _exchange_wait(started, after, False, name=f"all_gather_l{l}_wait")
            full = own_slot(lands, [shard[l, n] for n in names])
            arrived[l] = {n: _full_from_gathered(n, g) for n, g in zip(names, full)}
        names = WEIGHT_GROUPS["in"] if group == "in" else rest
        return {n: arrived[l][n] for n in names}

    scatters = {}

    def on_grads(l, group, gd):
        if (l, group) == (0, "in"):
            return None
        names = WEIGHT_GROUPS[group]
        blocks = [_dest_major(n, gd[n]).reshape((N_DEV,) + _shard_shape(n)).astype(BF16) for n in names]
        scatters[l, group] = (names, blocks, _exchange_start(blocks, True, name=f"rs_l{l}_{group}_start"))
        return scatters[l, group][2][-1]

    lns = [{n: w[n][l] for n in ("ln1_g", "ln1_b", "ln2_g", "ln2_b")} for l in range(DEPTH)]
    loss_local, grad_x, grads, small = _local_step(x[0], loss_target[0], rel_bias, b_in_fwd, lns, weights_of,
                                                   on_grads)
    loss = lax.psum(loss_local, ("x", "y", "c"))

    g8 = [_dest_major("w_in", grads[0]["w_in"])]
    from_sibling = _rs_sibling_exchange(g8, name="rs_sibling_exchange_l0_in")
    chip_parts = [_rs_chip_sum(g8[0], from_sibling[0], core, name="rs_chip_sum_l0_in")]
    parts = {(0, "w_in"): _rs_chip_exchange(chip_parts, name="rs_chip_exchange_l0_in")[0]}
    for (l, group), (names, blocks, started) in scatters.items():
        lands = _exchange_wait(started, parts[0, "w_in"], True, name=f"rs_l{l}_{group}_wait")
        own = [lax.dynamic_index_in_dim(b, me, 0, keepdims=False) for b in blocks]
        for n, p in zip(names, own_slot(lands, own)):
            parts[l, n] = p
    big = [{} for _ in range(4)]
    for n in BIG_WEIGHTS:
        res = _adam_sharded([parts[l, n] for l in range(DEPTH)], w[n], m[n], v[n], name=f"adam_{n}")
        for kind in range(4):
            big[kind][n] = res[kind]

    gs = _all_reduce_small(_pack_small(small), name="all_reduce_small")
    ds, ms, vs = _adam_small(gs, _pack_small(w), _pack_small(m), _pack_small(v), name="adam_small")
    sm = [_unpack_small(t) for t in (gs, ds, ms, vs)]

    outs = [loss, grad_x[None]]
    for kind in range(4):
        for n in ALL_WEIGHTS:
            outs.append(big[kind][n] if n in BIG_WEIGHTS else sm[kind][n])
    return tuple(outs)
```

```python
import functools
import math

import numpy as np
import jax
import jax.numpy as jnp
from jax import lax
from jax.experimental import pallas as pl
from jax.experimental.pallas import tpu as pltpu

F32 = jnp.float32
BF16 = jnp.bfloat16
MESH = pl.DeviceIdType.MESH

D_MODEL = 1024
DEPTH = 2
HEAD_DIM = 64
ATTN_GROUPS = ((128, 1), (512, 4), (2048, 16))
HEADS_PER_GROUP = 6
GROUP_WIDTH = HEADS_PER_GROUP * HEAD_DIM
ATTN_BLOCK = 128
NUM_BUCKETS = 32
MAX_DISTANCE = 2048
RET_HEADS = 4
RET_QK_DIM = 256
RET_V_DIM = 512
RET_CHUNK = 128
RET_T_CHUNKS = 4
ROPE_BASE = 10000.0
D_FF = 2816
ALPHA = (2 * DEPTH) ** 0.25
LN_EPS = 1e-5
GN_EPS = 1e-5
ATTN_W = 3 * GROUP_WIDTH
IN_COLS = 3 * ATTN_W + 2 * 1024 + 2 * 2048 + 2 * 1024
ADAM_LR, ADAM_B1, ADAM_B2, ADAM_EPS, ADAM_WD, ADAM_STEP = 0.001, 0.9, 0.999, 1e-08, 0.01, 10
N_DEV = 8
NEG = -1e30
LANES = 128
VMEM_LIMIT = 56 * 1024 * 1024
MM_TILE_CAP = 1664
MM_VMEM_BUDGET = 44 * 1024 * 1024

BIG_WEIGHTS = ("w_in", "w_attn_proj", "w_ret_proj", "w_out", "w_ffn_gate", "w_ffn_up", "w_ffn_down")
COL_SHARDED = ("w_in", "w_attn_proj", "w_ffn_gate", "w_ffn_up")
FULL_SHAPE = {"w_in": (D_MODEL, IN_COLS), "w_attn_proj": (GROUP_WIDTH, D_MODEL), "w_ret_proj": (2048, D_MODEL),
              "w_out": (D_MODEL, D_MODEL), "w_ffn_gate": (D_MODEL, D_FF), "w_ffn_up": (D_MODEL, D_FF),
              "w_ffn_down": (D_FF, D_MODEL)}
SMALL_WEIGHTS = ("rel_bias", "b_in", "ln1_g", "ln1_b", "ln2_g", "ln2_b")
SMALL_SHAPE = {"rel_bias": (NUM_BUCKETS, 18), "b_in": (DEPTH, IN_COLS), "ln1_g": (DEPTH, D_MODEL),
               "ln1_b": (DEPTH, D_MODEL), "ln2_g": (DEPTH, D_MODEL), "ln2_b": (DEPTH, D_MODEL)}
SMALL_ROWS = 256
ALL_WEIGHTS = ("rel_bias", "w_in", "b_in", "w_attn_proj", "w_ret_proj", "w_out", "ln1_g", "ln1_b",
               "w_ffn_gate", "w_ffn_up", "w_ffn_down", "ln2_g", "ln2_b")


def _cparams(sem=None):
    return pltpu.CompilerParams(dimension_semantics=sem, vmem_limit_bytes=VMEM_LIMIT)


def _div_tile(n, cap, unit):
    if n <= cap:
        return n
    best = None
    for t in range(unit, cap + 1, unit):
        if n % t == 0:
            best = t
    assert best is not None, (n, cap, unit)
    return best


def _mm(a, b, *, name, out_dtype=F32, bias=None, add=None, groups=None, lane_chunks=False, transpose_b=False,
        a_halves=False, b_halves=False):
    M, K = (a.shape[1], 2 * a.shape[2]) if a_halves else a.shape
    if b_halves:
        assert not transpose_b
        K2, N = b.shape[1], 2 * b.shape[2]
    else:
        N, K2 = b.shape if transpose_b else b.shape[::-1]
    assert K == K2 and a.dtype == BF16 and b.dtype == BF16
    has_bias, has_add = bias is not None, add is not None
    tm = _div_tile(M, 1024, 16)
    tn = N // groups if groups else _div_tile(N // 2 if b_halves else N, MM_TILE_CAP, LANES)
    out_bytes = jnp.dtype(out_dtype).itemsize
    k_span = K // 2 if a_halves else K

    def vmem_bytes(tk):
        return 2 * (2 * tm * tk + 2 * tk * tn + out_bytes * tm * tn + (4 * tm * tn if has_add else 0))

    tk = max(t for t in range(LANES, k_span + 1, LANES)
             if k_span % t == 0 and (t == LANES or vmem_bytes(t) <= MM_VMEM_BUDGET))
    nk = K // tk
    nch = tn // LANES
    assert nk == 1 or (out_dtype == F32 and not lane_chunks and not has_bias)

    def body(*refs):
        a_ref, b_ref = refs[0], refs[1]
        pos = 2
        bias_ref = add_ref = None
        if has_bias:
            bias_ref = refs[pos]
            pos += 1
        if has_add:
            add_ref = refs[pos]
            pos += 1
        o_ref = refs[pos]

        def finish(r):
            if has_bias:
                r = r + bias_ref[...]
            if has_add:
                r = r + add_ref[...]
            if lane_chunks:
                for c in range(nch):
                    o_ref[c] = r[:, c * LANES:(c + 1) * LANES].astype(o_ref.dtype)
            else:
                o_ref[...] = r.astype(o_ref.dtype)

        def product():
            if transpose_b:
                return lax.dot_general(a_ref[...], b_ref[...], (((1,), (1,)), ((), ())), preferred_element_type=F32)
            return jnp.dot(a_ref[...], b_ref[...], preferred_element_type=F32)

        if nk == 1:
            finish(product())
        else:
            @pl.when(pl.program_id(2) == 0)
            def _():
                o_ref[...] = add_ref[...] if has_add else jnp.zeros_like(o_ref)

            o_ref[...] += product()

    kh, nh = k_span // tk, (N // 2) // tn
    if a_halves:
        a_spec = pl.BlockSpec((None, tm, tk), lambda i, j, k: (k // kh, i, k % kh))
    else:
        a_spec = pl.BlockSpec((tm, tk), lambda i, j, k: (i, k))
    if b_halves:
        b_spec = pl.BlockSpec((None, tk, tn), lambda i, j, k: (j // nh, k, j % nh))
    elif transpose_b:
        b_spec = pl.BlockSpec((tn, tk), lambda i, j, k: (j, k))
    else:
        b_spec = pl.BlockSpec((tk, tn), lambda i, j, k: (k, j))
    in_specs = [a_spec, b_spec]
    args = [a, b]
    if has_bias:
        in_specs.append(pl.BlockSpec((1, tn), lambda i, j, k: (0, j)))
        args.append(bias.reshape(1, N).astype(F32))
    if has_add:
        in_specs.append(pl.BlockSpec((tm, tn), lambda i, j, k: (i, j)))
        args.append(add)
    if lane_chunks:
        assert groups
        out_shape = jax.ShapeDtypeStruct((groups, nch, M, LANES), out_dtype)
        out_spec = pl.BlockSpec((None, nch, tm, LANES), lambda i, j, k: (j, 0, i, 0))
    elif groups:
        out_shape = jax.ShapeDtypeStruct((groups, M, tn), out_dtype)
        out_spec = pl.BlockSpec((None, tm, tn), lambda i, j, k: (j, i, 0))
    else:
        out_shape = jax.ShapeDtypeStruct((M, N), out_dtype)
        out_spec = pl.BlockSpec((tm, tn), lambda i, j, k: (i, j))
    out = pl.pallas_call(
        body, name=name, grid=(M // tm, N // tn, nk), in_specs=in_specs, out_specs=out_spec,
        out_shape=out_shape,
        compiler_params=_cparams(("parallel", "parallel", "arbitrary")))(*args)
    return out.reshape(groups * nch, M, LANES) if lane_chunks else out


def _row_spec(tr, w):
    return pl.BlockSpec((tr, w), lambda i: (i, 0))


def _vec_spec(w):
    return pl.BlockSpec((1, w), lambda i: (0, 0))


def _col_spec(w, tr):
    return pl.BlockSpec((w, tr), lambda i: (0, i))


def _cast_transpose(x, *, name):
    S, W = x.shape
    tr = 512

    def body(x_ref, o_ref, ot_ref):
        v = x_ref[...]
        o_ref[...] = v.astype(BF16)
        ot_ref[...] = v.T.astype(BF16)

    return pl.pallas_call(
        body, name=name, grid=(S // tr,), in_specs=[_row_spec(tr, W)],
        out_specs=[_row_spec(tr, W), _col_spec(W, tr)],
        out_shape=[jax.ShapeDtypeStruct((S, W), BF16), jax.ShapeDtypeStruct((W, S), BF16)],
        compiler_params=_cparams(("parallel",)))(x)


def _ln_fwd(x, sub, g, b, *, name):
    S, W = x.shape
    tr = 512

    def body(x_ref, s_ref, g_ref, b_ref, h_ref, y_ref, yb_ref, ybt_ref):
        h = ALPHA * x_ref[...] + s_ref[...]
        mu = jnp.mean(h, axis=-1, keepdims=True)
        d = h - mu
        var = jnp.mean(d * d, axis=-1, keepdims=True)
        y = d * lax.rsqrt(var + LN_EPS) * g_ref[...] + b_ref[...]
        h_ref[...] = h
        y_ref[...] = y
        yb_ref[...] = y.astype(BF16)
        ybt_ref[...] = y.T.astype(BF16)

    return pl.pallas_call(
        body, name=name, grid=(S // tr,),
        in_specs=[_row_spec(tr, W), _row_spec(tr, W), _vec_spec(W), _vec_spec(W)],
        out_specs=[_row_spec(tr, W)] * 3 + [_col_spec(W, tr)],
        out_shape=[jax.ShapeDtypeStruct((S, W), F32), jax.ShapeDtypeStruct((S, W), F32),
                   jax.ShapeDtypeStruct((S, W), BF16), jax.ShapeDtypeStruct((W, S), BF16)],
        compiler_params=_cparams(("parallel",)))(x, sub, g.reshape(1, W), b.reshape(1, W))


def _ln_bwd(dy, h, g, *, name):
    S, W = dy.shape
    tr = 512

    def body(dy_ref, h_ref, g_ref, dhb_ref, res_ref, dg_ref, db_ref):
        @pl.when(pl.program_id(0) == 0)
        def _():
            dg_ref[...] = jnp.zeros_like(dg_ref)
            db_ref[...] = jnp.zeros_like(db_ref)

        hh = h_ref[...]
        mu = jnp.mean(hh, axis=-1, keepdims=True)
        d = hh - mu
        var = jnp.mean(d * d, axis=-1, keepdims=True)
        rstd = lax.rsqrt(var + LN_EPS)
        xhat = d * rstd
        dyv = dy_ref[...]
        dg_ref[...] += jnp.sum(dyv * xhat, axis=0, keepdims=True)
        db_ref[...] += jnp.sum(dyv, axis=0, keepdims=True)
        dxh = dyv * g_ref[...]
        dh = rstd * (dxh - jnp.mean(dxh, axis=-1, keepdims=True)
                     - xhat * jnp.mean(dxh * xhat, axis=-1, keepdims=True))
        dhb_ref[...] = dh.astype(BF16)
        res_ref[...] = ALPHA * dh

    return pl.pallas_call(
        body, name=name, grid=(S // tr,),
        in_specs=[_row_spec(tr, W), _row_spec(tr, W), _vec_spec(W)],
        out_specs=[_row_spec(tr, W), _row_spec(tr, W), _vec_spec(W), _vec_spec(W)],
        out_shape=[jax.ShapeDtypeStruct((S, W), BF16), jax.ShapeDtypeStruct((S, W), F32),
                   jax.ShapeDtypeStruct((1, W), F32), jax.ShapeDtypeStruct((1, W), F32)],
        compiler_params=_cparams(("arbitrary",)))(dy, h, g.reshape(1, W))


def _loss_fwd_bwd(y, target, *, name):
    S, W = y.shape
    tr = 512

    def body(y_ref, t_ref, dy_ref, acc_ref):
        @pl.when(pl.program_id(0) == 0)
        def _():
            acc_ref[...] = jnp.zeros_like(acc_ref)

        e = y_ref[...] - t_ref[...]
        acc_ref[...] += jnp.sum(jnp.sum(e * e, axis=-1, keepdims=True), axis=0, keepdims=True)
        dy_ref[...] = e * (1.0 / W)

    return pl.pallas_call(
        body, name=name, grid=(S // tr,),
        in_specs=[_row_spec(tr, W), _row_spec(tr, W)],
        out_specs=[_row_spec(tr, W), pl.BlockSpec((1, 1), lambda i: (0, 0))],
        out_shape=[jax.ShapeDtypeStruct((S, W), F32), jax.ShapeDtypeStruct((1, 1), F32)],
        compiler_params=_cparams(("arbitrary",)))(y, target)


def _combine_fwd(os_, ls_, *, name):
    NCH, S, _ = os_[0].shape
    W = NCH * LANES
    tr = 512

    def body(o0, o1, o2, l0, l1, l2, yb_ref, ybt_ref, y_ref, w0_ref, w1_ref, w2_ref):
        for c in range(NCH):
            la, lb, lc = l0[c], l1[c], l2[c]
            m = jnp.maximum(jnp.maximum(la, lb), lc)
            ea, eb, ec = jnp.exp(la - m), jnp.exp(lb - m), jnp.exp(lc - m)
            inv = 1.0 / (ea + eb + ec)
            wa, wb, wc = ea * inv, eb * inv, ec * inv
            y = wa * o0[c] + wb * o1[c] + wc * o2[c]
            y_ref[c] = y
            yb_ref[:, c * LANES:(c + 1) * LANES] = y.astype(BF16)
            ybt_ref[c * LANES:(c + 1) * LANES, :] = y.T.astype(BF16)
            w0_ref[c] = wa
            w1_ref[c] = wb
            w2_ref[c] = wc

    ch = pl.BlockSpec((NCH, tr, LANES), lambda i: (0, i, 0))
    yb, ybt, y, w0, w1, w2 = pl.pallas_call(
        body, name=name, grid=(S // tr,),
        in_specs=[ch] * 6,
        out_specs=[_row_spec(tr, W), _col_spec(W, tr)] + [ch] * 4,
        out_shape=[jax.ShapeDtypeStruct((S, W), BF16), jax.ShapeDtypeStruct((W, S), BF16)]
        + [jax.ShapeDtypeStruct((NCH, S, LANES), F32)] * 4,
        compiler_params=_cparams(("parallel",)))(*os_, *ls_)
    return yb, ybt, y, (w0, w1, w2)


def _merge_fwd(gates, pa, pr, *, name):
    S, W = pa.shape
    tr = 512

    def body(g_ref, pa_ref, pr_ref, o_ref, ot_ref):
        m = jax.nn.sigmoid(g_ref[0]) * pa_ref[...] + jax.nn.sigmoid(g_ref[1]) * pr_ref[...]
        o_ref[...] = m.astype(BF16)
        ot_ref[...] = m.T.astype(BF16)

    return pl.pallas_call(
        body, name=name, grid=(S // tr,),
        in_specs=[pl.BlockSpec((2, tr, W), lambda i: (0, i, 0)), _row_spec(tr, W), _row_spec(tr, W)],
        out_specs=[_row_spec(tr, W), _col_spec(W, tr)],
        out_shape=[jax.ShapeDtypeStruct((S, W), BF16), jax.ShapeDtypeStruct((W, S), BF16)],
        compiler_params=_cparams(("parallel",)))(gates, pa, pr)


def _merge_bwd(dm, gates, pa, pr, *, name):
    S, W = pa.shape
    tr = 256

    def body(dm_ref, g_ref, pa_ref, pr_ref, dpa_ref, dpr_ref, dg_ref):
        dmv = dm_ref[...]
        sa, sb = jax.nn.sigmoid(g_ref[0]), jax.nn.sigmoid(g_ref[1])
        dpa_ref[...] = (dmv * sa).astype(BF16)
        dpr_ref[...] = (dmv * sb).astype(BF16)
        dg_ref[0] = (dmv * pa_ref[...] * (sa * (1.0 - sa))).astype(BF16)
        dg_ref[1] = (dmv * pr_ref[...] * (sb * (1.0 - sb))).astype(BF16)

    g3 = pl.BlockSpec((2, tr, W), lambda i: (0, i, 0))
    return pl.pallas_call(
        body, name=name, grid=(S // tr,),
        in_specs=[_row_spec(tr, W), g3, _row_spec(tr, W), _row_spec(tr, W)],
        out_specs=[_row_spec(tr, W), _row_spec(tr, W), g3],
        out_shape=[jax.ShapeDtypeStruct((S, W), BF16), jax.ShapeDtypeStruct((S, W), BF16),
                   jax.ShapeDtypeStruct((2, S, W), BF16)],
        compiler_params=_cparams(("parallel",)))(dm, gates, pa, pr)


def _ffn_in_swiglu(x, wg, wu, *, name):
    S, D = x.shape
    F = wg.shape[1]
    tm, tf = 512, F // 2

    def body(x_ref, wg_ref, wu_ref, uv_ref, h_ref, ht_ref):
        xv = x_ref[...]
        u = _dot(xv, wg_ref[...])
        v = _dot(xv, wu_ref[...])
        hh = u * jax.nn.sigmoid(u) * v
        uv_ref[0] = u.astype(BF16)
        uv_ref[1] = v.astype(BF16)
        h_ref[...] = hh.astype(BF16)
        ht_ref[...] = hh.T.astype(BF16)

    w_spec = pl.BlockSpec((D, tf), lambda j, i: (0, j))
    return pl.pallas_call(
        body, name=name, grid=(F // tf, S // tm),
        in_specs=[pl.BlockSpec((tm, D), lambda j, i: (i, 0)), w_spec, w_spec],
        out_specs=[pl.BlockSpec((2, tm, tf), lambda j, i: (0, i, j)), pl.BlockSpec((tm, tf), lambda j, i: (i, j)),
                   pl.BlockSpec((tf, tm), lambda j, i: (j, i))],
        out_shape=[jax.ShapeDtypeStruct((2, S, F), BF16), jax.ShapeDtypeStruct((S, F), BF16),
                   jax.ShapeDtypeStruct((F, S), BF16)],
        compiler_params=_cparams(("parallel", "parallel")))(x, wg, wu)


def _ffn_down_bwd_swiglu(dhb, wd, uv, *, name):
    S, D = dhb.shape
    F = wd.shape[0]
    tm, tf = 512, F // 2

    def body(d_ref, w_ref, uv_ref, o_ref):
        dh = _dot_nt(d_ref[...], w_ref[...])
        u, v = uv_ref[0].astype(F32), uv_ref[1].astype(F32)
        sg = jax.nn.sigmoid(u)
        o_ref[0] = (dh * v * (sg * (1.0 + u * (1.0 - sg)))).astype(BF16)
        o_ref[1] = (dh * (u * sg)).astype(BF16)

    half = pl.BlockSpec((2, tm, tf), lambda j, i: (0, i, j))
    return pl.pallas_call(
        body, name=name, grid=(F // tf, S // tm),
        in_specs=[pl.BlockSpec((tm, D), lambda j, i: (i, 0)), pl.BlockSpec((tf, D), lambda j, i: (j, 0)), half],
        out_specs=half, out_shape=jax.ShapeDtypeStruct((2, S, F), BF16),
        compiler_params=_cparams(("parallel", "parallel")))(dhb, wd, uv)


def _assemble_dz(da, dq_r, dk_r, dv_r, dg_r, dgates, *, name):
    S = dv_r.shape[0]
    tr = 256
    GW = GROUP_WIDTH
    NCH = GW // LANES

    def body(*refs):
        a_refs = refs[0:9]
        q_ref, k_ref, v_ref, g_ref, gt_ref, dz_ref, cs_ref = refs[9:]

        @pl.when(pl.program_id(0) == 0)
        def _():
            cs_ref[...] = jnp.zeros_like(cs_ref)

        def put(off, val):
            w = val.shape[-1]
            dz_ref[:, off:off + w] = val.astype(BF16)
            cs_ref[:, off:off + w] += jnp.sum(val.astype(F32), axis=0, keepdims=True)

        for which in range(3):
            for gi in range(3):
                for c in range(NCH):
                    put(which * ATTN_W + gi * GW + c * LANES, a_refs[3 * gi + which][c])
        off = 3 * ATTN_W
        put(off, q_ref[...])
        put(off + 1024, k_ref[...])
        put(off + 2048, v_ref[...])
        put(off + 4096, g_ref[...])
        put(off + 6144, gt_ref[0])
        put(off + 7168, gt_ref[1])

    flat_a = [t for grp in da for t in grp]
    return pl.pallas_call(
        body, name=name, grid=(S // tr,),
        in_specs=[pl.BlockSpec((NCH, tr, LANES), lambda i: (0, i, 0))] * 9 + [_row_spec(tr, 1024), _row_spec(tr, 1024),
                  _row_spec(tr, 2048), _row_spec(tr, 2048), pl.BlockSpec((2, tr, 1024), lambda i: (0, i, 0))],
        out_specs=[_row_spec(tr, IN_COLS), _vec_spec(IN_COLS)],
        out_shape=[jax.ShapeDtypeStruct((S, IN_COLS), BF16), jax.ShapeDtypeStruct((1, IN_COLS), F32)],
        compiler_params=_cparams(("arbitrary",)))(*flat_a, dq_r, dk_r, dv_r, dg_r, dgates)


def _t5_bucket(dist):
    max_exact = NUM_BUCKETS // 2
    large = max_exact + (np.log(np.maximum(dist, max_exact) / max_exact)
                         / np.log(MAX_DISTANCE / max_exact) * (NUM_BUCKETS - max_exact)).astype(np.int32)
    large = np.minimum(large, NUM_BUCKETS - 1)
    return np.where(dist < max_exact, dist, large).astype(np.int32)


def _attn_tables(dilation):
    W = ATTN_BLOCK
    qi = np.arange(W)[:, None]
    kj = np.arange(2 * W)[None, :]
    rel = qi + W - kj
    valid = (rel >= 0) & (rel <= W)
    buckets = _t5_bucket(np.clip(rel, 0, W) * dilation)
    return buckets, valid


def _attn_bias(rel_bias, gi, dilation):
    buckets, valid = _attn_tables(dilation)
    table = rel_bias[:, gi * HEADS_PER_GROUP:(gi + 1) * HEADS_PER_GROUP]
    onehot = (jnp.asarray(buckets.reshape(-1, 1)) == jnp.arange(NUM_BUCKETS)[None, :]).astype(F32)
    bias = jnp.dot(onehot, table.astype(F32), precision=lax.Precision.HIGHEST)
    bias = bias.T.reshape(HEADS_PER_GROUP, ATTN_BLOCK, 2 * ATTN_BLOCK)
    return jnp.where(jnp.asarray(valid)[None], bias, NEG)


def _dot_nt(a, b):
    return lax.dot_general(a, b, (((1,), (1,)), ((), ())), preferred_element_type=F32)


def _dot_tn(a, b):
    return lax.dot_general(a, b, (((0,), (0,)), ((), ())), preferred_element_type=F32)


def _dot(a, b):
    return jnp.dot(a, b, preferred_element_type=F32)


ATTN_RESIDUES_PER_STEP = 4
ATTN_UNITS_AT_ONCE = 8
HEADS_PER_CHUNK = LANES // HEAD_DIM
N_CHUNKS = GROUP_WIDTH // LANES


def _first_block_mask(has_prev):
    col = lax.broadcasted_iota(jnp.int32, (1, 2 * ATTN_BLOCK), 1)
    return jnp.where(jnp.logical_or(has_prev, col >= ATTN_BLOCK), 0.0, NEG).astype(F32)


def _head_lanes(hh):
    return slice(HEAD_DIM * hh, HEAD_DIM * (hh + 1))


def _attn_geometry(S, d):
    rps = ATTN_RESIDUES_PER_STEP if d == 1 else min(d, ATTN_RESIDUES_PER_STEP)
    rows_per_block = ATTN_BLOCK * (rps if d == 1 else d)
    return rows_per_block, S // rows_per_block, rps, 1 if d == 1 else d // rps


def _unit_rows(d, rps, rg, u):
    B = ATTN_BLOCK
    if d == 1:
        return pl.ds(B * u, B), pl.ds(B * (u - 1 if u else rps - 1), B), u == 0
    rows = pl.ds(rg * rps + u, B, stride=d)
    return rows, rows, True


def _attn_in_specs(gi, RB, last):
    def spec(which, prev):
        if prev:
            return pl.BlockSpec((None, RB, LANES),
                                lambda j, n, rg: (9 * which + 3 * gi + j, jnp.clip(n - 1, 0, last), 0))
        return pl.BlockSpec((None, RB, LANES), lambda j, n, rg: (9 * which + 3 * gi + j, jnp.minimum(n, last), 0))
    bias = pl.BlockSpec((HEADS_PER_CHUNK, ATTN_BLOCK, 2 * ATTN_BLOCK), lambda j, n, rg: (j, 0, 0))
    return [spec(0, False), spec(1, True), spec(1, False), spec(2, True), spec(2, False), bias]


def _attn_fwd(qkv, bias, gi, d, *, name):
    _, S, _ = qkv.shape
    B = ATTN_BLOCK
    RB, nb, rps, nrg = _attn_geometry(S, d)
    scale = HEAD_DIM ** -0.5
    units = [(rr, hh) for rr in range(rps) for hh in range(HEADS_PER_CHUNK)]

    def body(q_ref, kp_ref, kc_ref, vp_ref, vc_ref, b_ref, o_ref, l_ref):
        n, rg = pl.program_id(1), pl.program_id(2)
        first = _first_block_mask(n > 0)
        ur = [_unit_rows(d, rps, rg, u) for u in range(rps)]
        rows = [r_ for r_, _, _ in ur]
        edge = [first if in_prev else 0.0 for _, _, in_prev in ur]
        q = [q_ref[r_, :].astype(BF16) for r_ in rows]
        k2 = [jnp.concatenate([(kp_ref if in_prev else kc_ref)[pr, :], kc_ref[r_, :]], axis=0).astype(BF16)
              for r_, pr, in_prev in ur]
        v2 = [jnp.concatenate([(vp_ref if in_prev else vc_ref)[pr, :], vc_ref[r_, :]], axis=0).astype(BF16)
              for r_, pr, in_prev in ur]
        o_part, l_part = {}, {}
        for u0 in range(0, len(units), ATTN_UNITS_AT_ONCE):
            us = units[u0:u0 + ATTN_UNITS_AT_ONCE]
            s = [_dot_nt(q[rr][:, _head_lanes(hh)], k2[rr][:, _head_lanes(hh)]) * scale + b_ref[hh] + edge[rr]
                 for rr, hh in us]
            m = [jnp.max(x, axis=-1, keepdims=True) for x in s]
            p = [jnp.exp(x - mm) for x, mm in zip(s, m)]
            l = [jnp.sum(x, axis=-1, keepdims=True) for x in p]
            pb = [(x * (1.0 / ll)).astype(BF16) for x, ll in zip(p, l)]
            o = [_dot(x, v2[rr][:, _head_lanes(hh)]) for x, (rr, hh) in zip(pb, us)]
            for u, oo, mm, ll in zip(us, o, m, l):
                o_part[u] = oo
                l_part[u] = jnp.broadcast_to(mm + jnp.log(ll), (B, HEAD_DIM))
        for rr in range(rps):
            o_ref[rows[rr], :] = jnp.concatenate([o_part[(rr, hh)] for hh in range(HEADS_PER_CHUNK)], axis=1)
            l_ref[rows[rr], :] = jnp.concatenate([l_part[(rr, hh)] for hh in range(HEADS_PER_CHUNK)], axis=1)

    out_spec = pl.BlockSpec((None, RB, LANES), lambda j, n, rg: (j, n, 0))
    return pl.pallas_call(
        body, name=name, grid=(N_CHUNKS, nb, nrg),
        in_specs=_attn_in_specs(gi, RB, nb - 1),
        out_specs=[out_spec, out_spec],
        out_shape=[jax.ShapeDtypeStruct((N_CHUNKS, S, LANES), F32)] * 2,
        compiler_params=_cparams(("parallel", "arbitrary", "arbitrary")))(qkv, qkv, qkv, qkv, qkv, bias)


def _attn_bwd(qkv, bias, lse, dya, ya, wts, gi, d, *, name):
    _, S, _ = qkv.shape
    B = ATTN_BLOCK
    RB, nb, rps, nrg = _attn_geometry(S, d)
    scale = HEAD_DIM ** -0.5
    units = [(rr, hh) for rr in range(rps) for hh in range(HEADS_PER_CHUNK)]

    def body(q_ref, kp_ref, kc_ref, vp_ref, vc_ref, b_ref, l_ref, dya_ref, ya_ref, w_ref,
             dq_ref, dk_ref, dv_ref, db_ref, dk_carry, dv_carry):
        n, rg = pl.program_id(1), pl.program_id(2)
        ur = [_unit_rows(d, rps, rg, u) for u in range(rps)]
        rows = [r_ for r_, _, _ in ur]

        @pl.when((n == 0) & (rg == 0))
        def _():
            db_ref[...] = jnp.zeros_like(db_ref)
            dk_carry[...] = jnp.zeros_like(dk_carry)
            dv_carry[...] = jnp.zeros_like(dv_carry)

        @pl.when(n < nb)
        def _():
            first = _first_block_mask(n > 0)
            edge = [first if in_prev else 0.0 for _, _, in_prev in ur]
            q = [q_ref[r_, :].astype(BF16) for r_ in rows]
            k2 = [jnp.concatenate([(kp_ref if in_prev else kc_ref)[pr, :], kc_ref[r_, :]], axis=0).astype(BF16)
                  for r_, pr, in_prev in ur]
            v2 = [jnp.concatenate([(vp_ref if in_prev else vc_ref)[pr, :], vc_ref[r_, :]], axis=0).astype(BF16)
                  for r_, pr, in_prev in ur]
            lse_c = [l_ref[r_, :] for r_ in rows]
            dy_c = [dya_ref[r_, :] for r_ in rows]
            ya_c = [ya_ref[r_, :] for r_ in rows]
            w_c = [w_ref[r_, :] for r_ in rows]
            ds_sum = [None] * HEADS_PER_CHUNK
            dq_part, dk_part, dv_part = {}, {}, {}
            for u0 in range(0, len(units), ATTN_UNITS_AT_ONCE):
                us = units[u0:u0 + ATTN_UNITS_AT_ONCE]
                hl = [_head_lanes(hh) for _, hh in us]
                qh = [q[rr][:, sl] for (rr, _), sl in zip(us, hl)]
                kh = [k2[rr][:, sl] for (rr, _), sl in zip(us, hl)]
                vh = [v2[rr][:, sl] for (rr, _), sl in zip(us, hl)]
                s = [_dot_nt(a, k) * scale + b_ref[hh] + edge[rr] for a, k, (rr, hh) in zip(qh, kh, us)]
                p = [jnp.exp(x - lse_c[rr][:, HEAD_DIM * hh:HEAD_DIM * hh + 1]) for x, (rr, hh) in zip(s, us)]
                dy = [dy_c[rr][:, sl] for (rr, _), sl in zip(us, hl)]
                w = [w_c[rr][:, sl] for (rr, _), sl in zip(us, hl)]
                shift = [ww[:, 0:1] * jnp.sum(d_ * ya_c[rr][:, sl], axis=-1, keepdims=True)
                         for ww, d_, (rr, _), sl in zip(w, dy, us, hl)]
                do = [(ww * d_).astype(BF16) for ww, d_ in zip(w, dy)]
                ds = [pp * (_dot_nt(o_, v) - sh) for pp, o_, v, sh in zip(p, do, vh, shift)]
                for x, (_, hh) in zip(ds, us):
                    ds_sum[hh] = x if ds_sum[hh] is None else ds_sum[hh] + x
                dsb = [x.astype(BF16) for x in ds]
                pb = [x.astype(BF16) for x in p]
                for u, x, pp, a, k, o_ in zip(us, dsb, pb, qh, kh, do):
                    dq_part[u] = _dot(x, k) * scale
                    dk_part[u] = _dot_tn(x, a) * scale
                    dv_part[u] = _dot_tn(pp, o_)
            for hh in range(HEADS_PER_CHUNK):
                db_ref[hh] += ds_sum[hh]
            dk2, dv2 = [], []
            for rr in range(rps):
                dq_ref[rows[rr], :] = jnp.concatenate([dq_part[(rr, hh)] for hh in range(HEADS_PER_CHUNK)], axis=1)
                dk2.append(jnp.concatenate([dk_part[(rr, hh)] for hh in range(HEADS_PER_CHUNK)], axis=1))
                dv2.append(jnp.concatenate([dv_part[(rr, hh)] for hh in range(HEADS_PER_CHUNK)], axis=1))
            for out_ref, carry, d2 in ((dk_ref, dk_carry, dk2), (dv_ref, dv_carry, dv2)):
                if d == 1:
                    out_ref[...] = carry[...]
                    out_ref[ur[0][1], :] += d2[0][0:B]
                    for rr in range(rps):
                        nxt = d2[rr + 1][0:B] if rr + 1 < rps else 0.0
                        carry[rows[rr], :] = d2[rr][B:2 * B] + nxt
                else:
                    for rr in range(rps):
                        out_ref[rows[rr], :] = carry[rows[rr], :] + d2[rr][0:B]
                        carry[rows[rr], :] = d2[rr][B:2 * B]

        @pl.when(n == nb)
        def _():
            for r_ in rows:
                dk_ref[r_, :] = dk_carry[r_, :]
                dv_ref[r_, :] = dv_carry[r_, :]

    last = nb - 1
    cur = pl.BlockSpec((None, RB, LANES), lambda j, n, rg: (j, jnp.minimum(n, last), 0))
    lag = pl.BlockSpec((None, RB, LANES), lambda j, n, rg: (j, jnp.maximum(n - 1, 0), 0))
    db_spec = pl.BlockSpec((HEADS_PER_CHUNK, B, 2 * B), lambda j, n, rg: (j, 0, 0))
    dq, dk, dv, db = pl.pallas_call(
        body, name=name, grid=(N_CHUNKS, nb + 1, nrg),
        in_specs=_attn_in_specs(gi, RB, last) + [cur, cur, cur, cur],
        out_specs=[cur, lag, lag, db_spec],
        out_shape=[jax.ShapeDtypeStruct((N_CHUNKS, S, LANES), F32)] * 3
        + [jax.ShapeDtypeStruct((HEADS_PER_GROUP, B, 2 * B), F32)],
        scratch_shapes=[pltpu.VMEM((RB, LANES), F32), pltpu.VMEM((RB, LANES), F32)],
        compiler_params=_cparams(("arbitrary", "arbitrary", "arbitrary")))(
            qkv, qkv, qkv, qkv, qkv, bias, lse, dya, ya, wts)
    return (dq, dk, dv), db


def _bias_grad(dbs, *, name):
    nk = ATTN_BLOCK * 2 * ATTN_BLOCK
    buckets = []
    for (_, dil) in ATTN_GROUPS:
        b, valid = _attn_tables(dil)
        buckets.append(np.where(valid, b, -1).reshape(1, nk))
    bk = jnp.asarray(np.stack(buckets).astype(np.int32))
    flat = [x.reshape(HEADS_PER_GROUP, nk) for x in dbs]

    def body(bk_ref, d0, d1, d2, o_ref):
        ids = lax.broadcasted_iota(jnp.int32, (NUM_BUCKETS, nk), 0)
        for gi, dref in enumerate((d0, d1, d2)):
            onehot = (ids == bk_ref[gi]).astype(F32)
            o_ref[gi] = lax.dot_general(onehot, dref[...], (((1,), (1,)), ((), ())),
                                        preferred_element_type=F32, precision=lax.Precision.HIGHEST)

    out = pl.pallas_call(
        body, name=name,
        out_shape=jax.ShapeDtypeStruct((3, NUM_BUCKETS, HEADS_PER_GROUP), F32),
        compiler_params=_cparams())(bk, *flat)
    return jnp.transpose(out, (1, 0, 2)).reshape(NUM_BUCKETS, 3 * HEADS_PER_GROUP)


def _ret_tables(S):
    half = RET_QK_DIM // 2
    pos = jnp.arange(S, dtype=F32)
    inv_freq = ROPE_BASE ** (-jnp.arange(half, dtype=F32) / half)
    ang = pos[:, None] * inv_freq[None]
    cos, sin = jnp.cos(ang), jnp.sin(ang)
    H, C = RET_HEADS, RET_CHUNK
    log_g = jnp.log(1.0 - 2.0 ** (-5.0 - jnp.arange(H, dtype=F32)))
    n = jnp.arange(C, dtype=F32)
    diff = n[:, None] - n[None, :]
    dmask = jnp.where(diff >= 0, jnp.exp(log_g[:, None, None] * jnp.maximum(diff, 0.0)), 0.0)
    q_dec = jnp.exp(log_g[:, None] * (n + 1.0))
    k_dec = jnp.exp(log_g[:, None] * (C - 1.0 - n))
    chunk_dec = jnp.exp(log_g * C)
    qd = jnp.broadcast_to(q_dec[:, :, None], (H, C, RET_QK_DIM))
    kd = jnp.broadcast_to(k_dec[:, :, None], (H, C, RET_QK_DIM))
    cd = jnp.broadcast_to(chunk_dec[:, None, None], (H, 1, RET_V_DIM))
    return cos, sin, dmask, qd, kd, cd


def _rot(t, cos, sin):
    half = RET_QK_DIM // 2
    t1, t2 = t[:, :half], t[:, half:]
    return jnp.concatenate([t1 * cos - t2 * sin, t1 * sin + t2 * cos], axis=-1)


def _unrot(t, cos, sin):
    half = RET_QK_DIM // 2
    t1, t2 = t[:, :half], t[:, half:]
    return jnp.concatenate([t1 * cos + t2 * sin, t2 * cos - t1 * sin], axis=-1)


def _ret_specs(rev, nC):
    C, DK, DV = RET_CHUNK, RET_QK_DIM, RET_V_DIM
    cidx = (lambda c: nC - 1 - c) if rev else (lambda c: c)
    H = RET_HEADS
    return dict(
        qk=lambda which: pl.BlockSpec((None, C, H * DK), lambda c: (which, cidx(c), 0)),
        q=pl.BlockSpec((C, H * DK), lambda c: (cidx(c), 0)),
        v=pl.BlockSpec((C, H * DV), lambda c: (cidx(c), 0)),
        cs=pl.BlockSpec((C, DK // 2), lambda c: (cidx(c), 0)),
        dmask=pl.BlockSpec((H, C, C), lambda c: (0, 0, 0)),
        dec=pl.BlockSpec((H, C, DK), lambda c: (0, 0, 0)),
        cd=pl.BlockSpec((H, 1, DV), lambda c: (0, 0, 0)),
        st=pl.BlockSpec((H, None, DK, DV), lambda c: (0, cidx(c), 0, 0)),
    )


def _ret_fwd(qk, v, g, tables, *, name):
    _, S, _ = qk.shape
    nC = S // RET_CHUNK
    C, DK, DV, H = RET_CHUNK, RET_QK_DIM, RET_V_DIM, RET_HEADS
    cos, sin, dmask, qd, kd, cd = tables
    kscale = DK ** -0.5

    def body(q_ref, k_ref, v_ref, g_ref, cos_ref, sin_ref, dm_ref, qd_ref, kd_ref, cd_ref,
             o_ref, yb_ref, ybt_ref, st_ref, state):
        @pl.when(pl.program_id(0) == 0)
        def _():
            state[...] = jnp.zeros_like(state)

        tcol = pl.multiple_of((pl.program_id(0) % RET_T_CHUNKS) * C, C)
        cs, sn = cos_ref[...], sin_ref[...]
        for h in range(H):
            qs, vs = slice(DK * h, DK * (h + 1)), slice(DV * h, DV * (h + 1))
            Q = _rot(q_ref[:, qs], cs, sn)
            K = _rot(k_ref[:, qs], cs, sn) * kscale
            Qb, Kb, V = Q.astype(BF16), K.astype(BF16), v_ref[:, vs]
            sb = state[h].astype(BF16)
            st_ref[h] = sb
            A = _dot_nt(Qb, Kb) * dm_ref[h]
            o = _dot(A.astype(BF16), V) + _dot((Q * qd_ref[h]).astype(BF16), sb)
            state[h] = state[h] * cd_ref[h] + _dot_tn((K * kd_ref[h]).astype(BF16), V)
            mu = jnp.mean(o, axis=-1, keepdims=True)
            dd = o - mu
            var = jnp.mean(dd * dd, axis=-1, keepdims=True)
            yn = dd * lax.rsqrt(var + GN_EPS)
            gv = g_ref[:, vs]
            yb = gv * jax.nn.sigmoid(gv) * yn
            o_ref[:, vs] = o
            yb_ref[:, vs] = yb.astype(BF16)
            ybt_ref[vs, pl.ds(tcol, C)] = yb.T.astype(BF16)

    sp = _ret_specs(False, nC)
    return pl.pallas_call(
        body, name=name, grid=(nC,),
        in_specs=[sp["qk"](0), sp["qk"](1), sp["v"], sp["v"], sp["cs"], sp["cs"], sp["dmask"],
                  sp["dec"], sp["dec"], sp["cd"]],
        out_specs=[sp["v"], sp["v"], pl.BlockSpec((H * DV, RET_T_CHUNKS * C), lambda c: (0, c // RET_T_CHUNKS)),
                   sp["st"]],
        out_shape=[jax.ShapeDtypeStruct((S, H * DV), F32), jax.ShapeDtypeStruct((S, H * DV), BF16),
                   jax.ShapeDtypeStruct((H * DV, S), BF16), jax.ShapeDtypeStruct((H, nC, DK, DV), BF16)],
        scratch_shapes=[pltpu.VMEM((H, DK, DV), F32)],
        compiler_params=_cparams(("arbitrary",)))(qk, qk, v, g, cos, sin, dmask, qd, kd, cd)


def _ret_bwd(dyb, qk, v, g, o, states, tables, *, name):
    _, S, _ = qk.shape
    nC = S // RET_CHUNK
    C, DK, DV, H = RET_CHUNK, RET_QK_DIM, RET_V_DIM, RET_HEADS
    cos, sin, dmask, qd, kd, cd = tables
    kscale = DK ** -0.5

    def body(dy_ref, q_ref, k_ref, v_ref, g_ref, o_ref, st_ref, cos_ref, sin_ref, dm_ref, qd_ref, kd_ref,
             cd_ref, dq_ref, dk_ref, dv_ref, dg_ref, dstate):
        @pl.when(pl.program_id(0) == 0)
        def _():
            dstate[...] = jnp.zeros_like(dstate)

        cs, sn = cos_ref[...], sin_ref[...]
        for h in range(H):
            qs, vs = slice(DK * h, DK * (h + 1)), slice(DV * h, DV * (h + 1))
            ov = o_ref[:, vs]
            mu = jnp.mean(ov, axis=-1, keepdims=True)
            dd = ov - mu
            var = jnp.mean(dd * dd, axis=-1, keepdims=True)
            rstd = lax.rsqrt(var + GN_EPS)
            yn = dd * rstd
            gv, dy = g_ref[:, vs], dy_ref[:, vs]
            sg = jax.nn.sigmoid(gv)
            dg_ref[:, vs] = (dy * yn * (sg * (1.0 + gv * (1.0 - sg)))).astype(BF16)
            dyn = dy * (gv * sg)
            dO = rstd * (dyn - jnp.mean(dyn, axis=-1, keepdims=True)
                         - yn * jnp.mean(dyn * yn, axis=-1, keepdims=True))
            dOb = dO.astype(BF16)

            Q = _rot(q_ref[:, qs], cs, sn)
            K = _rot(k_ref[:, qs], cs, sn) * kscale
            Qb, Kb, V = Q.astype(BF16), K.astype(BF16), v_ref[:, vs]
            dm, qd_h, kd_h = dm_ref[h], qd_ref[h], kd_ref[h]
            Sb = st_ref[h]
            dSb = dstate[h].astype(BF16)
            Ab = (_dot_nt(Qb, Kb) * dm).astype(BF16)
            dAb = (_dot_nt(dOb, V) * dm).astype(BF16)
            Qd = (Q * qd_h).astype(BF16)
            Kd = (K * kd_h).astype(BF16)
            dQ = _dot(dAb, Kb) + _dot_nt(dOb, Sb) * qd_h
            dK = _dot_tn(dAb, Qb) + _dot_nt(V, dSb) * kd_h
            dv_ref[:, vs] = (_dot_tn(Ab, dOb) + _dot(Kd, dSb)).astype(BF16)
            dstate[h] = dstate[h] * cd_ref[h] + _dot_tn(Qd, dOb)
            dq_ref[:, qs] = _unrot(dQ, cs, sn).astype(BF16)
            dk_ref[:, qs] = (_unrot(dK, cs, sn) * kscale).astype(BF16)

    sp = _ret_specs(True, nC)
    dq, dk, dv, dg = pl.pallas_call(
        body, name=name, grid=(nC,),
        in_specs=[sp["v"], sp["qk"](0), sp["qk"](1), sp["v"], sp["v"], sp["v"], sp["st"], sp["cs"], sp["cs"],
                  sp["dmask"], sp["dec"], sp["dec"], sp["cd"]],
        out_specs=[sp["q"], sp["q"], sp["v"], sp["v"]],
        out_shape=[jax.ShapeDtypeStruct((S, H * DK), BF16), jax.ShapeDtypeStruct((S, H * DK), BF16),
                   jax.ShapeDtypeStruct((S, H * DV), BF16), jax.ShapeDtypeStruct((S, H * DV), BF16)],
        scratch_shapes=[pltpu.VMEM((H, DK, DV), F32)],
        compiler_params=_cparams(("arbitrary",)))(dyb, qk, qk, v, g, o, states, cos, sin, dmask, qd, kd, cd)
    return dq, dk, dv, dg


def _layer_fwd(l, x, xb, x_t, weights_of, b_in, biases, ln, tables):
    S = x.shape[0]
    tag = f"l{l}"
    W = dict(weights_of(l, "in", x))
    win = W["w_in"]
    c0, c1, c2, c3, c4 = 3 * ATTN_W, 3 * ATTN_W + 2048, 3 * ATTN_W + 4096, 3 * ATTN_W + 6144, IN_COLS
    qkv_a = _mm(xb, win[:, :c0], bias=b_in[:c0], groups=3, lane_chunks=True, name=f"{tag}_in_attn")
    qk_r = _mm(xb, win[:, c0:c1], bias=b_in[c0:c1], groups=2, name=f"{tag}_in_retqk")
    v_r = _mm(xb, win[:, c1:c2], bias=b_in[c1:c2], out_dtype=BF16, name=f"{tag}_in_retv")
    g_r = _mm(xb, win[:, c2:c3], bias=b_in[c2:c3], name=f"{tag}_in_retg")
    gates = _mm(xb, win[:, c3:c4], bias=b_in[c3:c4], groups=2, name=f"{tag}_in_gates")

    os_, ls_ = [], []
    for gi, (_, dil) in enumerate(ATTN_GROUPS):
        o, lse = _attn_fwd(qkv_a, biases[gi], gi, dil, name=f"{tag}_attn_fwd{gi}")
        os_.append(o)
        ls_.append(lse)
    ya_b, ya_t, ya, wts = _combine_fwd(os_, ls_, name=f"{tag}_combine")

    o_r, yb, yb_t, states = _ret_fwd(qk_r, v_r, g_r, tables, name=f"{tag}_ret_fwd")

    W.update(weights_of(l, "rest", yb))
    W["w_gu"] = jnp.concatenate([W["w_ffn_gate"], W["w_ffn_up"]], axis=1)
    pa = _mm(ya_b, W["w_attn_proj"], name=f"{tag}_attn_proj")
    pr = _mm(yb, W["w_ret_proj"], name=f"{tag}_ret_proj")
    merged, merged_t = _merge_fwd(gates, pa, pr, name=f"{tag}_merge")
    mix = _mm(merged, W["w_out"], name=f"{tag}_out_proj")
    h1, x1, x1b, x1_t = _ln_fwd(x, mix, ln["ln1_g"], ln["ln1_b"], name=f"{tag}_ln1")
    uv, hh, hh_t = _ffn_in_swiglu(x1b, W["w_ffn_gate"], W["w_ffn_up"], name=f"{tag}_ffn_in")
    f = _mm(hh, W["w_ffn_down"], name=f"{tag}_ffn_down")
    h2, x2, x2b, x2_t = _ln_fwd(x1, f, ln["ln2_g"], ln["ln2_b"], name=f"{tag}_ln2")
    saved = dict(x_t=x_t, qkv_a=qkv_a, qk_r=qk_r, v_r=v_r, g_r=g_r, gates=gates, ls=ls_, ya_t=ya_t, ya=ya,
                 wts=wts, o_r=o_r, yb_t=yb_t, states=states, pa=pa, pr=pr, merged_t=merged_t, h1=h1, x1_t=x1_t,
                 uv=uv, hh_t=hh_t, h2=h2)
    return x2, x2b, x2_t, saved, W


WEIGHT_GROUPS = {"in": ("w_in",), "proj": ("w_attn_proj", "w_ret_proj", "w_out"),
                 "ffn": ("w_ffn_gate", "w_ffn_up", "w_ffn_down")}


def _behind(value, token):
    return value if token is None else value + token[0, 0]


def _layer_bwd(l, dx2, sv, W, biases, ln, tables, token, on_grads):
    S = dx2.shape[0]
    tag = f"l{l}"
    g = {}

    def done(group):
        return None if on_grads is None else on_grads(l, group, {n: g[n] for n in WEIGHT_GROUPS[group]})

    dh2b, res2, g["ln2_g"], g["ln2_b"] = _ln_bwd(dx2, sv["h2"], _behind(ln["ln2_g"], token), name=f"{tag}_ln2_bwd")
    g["w_ffn_down"] = _mm(sv["hh_t"], dh2b, name=f"{tag}_dw_down")
    dudv = _ffn_down_bwd_swiglu(dh2b, W["w_ffn_down"], sv["uv"], name=f"{tag}_d_uv")
    dx1 = _mm(dudv, W["w_gu"], transpose_b=True, a_halves=True, add=res2, name=f"{tag}_d_x1")
    dwgu = _mm(sv["x1_t"], dudv, b_halves=True, name=f"{tag}_dw_gu")
    g["w_ffn_gate"], g["w_ffn_up"] = dwgu[:, :D_FF], dwgu[:, D_FF:]
    token = done("ffn")

    dh1b, res1, g["ln1_g"], g["ln1_b"] = _ln_bwd(dx1, sv["h1"], _behind(ln["ln1_g"], token), name=f"{tag}_ln1_bwd")
    dmerged = _mm(dh1b, W["w_out"], transpose_b=True, name=f"{tag}_d_merged")
    g["w_out"] = _mm(sv["merged_t"], dh1b, name=f"{tag}_dw_out")
    dpa, dpr, dgates = _merge_bwd(dmerged, sv["gates"], sv["pa"], sv["pr"], name=f"{tag}_merge_bwd")
    dya = _mm(dpa, W["w_attn_proj"], transpose_b=True, groups=1, lane_chunks=True, name=f"{tag}_d_ya")
    g["w_attn_proj"] = _mm(sv["ya_t"], dpa, name=f"{tag}_dw_ap")
    dyb = _mm(dpr, W["w_ret_proj"], transpose_b=True, name=f"{tag}_d_yb")
    g["w_ret_proj"] = _mm(sv["yb_t"], dpr, name=f"{tag}_dw_rp")
    token = done("proj")
    tables = tables[:-1] + (_behind(tables[-1], token),)

    da, dbs = [], []
    for gi, (_, dil) in enumerate(ATTN_GROUPS):
        dqkv, db = _attn_bwd(sv["qkv_a"], biases[gi], sv["ls"][gi], dya, sv["ya"], sv["wts"][gi], gi, dil,
                             name=f"{tag}_attn_bwd{gi}")
        da.append(dqkv)
        dbs.append(db)
    dq_r, dk_r, dv_r, dg_r = _ret_bwd(dyb, sv["qk_r"], sv["v_r"], sv["g_r"], sv["o_r"], sv["states"], tables,
                                 name=f"{tag}_ret_bwd")
    dz, colsum = _assemble_dz(da, dq_r, dk_r, dv_r, dg_r, dgates, name=f"{tag}_assemble_dz")
    g["b_in"] = colsum.reshape(IN_COLS)
    dx = _mm(dz, W["w_in"], transpose_b=True, add=res1, name=f"{tag}_d_x")
    g["w_in"] = _mm(sv["x_t"], dz, name=f"{tag}_dw_in")
    return dx, g, dbs, done("in")


HBM_SPEC = pl.BlockSpec(memory_space=pltpu.HBM)
OTHER_CHIPS = ((1, 0), (0, 1), (1, 1))


def _flip(v, f):
    return 1 - v if f else v


def _all_gather(shards, *, name):
    n = len(shards)

    def body(*refs):
        x_refs, out_refs = refs[:n], refs[n:2 * n]
        send_sems, recv_sems, local_sems = refs[2 * n:]
        x, y, c = lax.axis_index("x"), lax.axis_index("y"), lax.axis_index("c")
        me, sibling = (x, y, c), (x, y, 1 - c)
        chips = [(_flip(x, fx), _flip(y, fy)) for fx, fy in OTHER_CHIPS]

        def copy(a, k, block, to, src=None):
            px, py, pc = block
            rows = out_refs[a].at[4 * px + 2 * py + pc]
            return pltpu.make_async_remote_copy(
                src_ref=rows if src is None else src, dst_ref=rows,
                send_sem=send_sems.at[7 * a + k], recv_sem=recv_sems.at[7 * a + k], device_id=to, device_id_type=MESH)

        mine, first, passed = [], [], []
        for a in range(n):
            cp = pltpu.make_async_copy(x_refs[a], out_refs[a].at[4 * x + 2 * y + c], local_sems.at[a])
            cp.start()
            mine.append(cp)
            first.append(copy(a, 0, me, sibling, src=x_refs[a]))
            first += [copy(a, 1 + j, me, (*chip, c), src=x_refs[a]) for j, chip in enumerate(chips)]
        for cp in first:
            cp.start()
        for j, chip in enumerate(chips):
            for a in range(n):
                copy(a, 1 + j, (*chip, c), me).wait_recv()
                cp = copy(a, 4 + j, (*chip, c), sibling)
                cp.start()
                passed.append(cp)
        for a in range(n):
            copy(a, 0, sibling, me).wait_recv()
            for j, chip in enumerate(chips):
                copy(a, 4 + j, (*chip, 1 - c), me).wait_recv()
        for cp in first + passed:
            cp.wait_send()
        for cp in mine:
            cp.wait()

    return pl.pallas_call(
        body, name=name, out_shape=[jax.ShapeDtypeStruct((N_DEV,) + s.shape, s.dtype) for s in shards],
        in_specs=[HBM_SPEC] * n, out_specs=[HBM_SPEC] * n,
        scratch_shapes=[pltpu.SemaphoreType.DMA((7 * n,)), pltpu.SemaphoreType.DMA((7 * n,)),
                        pltpu.SemaphoreType.DMA((n,))],
    )(*shards)


def _rs_sibling_exchange(g8s, *, name):
    n = len(g8s)

    def body(*refs):
        g_refs, recv_refs = refs[:n], refs[n:2 * n]
        send_sems, recv_sems = refs[2 * n:]
        x, y, c = lax.axis_index("x"), lax.axis_index("y"), lax.axis_index("c")
        copies = []
        for a in range(n):
            for k in range(4):
                cp = pltpu.make_async_remote_copy(
                    src_ref=g_refs[a].at[k, 1 - c], dst_ref=recv_refs[a].at[k], send_sem=send_sems.at[4 * a + k],
                    recv_sem=recv_sems.at[4 * a + k], device_id=(x, y, 1 - c), device_id_type=MESH)
                cp.start()
                copies.append(cp)
        for cp in copies:
            cp.wait()

    return pl.pallas_call(
        body, name=name,
        out_shape=[jax.ShapeDtypeStruct((4,) + g.shape[2:], g.dtype) for g in g8s],
        in_specs=[HBM_SPEC] * n, out_specs=[HBM_SPEC] * n,
        scratch_shapes=[pltpu.SemaphoreType.DMA((4 * n,)), pltpu.SemaphoreType.DMA((4 * n,))],
    )(*g8s)


def _rs_chip_sum(g8, recv, core, *, name):
    _, _, R, Wd = g8.shape
    tr = _div_tile(R, 256, 16)

    def body(core_ref, g_ref, r_ref, o_ref):
        o_ref[...] = (g_ref[...] + r_ref[...]).astype(BF16)

    grid_spec = pltpu.PrefetchScalarGridSpec(
        num_scalar_prefetch=1, grid=(4, R // tr),
        in_specs=[pl.BlockSpec((None, None, tr, Wd), lambda k, i, core_ref: (k, core_ref[0], i, 0)),
                  pl.BlockSpec((None, tr, Wd), lambda k, i, core_ref: (k, i, 0))],
        out_specs=pl.BlockSpec((None, tr, Wd), lambda k, i, core_ref: (k, i, 0)))
    return pl.pallas_call(
        body, name=name, grid_spec=grid_spec, out_shape=jax.ShapeDtypeStruct((4, R, Wd), BF16),
        compiler_params=_cparams(("parallel", "parallel")))(core, g8, recv)


def _rs_chip_exchange(ps, *, name):
    n = len(ps)

    def body(*refs):
        p_refs, out_refs = refs[:n], refs[n:2 * n]
        send_sems, recv_sems, local_sems = refs[2 * n:]
        x, y, c = lax.axis_index("x"), lax.axis_index("y"), lax.axis_index("c")
        my_chip = 2 * x + y
        copies = []
        for a in range(n):
            mine = pltpu.make_async_copy(p_refs[a].at[my_chip], out_refs[a].at[my_chip], local_sems.at[a])
            mine.start()
            copies.append(mine)
            for j, (fx, fy) in enumerate(OTHER_CHIPS):
                px, py = _flip(x, fx), _flip(y, fy)
                cp = pltpu.make_async_remote_copy(
                    src_ref=p_refs[a].at[2 * px + py], dst_ref=out_refs[a].at[my_chip],
                    send_sem=send_sems.at[3 * a + j], recv_sem=recv_sems.at[3 * a + j],
                    device_id=(px, py, c), device_id_type=MESH)
                cp.start()
                copies.append(cp)
        for cp in copies:
            cp.wait()

    return pl.pallas_call(
        body, name=name, out_shape=[jax.ShapeDtypeStruct(p.shape, p.dtype) for p in ps],
        in_specs=[HBM_SPEC] * n, out_specs=[HBM_SPEC] * n,
        scratch_shapes=[pltpu.SemaphoreType.DMA((3 * n,)), pltpu.SemaphoreType.DMA((3 * n,)),
                        pltpu.SemaphoreType.DMA((n,))],
    )(*ps)


SEM_SPEC = pl.BlockSpec(memory_space=pltpu.SEMAPHORE)
DATAFLOW = pltpu.SideEffectType.DATAFLOW_SIDE_EFFECTING


def _direct_copies(src_refs, land_refs, send_sems, recv_sems, per_peer):
    x, y, c = lax.axis_index("x"), lax.axis_index("y"), lax.axis_index("c")
    me = 4 * x + 2 * y + c
    copies = []
    for a, (s, l) in enumerate(zip(src_refs, land_refs)):
        for rel in range(1, N_DEV):
            px, py, pc = _flip(x, rel & 4), _flip(y, rel & 2), _flip(c, rel & 1)
            copies.append(pltpu.make_async_remote_copy(
                src_ref=s.at[4 * px + 2 * py + pc] if per_peer else s, dst_ref=l.at[me],
                send_sem=send_sems.at[7 * a + rel - 1], recv_sem=recv_sems.at[7 * a + rel - 1],
                device_id=(px, py, pc), device_id_type=MESH))
    return copies


def _exchange_start(srcs, per_peer, *, name):
    n = len(srcs)
    lands = [lax.empty((N_DEV,) + (s.shape[1:] if per_peer else s.shape), s.dtype) for s in srcs]
    operands = [pltpu.with_memory_space_constraint(t, pltpu.HBM) for t in list(srcs) + lands]

    def body(*refs):
        src_refs, land_refs = refs[:n], refs[n:2 * n]
        send_sems, recv_sems = refs[2 * n], refs[2 * n + 1]
        token = refs[-1]
        for cp in _direct_copies(src_refs, land_refs, send_sems, recv_sems, per_peer):
            cp.start()
        token[...] = jnp.zeros_like(token)

    return pl.pallas_call(
        body, name=name,
        out_shape=(pltpu.SemaphoreType.DMA((7 * n,)), pltpu.SemaphoreType.DMA((7 * n,)),
                   *[pltpu.HBM(t.shape, t.dtype) for t in operands], jax.ShapeDtypeStruct((8, LANES), F32)),
        in_specs=[HBM_SPEC] * (2 * n),
        out_specs=(SEM_SPEC, SEM_SPEC, *[HBM_SPEC] * (2 * n), pl.BlockSpec(memory_space=pltpu.VMEM)),
        input_output_aliases={i: 2 + i for i in range(2 * n)},
        compiler_params=pltpu.CompilerParams(has_side_effects=DATAFLOW))(*operands)


def _exchange_wait(started, after, per_peer, *, name):
    n = (len(started) - 3) // 2
    send_sems, recv_sems = started[0], started[1]
    thru = list(started[2:2 + 2 * n])

    def body(*refs):
        src_refs, land_refs = refs[:n], refs[n:2 * n]
        send_s, recv_s = refs[2 * n], refs[2 * n + 1]
        for cp in _direct_copies(src_refs, land_refs, send_s, recv_s, per_peer):
            cp.wait_send()
            cp.wait_recv()

    outs = pl.pallas_call(
        body, name=name, out_shape=tuple(pltpu.HBM(t.shape, t.dtype) for t in thru),
        in_specs=[HBM_SPEC] * (2 * n) + [SEM_SPEC, SEM_SPEC, pl.BlockSpec(memory_space=pl.ANY)],
        out_specs=[HBM_SPEC] * (2 * n), input_output_aliases={i: i for i in range(2 * n)},
        compiler_params=pltpu.CompilerParams(has_side_effects=DATAFLOW))(*thru, send_sems, recv_sems, after)
    return list(outs[n:])


def _all_reduce_small(v, *, name):
    R, Wd = v.shape

    def body(v_ref, out_ref, slots, send_sems, recv_sems):
        x, y, c = lax.axis_index("x"), lax.axis_index("y"), lax.axis_index("c")
        me = 4 * x + 2 * y + c
        slots[me] = v_ref[...]
        copies = []
        for rel in range(1, N_DEV):
            peer = (_flip(x, rel & 4), _flip(y, rel & 2), _flip(c, rel & 1))
            cp = pltpu.make_async_remote_copy(
                src_ref=v_ref, dst_ref=slots.at[me], send_sem=send_sems.at[rel - 1],
                recv_sem=recv_sems.at[rel - 1], device_id=peer, device_id_type=MESH)
            cp.start()
            copies.append(cp)
        for cp in copies:
            cp.wait()
        acc = slots[0]
        for j in range(1, N_DEV):
            acc = acc + slots[j]
        out_ref[...] = acc

    vm = pl.BlockSpec(memory_space=pltpu.VMEM)
    return pl.pallas_call(
        body, name=name, out_shape=jax.ShapeDtypeStruct((R, Wd), F32),
        in_specs=[vm], out_specs=vm,
        scratch_shapes=[pltpu.VMEM((N_DEV, R, Wd), F32), pltpu.SemaphoreType.DMA((7,)),
                        pltpu.SemaphoreType.DMA((7,))],
    )(v)


def _adam_math(w, g, m, v):
    m2 = ADAM_B1 * m + (1.0 - ADAM_B1) * g
    v2 = ADAM_B2 * v + (1.0 - ADAM_B2) * (g * g)
    m_hat = m2 / (1.0 - ADAM_B1 ** ADAM_STEP)
    v_hat = v2 / (1.0 - ADAM_B2 ** ADAM_STEP)
    delta = -ADAM_LR * (m_hat / (jnp.sqrt(v_hat) + ADAM_EPS) + ADAM_WD * w)
    return delta, m2, v2


def _adam_sharded(parts, w, m, v, *, name):
    _, R, Wd = w.shape
    tr = _div_tile(R, 256, 16)

    def body(p0_ref, p1_ref, w_ref, m_ref, v_ref, g_ref, d_ref, m2_ref, v2_ref):
        def slot_sum(p_ref):
            g = p_ref[0].astype(F32)
            for s in range(1, p_ref.shape[0]):
                g = g + p_ref[s].astype(F32)
            return g

        g = jnp.where(pl.program_id(0) == 0, slot_sum(p0_ref), slot_sum(p1_ref))
        delta, m2, v2 = _adam_math(w_ref[...], g, m_ref[...], v_ref[...])
        g_ref[...] = g
        d_ref[...] = delta
        m2_ref[...] = m2
        v2_ref[...] = v2

    assert DEPTH == 2
    p_specs = [pl.BlockSpec((p.shape[0], tr, Wd), lambda l, i: (0, i, 0)) for p in parts]
    s_spec = pl.BlockSpec((None, tr, Wd), lambda l, i: (l, i, 0))
    return pl.pallas_call(
        body, name=name, grid=(DEPTH, R // tr),
        in_specs=p_specs + [s_spec, s_spec, s_spec],
        out_specs=[s_spec] * 4, out_shape=[jax.ShapeDtypeStruct((DEPTH, R, Wd), F32)] * 4,
        compiler_params=_cparams(("parallel", "parallel")))(parts[0], parts[1], w, m, v)


def _adam_small(g, w, m, v, *, name):
    R, Wd = w.shape

    def body(g_ref, w_ref, m_ref, v_ref, d_ref, m2_ref, v2_ref):
        delta, m2, v2 = _adam_math(w_ref[...], g_ref[...], m_ref[...], v_ref[...])
        d_ref[...] = delta
        m2_ref[...] = m2
        v2_ref[...] = v2

    return pl.pallas_call(
        body, name=name, out_shape=[jax.ShapeDtypeStruct((R, Wd), F32)] * 3,
        compiler_params=_cparams())(g, w, m, v)


def _shard_shape(name):
    r, c = FULL_SHAPE[name]
    return (r, c // N_DEV) if name in COL_SHARDED else (r // N_DEV, c)


def _full_from_gathered(name, g):
    if name in COL_SHARDED:
        return jnp.transpose(g, (1, 0, 2)).reshape(FULL_SHAPE[name])
    return g.reshape(FULL_SHAPE[name])


def _dest_major(name, gfull):
    r, c = _shard_shape(name)
    if name in COL_SHARDED:
        blk = jnp.transpose(gfull.reshape(r, N_DEV, c), (1, 0, 2))
    else:
        blk = gfull.reshape(N_DEV, r, c)
    return blk.reshape(4, 2, r, c)


def _pack_small(t):
    flat = jnp.concatenate([t[n].reshape(-1).astype(F32) for n in SMALL_WEIGHTS])
    return jnp.pad(flat, (0, SMALL_ROWS * LANES - flat.shape[0])).reshape(SMALL_ROWS, LANES)


def _unpack_small(packed):
    flat = packed.reshape(-1)
    out, off = {}, 0
    for n in SMALL_WEIGHTS:
        size = math.prod(SMALL_SHAPE[n])
        out[n] = flat[off:off + size].reshape(SMALL_SHAPE[n])
        off += size
    return out


def _after(value, token):
    return lax.optimization_barrier((value, token))[0]


def _local_step(x, target, rel_bias, b_in, lns, weights_of, on_grads=None):
    S = x.shape[0]
    tables = _ret_tables(S)
    biases = [_attn_bias(rel_bias, gi, dil) for gi, (_, dil) in enumerate(ATTN_GROUPS)]

    h = x
    hb, h_t = _cast_transpose(x, name="cast_x")
    saved, Ws = [], []
    for l in range(DEPTH):
        h, hb, h_t, sv, W = _layer_fwd(l, h, hb, h_t, weights_of, b_in[l], biases, lns[l], tables)
        saved.append(sv)
        Ws.append(W)
    dy, sq = _loss_fwd_bwd(h, target, name="loss")
    loss_local = 0.5 * sq[0, 0] / D_MODEL

    grads = [None] * DEPTH
    db_tot = None
    dx = dy
    token = None
    for l in reversed(range(DEPTH)):
        dx, g, dbs, token = _layer_bwd(l, dx, saved[l], Ws[l], biases, lns[l], tables, token, on_grads)
        grads[l] = g
        db_tot = dbs if db_tot is None else [a + b for a, b in zip(db_tot, dbs)]
    small = {"rel_bias": _bias_grad(db_tot, name="bias_grad"),
             "b_in": jnp.stack([grads[l]["b_in"] for l in range(DEPTH)])}
    for n in ("ln1_g", "ln1_b", "ln2_g", "ln2_b"):
        small[n] = jnp.stack([grads[l][n].reshape(D_MODEL) for l in range(DEPTH)])
    return loss_local, dx, grads, small


def kernel(x, rel_bias, w_in, b_in, w_attn_proj, w_ret_proj, w_out, ln1_g, ln1_b, w_ffn_gate, w_ffn_up, w_ffn_down, ln2_g, ln2_b, loss_target, m_rel_bias, m_w_in, m_b_in, m_w_attn_proj, m_w_ret_proj, m_w_out, m_ln1_g, m_ln1_b, m_w_ffn_gate, m_w_ffn_up, m_w_ffn_down, m_ln2_g, m_ln2_b, v_rel_bias, v_w_in, v_b_in, v_w_attn_proj, v_w_ret_proj, v_w_out, v_ln1_g, v_ln1_b, v_w_ffn_gate, v_w_ffn_up, v_w_ffn_down, v_ln2_g, v_ln2_b):
    w = dict(rel_bias=rel_bias, w_in=w_in, b_in=b_in, w_attn_proj=w_attn_proj, w_ret_proj=w_ret_proj, w_out=w_out,
             ln1_g=ln1_g, ln1_b=ln1_b, w_ffn_gate=w_ffn_gate, w_ffn_up=w_ffn_up, w_ffn_down=w_ffn_down,
             ln2_g=ln2_g, ln2_b=ln2_b)
    m = dict(rel_bias=m_rel_bias, w_in=m_w_in, b_in=m_b_in, w_attn_proj=m_w_attn_proj, w_ret_proj=m_w_ret_proj,
             w_out=m_w_out, ln1_g=m_ln1_g, ln1_b=m_ln1_b, w_ffn_gate=m_w_ffn_gate, w_ffn_up=m_w_ffn_up,
             w_ffn_down=m_w_ffn_down, ln2_g=m_ln2_g, ln2_b=m_ln2_b)
    v = dict(rel_bias=v_rel_bias, w_in=v_w_in, b_in=v_b_in, w_attn_proj=v_w_attn_proj, w_ret_proj=v_w_ret_proj,
             w_out=v_w_out, ln1_g=v_ln1_g, ln1_b=v_ln1_b, w_ffn_gate=v_w_ffn_gate, w_ffn_up=v_w_ffn_up,
             w_ffn_down=v_w_ffn_down, ln2_g=v_ln2_g, ln2_b=v_ln2_b)

    assert DEPTH == 2
    me = 4 * lax.axis_index("x") + 2 * lax.axis_index("y") + lax.axis_index("c")
    core = lax.axis_index("c").astype(jnp.int32).reshape(1)

    def own_slot(lands, blocks):
        return [lax.dynamic_update_index_in_dim(land, blk, me, 0) for land, blk in zip(lands, blocks)]

    shard = {(l, n): w[n][l].astype(BF16) for l in range(DEPTH) for n in BIG_WEIGHTS}
    rest = WEIGHT_GROUPS["proj"] + WEIGHT_GROUPS["ffn"]
    (w_in0,) = _all_gather([shard[0, "w_in"]], name="all_gather_l0_in")
    gathers = {0: (rest, _exchange_start(_after([shard[0, n] for n in rest], w_in0), False,
                                         name="all_gather_l0_rest_start"))}
    first_token = gathers[0][1][-1][0, 0].astype(BF16)
    gathers[1] = (BIG_WEIGHTS, _exchange_start([shard[1, n] + first_token for n in BIG_WEIGHTS], False,
                                               name="all_gather_l1_start"))
    b_in_fwd = [_behind(b_in[0], gathers[1][1][-1]), b_in[1]]
    arrived = {}

    def weights_of(l, group, after):
        if (l, group) == (0, "in"):
            return {"w_in": _full_from_gathered("w_in", w_in0)}
        if l not in arrived:
            names, started = gathers[l]
            lands = _exchange_wait(started, after, False, name=f"all_gather_l{l}_wait")
            full = own_slot(lands, [shard[l, n] for n in names])
            arrived[l] = {n: _full_from_gathered(n, g) for n, g in zip(names, full)}
        names = WEIGHT_GROUPS["in"] if group == "in" else rest
        return {n: arrived[l][n] for n in names}

    scatters = {}

    def on_grads(l, group, gd):
        if (l, group) == (0, "in"):
            return None
        names = WEIGHT_GROUPS[group]
        blocks = [_dest_major(n, gd[n]).reshape((N_DEV,) + _shard_shape(n)).astype(BF16) for n in names]
        scatters[l, group] = (names, blocks, _exchange_start(blocks, True, name=f"rs_l{l}_{group}_start"))
        return scatters[l, group][2][-1]

    lns = [{n: w[n][l] for n in ("ln1_g", "ln1_b", "ln2_g", "ln2_b")} for l in range(DEPTH)]
    loss_local, grad_x, grads, small = _local_step(x[0], loss_target[0], rel_bias, b_in_fwd, lns, weights_of,
                                                   on_grads)
    loss = lax.psum(loss_local, ("x", "y", "c"))

    g8 = [_dest_major("w_in", grads[0]["w_in"])]
    from_sibling = _rs_sibling_exchange(g8, name="rs_sibling_exchange_l0_in")
    chip_parts = [_rs_chip_sum(g8[0], from_sibling[0], core, name="rs_chip_sum_l0_in")]
    parts = {(0, "w_in"): _rs_chip_exchange(chip_parts, name="rs_chip_exchange_l0_in")[0]}
    for (l, group), (names, blocks, started) in scatters.items():
        lands = _exchange_wait(started, parts[0, "w_in"], True, name=f"rs_l{l}_{group}_wait")
        own = [lax.dynamic_index_in_dim(b, me, 0, keepdims=False) for b in blocks]
        for n, p in zip(names, own_slot(lands, own)):
            parts[l, n] = p
    big = [{} for _ in range(4)]
    for n in BIG_WEIGHTS:
        res = _adam_sharded([parts[l, n] for l in range(DEPTH)], w[n], m[n], v[n], name=f"adam_{n}")
        for kind in range(4):
            big[kind][n] = res[kind]

    gs = _all_reduce_small(_pack_small(small), name="all_reduce_small")
    ds, ms, vs = _adam_small(gs, _pack_small(w), _pack_small(m), _pack_small(v), name="adam_small")
    sm = [_unpack_small(t) for t in (gs, ds, ms, vs)]

    outs = [loss, grad_x[None]]
    for kind in range(4):
        for n in ALL_WEIGHTS:
            outs.append(big[kind][n] if n in BIG_WEIGHTS else sm[kind][n])
    return tuple(outs)
```

```python
import math

import numpy as np
import jax
import jax.numpy as jnp
from jax import lax
from jax.experimental import pallas as pl
from jax.experimental.pallas import tpu as pltpu

F32 = jnp.float32
BF16 = jnp.bfloat16
MESH = pl.DeviceIdType.MESH

D_MODEL = 1024
DEPTH = 2
HEAD_DIM = 64
ATTN_GROUPS = ((128, 1), (512, 4), (2048, 16))
HEADS_PER_GROUP = 6
GROUP_WIDTH = HEADS_PER_GROUP * HEAD_DIM
ATTN_BLOCK = 128
NUM_BUCKETS = 32
MAX_DISTANCE = 2048
RET_HEADS = 4
RET_QK_DIM = 256
RET_V_DIM = 512
RET_CHUNK = 128
RET_T_CHUNKS = 4
ROPE_BASE = 10000.0
D_FF = 2816
ALPHA = (2 * DEPTH) ** 0.25
LN_EPS = 1e-5
GN_EPS = 1e-5
ATTN_W = 3 * GROUP_WIDTH
IN_COLS = 3 * ATTN_W + 2 * 1024 + 2 * 2048 + 2 * 1024
ADAM_LR, ADAM_B1, ADAM_B2, ADAM_EPS, ADAM_WD, ADAM_STEP = 0.001, 0.9, 0.999, 1e-08, 0.01, 10
N_DEV = 8
NEG = -1e30
LANES = 128
VMEM_LIMIT = 56 * 1024 * 1024
MM_TILE_CAP = 1664
MM_VMEM_BUDGET = 44 * 1024 * 1024

BIG_WEIGHTS = ("w_in", "w_attn_proj", "w_ret_proj", "w_out", "w_ffn_gate", "w_ffn_up", "w_ffn_down")
COL_SHARDED = ("w_in", "w_attn_proj", "w_ffn_gate", "w_ffn_up")
FULL_SHAPE = {"w_in": (D_MODEL, IN_COLS), "w_attn_proj": (GROUP_WIDTH, D_MODEL), "w_ret_proj": (2048, D_MODEL),
              "w_out": (D_MODEL, D_MODEL), "w_ffn_gate": (D_MODEL, D_FF), "w_ffn_up": (D_MODEL, D_FF),
              "w_ffn_down": (D_FF, D_MODEL)}
SMALL_WEIGHTS = ("rel_bias", "b_in", "ln1_g", "ln1_b", "ln2_g", "ln2_b")
SMALL_SHAPE = {"rel_bias": (NUM_BUCKETS, 18), "b_in": (DEPTH, IN_COLS), "ln1_g": (DEPTH, D_MODEL),
               "ln1_b": (DEPTH, D_MODEL), "ln2_g": (DEPTH, D_MODEL), "ln2_b": (DEPTH, D_MODEL)}
SMALL_ROWS = 256
ALL_WEIGHTS = ("rel_bias", "w_in", "b_in", "w_attn_proj", "w_ret_proj", "w_out", "ln1_g", "ln1_b",
               "w_ffn_gate", "w_ffn_up", "w_ffn_down", "ln2_g", "ln2_b")


def _cparams(sem=None):
    return pltpu.CompilerParams(dimension_semantics=sem, vmem_limit_bytes=VMEM_LIMIT)


def _div_tile(n, cap, unit):
    if n <= cap:
        return n
    best = None
    for t in range(unit, cap + 1, unit):
        if n % t == 0:
            best = t
    assert best is not None, (n, cap, unit)
    return best


def _mm(a, b, *, name, out_dtype=F32, bias=None, add=None, groups=None, lane_chunks=False, transpose_b=False,
        a_halves=False, b_halves=False):
    M, K = (a.shape[1], 2 * a.shape[2]) if a_halves else a.shape
    if b_halves:
        assert not transpose_b
        K2, N = b.shape[1], 2 * b.shape[2]
    else:
        N, K2 = b.shape if transpose_b else b.shape[::-1]
    assert K == K2 and a.dtype == BF16 and b.dtype == BF16
    has_bias, has_add = bias is not None, add is not None
    tm = _div_tile(M, 1024, 16)
    tn = N // groups if groups else _div_tile(N // 2 if b_halves else N, MM_TILE_CAP, LANES)
    out_bytes = jnp.dtype(out_dtype).itemsize
    k_span = K // 2 if a_halves else K

    def vmem_bytes(tk):
        return 2 * (2 * tm * tk + 2 * tk * tn + out_bytes * tm * tn + (4 * tm * tn if has_add else 0))

    tk = max(t for t in range(LANES, k_span + 1, LANES)
             if k_span % t == 0 and (t == LANES or vmem_bytes(t) <= MM_VMEM_BUDGET))
    nk = K // tk
    nch = tn // LANES
    assert nk == 1 or (out_dtype == F32 and not lane_chunks and not has_bias)

    def body(*refs):
        a_ref, b_ref = refs[0], refs[1]
        pos = 2
        bias_ref = add_ref = None
        if has_bias:
            bias_ref = refs[pos]
            pos += 1
        if has_add:
            add_ref = refs[pos]
            pos += 1
        o_ref = refs[pos]

        def finish(r):
            if has_bias:
                r = r + bias_ref[...]
            if has_add:
                r = r + add_ref[...]
            if lane_chunks:
                for c in range(nch):
                    o_ref[c] = r[:, c * LANES:(c + 1) * LANES].astype(o_ref.dtype)
            else:
                o_ref[...] = r.astype(o_ref.dtype)

        def product():
            if transpose_b:
                return lax.dot_general(a_ref[...], b_ref[...], (((1,), (1,)), ((), ())), preferred_element_type=F32)
            return jnp.dot(a_ref[...], b_ref[...], preferred_element_type=F32)

        if nk == 1:
            finish(product())
        else:
            @pl.when(pl.program_id(2) == 0)
            def _():
                o_ref[...] = add_ref[...] if has_add else jnp.zeros_like(o_ref)

            o_ref[...] += product()

    kh, nh = k_span // tk, (N // 2) // tn
    if a_halves:
        a_spec = pl.BlockSpec((None, tm, tk), lambda i, j, k: (k // kh, i, k % kh))
    else:
        a_spec = pl.BlockSpec((tm, tk), lambda i, j, k: (i, k))
    if b_halves:
        b_spec = pl.BlockSpec((None, tk, tn), lambda i, j, k: (j // nh, k, j % nh))
    elif transpose_b:
        b_spec = pl.BlockSpec((tn, tk), lambda i, j, k: (j, k))
    else:
        b_spec = pl.BlockSpec((tk, tn), lambda i, j, k: (k, j))
    in_specs = [a_spec, b_spec]
    args = [a, b]
    if has_bias:
        in_specs.append(pl.BlockSpec((1, tn), lambda i, j, k: (0, j)))
        args.append(bias.reshape(1, N).astype(F32))
    if has_add:
        in_specs.append(pl.BlockSpec((tm, tn), lambda i, j, k: (i, j)))
        args.append(add)
    if lane_chunks:
        assert groups
        out_shape = jax.ShapeDtypeStruct((groups, nch, M, LANES), out_dtype)
        out_spec = pl.BlockSpec((None, nch, tm, LANES), lambda i, j, k: (j, 0, i, 0))
    elif groups:
        out_shape = jax.ShapeDtypeStruct((groups, M, tn), out_dtype)
        out_spec = pl.BlockSpec((None, tm, tn), lambda i, j, k: (j, i, 0))
    else:
        out_shape = jax.ShapeDtypeStruct((M, N), out_dtype)
        out_spec = pl.BlockSpec((tm, tn), lambda i, j, k: (i, j))
    out = pl.pallas_call(
        body, name=name, grid=(M // tm, N // tn, nk), in_specs=in_specs, out_specs=out_spec,
        out_shape=out_shape,
        compiler_params=_cparams(("parallel", "parallel", "arbitrary")))(*args)
    return out.reshape(groups * nch, M, LANES) if lane_chunks else out


def _row_spec(tr, w):
    return pl.BlockSpec((tr, w), lambda i: (i, 0))


def _vec_spec(w):
    return pl.BlockSpec((1, w), lambda i: (0, 0))


def _col_spec(w, tr):
    return pl.BlockSpec((w, tr), lambda i: (0, i))


def _cast_transpose(x, *, name):
    S, W = x.shape
    tr = 512

    def body(x_ref, o_ref, ot_ref):
        v = x_ref[...]
        o_ref[...] = v.astype(BF16)
        ot_ref[...] = v.T.astype(BF16)

    return pl.pallas_call(
        body, name=name, grid=(S // tr,), in_specs=[_row_spec(tr, W)],
        out_specs=[_row_spec(tr, W), _col_spec(W, tr)],
        out_shape=[jax.ShapeDtypeStruct((S, W), BF16), jax.ShapeDtypeStruct((W, S), BF16)],
        compiler_params=_cparams(("parallel",)))(x)


def _ln_fwd(x, sub, g, b, *, name):
    S, W = x.shape
    tr = 512

    def body(x_ref, s_ref, g_ref, b_ref, h_ref, y_ref, yb_ref, ybt_ref):
        h = ALPHA * x_ref[...] + s_ref[...]
        mu = jnp.mean(h, axis=-1, keepdims=True)
        d = h - mu
        var = jnp.mean(d * d, axis=-1, keepdims=True)
        y = d * lax.rsqrt(var + LN_EPS) * g_ref[...] + b_ref[...]
        h_ref[...] = h
        y_ref[...] = y
        yb_ref[...] = y.astype(BF16)
        ybt_ref[...] = y.T.astype(BF16)

    return pl.pallas_call(
        body, name=name, grid=(S // tr,),
        in_specs=[_row_spec(tr, W), _row_spec(tr, W), _vec_spec(W), _vec_spec(W)],
        out_specs=[_row_spec(tr, W)] * 3 + [_col_spec(W, tr)],
        out_shape=[jax.ShapeDtypeStruct((S, W), F32), jax.ShapeDtypeStruct((S, W), F32),
                   jax.ShapeDtypeStruct((S, W), BF16), jax.ShapeDtypeStruct((W, S), BF16)],
        compiler_params=_cparams(("parallel",)))(x, sub, g.reshape(1, W), b.reshape(1, W))


def _ln_bwd(dy, h, g, *, name):
    S, W = dy.shape
    tr = 512

    def body(dy_ref, h_ref, g_ref, dhb_ref, res_ref, dg_ref, db_ref):
        @pl.when(pl.program_id(0) == 0)
        def _():
            dg_ref[...] = jnp.zeros_like(dg_ref)
            db_ref[...] = jnp.zeros_like(db_ref)

        hh = h_ref[...]
        mu = jnp.mean(hh, axis=-1, keepdims=True)
        d = hh - mu
        var = jnp.mean(d * d, axis=-1, keepdims=True)
        rstd = lax.rsqrt(var + LN_EPS)
        xhat = d * rstd
        dyv = dy_ref[...]
        dg_ref[...] += jnp.sum(dyv * xhat, axis=0, keepdims=True)
        db_ref[...] += jnp.sum(dyv, axis=0, keepdims=True)
        dxh = dyv * g_ref[...]
        dh = rstd * (dxh - jnp.mean(dxh, axis=-1, keepdims=True)
                     - xhat * jnp.mean(dxh * xhat, axis=-1, keepdims=True))
        dhb_ref[...] = dh.astype(BF16)
        res_ref[...] = ALPHA * dh

    return pl.pallas_call(
        body, name=name, grid=(S // tr,),
        in_specs=[_row_spec(tr, W), _row_spec(tr, W), _vec_spec(W)],
        out_specs=[_row_spec(tr, W), _row_spec(tr, W), _vec_spec(W), _vec_spec(W)],
        out_shape=[jax.ShapeDtypeStruct((S, W), BF16), jax.ShapeDtypeStruct((S, W), F32),
                   jax.ShapeDtypeStruct((1, W), F32), jax.ShapeDtypeStruct((1, W), F32)],
        compiler_params=_cparams(("arbitrary",)))(dy, h, g.reshape(1, W))


def _loss_fwd_bwd(y, target, *, name):
    S, W = y.shape
    tr = 512

    def body(y_ref, t_ref, dy_ref, acc_ref):
        @pl.when(pl.program_id(0) == 0)
        def _():
            acc_ref[...] = jnp.zeros_like(acc_ref)

        e = y_ref[...] - t_ref[...]
        acc_ref[...] += jnp.sum(jnp.sum(e * e, axis=-1, keepdims=True), axis=0, keepdims=True)
        dy_ref[...] = e * (1.0 / W)

    return pl.pallas_call(
        body, name=name, grid=(S // tr,),
        in_specs=[_row_spec(tr, W), _row_spec(tr, W)],
        out_specs=[_row_spec(tr, W), pl.BlockSpec((1, 1), lambda i: (0, 0))],
        out_shape=[jax.ShapeDtypeStruct((S, W), F32), jax.ShapeDtypeStruct((1, 1), F32)],
        compiler_params=_cparams(("arbitrary",)))(y, target)


def _combine_fwd(os_, ls_, *, name):
    NCH, S, _ = os_[0].shape
    W = NCH * LANES
    tr = 512

    def body(o0, o1, o2, l0, l1, l2, yb_ref, ybt_ref, y_ref, w0_ref, w1_ref, w2_ref):
        for c in range(NCH):
            la, lb, lc = l0[c], l1[c], l2[c]
            m = jnp.maximum(jnp.maximum(la, lb), lc)
            ea, eb, ec = jnp.exp(la - m), jnp.exp(lb - m), jnp.exp(lc - m)
            inv = 1.0 / (ea + eb + ec)
            wa, wb, wc = ea * inv, eb * inv, ec * inv
            y = wa * o0[c] + wb * o1[c] + wc * o2[c]
            y_ref[c] = y
            yb_ref[:, c * LANES:(c + 1) * LANES] = y.astype(BF16)
            ybt_ref[c * LANES:(c + 1) * LANES, :] = y.T.astype(BF16)
            w0_ref[c] = wa
            w1_ref[c] = wb
            w2_ref[c] = wc

    ch = pl.BlockSpec((NCH, tr, LANES), lambda i: (0, i, 0))
    yb, ybt, y, w0, w1, w2 = pl.pallas_call(
        body, name=name, grid=(S // tr,),
        in_specs=[ch] * 6,
        out_specs=[_row_spec(tr, W), _col_spec(W, tr)] + [ch] * 4,
        out_shape=[jax.ShapeDtypeStruct((S, W), BF16), jax.ShapeDtypeStruct((W, S), BF16)]
        + [jax.ShapeDtypeStruct((NCH, S, LANES), F32)] * 4,
        compiler_params=_cparams(("parallel",)))(*os_, *ls_)
    return yb, ybt, y, (w0, w1, w2)


def _ret_proj_merge(yb, wrp, gates, pa, *, name):
    S, K = yb.shape
    W = wrp.shape[1]
    tr = 512

    def body(y_ref, w_ref, g_ref, pa_ref, pr_ref, o_ref, ot_ref):
        pr = _dot(y_ref[...], w_ref[...])
        m = jax.nn.sigmoid(g_ref[0]) * pa_ref[...] + jax.nn.sigmoid(g_ref[1]) * pr
        pr_ref[...] = pr
        o_ref[...] = m.astype(BF16)
        ot_ref[...] = m.T.astype(BF16)

    return pl.pallas_call(
        body, name=name, grid=(S // tr,),
        in_specs=[_row_spec(tr, K), pl.BlockSpec((K, W), lambda i: (0, 0)),
                  pl.BlockSpec((2, tr, W), lambda i: (0, i, 0)), _row_spec(tr, W)],
        out_specs=[_row_spec(tr, W), _row_spec(tr, W), _col_spec(W, tr)],
        out_shape=[jax.ShapeDtypeStruct((S, W), F32), jax.ShapeDtypeStruct((S, W), BF16),
                   jax.ShapeDtypeStruct((W, S), BF16)],
        compiler_params=_cparams(("parallel",)))(yb, wrp, gates, pa)


def _out_proj_bwd_merge(dhb, wout, gates, pa, pr, *, name):
    S, W = pa.shape
    tr = 512

    def body(d_ref, w_ref, g_ref, pa_ref, pr_ref, dpa_ref, dpr_ref, dg_ref):
        dmv = _dot_nt(d_ref[...], w_ref[...])
        sa, sb = jax.nn.sigmoid(g_ref[0]), jax.nn.sigmoid(g_ref[1])
        dpa_ref[...] = (dmv * sa).astype(BF16)
        dpr_ref[...] = (dmv * sb).astype(BF16)
        dg_ref[0] = (dmv * pa_ref[...] * (sa * (1.0 - sa))).astype(BF16)
        dg_ref[1] = (dmv * pr_ref[...] * (sb * (1.0 - sb))).astype(BF16)

    g3 = pl.BlockSpec((2, tr, W), lambda i: (0, i, 0))
    return pl.pallas_call(
        body, name=name, grid=(S // tr,),
        in_specs=[_row_spec(tr, W), pl.BlockSpec((W, W), lambda i: (0, 0)), g3, _row_spec(tr, W), _row_spec(tr, W)],
        out_specs=[_row_spec(tr, W), _row_spec(tr, W), g3],
        out_shape=[jax.ShapeDtypeStruct((S, W), BF16), jax.ShapeDtypeStruct((S, W), BF16),
                   jax.ShapeDtypeStruct((2, S, W), BF16)],
        compiler_params=_cparams(("parallel",)))(dhb, wout, gates, pa, pr)


def _ffn_in_swiglu(x, wg, wu, *, name):
    S, D = x.shape
    F = wg.shape[1]
    tm, tf = 512, F // 2

    def body(x_ref, wg_ref, wu_ref, uv_ref, h_ref, ht_ref):
        xv = x_ref[...]
        u = _dot(xv, wg_ref[...])
        v = _dot(xv, wu_ref[...])
        hh = u * jax.nn.sigmoid(u) * v
        uv_ref[0] = u.astype(BF16)
        uv_ref[1] = v.astype(BF16)
        h_ref[...] = hh.astype(BF16)
        ht_ref[...] = hh.T.astype(BF16)

    w_spec = pl.BlockSpec((D, tf), lambda j, i: (0, j))
    return pl.pallas_call(
        body, name=name, grid=(F // tf, S // tm),
        in_specs=[pl.BlockSpec((tm, D), lambda j, i: (i, 0)), w_spec, w_spec],
        out_specs=[pl.BlockSpec((2, tm, tf), lambda j, i: (0, i, j)), pl.BlockSpec((tm, tf), lambda j, i: (i, j)),
                   pl.BlockSpec((tf, tm), lambda j, i: (j, i))],
        out_shape=[jax.ShapeDtypeStruct((2, S, F), BF16), jax.ShapeDtypeStruct((S, F), BF16),
                   jax.ShapeDtypeStruct((F, S), BF16)],
        compiler_params=_cparams(("parallel", "parallel")))(x, wg, wu)


def _ffn_down_bwd_swiglu(dhb, wd, uv, *, name):
    S, D = dhb.shape
    F = wd.shape[0]
    tm, tf = 512, F // 2

    def body(d_ref, w_ref, uv_ref, o_ref):
        dh = _dot_nt(d_ref[...], w_ref[...])
        u, v = uv_ref[0].astype(F32), uv_ref[1].astype(F32)
        sg = jax.nn.sigmoid(u)
        o_ref[0] = (dh * v * (sg * (1.0 + u * (1.0 - sg)))).astype(BF16)
        o_ref[1] = (dh * (u * sg)).astype(BF16)

    half = pl.BlockSpec((2, tm, tf), lambda j, i: (0, i, j))
    return pl.pallas_call(
        body, name=name, grid=(F // tf, S // tm),
        in_specs=[pl.BlockSpec((tm, D), lambda j, i: (i, 0)), pl.BlockSpec((tf, D), lambda j, i: (j, 0)), half],
        out_specs=half, out_shape=jax.ShapeDtypeStruct((2, S, F), BF16),
        compiler_params=_cparams(("parallel", "parallel")))(dhb, wd, uv)


def _assemble_dz(da, dq_r, dk_r, dv_r, dg_r, dgates, *, name):
    S = dv_r.shape[0]
    tr = 256
    GW = GROUP_WIDTH
    NCH = GW // LANES

    def body(*refs):
        a_refs = refs[0:9]
        q_ref, k_ref, v_ref, g_ref, gt_ref, dz_ref, cs_ref = refs[9:]

        @pl.when(pl.program_id(0) == 0)
        def _():
            cs_ref[...] = jnp.zeros_like(cs_ref)

        def put(off, val):
            w = val.shape[-1]
            dz_ref[:, off:off + w] = val.astype(BF16)
            cs_ref[:, off:off + w] += jnp.sum(val.astype(F32), axis=0, keepdims=True)

        for which in range(3):
            for gi in range(3):
                for c in range(NCH):
                    put(which * ATTN_W + gi * GW + c * LANES, a_refs[3 * gi + which][c])
        off = 3 * ATTN_W
        put(off, q_ref[...])
        put(off + 1024, k_ref[...])
        put(off + 2048, v_ref[...])
        put(off + 4096, g_ref[...])
        put(off + 6144, gt_ref[0])
        put(off + 7168, gt_ref[1])

    flat_a = [t for grp in da for t in grp]
    return pl.pallas_call(
        body, name=name, grid=(S // tr,),
        in_specs=[pl.BlockSpec((NCH, tr, LANES), lambda i: (0, i, 0))] * 9 + [_row_spec(tr, 1024), _row_spec(tr, 1024),
                  _row_spec(tr, 2048), _row_spec(tr, 2048), pl.BlockSpec((2, tr, 1024), lambda i: (0, i, 0))],
        out_specs=[_row_spec(tr, IN_COLS), _vec_spec(IN_COLS)],
        out_shape=[jax.ShapeDtypeStruct((S, IN_COLS), BF16), jax.ShapeDtypeStruct((1, IN_COLS), F32)],
        compiler_params=_cparams(("arbitrary",)))(*flat_a, dq_r, dk_r, dv_r, dg_r, dgates)


def _t5_bucket(dist):
    max_exact = NUM_BUCKETS // 2
    large = max_exact + (np.log(np.maximum(dist, max_exact) / max_exact)
                         / np.log(MAX_DISTANCE / max_exact) * (NUM_BUCKETS - max_exact)).astype(np.int32)
    large = np.minimum(large, NUM_BUCKETS - 1)
    return np.where(dist < max_exact, dist, large).astype(np.int32)


def _attn_tables(dilation):
    W = ATTN_BLOCK
    qi = np.arange(W)[:, None]
    kj = np.arange(2 * W)[None, :]
    rel = qi + W - kj
    valid = (rel >= 0) & (rel <= W)
    buckets = _t5_bucket(np.clip(rel, 0, W) * dilation)
    return buckets, valid


def _attn_bias(rel_bias, gi, dilation):
    buckets, valid = _attn_tables(dilation)
    table = rel_bias[:, gi * HEADS_PER_GROUP:(gi + 1) * HEADS_PER_GROUP]
    onehot = (jnp.asarray(buckets.reshape(-1, 1)) == jnp.arange(NUM_BUCKETS)[None, :]).astype(F32)
    bias = jnp.dot(onehot, table.astype(F32), precision=lax.Precision.HIGHEST)
    bias = bias.T.reshape(HEADS_PER_GROUP, ATTN_BLOCK, 2 * ATTN_BLOCK)
    return jnp.where(jnp.asarray(valid)[None], bias, NEG)


def _dot_nt(a, b):
    return lax.dot_general(a, b, (((1,), (1,)), ((), ())), preferred_element_type=F32)


def _dot_tn(a, b):
    return lax.dot_general(a, b, (((0,), (0,)), ((), ())), preferred_element_type=F32)


def _dot(a, b):
    return jnp.dot(a, b, preferred_element_type=F32)


ATTN_RESIDUES_PER_STEP = 4
ATTN_UNITS_AT_ONCE = 8
HEADS_PER_CHUNK = LANES // HEAD_DIM
N_CHUNKS = GROUP_WIDTH // LANES


def _first_block_mask(has_prev):
    col = lax.broadcasted_iota(jnp.int32, (1, 2 * ATTN_BLOCK), 1)
    return jnp.where(jnp.logical_or(has_prev, col >= ATTN_BLOCK), 0.0, NEG).astype(F32)


def _head_lanes(hh):
    return slice(HEAD_DIM * hh, HEAD_DIM * (hh + 1))


def _attn_geometry(S, d):
    rps = ATTN_RESIDUES_PER_STEP if d == 1 else min(d, ATTN_RESIDUES_PER_STEP)
    rows_per_block = ATTN_BLOCK * (rps if d == 1 else d)
    return rows_per_block, S // rows_per_block, rps, 1 if d == 1 else d // rps


def _unit_rows(d, rps, rg, u):
    B = ATTN_BLOCK
    if d == 1:
        return pl.ds(B * u, B), pl.ds(B * (u - 1 if u else rps - 1), B), u == 0
    rows = pl.ds(rg * rps + u, B, stride=d)
    return rows, rows, True


def _attn_in_specs(gi, RB, last):
    def spec(which, prev):
        if prev:
            return pl.BlockSpec((None, RB, LANES),
                                lambda j, n, rg: (9 * which + 3 * gi + j, jnp.clip(n - 1, 0, last), 0))
        return pl.BlockSpec((None, RB, LANES), lambda j, n, rg: (9 * which + 3 * gi + j, jnp.minimum(n, last), 0))
    bias = pl.BlockSpec((HEADS_PER_CHUNK, ATTN_BLOCK, 2 * ATTN_BLOCK), lambda j, n, rg: (j, 0, 0))
    return [spec(0, False), spec(1, True), spec(1, False), spec(2, True), spec(2, False), bias]


def _attn_fwd(qkv, bias, gi, d, *, name):
    _, S, _ = qkv.shape
    B = ATTN_BLOCK
    RB, nb, rps, nrg = _attn_geometry(S, d)
    scale = HEAD_DIM ** -0.5
    units = [(rr, hh) for rr in range(rps) for hh in range(HEADS_PER_CHUNK)]

    def body(q_ref, kp_ref, kc_ref, vp_ref, vc_ref, b_ref, o_ref, l_ref):
        n, rg = pl.program_id(1), pl.program_id(2)
        first = _first_block_mask(n > 0)
        ur = [_unit_rows(d, rps, rg, u) for u in range(rps)]
        rows = [r_ for r_, _, _ in ur]
        edge = [first if in_prev else 0.0 for _, _, in_prev in ur]
        q = [q_ref[r_, :].astype(BF16) for r_ in rows]
        k2 = [jnp.concatenate([(kp_ref if in_prev else kc_ref)[pr, :], kc_ref[r_, :]], axis=0).astype(BF16)
              for r_, pr, in_prev in ur]
        v2 = [jnp.concatenate([(vp_ref if in_prev else vc_ref)[pr, :], vc_ref[r_, :]], axis=0).astype(BF16)
              for r_, pr, in_prev in ur]
        o_part, l_part = {}, {}
        for u0 in range(0, len(units), ATTN_UNITS_AT_ONCE):
            us = units[u0:u0 + ATTN_UNITS_AT_ONCE]
            s = [_dot_nt(q[rr][:, _head_lanes(hh)], k2[rr][:, _head_lanes(hh)]) * scale + b_ref[hh] + edge[rr]
                 for rr, hh in us]
            m = [jnp.max(x, axis=-1, keepdims=True) for x in s]
            p = [jnp.exp(x - mm) for x, mm in zip(s, m)]
            l = [jnp.sum(x, axis=-1, keepdims=True) for x in p]
            pb = [(x * (1.0 / ll)).astype(BF16) for x, ll in zip(p, l)]
            o = [_dot(x, v2[rr][:, _head_lanes(hh)]) for x, (rr, hh) in zip(pb, us)]
            for u, oo, mm, ll in zip(us, o, m, l):
                o_part[u] = oo
                l_part[u] = jnp.broadcast_to(mm + jnp.log(ll), (B, HEAD_DIM))
        for rr in range(rps):
            o_ref[rows[rr], :] = jnp.concatenate([o_part[(rr, hh)] for hh in range(HEADS_PER_CHUNK)], axis=1)
            l_ref[rows[rr], :] = jnp.concatenate([l_part[(rr, hh)] for hh in range(HEADS_PER_CHUNK)], axis=1)

    out_spec = pl.BlockSpec((None, RB, LANES), lambda j, n, rg: (j, n, 0))
    return pl.pallas_call(
        body, name=name, grid=(N_CHUNKS, nb, nrg),
        in_specs=_attn_in_specs(gi, RB, nb - 1),
        out_specs=[out_spec, out_spec],
        out_shape=[jax.ShapeDtypeStruct((N_CHUNKS, S, LANES), F32)] * 2,
        compiler_params=_cparams(("parallel", "arbitrary", "arbitrary")))(qkv, qkv, qkv, qkv, qkv, bias)


def _attn_bwd(qkv, bias, lse, dya, ya, wts, gi, d, *, name):
    _, S, _ = qkv.shape
    B = ATTN_BLOCK
    RB, nb, rps, nrg = _attn_geometry(S, d)
    scale = HEAD_DIM ** -0.5
    units = [(rr, hh) for rr in range(rps) for hh in range(HEADS_PER_CHUNK)]

    def body(q_ref, kp_ref, kc_ref, vp_ref, vc_ref, b_ref, l_ref, dya_ref, ya_ref, w_ref,
             dq_ref, dk_ref, dv_ref, db_ref, dk_carry, dv_carry):
        n, rg = pl.program_id(1), pl.program_id(2)
        ur = [_unit_rows(d, rps, rg, u) for u in range(rps)]
        rows = [r_ for r_, _, _ in ur]

        @pl.when((n == 0) & (rg == 0))
        def _():
            db_ref[...] = jnp.zeros_like(db_ref)
            dk_carry[...] = jnp.zeros_like(dk_carry)
            dv_carry[...] = jnp.zeros_like(dv_carry)

        @pl.when(n < nb)
        def _():
            first = _first_block_mask(n > 0)
            edge = [first if in_prev else 0.0 for _, _, in_prev in ur]
            q = [q_ref[r_, :].astype(BF16) for r_ in rows]
            k2 = [jnp.concatenate([(kp_ref if in_prev else kc_ref)[pr, :], kc_ref[r_, :]], axis=0).astype(BF16)
                  for r_, pr, in_prev in ur]
            v2 = [jnp.concatenate([(vp_ref if in_prev else vc_ref)[pr, :], vc_ref[r_, :]], axis=0).astype(BF16)
                  for r_, pr, in_prev in ur]
            lse_c = [l_ref[r_, :] for r_ in rows]
            dy_c = [dya_ref[r_, :] for r_ in rows]
            ya_c = [ya_ref[r_, :] for r_ in rows]
            w_c = [w_ref[r_, :] for r_ in rows]
            ds_sum = [None] * HEADS_PER_CHUNK
            dq_part, dk_part, dv_part = {}, {}, {}
            for u0 in range(0, len(units), ATTN_UNITS_AT_ONCE):
                us = units[u0:u0 + ATTN_UNITS_AT_ONCE]
                hl = [_head_lanes(hh) for _, hh in us]
                qh = [q[rr][:, sl] for (rr, _), sl in zip(us, hl)]
                kh = [k2[rr][:, sl] for (rr, _), sl in zip(us, hl)]
                vh = [v2[rr][:, sl] for (rr, _), sl in zip(us, hl)]
                s = [_dot_nt(a, k) * scale + b_ref[hh] + edge[rr] for a, k, (rr, hh) in zip(qh, kh, us)]
                p = [jnp.exp(x - lse_c[rr][:, HEAD_DIM * hh:HEAD_DIM * hh + 1]) for x, (rr, hh) in zip(s, us)]
                dy = [dy_c[rr][:, sl] for (rr, _), sl in zip(us, hl)]
                w = [w_c[rr][:, sl] for (rr, _), sl in zip(us, hl)]
                shift = [ww[:, 0:1] * jnp.sum(d_ * ya_c[rr][:, sl], axis=-1, keepdims=True)
                         for ww, d_, (rr, _), sl in zip(w, dy, us, hl)]
                do = [(ww * d_).astype(BF16) for ww, d_ in zip(w, dy)]
                ds = [pp * (_dot_nt(o_, v) - sh) for pp, o_, v, sh in zip(p, do, vh, shift)]
                for x, (_, hh) in zip(ds, us):
                    ds_sum[hh] = x if ds_sum[hh] is None else ds_sum[hh] + x
                dsb = [x.astype(BF16) for x in ds]
                pb = [x.astype(BF16) for x in p]
                for u, x, pp, a, k, o_ in zip(us, dsb, pb, qh, kh, do):
                    dq_part[u] = _dot(x, k) * scale
                    dk_part[u] = _dot_tn(x, a) * scale
                    dv_part[u] = _dot_tn(pp, o_)
            for hh in range(HEADS_PER_CHUNK):
                db_ref[hh] += ds_sum[hh]
            dk2, dv2 = [], []
            for rr in range(rps):
                dq_ref[rows[rr], :] = jnp.concatenate([dq_part[(rr, hh)] for hh in range(HEADS_PER_CHUNK)], axis=1)
                dk2.append(jnp.concatenate([dk_part[(rr, hh)] for hh in range(HEADS_PER_CHUNK)], axis=1))
                dv2.append(jnp.concatenate([dv_part[(rr, hh)] for hh in range(HEADS_PER_CHUNK)], axis=1))
            for out_ref, carry, d2 in ((dk_ref, dk_carry, dk2), (dv_ref, dv_carry, dv2)):
                if d == 1:
                    out_ref[...] = carry[...]
                    out_ref[ur[0][1], :] += d2[0][0:B]
                    for rr in range(rps):
                        nxt = d2[rr + 1][0:B] if rr + 1 < rps else 0.0
                        carry[rows[rr], :] = d2[rr][B:2 * B] + nxt
                else:
                    for rr in range(rps):
                        out_ref[rows[rr], :] = carry[rows[rr], :] + d2[rr][0:B]
                        carry[rows[rr], :] = d2[rr][B:2 * B]

        @pl.when(n == nb)
        def _():
            for r_ in rows:
                dk_ref[r_, :] = dk_carry[r_, :]
                dv_ref[r_, :] = dv_carry[r_, :]

    last = nb - 1
    cur = pl.BlockSpec((None, RB, LANES), lambda j, n, rg: (j, jnp.minimum(n, last), 0))
    lag = pl.BlockSpec((None, RB, LANES), lambda j, n, rg: (j, jnp.maximum(n - 1, 0), 0))
    db_spec = pl.BlockSpec((HEADS_PER_CHUNK, B, 2 * B), lambda j, n, rg: (j, 0, 0))
    dq, dk, dv, db = pl.pallas_call(
        body, name=name, grid=(N_CHUNKS, nb + 1, nrg),
        in_specs=_attn_in_specs(gi, RB, last) + [cur, cur, cur, cur],
        out_specs=[cur, lag, lag, db_spec],
        out_shape=[jax.ShapeDtypeStruct((N_CHUNKS, S, LANES), F32)] * 3
        + [jax.ShapeDtypeStruct((HEADS_PER_GROUP, B, 2 * B), F32)],
        scratch_shapes=[pltpu.VMEM((RB, LANES), F32), pltpu.VMEM((RB, LANES), F32)],
        compiler_params=_cparams(("arbitrary", "arbitrary", "arbitrary")))(
            qkv, qkv, qkv, qkv, qkv, bias, lse, dya, ya, wts)
    return (dq, dk, dv), db


def _bias_grad(dbs, *, name):
    nk = ATTN_BLOCK * 2 * ATTN_BLOCK
    buckets = []
    for (_, dil) in ATTN_GROUPS:
        b, valid = _attn_tables(dil)
        buckets.append(np.where(valid, b, -1).reshape(1, nk))
    bk = jnp.asarray(np.stack(buckets).astype(np.int32))
    flat = [x.reshape(HEADS_PER_GROUP, nk) for x in dbs]

    def body(bk_ref, d0, d1, d2, o_ref):
        ids = lax.broadcasted_iota(jnp.int32, (NUM_BUCKETS, nk), 0)
        for gi, dref in enumerate((d0, d1, d2)):
            onehot = (ids == bk_ref[gi]).astype(F32)
            o_ref[gi] = lax.dot_general(onehot, dref[...], (((1,), (1,)), ((), ())),
                                        preferred_element_type=F32, precision=lax.Precision.HIGHEST)

    out = pl.pallas_call(
        body, name=name,
        out_shape=jax.ShapeDtypeStruct((3, NUM_BUCKETS, HEADS_PER_GROUP), F32),
        compiler_params=_cparams())(bk, *flat)
    return jnp.transpose(out, (1, 0, 2)).reshape(NUM_BUCKETS, 3 * HEADS_PER_GROUP)


def _ret_tables(S):
    half = RET_QK_DIM // 2
    pos = jnp.arange(S, dtype=F32)
    inv_freq = ROPE_BASE ** (-jnp.arange(half, dtype=F32) / half)
    ang = pos[:, None] * inv_freq[None]
    cos, sin = jnp.cos(ang), jnp.sin(ang)
    H, C = RET_HEADS, RET_CHUNK
    log_g = jnp.log(1.0 - 2.0 ** (-5.0 - jnp.arange(H, dtype=F32)))
    n = jnp.arange(C, dtype=F32)
    diff = n[:, None] - n[None, :]
    dmask = jnp.where(diff >= 0, jnp.exp(log_g[:, None, None] * jnp.maximum(diff, 0.0)), 0.0)
    q_dec = jnp.exp(log_g[:, None] * (n + 1.0))
    k_dec = jnp.exp(log_g[:, None] * (C - 1.0 - n))
    chunk_dec = jnp.exp(log_g * C)
    qd = jnp.broadcast_to(q_dec[:, :, None], (H, C, RET_QK_DIM))
    kd = jnp.broadcast_to(k_dec[:, :, None], (H, C, RET_QK_DIM))
    cd = jnp.broadcast_to(chunk_dec[:, None, None], (H, 1, RET_V_DIM))
    return cos, sin, dmask, qd, kd, cd


def _rot(t, cos, sin):
    half = RET_QK_DIM // 2
    t1, t2 = t[:, :half], t[:, half:]
    return jnp.concatenate([t1 * cos - t2 * sin, t1 * sin + t2 * cos], axis=-1)


def _unrot(t, cos, sin):
    half = RET_QK_DIM // 2
    t1, t2 = t[:, :half], t[:, half:]
    return jnp.concatenate([t1 * cos + t2 * sin, t2 * cos - t1 * sin], axis=-1)


def _ret_specs(rev, nC):
    C, DK, DV = RET_CHUNK, RET_QK_DIM, RET_V_DIM
    cidx = (lambda c: nC - 1 - c) if rev else (lambda c: c)
    H = RET_HEADS
    return dict(
        qk=lambda which: pl.BlockSpec((None, C, H * DK), lambda c: (which, cidx(c), 0)),
        q=pl.BlockSpec((C, H * DK), lambda c: (cidx(c), 0)),
        v=pl.BlockSpec((C, H * DV), lambda c: (cidx(c), 0)),
        cs=pl.BlockSpec((C, DK // 2), lambda c: (cidx(c), 0)),
        dmask=pl.BlockSpec((H, C, C), lambda c: (0, 0, 0)),
        dec=pl.BlockSpec((H, C, DK), lambda c: (0, 0, 0)),
        cd=pl.BlockSpec((H, 1, DV), lambda c: (0, 0, 0)),
        st=pl.BlockSpec((H, None, DK, DV), lambda c: (0, cidx(c), 0, 0)),
    )


def _ret_fwd(qk, v, g, tables, *, name):
    _, S, _ = qk.shape
    nC = S // RET_CHUNK
    C, DK, DV, H = RET_CHUNK, RET_QK_DIM, RET_V_DIM, RET_HEADS
    cos, sin, dmask, qd, kd, cd = tables
    kscale = DK ** -0.5

    def body(q_ref, k_ref, v_ref, g_ref, cos_ref, sin_ref, dm_ref, qd_ref, kd_ref, cd_ref,
             o_ref, yb_ref, ybt_ref, st_ref, state):
        @pl.when(pl.program_id(0) == 0)
        def _():
            state[...] = jnp.zeros_like(state)

        tcol = pl.multiple_of((pl.program_id(0) % RET_T_CHUNKS) * C, C)
        cs, sn = cos_ref[...], sin_ref[...]
        for h in range(H):
            qs, vs = slice(DK * h, DK * (h + 1)), slice(DV * h, DV * (h + 1))
            Q = _rot(q_ref[:, qs], cs, sn)
            K = _rot(k_ref[:, qs], cs, sn) * kscale
            Qb, Kb, V = Q.astype(BF16), K.astype(BF16), v_ref[:, vs]
            sb = state[h].astype(BF16)
            st_ref[h] = sb
            A = _dot_nt(Qb, Kb) * dm_ref[h]
            o = _dot(A.astype(BF16), V) + _dot((Q * qd_ref[h]).astype(BF16), sb)
            state[h] = state[h] * cd_ref[h] + _dot_tn((K * kd_ref[h]).astype(BF16), V)
            mu = jnp.mean(o, axis=-1, keepdims=True)
            dd = o - mu
            var = jnp.mean(dd * dd, axis=-1, keepdims=True)
            yn = dd * lax.rsqrt(var + GN_EPS)
            gv = g_ref[:, vs]
            yb = gv * jax.nn.sigmoid(gv) * yn
            o_ref[:, vs] = o
            yb_ref[:, vs] = yb.astype(BF16)
            ybt_ref[vs, pl.ds(tcol, C)] = yb.T.astype(BF16)

    sp = _ret_specs(False, nC)
    return pl.pallas_call(
        body, name=name, grid=(nC,),
        in_specs=[sp["qk"](0), sp["qk"](1), sp["v"], sp["v"], sp["cs"], sp["cs"], sp["dmask"],
                  sp["dec"], sp["dec"], sp["cd"]],
        out_specs=[sp["v"], sp["v"], pl.BlockSpec((H * DV, RET_T_CHUNKS * C), lambda c: (0, c // RET_T_CHUNKS)),
                   sp["st"]],
        out_shape=[jax.ShapeDtypeStruct((S, H * DV), F32), jax.ShapeDtypeStruct((S, H * DV), BF16),
                   jax.ShapeDtypeStruct((H * DV, S), BF16), jax.ShapeDtypeStruct((H, nC, DK, DV), BF16)],
        scratch_shapes=[pltpu.VMEM((H, DK, DV), F32)],
        compiler_params=_cparams(("arbitrary",)))(qk, qk, v, g, cos, sin, dmask, qd, kd, cd)


def _ret_bwd(dyb, qk, v, g, o, states, tables, *, name):
    _, S, _ = qk.shape
    nC = S // RET_CHUNK
    C, DK, DV, H = RET_CHUNK, RET_QK_DIM, RET_V_DIM, RET_HEADS
    cos, sin, dmask, qd, kd, cd = tables
    kscale = DK ** -0.5

    def body(dy_ref, q_ref, k_ref, v_ref, g_ref, o_ref, st_ref, cos_ref, sin_ref, dm_ref, qd_ref, kd_ref,
             cd_ref, dq_ref, dk_ref, dv_ref, dg_ref, dstate):
        @pl.when(pl.program_id(0) == 0)
        def _():
            dstate[...] = jnp.zeros_like(dstate)

        cs, sn = cos_ref[...], sin_ref[...]
        for h in range(H):
            qs, vs = slice(DK * h, DK * (h + 1)), slice(DV * h, DV * (h + 1))
            ov = o_ref[:, vs]
            mu = jnp.mean(ov, axis=-1, keepdims=True)
            dd = ov - mu
            var = jnp.mean(dd * dd, axis=-1, keepdims=True)
            rstd = lax.rsqrt(var + GN_EPS)
            yn = dd * rstd
            gv, dy = g_ref[:, vs], dy_ref[:, vs]
            sg = jax.nn.sigmoid(gv)
            dg_ref[:, vs] = (dy * yn * (sg * (1.0 + gv * (1.0 - sg)))).astype(BF16)
            dyn = dy * (gv * sg)
            dO = rstd * (dyn - jnp.mean(dyn, axis=-1, keepdims=True)
                         - yn * jnp.mean(dyn * yn, axis=-1, keepdims=True))
            dOb = dO.astype(BF16)

            Q = _rot(q_ref[:, qs], cs, sn)
            K = _rot(k_ref[:, qs], cs, sn) * kscale
            Qb, Kb, V = Q.astype(BF16), K.astype(BF16), v_ref[:, vs]
            dm, qd_h, kd_h = dm_ref[h], qd_ref[h], kd_ref[h]
            Sb = st_ref[h]
            dSb = dstate[h].astype(BF16)
            Ab = (_dot_nt(Qb, Kb) * dm).astype(BF16)
            dAb = (_dot_nt(dOb, V) * dm).astype(BF16)
            Qd = (Q * qd_h).astype(BF16)
            Kd = (K * kd_h).astype(BF16)
            dQ = _dot(dAb, Kb) + _dot_nt(dOb, Sb) * qd_h
            dK = _dot_tn(dAb, Qb) + _dot_nt(V, dSb) * kd_h
            dv_ref[:, vs] = (_dot_tn(Ab, dOb) + _dot(Kd, dSb)).astype(BF16)
            dstate[h] = dstate[h] * cd_ref[h] + _dot_tn(Qd, dOb)
            dq_ref[:, qs] = _unrot(dQ, cs, sn).astype(BF16)
            dk_ref[:, qs] = (_unrot(dK, cs, sn) * kscale).astype(BF16)

    sp = _ret_specs(True, nC)
    dq, dk, dv, dg = pl.pallas_call(
        body, name=name, grid=(nC,),
        in_specs=[sp["v"], sp["qk"](0), sp["qk"](1), sp["v"], sp["v"], sp["v"], sp["st"], sp["cs"], sp["cs"],
                  sp["dmask"], sp["dec"], sp["dec"], sp["cd"]],
        out_specs=[sp["q"], sp["q"], sp["v"], sp["v"]],
        out_shape=[jax.ShapeDtypeStruct((S, H * DK), BF16), jax.ShapeDtypeStruct((S, H * DK), BF16),
                   jax.ShapeDtypeStruct((S, H * DV), BF16), jax.ShapeDtypeStruct((S, H * DV), BF16)],
        scratch_shapes=[pltpu.VMEM((H, DK, DV), F32)],
        compiler_params=_cparams(("arbitrary",)))(dyb, qk, qk, v, g, o, states, cos, sin, dmask, qd, kd, cd)
    return dq, dk, dv, dg


def _layer_fwd(l, x, xb, x_t, weights_of, b_in, biases, ln, tables):
    S = x.shape[0]
    tag = f"l{l}"
    W = dict(weights_of(l, "in", x))
    win = W["w_in"]
    c0, c1, c2, c3, c4 = 3 * ATTN_W, 3 * ATTN_W + 2048, 3 * ATTN_W + 4096, 3 * ATTN_W + 6144, IN_COLS
    qkv_a = _mm(xb, win[:, :c0], bias=b_in[:c0], groups=3, lane_chunks=True, name=f"{tag}_in_attn")
    qk_r = _mm(xb, win[:, c0:c1], bias=b_in[c0:c1], groups=2, name=f"{tag}_in_retqk")
    v_r = _mm(xb, win[:, c1:c2], bias=b_in[c1:c2], out_dtype=BF16, name=f"{tag}_in_retv")
    g_r = _mm(xb, win[:, c2:c3], bias=b_in[c2:c3], name=f"{tag}_in_retg")
    gates = _mm(xb, win[:, c3:c4], bias=b_in[c3:c4], groups=2, name=f"{tag}_in_gates")

    os_, ls_ = [], []
    for gi, (_, dil) in enumerate(ATTN_GROUPS):
        o, lse = _attn_fwd(qkv_a, biases[gi], gi, dil, name=f"{tag}_attn_fwd{gi}")
        os_.append(o)
        ls_.append(lse)
    ya_b, ya_t, ya, wts = _combine_fwd(os_, ls_, name=f"{tag}_combine")

    o_r, yb, yb_t, states = _ret_fwd(qk_r, v_r, g_r, tables, name=f"{tag}_ret_fwd")

    W.update(weights_of(l, "rest", yb))
    W["w_gu"] = jnp.concatenate([W["w_ffn_gate"], W["w_ffn_up"]], axis=1)
    pa = _mm(ya_b, W["w_attn_proj"], name=f"{tag}_attn_proj")
    pr, merged, merged_t = _ret_proj_merge(yb, W["w_ret_proj"], gates, pa, name=f"{tag}_ret_proj")
    mix = _mm(merged, W["w_out"], name=f"{tag}_out_proj")
    h1, x1, x1b, x1_t = _ln_fwd(x, mix, ln["ln1_g"], ln["ln1_b"], name=f"{tag}_ln1")
    uv, hh, hh_t = _ffn_in_swiglu(x1b, W["w_ffn_gate"], W["w_ffn_up"], name=f"{tag}_ffn_in")
    f = _mm(hh, W["w_ffn_down"], name=f"{tag}_ffn_down")
    h2, x2, x2b, x2_t = _ln_fwd(x1, f, ln["ln2_g"], ln["ln2_b"], name=f"{tag}_ln2")
    saved = dict(x_t=x_t, qkv_a=qkv_a, qk_r=qk_r, v_r=v_r, g_r=g_r, gates=gates, ls=ls_, ya_t=ya_t, ya=ya,
                 wts=wts, o_r=o_r, yb_t=yb_t, states=states, pa=pa, pr=pr, merged_t=merged_t, h1=h1, x1_t=x1_t,
                 uv=uv, hh_t=hh_t, h2=h2)
    return x2, x2b, x2_t, saved, W


WEIGHT_GROUPS = {"in": ("w_in",), "proj": ("w_attn_proj", "w_ret_proj", "w_out"),
                 "ffn": ("w_ffn_gate", "w_ffn_up", "w_ffn_down")}


def _behind(value, token):
    return value if token is None else value + token[0, 0]


def _layer_bwd(l, dx2, sv, W, biases, ln, tables, token, on_grads):
    S = dx2.shape[0]
    tag = f"l{l}"
    g = {}

    def done(group):
        return None if on_grads is None else on_grads(l, group, {n: g[n] for n in WEIGHT_GROUPS[group]})

    dh2b, res2, g["ln2_g"], g["ln2_b"] = _ln_bwd(dx2, sv["h2"], _behind(ln["ln2_g"], token), name=f"{tag}_ln2_bwd")
    g["w_ffn_down"] = _mm(sv["hh_t"], dh2b, name=f"{tag}_dw_down")
    dudv = _ffn_down_bwd_swiglu(dh2b, W["w_ffn_down"], sv["uv"], name=f"{tag}_d_uv")
    dx1 = _mm(dudv, W["w_gu"], transpose_b=True, a_halves=True, add=res2, name=f"{tag}_d_x1")
    dwgu = _mm(sv["x1_t"], dudv, b_halves=True, name=f"{tag}_dw_gu")
    g["w_ffn_gate"], g["w_ffn_up"] = dwgu[:, :D_FF], dwgu[:, D_FF:]
    token = done("ffn")

    dh1b, res1, g["ln1_g"], g["ln1_b"] = _ln_bwd(dx1, sv["h1"], _behind(ln["ln1_g"], token), name=f"{tag}_ln1_bwd")
    g["w_out"] = _mm(sv["merged_t"], dh1b, name=f"{tag}_dw_out")
    dpa, dpr, dgates = _out_proj_bwd_merge(dh1b, W["w_out"], sv["gates"], sv["pa"], sv["pr"], name=f"{tag}_d_merged")
    dya = _mm(dpa, W["w_attn_proj"], transpose_b=True, groups=1, lane_chunks=True, name=f"{tag}_d_ya")
    g["w_attn_proj"] = _mm(sv["ya_t"], dpa, name=f"{tag}_dw_ap")
    dyb = _mm(dpr, W["w_ret_proj"], transpose_b=True, name=f"{tag}_d_yb")
    g["w_ret_proj"] = _mm(sv["yb_t"], dpr, name=f"{tag}_dw_rp")
    token = done("proj")
    tables = tables[:-1] + (_behind(tables[-1], token),)

    da, dbs = [], []
    for gi, (_, dil) in enumerate(ATTN_GROUPS):
        dqkv, db = _attn_bwd(sv["qkv_a"], biases[gi], sv["ls"][gi], dya, sv["ya"], sv["wts"][gi], gi, dil,
                             name=f"{tag}_attn_bwd{gi}")
        da.append(dqkv)
        dbs.append(db)
    dq_r, dk_r, dv_r, dg_r = _ret_bwd(dyb, sv["qk_r"], sv["v_r"], sv["g_r"], sv["o_r"], sv["states"], tables,
                                 name=f"{tag}_ret_bwd")
    dz, colsum = _assemble_dz(da, dq_r, dk_r, dv_r, dg_r, dgates, name=f"{tag}_assemble_dz")
    g["b_in"] = colsum.reshape(IN_COLS)
    dx = _mm(dz, W["w_in"], transpose_b=True, add=res1, name=f"{tag}_d_x")
    g["w_in"] = _mm(sv["x_t"], dz, name=f"{tag}_dw_in")
    return dx, g, dbs, done("in")


HBM_SPEC = pl.BlockSpec(memory_space=pltpu.HBM)
OTHER_CHIPS = ((1, 0), (0, 1), (1, 1))


def _flip(v, f):
    return 1 - v if f else v


def _all_gather(shards, *, name):
    n = len(shards)

    def body(*refs):
        x_refs, out_refs = refs[:n], refs[n:2 * n]
        send_sems, recv_sems, local_sems = refs[2 * n:]
        x, y, c = lax.axis_index("x"), lax.axis_index("y"), lax.axis_index("c")
        me, sibling = (x, y, c), (x, y, 1 - c)
        chips = [(_flip(x, fx), _flip(y, fy)) for fx, fy in OTHER_CHIPS]

        def copy(a, k, block, to, src=None):
            px, py, pc = block
            rows = out_refs[a].at[4 * px + 2 * py + pc]
            return pltpu.make_async_remote_copy(
                src_ref=rows if src is None else src, dst_ref=rows,
                send_sem=send_sems.at[7 * a + k], recv_sem=recv_sems.at[7 * a + k], device_id=to, device_id_type=MESH)

        mine, first, passed = [], [], []
        for a in range(n):
            cp = pltpu.make_async_copy(x_refs[a], out_refs[a].at[4 * x + 2 * y + c], local_sems.at[a])
            cp.start()
            mine.append(cp)
            first.append(copy(a, 0, me, sibling, src=x_refs[a]))
            first += [copy(a, 1 + j, me, (*chip, c), src=x_refs[a]) for j, chip in enumerate(chips)]
        for cp in first:
            cp.start()
        for j, chip in enumerate(chips):
            for a in range(n):
                copy(a, 1 + j, (*chip, c), me).wait_recv()
                cp = copy(a, 4 + j, (*chip, c), sibling)
                cp.start()
                passed.append(cp)
        for a in range(n):
            copy(a, 0, sibling, me).wait_recv()
            for j, chip in enumerate(chips):
                copy(a, 4 + j, (*chip, 1 - c), me).wait_recv()
        for cp in first + passed:
            cp.wait_send()
        for cp in mine:
            cp.wait()

    return pl.pallas_call(
        body, name=name, out_shape=[jax.ShapeDtypeStruct((N_DEV,) + s.shape, s.dtype) for s in shards],
        in_specs=[HBM_SPEC] * n, out_specs=[HBM_SPEC] * n,
        scratch_shapes=[pltpu.SemaphoreType.DMA((7 * n,)), pltpu.SemaphoreType.DMA((7 * n,)),
                        pltpu.SemaphoreType.DMA((n,))],
    )(*shards)


def _rs_sibling_exchange(g8s, *, name):
    n = len(g8s)

    def body(*refs):
        g_refs, recv_refs = refs[:n], refs[n:2 * n]
        send_sems, recv_sems = refs[2 * n:]
        x, y, c = lax.axis_index("x"), lax.axis_index("y"), lax.axis_index("c")
        copies = []
        for a in range(n):
            for k in range(4):
                cp = pltpu.make_async_remote_copy(
                    src_ref=g_refs[a].at[k, 1 - c], dst_ref=recv_refs[a].at[k], send_sem=send_sems.at[4 * a + k],
                    recv_sem=recv_sems.at[4 * a + k], device_id=(x, y, 1 - c), device_id_type=MESH)
                cp.start()
                copies.append(cp)
        for cp in copies:
            cp.wait()

    return pl.pallas_call(
        body, name=name,
        out_shape=[jax.ShapeDtypeStruct((4,) + g.shape[2:], g.dtype) for g in g8s],
        in_specs=[HBM_SPEC] * n, out_specs=[HBM_SPEC] * n,
        scratch_shapes=[pltpu.SemaphoreType.DMA((4 * n,)), pltpu.SemaphoreType.DMA((4 * n,))],
    )(*g8s)


def _rs_chip_sum(g8, recv, core, *, name):
    _, _, R, Wd = g8.shape
    tr = _div_tile(R, 256, 16)

    def body(core_ref, g_ref, r_ref, o_ref):
        o_ref[...] = (g_ref[...] + r_ref[...]).astype(BF16)

    grid_spec = pltpu.PrefetchScalarGridSpec(
        num_scalar_prefetch=1, grid=(4, R // tr),
        in_specs=[pl.BlockSpec((None, None, tr, Wd), lambda k, i, core_ref: (k, core_ref[0], i, 0)),
                  pl.BlockSpec((None, tr, Wd), lambda k, i, core_ref: (k, i, 0))],
        out_specs=pl.BlockSpec((None, tr, Wd), lambda k, i, core_ref: (k, i, 0)))
    return pl.pallas_call(
        body, name=name, grid_spec=grid_spec, out_shape=jax.ShapeDtypeStruct((4, R, Wd), BF16),
        compiler_params=_cparams(("parallel", "parallel")))(core, g8, recv)


def _rs_chip_exchange(ps, *, name):
    n = len(ps)

    def body(*refs):
        p_refs, out_refs = refs[:n], refs[n:2 * n]
        send_sems, recv_sems, local_sems = refs[2 * n:]
        x, y, c = lax.axis_index("x"), lax.axis_index("y"), lax.axis_index("c")
        my_chip = 2 * x + y
        copies = []
        for a in range(n):
            mine = pltpu.make_async_copy(p_refs[a].at[my_chip], out_refs[a].at[my_chip], local_sems.at[a])
            mine.start()
            copies.append(mine)
            for j, (fx, fy) in enumerate(OTHER_CHIPS):
                px, py = _flip(x, fx), _flip(y, fy)
                cp = pltpu.make_async_remote_copy(
                    src_ref=p_refs[a].at[2 * px + py], dst_ref=out_refs[a].at[my_chip],
                    send_sem=send_sems.at[3 * a + j], recv_sem=recv_sems.at[3 * a + j],
                    device_id=(px, py, c), device_id_type=MESH)
                cp.start()
                copies.append(cp)
        for cp in copies:
            cp.wait()

    return pl.pallas_call(
        body, name=name, out_shape=[jax.ShapeDtypeStruct(p.shape, p.dtype) for p in ps],
        in_specs=[HBM_SPEC] * n, out_specs=[HBM_SPEC] * n,
        scratch_shapes=[pltpu.SemaphoreType.DMA((3 * n,)), pltpu.SemaphoreType.DMA((3 * n,)),
                        pltpu.SemaphoreType.DMA((n,))],
    )(*ps)


SEM_SPEC = pl.BlockSpec(memory_space=pltpu.SEMAPHORE)
DATAFLOW = pltpu.SideEffectType.DATAFLOW_SIDE_EFFECTING


def _direct_copies(src_refs, land_refs, send_sems, recv_sems, per_peer):
    x, y, c = lax.axis_index("x"), lax.axis_index("y"), lax.axis_index("c")
    me = 4 * x + 2 * y + c
    copies = []
    for a, (s, l) in enumerate(zip(src_refs, land_refs)):
        for rel in range(1, N_DEV):
            px, py, pc = _flip(x, rel & 4), _flip(y, rel & 2), _flip(c, rel & 1)
            copies.append(pltpu.make_async_remote_copy(
                src_ref=s.at[4 * px + 2 * py + pc] if per_peer else s, dst_ref=l.at[me],
                send_sem=send_sems.at[7 * a + rel - 1], recv_sem=recv_sems.at[7 * a + rel - 1],
                device_id=(px, py, pc), device_id_type=MESH))
    return copies


def _exchange_start(srcs, per_peer, *, name):
    n = len(srcs)
    lands = [lax.empty((N_DEV,) + (s.shape[1:] if per_peer else s.shape), s.dtype) for s in srcs]
    operands = [pltpu.with_memory_space_constraint(t, pltpu.HBM) for t in list(srcs) + lands]

    def body(*refs):
        src_refs, land_refs = refs[:n], refs[n:2 * n]
        send_sems, recv_sems = refs[2 * n], refs[2 * n + 1]
        token = refs[-1]
        for cp in _direct_copies(src_refs, land_refs, send_sems, recv_sems, per_peer):
            cp.start()
        token[...] = jnp.zeros_like(token)

    return pl.pallas_call(
        body, name=name,
        out_shape=(pltpu.SemaphoreType.DMA((7 * n,)), pltpu.SemaphoreType.DMA((7 * n,)),
                   *[pltpu.HBM(t.shape, t.dtype) for t in operands], jax.ShapeDtypeStruct((8, LANES), F32)),
        in_specs=[HBM_SPEC] * (2 * n),
        out_specs=(SEM_SPEC, SEM_SPEC, *[HBM_SPEC] * (2 * n), pl.BlockSpec(memory_space=pltpu.VMEM)),
        input_output_aliases={i: 2 + i for i in range(2 * n)},
        compiler_params=pltpu.CompilerParams(has_side_effects=DATAFLOW))(*operands)


def _exchange_wait(started, after, per_peer, *, name):
    n = (len(started) - 3) // 2
    send_sems, recv_sems = started[0], started[1]
    thru = list(started[2:2 + 2 * n])

    def body(*refs):
        src_refs, land_refs = refs[:n], refs[n:2 * n]
        send_s, recv_s = refs[2 * n], refs[2 * n + 1]
        for cp in _direct_copies(src_refs, land_refs, send_s, recv_s, per_peer):
            cp.wait_send()
            cp.wait_recv()

    outs = pl.pallas_call(
        body, name=name, out_shape=tuple(pltpu.HBM(t.shape, t.dtype) for t in thru),
        in_specs=[HBM_SPEC] * (2 * n) + [SEM_SPEC, SEM_SPEC, pl.BlockSpec(memory_space=pl.ANY)],
        out_specs=[HBM_SPEC] * (2 * n), input_output_aliases={i: i for i in range(2 * n)},
        compiler_params=pltpu.CompilerParams(has_side_effects=DATAFLOW))(*thru, send_sems, recv_sems, after)
    return list(outs[n:])


def _all_reduce_small(v, *, name):
    R, Wd = v.shape

    def body(v_ref, out_ref, slots, send_sems, recv_sems):
        x, y, c = lax.axis_index("x"), lax.axis_index("y"), lax.axis_index("c")
        me = 4 * x + 2 * y + c
        slots[me] = v_ref[...]
        copies = []
        for rel in range(1, N_DEV):
            peer = (_flip(x, rel & 4), _flip(y, rel & 2), _flip(c, rel & 1))
            cp = pltpu.make_async_remote_copy(
                src_ref=v_ref, dst_ref=slots.at[me], send_sem=send_sems.at[rel - 1],
                recv_sem=recv_sems.at[rel - 1], device_id=peer, device_id_type=MESH)
            cp.start()
            copies.append(cp)
        for cp in copies:
            cp.wait()
        acc = slots[0]
        for j in range(1, N_DEV):
            acc = acc + slots[j]
        out_ref[...] = acc

    vm = pl.BlockSpec(memory_space=pltpu.VMEM)
    return pl.pallas_call(
        body, name=name, out_shape=jax.ShapeDtypeStruct((R, Wd), F32),
        in_specs=[vm], out_specs=vm,
        scratch_shapes=[pltpu.VMEM((N_DEV, R, Wd), F32), pltpu.SemaphoreType.DMA((7,)),
                        pltpu.SemaphoreType.DMA((7,))],
    )(v)


def _adam_math(w, g, m, v):
    m2 = ADAM_B1 * m + (1.0 - ADAM_B1) * g
    v2 = ADAM_B2 * v + (1.0 - ADAM_B2) * (g * g)
    m_hat = m2 / (1.0 - ADAM_B1 ** ADAM_STEP)
    v_hat = v2 / (1.0 - ADAM_B2 ** ADAM_STEP)
    delta = -ADAM_LR * (m_hat / (jnp.sqrt(v_hat) + ADAM_EPS) + ADAM_WD * w)
    return delta, m2, v2


def _adam_sharded(parts, w, m, v, *, name):
    _, R, Wd = w.shape
    tr = _div_tile(R, 256, 16)

    def body(p0_ref, p1_ref, w_ref, m_ref, v_ref, g_ref, d_ref, m2_ref, v2_ref):
        def slot_sum(p_ref):
            g = p_ref[0].astype(F32)
            for s in range(1, p_ref.shape[0]):
                g = g + p_ref[s].astype(F32)
            return g

        g = jnp.where(pl.program_id(0) == 0, slot_sum(p0_ref), slot_sum(p1_ref))
        delta, m2, v2 = _adam_math(w_ref[...], g, m_ref[...], v_ref[...])
        g_ref[...] = g
        d_ref[...] = delta
        m2_ref[...] = m2
        v2_ref[...] = v2

    assert DEPTH == 2
    p_specs = [pl.BlockSpec((p.shape[0], tr, Wd), lambda l, i: (0, i, 0)) for p in parts]
    s_spec = pl.BlockSpec((None, tr, Wd), lambda l, i: (l, i, 0))
    return pl.pallas_call(
        body, name=name, grid=(DEPTH, R // tr),
        in_specs=p_specs + [s_spec, s_spec, s_spec],
        out_specs=[s_spec] * 4, out_shape=[jax.ShapeDtypeStruct((DEPTH, R, Wd), F32)] * 4,
        compiler_params=_cparams(("parallel", "parallel")))(parts[0], parts[1], w, m, v)


def _adam_small(g, w, m, v, *, name):
    R, Wd = w.shape

    def body(g_ref, w_ref, m_ref, v_ref, d_ref, m2_ref, v2_ref):
        delta, m2, v2 = _adam_math(w_ref[...], g_ref[...], m_ref[...], v_ref[...])
        d_ref[...] = delta
        m2_ref[...] = m2
        v2_ref[...] = v2

    return pl.pallas_call(
        body, name=name, out_shape=[jax.ShapeDtypeStruct((R, Wd), F32)] * 3,
        compiler_params=_cparams())(g, w, m, v)


def _shard_shape(name):
    r, c = FULL_SHAPE[name]
    return (r, c // N_DEV) if name in COL_SHARDED else (r // N_DEV, c)


def _full_from_gathered(name, g):
    if name in COL_SHARDED:
        return jnp.transpose(g, (1, 0, 2)).reshape(FULL_SHAPE[name])
    return g.reshape(FULL_SHAPE[name])


def _dest_major(name, gfull):
    r, c = _shard_shape(name)
    if name in COL_SHARDED:
        blk = jnp.transpose(gfull.reshape(r, N_DEV, c), (1, 0, 2))
    else:
        blk = gfull.reshape(N_DEV, r, c)
    return blk.reshape(4, 2, r, c)


def _pack_small(t):
    flat = jnp.concatenate([t[n].reshape(-1).astype(F32) for n in SMALL_WEIGHTS])
    return jnp.pad(flat, (0, SMALL_ROWS * LANES - flat.shape[0])).reshape(SMALL_ROWS, LANES)


def _unpack_small(packed):
    flat = packed.reshape(-1)
    out, off = {}, 0
    for n in SMALL_WEIGHTS:
        size = math.prod(SMALL_SHAPE[n])
        out[n] = flat[off:off + size].reshape(SMALL_SHAPE[n])
        off += size
    return out


def _after(value, token):
    return lax.optimization_barrier((value, token))[0]


def _local_step(x, target, rel_bias, b_in, lns, weights_of, on_grads=None):
    S = x.shape[0]
    tables = _ret_tables(S)
    biases = [_attn_bias(rel_bias, gi, dil) for gi, (_, dil) in enumerate(ATTN_GROUPS)]

    h = x
    hb, h_t = _cast_transpose(x, name="cast_x")
    saved, Ws = [], []
    for l in range(DEPTH):
        h, hb, h_t, sv, W = _layer_fwd(l, h, hb, h_t, weights_of, b_in[l], biases, lns[l], tables)
        saved.append(sv)
        Ws.append(W)
    dy, sq = _loss_fwd_bwd(h, target, name="loss")
    loss_local = 0.5 * sq[0, 0] / D_MODEL

    grads = [None] * DEPTH
    db_tot = None
    dx = dy
    token = None
    for l in reversed(range(DEPTH)):
        dx, g, dbs, token = _layer_bwd(l, dx, saved[l], Ws[l], biases, lns[l], tables, token, on_grads)
        grads[l] = g
        db_tot = dbs if db_tot is None else [a + b for a, b in zip(db_tot, dbs)]
    small = {"rel_bias": _bias_grad(db_tot, name="bias_grad"),
             "b_in": jnp.stack([grads[l]["b_in"] for l in range(DEPTH)])}
    for n in ("ln1_g", "ln1_b", "ln2_g", "ln2_b"):
        small[n] = jnp.stack([grads[l][n].reshape(D_MODEL) for l in range(DEPTH)])
    return loss_local, dx, grads, small


def kernel(x, rel_bias, w_in, b_in, w_attn_proj, w_ret_proj, w_out, ln1_g, ln1_b, w_ffn_gate, w_ffn_up, w_ffn_down, ln2_g, ln2_b, loss_target, m_rel_bias, m_w_in, m_b_in, m_w_attn_proj, m_w_ret_proj, m_w_out, m_ln1_g, m_ln1_b, m_w_ffn_gate, m_w_ffn_up, m_w_ffn_down, m_ln2_g, m_ln2_b, v_rel_bias, v_w_in, v_b_in, v_w_attn_proj, v_w_ret_proj, v_w_out, v_ln1_g, v_ln1_b, v_w_ffn_gate, v_w_ffn_up, v_w_ffn_down, v_ln2_g, v_ln2_b):
    w = dict(rel_bias=rel_bias, w_in=w_in, b_in=b_in, w_attn_proj=w_attn_proj, w_ret_proj=w_ret_proj, w_out=w_out,
             ln1_g=ln1_g, ln1_b=ln1_b, w_ffn_gate=w_ffn_gate, w_ffn_up=w_ffn_up, w_ffn_down=w_ffn_down,
             ln2_g=ln2_g, ln2_b=ln2_b)
    m = dict(rel_bias=m_rel_bias, w_in=m_w_in, b_in=m_b_in, w_attn_proj=m_w_attn_proj, w_ret_proj=m_w_ret_proj,
             w_out=m_w_out, ln1_g=m_ln1_g, ln1_b=m_ln1_b, w_ffn_gate=m_w_ffn_gate, w_ffn_up=m_w_ffn_up,
             w_ffn_down=m_w_ffn_down, ln2_g=m_ln2_g, ln2_b=m_ln2_b)
    v = dict(rel_bias=v_rel_bias, w_in=v_w_in, b_in=v_b_in, w_attn_proj=v_w_attn_proj, w_ret_proj=v_w_ret_proj,
             w_out=v_w_out, ln1_g=v_ln1_g, ln1_b=v_ln1_b, w_ffn_gate=v_w_ffn_gate, w_ffn_up=v_w_ffn_up,
             w_ffn_down=v_w_ffn_down, ln2_g=v_ln2_g, ln2_b=v_ln2_b)

    assert DEPTH == 2
    me = 4 * lax.axis_index("x") + 2 * lax.axis_index("y") + lax.axis_index("c")
    core = lax.axis_index("c").astype(jnp.int32).reshape(1)

    def own_slot(lands, blocks):
        return [lax.dynamic_update_index_in_dim(land, blk, me, 0) for land, blk in zip(lands, blocks)]

    shard = {(l, n): w[n][l].astype(BF16) for l in range(DEPTH) for n in BIG_WEIGHTS}
    rest = WEIGHT_GROUPS["proj"] + WEIGHT_GROUPS["ffn"]
    (w_in0,) = _all_gather([shard[0, "w_in"]], name="all_gather_l0_in")
    gathers = {0: (rest, _exchange_start(_after([shard[0, n] for n in rest], w_in0), False,
                                         name="all_gather_l0_rest_start"))}
    first_token = gathers[0][1][-1][0, 0].astype(BF16)
    gathers[1] = (BIG_WEIGHTS, _exchange_start([shard[1, n] + first_token for n in BIG_WEIGHTS], False,
                                               name="all_gather_l1_start"))
    b_in_fwd = [_behind(b_in[0], gathers[1][1][-1]), b_in[1]]
    arrived = {}

    def weights_of(l, group, after):
        if (l, group) == (0, "in"):
            return {"w_in": _full_from_gathered("w_in", w_in0)}
        if l not in arrived:
            names, started = gathers[l]
            lands = _exchange_wait(started, after, False, name=f"all_gather_l{l}_wait")
            full = own_slot(lands, [shard[l, n] for n in names])
            arrived[l] = {n: _full_from_gathered(n, g) for n, g in zip(names, full)}
        names = WEIGHT_GROUPS["in"] if group == "in" else rest
        return {n: arrived[l][n] for n in names}

    scatters = {}

    def on_grads(l, group, gd):
        if (l, group) == (0, "in"):
            return None
        names = WEIGHT_GROUPS[group]
        blocks = [_dest_major(n, gd[n]).reshape((N_DEV,) + _shard_shape(n)).astype(BF16) for n in names]
        scatters[l, group] = (names, blocks, _exchange_start(blocks, True, name=f"rs_l{l}_{group}_start"))
        return scatters[l, group][2][-1]

    lns = [{n: w[n][l] for n in ("ln1_g", "ln1_b", "ln2_g", "ln2_b")} for l in range(DEPTH)]
    loss_local, grad_x, grads, small = _local_step(x[0], loss_target[0], rel_bias, b_in_fwd, lns, weights_of,
                                                   on_grads)
    loss = lax.psum(loss_local, ("x", "y", "c"))

    g8 = [_dest_major("w_in", grads[0]["w_in"])]
    from_sibling = _rs_sibling_exchange(g8, name="rs_sibling_exchange_l0_in")
    chip_parts = [_rs_chip_sum(g8[0], from_sibling[0], core, name="rs_chip_sum_l0_in")]
    parts = {(0, "w_in"): _rs_chip_exchange(chip_parts, name="rs_chip_exchange_l0_in")[0]}
    for (l, group), (names, blocks, started) in scatters.items():
        lands = _exchange_wait(started, parts[0, "w_in"], True, name=f"rs_l{l}_{group}_wait")
        own = [lax.dynamic_index_in_dim(b, me, 0, keepdims=False) for b in blocks]
        for n, p in zip(names, own_slot(lands, own)):
            parts[l, n] = p
    big = [{} for _ in range(4)]
    for n in BIG_WEIGHTS:
        res = _adam_sharded([parts[l, n] for l in range(DEPTH)], w[n], m[n], v[n], name=f"adam_{n}")
        for kind in range(4):
            big[kind][n] = res[kind]

    gs = _all_reduce_small(_pack_small(small), name="all_reduce_small")
    ds, ms, vs = _adam_small(gs, _pack_small(w), _pack_small(m), _pack_small(v), name="adam_small")
    sm = [_unpack_small(t) for t in (gs, ds, ms, vs)]

    outs = [loss, grad_x[None]]
    for kind in range(4):
        for n in ALL_WEIGHTS:
            outs.append(big[kind][n] if n in BIG_WEIGHTS else sm[kind][n])
    return tuple(outs)
```

```python
import math

import numpy as np
import jax
import jax.numpy as jnp
from jax import lax
from jax.experimental import pallas as pl
from jax.experimental.pallas import tpu as pltpu

F32 = jnp.float32
BF16 = jnp.bfloat16
MESH = pl.DeviceIdType.MESH

D_MODEL = 1024
DEPTH = 2
HEAD_DIM = 64
ATTN_GROUPS = ((128, 1), (512, 4), (2048, 16))
HEADS_PER_GROUP = 6
GROUP_WIDTH = HEADS_PER_GROUP * HEAD_DIM
ATTN_BLOCK = 128
NUM_BUCKETS = 32
MAX_DISTANCE = 2048
RET_HEADS = 4
RET_QK_DIM = 256
RET_V_DIM = 512
RET_CHUNK = 128
RET_T_CHUNKS = 4
ROPE_BASE = 10000.0
D_FF = 2816
ALPHA = (2 * DEPTH) ** 0.25
LN_EPS = 1e-5
GN_EPS = 1e-5
ATTN_W = 3 * GROUP_WIDTH
IN_COLS = 3 * ATTN_W + 2 * 1024 + 2 * 2048 + 2 * 1024
ADAM_LR, ADAM_B1, ADAM_B2, ADAM_EPS, ADAM_WD, ADAM_STEP = 0.001, 0.9, 0.999, 1e-08, 0.01, 10
N_DEV = 8
NEG = -1e30
LANES = 128
VMEM_LIMIT = 56 * 1024 * 1024
MM_TILE_CAP = 1664
MM_VMEM_BUDGET = 44 * 1024 * 1024

BIG_WEIGHTS = ("w_in", "w_attn_proj", "w_ret_proj", "w_out", "w_ffn_gate", "w_ffn_up", "w_ffn_down")
COL_SHARDED = ("w_in", "w_attn_proj", "w_ffn_gate", "w_ffn_up")
FULL_SHAPE = {"w_in": (D_MODEL, IN_COLS), "w_attn_proj": (GROUP_WIDTH, D_MODEL), "w_ret_proj": (2048, D_MODEL),
              "w_out": (D_MODEL, D_MODEL), "w_ffn_gate": (D_MODEL, D_FF), "w_ffn_up": (D_MODEL, D_FF),
              "w_ffn_down": (D_FF, D_MODEL)}
SMALL_WEIGHTS = ("rel_bias", "b_in", "ln1_g", "ln1_b", "ln2_g", "ln2_b")
SMALL_SHAPE = {"rel_bias": (NUM_BUCKETS, 18), "b_in": (DEPTH, IN_COLS), "ln1_g": (DEPTH, D_MODEL),
               "ln1_b": (DEPTH, D_MODEL), "ln2_g": (DEPTH, D_MODEL), "ln2_b": (DEPTH, D_MODEL)}
SMALL_ROWS = 256
ALL_WEIGHTS = ("rel_bias", "w_in", "b_in", "w_attn_proj", "w_ret_proj", "w_out", "ln1_g", "ln1_b",
               "w_ffn_gate", "w_ffn_up", "w_ffn_down", "ln2_g", "ln2_b")


def _cparams(sem=None):
    return pltpu.CompilerParams(dimension_semantics=sem, vmem_limit_bytes=VMEM_LIMIT)


def _div_tile(n, cap, unit):
    if n <= cap:
        return n
    best = None
    for t in range(unit, cap + 1, unit):
        if n % t == 0:
            best = t
    assert best is not None, (n, cap, unit)
    return best


def _mm(a, b, *, name, out_dtype=F32, bias=None, add=None, groups=None, lane_chunks=False, transpose_b=False,
        a_halves=False, b_halves=False):
    M, K = (a.shape[1], 2 * a.shape[2]) if a_halves else a.shape
    if b_halves:
        assert not transpose_b
        K2, N = b.shape[1], 2 * b.shape[2]
    else:
        N, K2 = b.shape if transpose_b else b.shape[::-1]
    assert K == K2 and a.dtype == BF16 and b.dtype == BF16
    has_bias, has_add = bias is not None, add is not None
    tm = _div_tile(M, 1024, 16)
    tn = N // groups if groups else _div_tile(N // 2 if b_halves else N, MM_TILE_CAP, LANES)
    out_bytes = jnp.dtype(out_dtype).itemsize
    k_span = K // 2 if a_halves else K

    def vmem_bytes(tk):
        return 2 * (2 * tm * tk + 2 * tk * tn + out_bytes * tm * tn + (4 * tm * tn if has_add else 0))

    tk = max(t for t in range(LANES, k_span + 1, LANES)
             if k_span % t == 0 and (t == LANES or vmem_bytes(t) <= MM_VMEM_BUDGET))
    nk = K // tk
    nch = tn // LANES
    assert nk == 1 or (out_dtype == F32 and not lane_chunks and not has_bias)

    def body(*refs):
        a_ref, b_ref = refs[0], refs[1]
        pos = 2
        bias_ref = add_ref = None
        if has_bias:
            bias_ref = refs[pos]
            pos += 1
        if has_add:
            add_ref = refs[pos]
            pos += 1
        o_ref = refs[pos]

        def finish(r):
            if has_bias:
                r = r + bias_ref[...]
            if has_add:
                r = r + add_ref[...]
            if lane_chunks:
                for c in range(nch):
                    o_ref[c] = r[:, c * LANES:(c + 1) * LANES].astype(o_ref.dtype)
            else:
                o_ref[...] = r.astype(o_ref.dtype)

        def product():
            if transpose_b:
                return lax.dot_general(a_ref[...], b_ref[...], (((1,), (1,)), ((), ())), preferred_element_type=F32)
            return jnp.dot(a_ref[...], b_ref[...], preferred_element_type=F32)

        if nk == 1:
            finish(product())
        else:
            @pl.when(pl.program_id(2) == 0)
            def _():
                o_ref[...] = add_ref[...] if has_add else jnp.zeros_like(o_ref)

            o_ref[...] += product()

    kh, nh = k_span // tk, (N // 2) // tn
    if a_halves:
        a_spec = pl.BlockSpec((None, tm, tk), lambda i, j, k: (k // kh, i, k % kh))
    else:
        a_spec = pl.BlockSpec((tm, tk), lambda i, j, k: (i, k))
    if b_halves:
        b_spec = pl.BlockSpec((None, tk, tn), lambda i, j, k: (j // nh, k, j % nh))
    elif transpose_b:
        b_spec = pl.BlockSpec((tn, tk), lambda i, j, k: (j, k))
    else:
        b_spec = pl.BlockSpec((tk, tn), lambda i, j, k: (k, j))
    in_specs = [a_spec, b_spec]
    args = [a, b]
    if has_bias:
        in_specs.append(pl.BlockSpec((1, tn), lambda i, j, k: (0, j)))
        args.append(bias.reshape(1, N).astype(F32))
    if has_add:
        in_specs.append(pl.BlockSpec((tm, tn), lambda i, j, k: (i, j)))
        args.append(add)
    if lane_chunks:
        assert groups
        out_shape = jax.ShapeDtypeStruct((groups, nch, M, LANES), out_dtype)
        out_spec = pl.BlockSpec((None, nch, tm, LANES), lambda i, j, k: (j, 0, i, 0))
    elif groups:
        out_shape = jax.ShapeDtypeStruct((groups, M, tn), out_dtype)
        out_spec = pl.BlockSpec((None, tm, tn), lambda i, j, k: (j, i, 0))
    else:
        out_shape = jax.ShapeDtypeStruct((M, N), out_dtype)
        out_spec = pl.BlockSpec((tm, tn), lambda i, j, k: (i, j))
    out = pl.pallas_call(
        body, name=name, grid=(M // tm, N // tn, nk), in_specs=in_specs, out_specs=out_spec,
        out_shape=out_shape,
        compiler_params=_cparams(("parallel", "parallel", "arbitrary")))(*args)
    return out.reshape(groups * nch, M, LANES) if lane_chunks else out


def _row_spec(tr, w):
    return pl.BlockSpec((tr, w), lambda i: (i, 0))


def _vec_spec(w):
    return pl.BlockSpec((1, w), lambda i: (0, 0))


def _col_spec(w, tr):
    return pl.BlockSpec((w, tr), lambda i: (0, i))


def _cast_transpose(x, *, name):
    S, W = x.shape
    tr = 512

    def body(x_ref, o_ref, ot_ref):
        v = x_ref[...]
        o_ref[...] = v.astype(BF16)
        ot_ref[...] = v.T.astype(BF16)

    return pl.pallas_call(
        body, name=name, grid=(S // tr,), in_specs=[_row_spec(tr, W)],
        out_specs=[_row_spec(tr, W), _col_spec(W, tr)],
        out_shape=[jax.ShapeDtypeStruct((S, W), BF16), jax.ShapeDtypeStruct((W, S), BF16)],
        compiler_params=_cparams(("parallel",)))(x)


def _proj_ln(a, w, x, g, b, *, name):
    S, K = a.shape
    W = w.shape[1]
    tr = 512

    def body(a_ref, w_ref, x_ref, g_ref, b_ref, h_ref, y_ref, yb_ref, ybt_ref):
        h = ALPHA * x_ref[...] + _dot(a_ref[...], w_ref[...])
        mu = jnp.mean(h, axis=-1, keepdims=True)
        d = h - mu
        var = jnp.mean(d * d, axis=-1, keepdims=True)
        y = d * lax.rsqrt(var + LN_EPS) * g_ref[...] + b_ref[...]
        h_ref[...] = h
        y_ref[...] = y
        yb_ref[...] = y.astype(BF16)
        ybt_ref[...] = y.T.astype(BF16)

    return pl.pallas_call(
        body, name=name, grid=(S // tr,),
        in_specs=[_row_spec(tr, K), pl.BlockSpec((K, W), lambda i: (0, 0)), _row_spec(tr, W), _vec_spec(W), _vec_spec(W)],
        out_specs=[_row_spec(tr, W)] * 3 + [_col_spec(W, tr)],
        out_shape=[jax.ShapeDtypeStruct((S, W), F32), jax.ShapeDtypeStruct((S, W), F32),
                   jax.ShapeDtypeStruct((S, W), BF16), jax.ShapeDtypeStruct((W, S), BF16)],
        compiler_params=_cparams(("parallel",)))(a, w, x, g.reshape(1, W), b.reshape(1, W))


def _ln_bwd(dy, h, g, *, name):
    S, W = dy.shape
    tr = 512

    def body(dy_ref, h_ref, g_ref, dhb_ref, res_ref, dg_ref, db_ref):
        @pl.when(pl.program_id(0) == 0)
        def _():
            dg_ref[...] = jnp.zeros_like(dg_ref)
            db_ref[...] = jnp.zeros_like(db_ref)

        hh = h_ref[...]
        mu = jnp.mean(hh, axis=-1, keepdims=True)
        d = hh - mu
        var = jnp.mean(d * d, axis=-1, keepdims=True)
        rstd = lax.rsqrt(var + LN_EPS)
        xhat = d * rstd
        dyv = dy_ref[...]
        dg_ref[...] += jnp.sum(dyv * xhat, axis=0, keepdims=True)
        db_ref[...] += jnp.sum(dyv, axis=0, keepdims=True)
        dxh = dyv * g_ref[...]
        dh = rstd * (dxh - jnp.mean(dxh, axis=-1, keepdims=True)
                     - xhat * jnp.mean(dxh * xhat, axis=-1, keepdims=True))
        dhb_ref[...] = dh.astype(BF16)
        res_ref[...] = ALPHA * dh

    return pl.pallas_call(
        body, name=name, grid=(S // tr,),
        in_specs=[_row_spec(tr, W), _row_spec(tr, W), _vec_spec(W)],
        out_specs=[_row_spec(tr, W), _row_spec(tr, W), _vec_spec(W), _vec_spec(W)],
        out_shape=[jax.ShapeDtypeStruct((S, W), BF16), jax.ShapeDtypeStruct((S, W), F32),
                   jax.ShapeDtypeStruct((1, W), F32), jax.ShapeDtypeStruct((1, W), F32)],
        compiler_params=_cparams(("arbitrary",)))(dy, h, g.reshape(1, W))


def _loss_fwd_bwd(y, target, *, name):
    S, W = y.shape
    tr = 512

    def body(y_ref, t_ref, dy_ref, acc_ref):
        @pl.when(pl.program_id(0) == 0)
        def _():
            acc_ref[...] = jnp.zeros_like(acc_ref)

        e = y_ref[...] - t_ref[...]
        acc_ref[...] += jnp.sum(jnp.sum(e * e, axis=-1, keepdims=True), axis=0, keepdims=True)
        dy_ref[...] = e * (1.0 / W)

    return pl.pallas_call(
        body, name=name, grid=(S // tr,),
        in_specs=[_row_spec(tr, W), _row_spec(tr, W)],
        out_specs=[_row_spec(tr, W), pl.BlockSpec((1, 1), lambda i: (0, 0))],
        out_shape=[jax.ShapeDtypeStruct((S, W), F32), jax.ShapeDtypeStruct((1, 1), F32)],
        compiler_params=_cparams(("arbitrary",)))(y, target)


def _combine_fwd(os_, ls_, *, name):
    NCH, S, _ = os_[0].shape
    W = NCH * LANES
    tr = 512

    def body(o0, o1, o2, l0, l1, l2, yb_ref, ybt_ref, y_ref, w0_ref, w1_ref, w2_ref):
        for c in range(NCH):
            la, lb, lc = l0[c], l1[c], l2[c]
            m = jnp.maximum(jnp.maximum(la, lb), lc)
            ea, eb, ec = jnp.exp(la - m), jnp.exp(lb - m), jnp.exp(lc - m)
            inv = 1.0 / (ea + eb + ec)
            wa, wb, wc = ea * inv, eb * inv, ec * inv
            y = wa * o0[c] + wb * o1[c] + wc * o2[c]
            y_ref[c] = y
            yb_ref[:, c * LANES:(c + 1) * LANES] = y.astype(BF16)
            ybt_ref[c * LANES:(c + 1) * LANES, :] = y.T.astype(BF16)
            w0_ref[c] = wa
            w1_ref[c] = wb
            w2_ref[c] = wc

    ch = pl.BlockSpec((NCH, tr, LANES), lambda i: (0, i, 0))
    yb, ybt, y, w0, w1, w2 = pl.pallas_call(
        body, name=name, grid=(S // tr,),
        in_specs=[ch] * 6,
        out_specs=[_row_spec(tr, W), _col_spec(W, tr)] + [ch] * 4,
        out_shape=[jax.ShapeDtypeStruct((S, W), BF16), jax.ShapeDtypeStruct((W, S), BF16)]
        + [jax.ShapeDtypeStruct((NCH, S, LANES), F32)] * 4,
        compiler_params=_cparams(("parallel",)))(*os_, *ls_)
    return yb, ybt, y, (w0, w1, w2)


def _ret_proj_merge(yb, wrp, gates, pa, *, name):
    S, K = yb.shape
    W = wrp.shape[1]
    tr = 512

    def body(y_ref, w_ref, g_ref, pa_ref, pr_ref, o_ref, ot_ref):
        pr = _dot(y_ref[...], w_ref[...])
        m = jax.nn.sigmoid(g_ref[0]) * pa_ref[...] + jax.nn.sigmoid(g_ref[1]) * pr
        pr_ref[...] = pr
        o_ref[...] = m.astype(BF16)
        ot_ref[...] = m.T.astype(BF16)

    return pl.pallas_call(
        body, name=name, grid=(S // tr,),
        in_specs=[_row_spec(tr, K), pl.BlockSpec((K, W), lambda i: (0, 0)),
                  pl.BlockSpec((2, tr, W), lambda i: (0, i, 0)), _row_spec(tr, W)],
        out_specs=[_row_spec(tr, W), _row_spec(tr, W), _col_spec(W, tr)],
        out_shape=[jax.ShapeDtypeStruct((S, W), F32), jax.ShapeDtypeStruct((S, W), BF16),
                   jax.ShapeDtypeStruct((W, S), BF16)],
        compiler_params=_cparams(("parallel",)))(yb, wrp, gates, pa)


def _out_proj_bwd_merge(dhb, wout, gates, pa, pr, *, name):
    S, W = pa.shape
    tr = 512

    def body(d_ref, w_ref, g_ref, pa_ref, pr_ref, dpa_ref, dpr_ref, dg_ref):
        dmv = _dot_nt(d_ref[...], w_ref[...])
        sa, sb = jax.nn.sigmoid(g_ref[0]), jax.nn.sigmoid(g_ref[1])
        dpa_ref[...] = (dmv * sa).astype(BF16)
        dpr_ref[...] = (dmv * sb).astype(BF16)
        dg_ref[0] = (dmv * pa_ref[...] * (sa * (1.0 - sa))).astype(BF16)
        dg_ref[1] = (dmv * pr_ref[...] * (sb * (1.0 - sb))).astype(BF16)

    g3 = pl.BlockSpec((2, tr, W), lambda i: (0, i, 0))
    return pl.pallas_call(
        body, name=name, grid=(S // tr,),
        in_specs=[_row_spec(tr, W), pl.BlockSpec((W, W), lambda i: (0, 0)), g3, _row_spec(tr, W), _row_spec(tr, W)],
        out_specs=[_row_spec(tr, W), _row_spec(tr, W), g3],
        out_shape=[jax.ShapeDtypeStruct((S, W), BF16), jax.ShapeDtypeStruct((S, W), BF16),
                   jax.ShapeDtypeStruct((2, S, W), BF16)],
        compiler_params=_cparams(("parallel",)))(dhb, wout, gates, pa, pr)


def _ffn_in_swiglu(x, wg, wu, *, name):
    S, D = x.shape
    F = wg.shape[1]
    tm, tf = 512, F // 2

    def body(x_ref, wg_ref, wu_ref, uv_ref, h_ref, ht_ref):
        xv = x_ref[...]
        u = _dot(xv, wg_ref[...])
        v = _dot(xv, wu_ref[...])
        hh = u * jax.nn.sigmoid(u) * v
        uv_ref[0] = u.astype(BF16)
        uv_ref[1] = v.astype(BF16)
        h_ref[...] = hh.astype(BF16)
        ht_ref[...] = hh.T.astype(BF16)

    w_spec = pl.BlockSpec((D, tf), lambda j, i: (0, j))
    return pl.pallas_call(
        body, name=name, grid=(F // tf, S // tm),
        in_specs=[pl.BlockSpec((tm, D), lambda j, i: (i, 0)), w_spec, w_spec],
        out_specs=[pl.BlockSpec((2, tm, tf), lambda j, i: (0, i, j)), pl.BlockSpec((tm, tf), lambda j, i: (i, j)),
                   pl.BlockSpec((tf, tm), lambda j, i: (j, i))],
        out_shape=[jax.ShapeDtypeStruct((2, S, F), BF16), jax.ShapeDtypeStruct((S, F), BF16),
                   jax.ShapeDtypeStruct((F, S), BF16)],
        compiler_params=_cparams(("parallel", "parallel")))(x, wg, wu)


def _ffn_down_bwd_swiglu(dhb, wd, uv, *, name):
    S, D = dhb.shape
    F = wd.shape[0]
    tm, tf = 512, F // 2

    def body(d_ref, w_ref, uv_ref, o_ref):
        dh = _dot_nt(d_ref[...], w_ref[...])
        u, v = uv_ref[0].astype(F32), uv_ref[1].astype(F32)
        sg = jax.nn.sigmoid(u)
        o_ref[0] = (dh * v * (sg * (1.0 + u * (1.0 - sg)))).astype(BF16)
        o_ref[1] = (dh * (u * sg)).astype(BF16)

    half = pl.BlockSpec((2, tm, tf), lambda j, i: (0, i, j))
    return pl.pallas_call(
        body, name=name, grid=(F // tf, S // tm),
        in_specs=[pl.BlockSpec((tm, D), lambda j, i: (i, 0)), pl.BlockSpec((tf, D), lambda j, i: (j, 0)), half],
        out_specs=half, out_shape=jax.ShapeDtypeStruct((2, S, F), BF16),
        compiler_params=_cparams(("parallel", "parallel")))(dhb, wd, uv)


def _assemble_dz(da, dq_r, dk_r, dv_r, dg_r, dgates, *, name):
    S = dv_r.shape[0]
    tr = 256
    GW = GROUP_WIDTH
    NCH = GW // LANES

    def body(*refs):
        a_refs = refs[0:9]
        q_ref, k_ref, v_ref, g_ref, gt_ref, dz_ref, cs_ref = refs[9:]

        @pl.when(pl.program_id(0) == 0)
        def _():
            cs_ref[...] = jnp.zeros_like(cs_ref)

        def put(off, val):
            w = val.shape[-1]
            dz_ref[:, off:off + w] = val.astype(BF16)
            cs_ref[:, off:off + w] += jnp.sum(val.astype(F32), axis=0, keepdims=True)

        for which in range(3):
            for gi in range(3):
                for c in range(NCH):
                    put(which * ATTN_W + gi * GW + c * LANES, a_refs[3 * gi + which][c])
        off = 3 * ATTN_W
        put(off, q_ref[...])
        put(off + 1024, k_ref[...])
        put(off + 2048, v_ref[...])
        put(off + 4096, g_ref[...])
        put(off + 6144, gt_ref[0])
        put(off + 7168, gt_ref[1])

    flat_a = [t for grp in da for t in grp]
    return pl.pallas_call(
        body, name=name, grid=(S // tr,),
        in_specs=[pl.BlockSpec((NCH, tr, LANES), lambda i: (0, i, 0))] * 9 + [_row_spec(tr, 1024), _row_spec(tr, 1024),
                  _row_spec(tr, 2048), _row_spec(tr, 2048), pl.BlockSpec((2, tr, 1024), lambda i: (0, i, 0))],
        out_specs=[_row_spec(tr, IN_COLS), _vec_spec(IN_COLS)],
        out_shape=[jax.ShapeDtypeStruct((S, IN_COLS), BF16), jax.ShapeDtypeStruct((1, IN_COLS), F32)],
        compiler_params=_cparams(("arbitrary",)))(*flat_a, dq_r, dk_r, dv_r, dg_r, dgates)


def _t5_bucket(dist):
    max_exact = NUM_BUCKETS // 2
    large = max_exact + (np.log(np.maximum(dist, max_exact) / max_exact)
                         / np.log(MAX_DISTANCE / max_exact) * (NUM_BUCKETS - max_exact)).astype(np.int32)
    large = np.minimum(large, NUM_BUCKETS - 1)
    return np.where(dist < max_exact, dist, large).astype(np.int32)


def _attn_tables(dilation):
    W = ATTN_BLOCK
    qi = np.arange(W)[:, None]
    kj = np.arange(2 * W)[None, :]
    rel = qi + W - kj
    valid = (rel >= 0) & (rel <= W)
    buckets = _t5_bucket(np.clip(rel, 0, W) * dilation)
    return buckets, valid


def _attn_bias(rel_bias, gi, dilation):
    buckets, valid = _attn_tables(dilation)
    table = rel_bias[:, gi * HEADS_PER_GROUP:(gi + 1) * HEADS_PER_GROUP]
    onehot = (jnp.asarray(buckets.reshape(-1, 1)) == jnp.arange(NUM_BUCKETS)[None, :]).astype(F32)
    bias = jnp.dot(onehot, table.astype(F32), precision=lax.Precision.HIGHEST)
    bias = bias.T.reshape(HEADS_PER_GROUP, ATTN_BLOCK, 2 * ATTN_BLOCK)
    return jnp.where(jnp.asarray(valid)[None], bias, NEG)


def _dot_nt(a, b):
    return lax.dot_general(a, b, (((1,), (1,)), ((), ())), preferred_element_type=F32)


def _dot_tn(a, b):
    return lax.dot_general(a, b, (((0,), (0,)), ((), ())), preferred_element_type=F32)


def _dot(a, b):
    return jnp.dot(a, b, preferred_element_type=F32)


ATTN_RESIDUES_PER_STEP = 4
ATTN_UNITS_AT_ONCE = 8
HEADS_PER_CHUNK = LANES // HEAD_DIM
N_CHUNKS = GROUP_WIDTH // LANES


def _first_block_mask(has_prev):
    col = lax.broadcasted_iota(jnp.int32, (1, 2 * ATTN_BLOCK), 1)
    return jnp.where(jnp.logical_or(has_prev, col >= ATTN_BLOCK), 0.0, NEG).astype(F32)


def _head_lanes(hh):
    return slice(HEAD_DIM * hh, HEAD_DIM * (hh + 1))


def _attn_geometry(S, d):
    rps = ATTN_RESIDUES_PER_STEP if d == 1 else min(d, ATTN_RESIDUES_PER_STEP)
    rows_per_block = ATTN_BLOCK * (rps if d == 1 else d)
    return rows_per_block, S // rows_per_block, rps, 1 if d == 1 else d // rps


def _unit_rows(d, rps, rg, u):
    B = ATTN_BLOCK
    if d == 1:
        return pl.ds(B * u, B), pl.ds(B * (u - 1 if u else rps - 1), B), u == 0
    rows = pl.ds(rg * rps + u, B, stride=d)
    return rows, rows, True


def _attn_in_specs(gi, RB, last):
    def spec(which, prev):
        if prev:
            return pl.BlockSpec((None, RB, LANES),
                                lambda j, n, rg: (9 * which + 3 * gi + j, jnp.clip(n - 1, 0, last), 0))
        return pl.BlockSpec((None, RB, LANES), lambda j, n, rg: (9 * which + 3 * gi + j, jnp.minimum(n, last), 0))
    bias = pl.BlockSpec((HEADS_PER_CHUNK, ATTN_BLOCK, 2 * ATTN_BLOCK), lambda j, n, rg: (j, 0, 0))
    return [spec(0, False), spec(1, True), spec(1, False), spec(2, True), spec(2, False), bias]


def _attn_fwd(qkv, bias, gi, d, *, name):
    _, S, _ = qkv.shape
    B = ATTN_BLOCK
    RB, nb, rps, nrg = _attn_geometry(S, d)
    scale = HEAD_DIM ** -0.5
    units = [(rr, hh) for rr in range(rps) for hh in range(HEADS_PER_CHUNK)]

    def body(q_ref, kp_ref, kc_ref, vp_ref, vc_ref, b_ref, o_ref, l_ref):
        n, rg = pl.program_id(1), pl.program_id(2)
        first = _first_block_mask(n > 0)
        ur = [_unit_rows(d, rps, rg, u) for u in range(rps)]
        rows = [r_ for r_, _, _ in ur]
        edge = [first if in_prev else 0.0 for _, _, in_prev in ur]
        q = [q_ref[r_, :].astype(BF16) for r_ in rows]
        k2 = [jnp.concatenate([(kp_ref if in_prev else kc_ref)[pr, :], kc_ref[r_, :]], axis=0).astype(BF16)
              for r_, pr, in_prev in ur]
        v2 = [jnp.concatenate([(vp_ref if in_prev else vc_ref)[pr, :], vc_ref[r_, :]], axis=0).astype(BF16)
              for r_, pr, in_prev in ur]
        o_part, l_part = {}, {}
        for u0 in range(0, len(units), ATTN_UNITS_AT_ONCE):
            us = units[u0:u0 + ATTN_UNITS_AT_ONCE]
            s = [_dot_nt(q[rr][:, _head_lanes(hh)], k2[rr][:, _head_lanes(hh)]) * scale + b_ref[hh] + edge[rr]
                 for rr, hh in us]
            m = [jnp.max(x, axis=-1, keepdims=True) for x in s]
            p = [jnp.exp(x - mm) for x, mm in zip(s, m)]
            l = [jnp.sum(x, axis=-1, keepdims=True) for x in p]
            pb = [(x * (1.0 / ll)).astype(BF16) for x, ll in zip(p, l)]
            o = [_dot(x, v2[rr][:, _head_lanes(hh)]) for x, (rr, hh) in zip(pb, us)]
            for u, oo, mm, ll in zip(us, o, m, l):
                o_part[u] = oo
                l_part[u] = jnp.broadcast_to(mm + jnp.log(ll), (B, HEAD_DIM))
        for rr in range(rps):
            o_ref[rows[rr], :] = jnp.concatenate([o_part[(rr, hh)] for hh in range(HEADS_PER_CHUNK)], axis=1)
            l_ref[rows[rr], :] = jnp.concatenate([l_part[(rr, hh)] for hh in range(HEADS_PER_CHUNK)], axis=1)

    out_spec = pl.BlockSpec((None, RB, LANES), lambda j, n, rg: (j, n, 0))
    return pl.pallas_call(
        body, name=name, grid=(N_CHUNKS, nb, nrg),
        in_specs=_attn_in_specs(gi, RB, nb - 1),
        out_specs=[out_spec, out_spec],
        out_shape=[jax.ShapeDtypeStruct((N_CHUNKS, S, LANES), F32)] * 2,
        compiler_params=_cparams(("parallel", "arbitrary", "arbitrary")))(qkv, qkv, qkv, qkv, qkv, bias)


def _attn_bwd(qkv, bias, lse, dya, ya, wts, gi, d, *, name):
    _, S, _ = qkv.shape
    B = ATTN_BLOCK
    RB, nb, rps, nrg = _attn_geometry(S, d)
    scale = HEAD_DIM ** -0.5
    units = [(rr, hh) for rr in range(rps) for hh in range(HEADS_PER_CHUNK)]

    def body(q_ref, kp_ref, kc_ref, vp_ref, vc_ref, b_ref, l_ref, dya_ref, ya_ref, w_ref,
             dq_ref, dk_ref, dv_ref, db_ref, dk_carry, dv_carry):
        n, rg = pl.program_id(1), pl.program_id(2)
        ur = [_unit_rows(d, rps, rg, u) for u in range(rps)]
        rows = [r_ for r_, _, _ in ur]

        @pl.when((n == 0) & (rg == 0))
        def _():
            db_ref[...] = jnp.zeros_like(db_ref)
            dk_carry[...] = jnp.zeros_like(dk_carry)
            dv_carry[...] = jnp.zeros_like(dv_carry)

        @pl.when(n < nb)
        def _():
            first = _first_block_mask(n > 0)
            edge = [first if in_prev else 0.0 for _, _, in_prev in ur]
            q = [q_ref[r_, :].astype(BF16) for r_ in rows]
            k2 = [jnp.concatenate([(kp_ref if in_prev else kc_ref)[pr, :], kc_ref[r_, :]], axis=0).astype(BF16)
                  for r_, pr, in_prev in ur]
            v2 = [jnp.concatenate([(vp_ref if in_prev else vc_ref)[pr, :], vc_ref[r_, :]], axis=0).astype(BF16)
                  for r_, pr, in_prev in ur]
            lse_c = [l_ref[r_, :] for r_ in rows]
            dy_c = [dya_ref[r_, :] for r_ in rows]
            ya_c = [ya_ref[r_, :] for r_ in rows]
            w_c = [w_ref[r_, :] for r_ in rows]
            ds_sum = [None] * HEADS_PER_CHUNK
            dq_part, dk_part, dv_part = {}, {}, {}
            for u0 in range(0, len(units), ATTN_UNITS_AT_ONCE):
                us = units[u0:u0 + ATTN_UNITS_AT_ONCE]
                hl = [_head_lanes(hh) for _, hh in us]
                qh = [q[rr][:, sl] for (rr, _), sl in zip(us, hl)]
                kh = [k2[rr][:, sl] for (rr, _), sl in zip(us, hl)]
                vh = [v2[rr][:, sl] for (rr, _), sl in zip(us, hl)]
                s = [_dot_nt(a, k) * scale + b_ref[hh] + edge[rr] for a, k, (rr, hh) in zip(qh, kh, us)]
                p = [jnp.exp(x - lse_c[rr][:, HEAD_DIM * hh:HEAD_DIM * hh + 1]) for x, (rr, hh) in zip(s, us)]
                dy = [dy_c[rr][:, sl] for (rr, _), sl in zip(us, hl)]
                w = [w_c[rr][:, sl] for (rr, _), sl in zip(us, hl)]
                shift = [ww[:, 0:1] * jnp.sum(d_ * ya_c[rr][:, sl], axis=-1, keepdims=True)
                         for ww, d_, (rr, _), sl in zip(w, dy, us, hl)]
                do = [(ww * d_).astype(BF16) for ww, d_ in zip(w, dy)]
                ds = [pp * (_dot_nt(o_, v) - sh) for pp, o_, v, sh in zip(p, do, vh, shift)]
                for x, (_, hh) in zip(ds, us):
                    ds_sum[hh] = x if ds_sum[hh] is None else ds_sum[hh] + x
                dsb = [x.astype(BF16) for x in ds]
                pb = [x.astype(BF16) for x in p]
                for u, x, pp, a, k, o_ in zip(us, dsb, pb, qh, kh, do):
                    dq_part[u] = _dot(x, k) * scale
                    dk_part[u] = _dot_tn(x, a) * scale
                    dv_part[u] = _dot_tn(pp, o_)
            for hh in range(HEADS_PER_CHUNK):
                db_ref[hh] += ds_sum[hh]
            dk2, dv2 = [], []
            for rr in range(rps):
                dq_ref[rows[rr], :] = jnp.concatenate([dq_part[(rr, hh)] for hh in range(HEADS_PER_CHUNK)], axis=1)
                dk2.append(jnp.concatenate([dk_part[(rr, hh)] for hh in range(HEADS_PER_CHUNK)], axis=1))
                dv2.append(jnp.concatenate([dv_part[(rr, hh)] for hh in range(HEADS_PER_CHUNK)], axis=1))
            for out_ref, carry, d2 in ((dk_ref, dk_carry, dk2), (dv_ref, dv_carry, dv2)):
                if d == 1:
                    out_ref[...] = carry[...]
                    out_ref[ur[0][1], :] += d2[0][0:B]
                    for rr in range(rps):
                        nxt = d2[rr + 1][0:B] if rr + 1 < rps else 0.0
                        carry[rows[rr], :] = d2[rr][B:2 * B] + nxt
                else:
                    for rr in range(rps):
                        out_ref[rows[rr], :] = carry[rows[rr], :] + d2[rr][0:B]
                        carry[rows[rr], :] = d2[rr][B:2 * B]

        @pl.when(n == nb)
        def _():
            for r_ in rows:
                dk_ref[r_, :] = dk_carry[r_, :]
                dv_ref[r_, :] = dv_carry[r_, :]

    last = nb - 1
    cur = pl.BlockSpec((None, RB, LANES), lambda j, n, rg: (j, jnp.minimum(n, last), 0))
    lag = pl.BlockSpec((None, RB, LANES), lambda j, n, rg: (j, jnp.maximum(n - 1, 0), 0))
    db_spec = pl.BlockSpec((HEADS_PER_CHUNK, B, 2 * B), lambda j, n, rg: (j, 0, 0))
    dq, dk, dv, db = pl.pallas_call(
        body, name=name, grid=(N_CHUNKS, nb + 1, nrg),
        in_specs=_attn_in_specs(gi, RB, last) + [cur, cur, cur, cur],
        out_specs=[cur, lag, lag, db_spec],
        out_shape=[jax.ShapeDtypeStruct((N_CHUNKS, S, LANES), F32)] * 3
        + [jax.ShapeDtypeStruct((HEADS_PER_GROUP, B, 2 * B), F32)],
        scratch_shapes=[pltpu.VMEM((RB, LANES), F32), pltpu.VMEM((RB, LANES), F32)],
        compiler_params=_cparams(("arbitrary", "arbitrary", "arbitrary")))(
            qkv, qkv, qkv, qkv, qkv, bias, lse, dya, ya, wts)
    return (dq, dk, dv), db


def _bias_grad(dbs, *, name):
    nk = ATTN_BLOCK * 2 * ATTN_BLOCK
    buckets = []
    for (_, dil) in ATTN_GROUPS:
        b, valid = _attn_tables(dil)
        buckets.append(np.where(valid, b, -1).reshape(1, nk))
    bk = jnp.asarray(np.stack(buckets).astype(np.int32))
    flat = [x.reshape(HEADS_PER_GROUP, nk) for x in dbs]

    def body(bk_ref, d0, d1, d2, o_ref):
        ids = lax.broadcasted_iota(jnp.int32, (NUM_BUCKETS, nk), 0)
        for gi, dref in enumerate((d0, d1, d2)):
            onehot = (ids == bk_ref[gi]).astype(F32)
            o_ref[gi] = lax.dot_general(onehot, dref[...], (((1,), (1,)), ((), ())),
                                        preferred_element_type=F32, precision=lax.Precision.HIGHEST)

    out = pl.pallas_call(
        body, name=name,
        out_shape=jax.ShapeDtypeStruct((3, NUM_BUCKETS, HEADS_PER_GROUP), F32),
        compiler_params=_cparams())(bk, *flat)
    return jnp.transpose(out, (1, 0, 2)).reshape(NUM_BUCKETS, 3 * HEADS_PER_GROUP)


def _ret_tables(S):
    half = RET_QK_DIM // 2
    pos = jnp.arange(S, dtype=F32)
    inv_freq = ROPE_BASE ** (-jnp.arange(half, dtype=F32) / half)
    ang = pos[:, None] * inv_freq[None]
    cos, sin = jnp.cos(ang), jnp.sin(ang)
    H, C = RET_HEADS, RET_CHUNK
    log_g = jnp.log(1.0 - 2.0 ** (-5.0 - jnp.arange(H, dtype=F32)))
    n = jnp.arange(C, dtype=F32)
    diff = n[:, None] - n[None, :]
    dmask = jnp.where(diff >= 0, jnp.exp(log_g[:, None, None] * jnp.maximum(diff, 0.0)), 0.0)
    q_dec = jnp.exp(log_g[:, None] * (n + 1.0))
    k_dec = jnp.exp(log_g[:, None] * (C - 1.0 - n))
    chunk_dec = jnp.exp(log_g * C)
    qd = jnp.broadcast_to(q_dec[:, :, None], (H, C, RET_QK_DIM))
    kd = jnp.broadcast_to(k_dec[:, :, None], (H, C, RET_QK_DIM))
    cd = jnp.broadcast_to(chunk_dec[:, None, None], (H, 1, RET_V_DIM))
    return cos, sin, dmask, qd, kd, cd


def _rot(t, cos, sin):
    half = RET_QK_DIM // 2
    t1, t2 = t[:, :half], t[:, half:]
    return jnp.concatenate([t1 * cos - t2 * sin, t1 * sin + t2 * cos], axis=-1)


def _unrot(t, cos, sin):
    half = RET_QK_DIM // 2
    t1, t2 = t[:, :half], t[:, half:]
    return jnp.concatenate([t1 * cos + t2 * sin, t2 * cos - t1 * sin], axis=-1)


def _ret_specs(rev, nC):
    C, DK, DV = RET_CHUNK, RET_QK_DIM, RET_V_DIM
    cidx = (lambda c: nC - 1 - c) if rev else (lambda c: c)
    H = RET_HEADS
    return dict(
        qk=lambda which: pl.BlockSpec((None, C, H * DK), lambda c: (which, cidx(c), 0)),
        q=pl.BlockSpec((C, H * DK), lambda c: (cidx(c), 0)),
        v=pl.BlockSpec((C, H * DV), lambda c: (cidx(c), 0)),
        cs=pl.BlockSpec((C, DK // 2), lambda c: (cidx(c), 0)),
        dmask=pl.BlockSpec((H, C, C), lambda c: (0, 0, 0)),
        dec=pl.BlockSpec((H, C, DK), lambda c: (0, 0, 0)),
        cd=pl.BlockSpec((H, 1, DV), lambda c: (0, 0, 0)),
        st=pl.BlockSpec((H, None, DK, DV), lambda c: (0, cidx(c), 0, 0)),
    )


def _ret_fwd(qk, v, g, tables, *, name):
    _, S, _ = qk.shape
    nC = S // RET_CHUNK
    C, DK, DV, H = RET_CHUNK, RET_QK_DIM, RET_V_DIM, RET_HEADS
    cos, sin, dmask, qd, kd, cd = tables
    kscale = DK ** -0.5

    def body(q_ref, k_ref, v_ref, g_ref, cos_ref, sin_ref, dm_ref, qd_ref, kd_ref, cd_ref,
             o_ref, yb_ref, ybt_ref, st_ref, state):
        @pl.when(pl.program_id(0) == 0)
        def _():
            state[...] = jnp.zeros_like(state)

        tcol = pl.multiple_of((pl.program_id(0) % RET_T_CHUNKS) * C, C)
        cs, sn = cos_ref[...], sin_ref[...]
        for h in range(H):
            qs, vs = slice(DK * h, DK * (h + 1)), slice(DV * h, DV * (h + 1))
            Q = _rot(q_ref[:, qs], cs, sn)
            K = _rot(k_ref[:, qs], cs, sn) * kscale
            Qb, Kb, V = Q.astype(BF16), K.astype(BF16), v_ref[:, vs]
            sb = state[h].astype(BF16)
            st_ref[h] = sb
            A = _dot_nt(Qb, Kb) * dm_ref[h]
            o = _dot(A.astype(BF16), V) + _dot((Q * qd_ref[h]).astype(BF16), sb)
            state[h] = state[h] * cd_ref[h] + _dot_tn((K * kd_ref[h]).astype(BF16), V)
            mu = jnp.mean(o, axis=-1, keepdims=True)
            dd = o - mu
            var = jnp.mean(dd * dd, axis=-1, keepdims=True)
            yn = dd * lax.rsqrt(var + GN_EPS)
            gv = g_ref[:, vs]
            yb = gv * jax.nn.sigmoid(gv) * yn
            o_ref[:, vs] = o
            yb_ref[:, vs] = yb.astype(BF16)
            ybt_ref[vs, pl.ds(tcol, C)] = yb.T.astype(BF16)

    sp = _ret_specs(False, nC)
    return pl.pallas_call(
        body, name=name, grid=(nC,),
        in_specs=[sp["qk"](0), sp["qk"](1), sp["v"], sp["v"], sp["cs"], sp["cs"], sp["dmask"],
                  sp["dec"], sp["dec"], sp["cd"]],
        out_specs=[sp["v"], sp["v"], pl.BlockSpec((H * DV, RET_T_CHUNKS * C), lambda c: (0, c // RET_T_CHUNKS)),
                   sp["st"]],
        out_shape=[jax.ShapeDtypeStruct((S, H * DV), F32), jax.ShapeDtypeStruct((S, H * DV), BF16),
                   jax.ShapeDtypeStruct((H * DV, S), BF16), jax.ShapeDtypeStruct((H, nC, DK, DV), BF16)],
        scratch_shapes=[pltpu.VMEM((H, DK, DV), F32)],
        compiler_params=_cparams(("arbitrary",)))(qk, qk, v, g, cos, sin, dmask, qd, kd, cd)


def _ret_bwd(dyb, qk, v, g, o, states, tables, *, name):
    _, S, _ = qk.shape
    nC = S // RET_CHUNK
    C, DK, DV, H = RET_CHUNK, RET_QK_DIM, RET_V_DIM, RET_HEADS
    cos, sin, dmask, qd, kd, cd = tables
    kscale = DK ** -0.5

    def body(dy_ref, q_ref, k_ref, v_ref, g_ref, o_ref, st_ref, cos_ref, sin_ref, dm_ref, qd_ref, kd_ref,
             cd_ref, dq_ref, dk_ref, dv_ref, dg_ref, dstate):
        @pl.when(pl.program_id(0) == 0)
        def _():
            dstate[...] = jnp.zeros_like(dstate)

        cs, sn = cos_ref[...], sin_ref[...]
        for h in range(H):
            qs, vs = slice(DK * h, DK * (h + 1)), slice(DV * h, DV * (h + 1))
            ov = o_ref[:, vs]
            mu = jnp.mean(ov, axis=-1, keepdims=True)
            dd = ov - mu
            var = jnp.mean(dd * dd, axis=-1, keepdims=True)
            rstd = lax.rsqrt(var + GN_EPS)
            yn = dd * rstd
            gv, dy = g_ref[:, vs], dy_ref[:, vs]
            sg = jax.nn.sigmoid(gv)
            dg_ref[:, vs] = (dy * yn * (sg * (1.0 + gv * (1.0 - sg)))).astype(BF16)
            dyn = dy * (gv * sg)
            dO = rstd * (dyn - jnp.mean(dyn, axis=-1, keepdims=True)
                         - yn * jnp.mean(dyn * yn, axis=-1, keepdims=True))
            dOb = dO.astype(BF16)

            Q = _rot(q_ref[:, qs], cs, sn)
            K = _rot(k_ref[:, qs], cs, sn) * kscale
            Qb, Kb, V = Q.astype(BF16), K.astype(BF16), v_ref[:, vs]
            dm, qd_h, kd_h = dm_ref[h], qd_ref[h], kd_ref[h]
            Sb = st_ref[h]
            dSb = dstate[h].astype(BF16)
            Ab = (_dot_nt(Qb, Kb) * dm).astype(BF16)
            dAb = (_dot_nt(dOb, V) * dm).astype(BF16)
            Qd = (Q * qd_h).astype(BF16)
            Kd = (K * kd_h).astype(BF16)
            dQ = _dot(dAb, Kb) + _dot_nt(dOb, Sb) * qd_h
            dK = _dot_tn(dAb, Qb) + _dot_nt(V, dSb) * kd_h
            dv_ref[:, vs] = (_dot_tn(Ab, dOb) + _dot(Kd, dSb)).astype(BF16)
            dstate[h] = dstate[h] * cd_ref[h] + _dot_tn(Qd, dOb)
            dq_ref[:, qs] = _unrot(dQ, cs, sn).astype(BF16)
            dk_ref[:, qs] = (_unrot(dK, cs, sn) * kscale).astype(BF16)

    sp = _ret_specs(True, nC)
    dq, dk, dv, dg = pl.pallas_call(
        body, name=name, grid=(nC,),
        in_specs=[sp["v"], sp["qk"](0), sp["qk"](1), sp["v"], sp["v"], sp["v"], sp["st"], sp["cs"], sp["cs"],
                  sp["dmask"], sp["dec"], sp["dec"], sp["cd"]],
        out_specs=[sp["q"], sp["q"], sp["v"], sp["v"]],
        out_shape=[jax.ShapeDtypeStruct((S, H * DK), BF16), jax.ShapeDtypeStruct((S, H * DK), BF16),
                   jax.ShapeDtypeStruct((S, H * DV), BF16), jax.ShapeDtypeStruct((S, H * DV), BF16)],
        scratch_shapes=[pltpu.VMEM((H, DK, DV), F32)],
        compiler_params=_cparams(("arbitrary",)))(dyb, qk, qk, v, g, o, states, cos, sin, dmask, qd, kd, cd)
    return dq, dk, dv, dg


def _layer_fwd(l, x, xb, x_t, weights_of, b_in, biases, ln, tables):
    S = x.shape[0]
    tag = f"l{l}"
    W = dict(weights_of(l, "in", x))
    win = W["w_in"]
    c0, c1, c2, c3, c4 = 3 * ATTN_W, 3 * ATTN_W + 2048, 3 * ATTN_W + 4096, 3 * ATTN_W + 6144, IN_COLS
    qkv_a = _mm(xb, win[:, :c0], bias=b_in[:c0], groups=3, lane_chunks=True, name=f"{tag}_in_attn")
    qk_r = _mm(xb, win[:, c0:c1], bias=b_in[c0:c1], groups=2, name=f"{tag}_in_retqk")
    v_r = _mm(xb, win[:, c1:c2], bias=b_in[c1:c2], out_dtype=BF16, name=f"{tag}_in_retv")
    g_r = _mm(xb, win[:, c2:c3], bias=b_in[c2:c3], name=f"{tag}_in_retg")
    gates = _mm(xb, win[:, c3:c4], bias=b_in[c3:c4], groups=2, name=f"{tag}_in_gates")

    os_, ls_ = [], []
    for gi, (_, dil) in enumerate(ATTN_GROUPS):
        o, lse = _attn_fwd(qkv_a, biases[gi], gi, dil, name=f"{tag}_attn_fwd{gi}")
        os_.append(o)
        ls_.append(lse)
    ya_b, ya_t, ya, wts = _combine_fwd(os_, ls_, name=f"{tag}_combine")

    o_r, yb, yb_t, states = _ret_fwd(qk_r, v_r, g_r, tables, name=f"{tag}_ret_fwd")

    W.update(weights_of(l, "rest", yb))
    W["w_gu"] = jnp.concatenate([W["w_ffn_gate"], W["w_ffn_up"]], axis=1)
    pa = _mm(ya_b, W["w_attn_proj"], name=f"{tag}_attn_proj")
    pr, merged, merged_t = _ret_proj_merge(yb, W["w_ret_proj"], gates, pa, name=f"{tag}_ret_proj")
    h1, x1, x1b, x1_t = _proj_ln(merged, W["w_out"], x, ln["ln1_g"], ln["ln1_b"], name=f"{tag}_out_proj_ln1")
    uv, hh, hh_t = _ffn_in_swiglu(x1b, W["w_ffn_gate"], W["w_ffn_up"], name=f"{tag}_ffn_in")
    h2, x2, x2b, x2_t = _proj_ln(hh, W["w_ffn_down"], x1, ln["ln2_g"], ln["ln2_b"], name=f"{tag}_ffn_down_ln2")
    saved = dict(x_t=x_t, qkv_a=qkv_a, qk_r=qk_r, v_r=v_r, g_r=g_r, gates=gates, ls=ls_, ya_t=ya_t, ya=ya,
                 wts=wts, o_r=o_r, yb_t=yb_t, states=states, pa=pa, pr=pr, merged_t=merged_t, h1=h1, x1_t=x1_t,
                 uv=uv, hh_t=hh_t, h2=h2)
    return x2, x2b, x2_t, saved, W


WEIGHT_GROUPS = {"in": ("w_in",), "proj": ("w_attn_proj", "w_ret_proj", "w_out"),
                 "ffn": ("w_ffn_gate", "w_ffn_up", "w_ffn_down")}


def _behind(value, token):
    return value if token is None else value + token[0, 0]


def _layer_bwd(l, dx2, sv, W, biases, ln, tables, token, on_grads):
    S = dx2.shape[0]
    tag = f"l{l}"
    g = {}

    def done(group):
        return None if on_grads is None else on_grads(l, group, {n: g[n] for n in WEIGHT_GROUPS[group]})

    dh2b, res2, g["ln2_g"], g["ln2_b"] = _ln_bwd(dx2, sv["h2"], _behind(ln["ln2_g"], token), name=f"{tag}_ln2_bwd")
    g["w_ffn_down"] = _mm(sv["hh_t"], dh2b, name=f"{tag}_dw_down")
    dudv = _ffn_down_bwd_swiglu(dh2b, W["w_ffn_down"], sv["uv"], name=f"{tag}_d_uv")
    dx1 = _mm(dudv, W["w_gu"], transpose_b=True, a_halves=True, add=res2, name=f"{tag}_d_x1")
    dwgu = _mm(sv["x1_t"], dudv, b_halves=True, name=f"{tag}_dw_gu")
    g["w_ffn_gate"], g["w_ffn_up"] = dwgu[:, :D_FF], dwgu[:, D_FF:]
    token = done("ffn")

    dh1b, res1, g["ln1_g"], g["ln1_b"] = _ln_bwd(dx1, sv["h1"], _behind(ln["ln1_g"], token), name=f"{tag}_ln1_bwd")
    g["w_out"] = _mm(sv["merged_t"], dh1b, name=f"{tag}_dw_out")
    dpa, dpr, dgates = _out_proj_bwd_merge(dh1b, W["w_out"], sv["gates"], sv["pa"], sv["pr"], name=f"{tag}_d_merged")
    dya = _mm(dpa, W["w_attn_proj"], transpose_b=True, groups=1, lane_chunks=True, name=f"{tag}_d_ya")
    g["w_attn_proj"] = _mm(sv["ya_t"], dpa, name=f"{tag}_dw_ap")
    dyb = _mm(dpr, W["w_ret_proj"], transpose_b=True, name=f"{tag}_d_yb")
    g["w_ret_proj"] = _mm(sv["yb_t"], dpr, name=f"{tag}_dw_rp")
    token = done("proj")
    tables = tables[:-1] + (_behind(tables[-1], token),)

    da, dbs = [], []
    for gi, (_, dil) in enumerate(ATTN_GROUPS):
        dqkv, db = _attn_bwd(sv["qkv_a"], biases[gi], sv["ls"][gi], dya, sv["ya"], sv["wts"][gi], gi, dil,
                             name=f"{tag}_attn_bwd{gi}")
        da.append(dqkv)
        dbs.append(db)
    dq_r, dk_r, dv_r, dg_r = _ret_bwd(dyb, sv["qk_r"], sv["v_r"], sv["g_r"], sv["o_r"], sv["states"], tables,
                                 name=f"{tag}_ret_bwd")
    dz, colsum = _assemble_dz(da, dq_r, dk_r, dv_r, dg_r, dgates, name=f"{tag}_assemble_dz")
    g["b_in"] = colsum.reshape(IN_COLS)
    dx = _mm(dz, W["w_in"], transpose_b=True, add=res1, name=f"{tag}_d_x")
    g["w_in"] = _mm(sv["x_t"], dz, name=f"{tag}_dw_in")
    return dx, g, dbs, done("in")


HBM_SPEC = pl.BlockSpec(memory_space=pltpu.HBM)
OTHER_CHIPS = ((1, 0), (0, 1), (1, 1))


def _flip(v, f):
    return 1 - v if f else v


def _all_gather(shards, *, name):
    n = len(shards)

    def body(*refs):
        x_refs, out_refs = refs[:n], refs[n:2 * n]
        send_sems, recv_sems, local_sems = refs[2 * n:]
        x, y, c = lax.axis_index("x"), lax.axis_index("y"), lax.axis_index("c")
        me, sibling = (x, y, c), (x, y, 1 - c)
        chips = [(_flip(x, fx), _flip(y, fy)) for fx, fy in OTHER_CHIPS]

        def copy(a, k, block, to, src=None):
            px, py, pc = block
            rows = out_refs[a].at[4 * px + 2 * py + pc]
            return pltpu.make_async_remote_copy(
                src_ref=rows if src is None else src, dst_ref=rows,
                send_sem=send_sems.at[7 * a + k], recv_sem=recv_sems.at[7 * a + k], device_id=to, device_id_type=MESH)

        mine, first, passed = [], [], []
        for a in range(n):
            cp = pltpu.make_async_copy(x_refs[a], out_refs[a].at[4 * x + 2 * y + c], local_sems.at[a])
            cp.start()
            mine.append(cp)
            first.append(copy(a, 0, me, sibling, src=x_refs[a]))
            first += [copy(a, 1 + j, me, (*chip, c), src=x_refs[a]) for j, chip in enumerate(chips)]
        for cp in first:
            cp.start()
        for j, chip in enumerate(chips):
            for a in range(n):
                copy(a, 1 + j, (*chip, c), me).wait_recv()
                cp = copy(a, 4 + j, (*chip, c), sibling)
                cp.start()
                passed.append(cp)
        for a in range(n):
            copy(a, 0, sibling, me).wait_recv()
            for j, chip in enumerate(chips):
                copy(a, 4 + j, (*chip, 1 - c), me).wait_recv()
        for cp in first + passed:
            cp.wait_send()
        for cp in mine:
            cp.wait()

    return pl.pallas_call(
        body, name=name, out_shape=[jax.ShapeDtypeStruct((N_DEV,) + s.shape, s.dtype) for s in shards],
        in_specs=[HBM_SPEC] * n, out_specs=[HBM_SPEC] * n,
        scratch_shapes=[pltpu.SemaphoreType.DMA((7 * n,)), pltpu.SemaphoreType.DMA((7 * n,)),
                        pltpu.SemaphoreType.DMA((n,))],
    )(*shards)


def _rs_sibling_exchange(g8s, *, name):
    n = len(g8s)

    def body(*refs):
        g_refs, recv_refs = refs[:n], refs[n:2 * n]
        send_sems, recv_sems = refs[2 * n:]
        x, y, c = lax.axis_index("x"), lax.axis_index("y"), lax.axis_index("c")
        copies = []
        for a in range(n):
            for k in range(4):
                cp = pltpu.make_async_remote_copy(
                    src_ref=g_refs[a].at[k, 1 - c], dst_ref=recv_refs[a].at[k], send_sem=send_sems.at[4 * a + k],
                    recv_sem=recv_sems.at[4 * a + k], device_id=(x, y, 1 - c), device_id_type=MESH)
                cp.start()
                copies.append(cp)
        for cp in copies:
            cp.wait()

    return pl.pallas_call(
        body, name=name,
        out_shape=[jax.ShapeDtypeStruct((4,) + g.shape[2:], g.dtype) for g in g8s],
        in_specs=[HBM_SPEC] * n, out_specs=[HBM_SPEC] * n,
        scratch_shapes=[pltpu.SemaphoreType.DMA((4 * n,)), pltpu.SemaphoreType.DMA((4 * n,))],
    )(*g8s)


def _rs_chip_sum(g8, recv, core, *, name):
    _, _, R, Wd = g8.shape
    tr = _div_tile(R, 256, 16)

    def body(core_ref, g_ref, r_ref, o_ref):
        o_ref[...] = (g_ref[...] + r_ref[...]).astype(BF16)

    grid_spec = pltpu.PrefetchScalarGridSpec(
        num_scalar_prefetch=1, grid=(4, R // tr),
        in_specs=[pl.BlockSpec((None, None, tr, Wd), lambda k, i, core_ref: (k, core_ref[0], i, 0)),
                  pl.BlockSpec((None, tr, Wd), lambda k, i, core_ref: (k, i, 0))],
        out_specs=pl.BlockSpec((None, tr, Wd), lambda k, i, core_ref: (k, i, 0)))
    return pl.pallas_call(
        body, name=name, grid_spec=grid_spec, out_shape=jax.ShapeDtypeStruct((4, R, Wd), BF16),
        compiler_params=_cparams(("parallel", "parallel")))(core, g8, recv)


def _rs_chip_exchange(ps, *, name):
    n = len(ps)

    def body(*refs):
        p_refs, out_refs = refs[:n], refs[n:2 * n]
        send_sems, recv_sems, local_sems = refs[2 * n:]
        x, y, c = lax.axis_index("x"), lax.axis_index("y"), lax.axis_index("c")
        my_chip = 2 * x + y
        copies = []
        for a in range(n):
            mine = pltpu.make_async_copy(p_refs[a].at[my_chip], out_refs[a].at[my_chip], local_sems.at[a])
            mine.start()
            copies.append(mine)
            for j, (fx, fy) in enumerate(OTHER_CHIPS):
                px, py = _flip(x, fx), _flip(y, fy)
                cp = pltpu.make_async_remote_copy(
                    src_ref=p_refs[a].at[2 * px + py], dst_ref=out_refs[a].at[my_chip],
                    send_sem=send_sems.at[3 * a + j], recv_sem=recv_sems.at[3 * a + j],
                    device_id=(px, py, c), device_id_type=MESH)
                cp.start()
                copies.append(cp)
        for cp in copies:
            cp.wait()

    return pl.pallas_call(
        body, name=name, out_shape=[jax.ShapeDtypeStruct(p.shape, p.dtype) for p in ps],
        in_specs=[HBM_SPEC] * n, out_specs=[HBM_SPEC] * n,
        scratch_shapes=[pltpu.SemaphoreType.DMA((3 * n,)), pltpu.SemaphoreType.DMA((3 * n,)),
                        pltpu.SemaphoreType.DMA((n,))],
    )(*ps)


SEM_SPEC = pl.BlockSpec(memory_space=pltpu.SEMAPHORE)
DATAFLOW = pltpu.SideEffectType.DATAFLOW_SIDE_EFFECTING


def _direct_copies(src_refs, land_refs, send_sems, recv_sems, per_peer):
    x, y, c = lax.axis_index("x"), lax.axis_index("y"), lax.axis_index("c")
    me = 4 * x + 2 * y + c
    copies = []
    for a, (s, l) in enumerate(zip(src_refs, land_refs)):
        for rel in range(1, N_DEV):
            px, py, pc = _flip(x, rel & 4), _flip(y, rel & 2), _flip(c, rel & 1)
            copies.append(pltpu.make_async_remote_copy(
                src_ref=s.at[4 * px + 2 * py + pc] if per_peer else s, dst_ref=l.at[me],
                send_sem=send_sems.at[7 * a + rel - 1], recv_sem=recv_sems.at[7 * a + rel - 1],
                device_id=(px, py, pc), device_id_type=MESH))
    return copies


def _exchange_start(srcs, per_peer, *, name):
    n = len(srcs)
    lands = [lax.empty((N_DEV,) + (s.shape[1:] if per_peer else s.shape), s.dtype) for s in srcs]
    operands = [pltpu.with_memory_space_constraint(t, pltpu.HBM) for t in list(srcs) + lands]

    def body(*refs):
        src_refs, land_refs = refs[:n], refs[n:2 * n]
        send_sems, recv_sems = refs[2 * n], refs[2 * n + 1]
        token = refs[-1]
        for cp in _direct_copies(src_refs, land_refs, send_sems, recv_sems, per_peer):
            cp.start()
        token[...] = jnp.zeros_like(token)

    return pl.pallas_call(
        body, name=name,
        out_shape=(pltpu.SemaphoreType.DMA((7 * n,)), pltpu.SemaphoreType.DMA((7 * n,)),
                   *[pltpu.HBM(t.shape, t.dtype) for t in operands], jax.ShapeDtypeStruct((8, LANES), F32)),
        in_specs=[HBM_SPEC] * (2 * n),
        out_specs=(SEM_SPEC, SEM_SPEC, *[HBM_SPEC] * (2 * n), pl.BlockSpec(memory_space=pltpu.VMEM)),
        input_output_aliases={i: 2 + i for i in range(2 * n)},
        compiler_params=pltpu.CompilerParams(has_side_effects=DATAFLOW))(*operands)


def _exchange_wait(started, after, per_peer, *, name):
    n = (len(started) - 3) // 2
    send_sems, recv_sems = started[0], started[1]
    thru = list(started[2:2 + 2 * n])

    def body(*refs):
        src_refs, land_refs = refs[:n], refs[n:2 * n]
        send_s, recv_s = refs[2 * n], refs[2 * n + 1]
        for cp in _direct_copies(src_refs, land_refs, send_s, recv_s, per_peer):
            cp.wait_send()
            cp.wait_recv()

    outs = pl.pallas_call(
        body, name=name, out_shape=tuple(pltpu.HBM(t.shape, t.dtype) for t in thru),
        in_specs=[HBM_SPEC] * (2 * n) + [SEM_SPEC, SEM_SPEC, pl.BlockSpec(memory_space=pl.ANY)],
        out_specs=[HBM_SPEC] * (2 * n), input_output_aliases={i: i for i in range(2 * n)},
        compiler_params=pltpu.CompilerParams(has_side_effects=DATAFLOW))(*thru, send_sems, recv_sems, after)
    return list(outs[n:])


def _all_reduce_small(v, *, name):
    R, Wd = v.shape

    def body(v_ref, out_ref, slots, send_sems, recv_sems):
        x, y, c = lax.axis_index("x"), lax.axis_index("y"), lax.axis_index("c")
        me = 4 * x + 2 * y + c
        slots[me] = v_ref[...]
        copies = []
        for rel in range(1, N_DEV):
            peer = (_flip(x, rel & 4), _flip(y, rel & 2), _flip(c, rel & 1))
            cp = pltpu.make_async_remote_copy(
                src_ref=v_ref, dst_ref=slots.at[me], send_sem=send_sems.at[rel - 1],
                recv_sem=recv_sems.at[rel - 1], device_id=peer, device_id_type=MESH)
            cp.start()
            copies.append(cp)
        for cp in copies:
            cp.wait()
        acc = slots[0]
        for j in range(1, N_DEV):
            acc = acc + slots[j]
        out_ref[...] = acc

    vm = pl.BlockSpec(memory_space=pltpu.VMEM)
    return pl.pallas_call(
        body, name=name, out_shape=jax.ShapeDtypeStruct((R, Wd), F32),
        in_specs=[vm], out_specs=vm,
        scratch_shapes=[pltpu.VMEM((N_DEV, R, Wd), F32), pltpu.SemaphoreType.DMA((7,)),
                        pltpu.SemaphoreType.DMA((7,))],
    )(v)


def _adam_math(w, g, m, v):
    m2 = ADAM_B1 * m + (1.0 - ADAM_B1) * g
    v2 = ADAM_B2 * v + (1.0 - ADAM_B2) * (g * g)
    m_hat = m2 / (1.0 - ADAM_B1 ** ADAM_STEP)
    v_hat = v2 / (1.0 - ADAM_B2 ** ADAM_STEP)
    delta = -ADAM_LR * (m_hat / (jnp.sqrt(v_hat) + ADAM_EPS) + ADAM_WD * w)
    return delta, m2, v2


def _adam_sharded(parts, w, m, v, *, name):
    _, R, Wd = w.shape
    tr = _div_tile(R, 256, 16)

    def body(p0_ref, p1_ref, w_ref, m_ref, v_ref, g_ref, d_ref, m2_ref, v2_ref):
        def slot_sum(p_ref):
            g = p_ref[0].astype(F32)
            for s in range(1, p_ref.shape[0]):
                g = g + p_ref[s].astype(F32)
            return g

        g = jnp.where(pl.program_id(0) == 0, slot_sum(p0_ref), slot_sum(p1_ref))
        delta, m2, v2 = _adam_math(w_ref[...], g, m_ref[...], v_ref[...])
        g_ref[...] = g
        d_ref[...] = delta
        m2_ref[...] = m2
        v2_ref[...] = v2

    assert DEPTH == 2
    p_specs = [pl.BlockSpec((p.shape[0], tr, Wd), lambda l, i: (0, i, 0)) for p in parts]
    s_spec = pl.BlockSpec((None, tr, Wd), lambda l, i: (l, i, 0))
    return pl.pallas_call(
        body, name=name, grid=(DEPTH, R // tr),
        in_specs=p_specs + [s_spec, s_spec, s_spec],
        out_specs=[s_spec] * 4, out_shape=[jax.ShapeDtypeStruct((DEPTH, R, Wd), F32)] * 4,
        compiler_params=_cparams(("parallel", "parallel")))(parts[0], parts[1], w, m, v)


def _adam_small(g, w, m, v, *, name):
    R, Wd = w.shape

    def body(g_ref, w_ref, m_ref, v_ref, d_ref, m2_ref, v2_ref):
        delta, m2, v2 = _adam_math(w_ref[...], g_ref[...], m_ref[...], v_ref[...])
        d_ref[...] = delta
        m2_ref[...] = m2
        v2_ref[...] = v2

    return pl.pallas_call(
        body, name=name, out_shape=[jax.ShapeDtypeStruct((R, Wd), F32)] * 3,
        compiler_params=_cparams())(g, w, m, v)


def _shard_shape(name):
    r, c = FULL_SHAPE[name]
    return (r, c // N_DEV) if name in COL_SHARDED else (r // N_DEV, c)


def _full_from_gathered(name, g):
    if name in COL_SHARDED:
        return jnp.transpose(g, (1, 0, 2)).reshape(FULL_SHAPE[name])
    return g.reshape(FULL_SHAPE[name])


def _dest_major(name, gfull):
    r, c = _shard_shape(name)
    if name in COL_SHARDED:
        blk = jnp.transpose(gfull.reshape(r, N_DEV, c), (1, 0, 2))
    else:
        blk = gfull.reshape(N_DEV, r, c)
    return blk.reshape(4, 2, r, c)


def _pack_small(t):
    flat = jnp.concatenate([t[n].reshape(-1).astype(F32) for n in SMALL_WEIGHTS])
    return jnp.pad(flat, (0, SMALL_ROWS * LANES - flat.shape[0])).reshape(SMALL_ROWS, LANES)


def _unpack_small(packed):
    flat = packed.reshape(-1)
    out, off = {}, 0
    for n in SMALL_WEIGHTS:
        size = math.prod(SMALL_SHAPE[n])
        out[n] = flat[off:off + size].reshape(SMALL_SHAPE[n])
        off += size
    return out


def _after(value, token):
    return lax.optimization_barrier((value, token))[0]


def _local_step(x, target, rel_bias, b_in, lns, weights_of, on_grads=None):
    S = x.shape[0]
    tables = _ret_tables(S)
    biases = [_attn_bias(rel_bias, gi, dil) for gi, (_, dil) in enumerate(ATTN_GROUPS)]

    h = x
    hb, h_t = _cast_transpose(x, name="cast_x")
    saved, Ws = [], []
    for l in range(DEPTH):
        h, hb, h_t, sv, W = _layer_fwd(l, h, hb, h_t, weights_of, b_in[l], biases, lns[l], tables)
        saved.append(sv)
        Ws.append(W)
    dy, sq = _loss_fwd_bwd(h, target, name="loss")
    loss_local = 0.5 * sq[0, 0] / D_MODEL

    grads = [None] * DEPTH
    db_tot = None
    dx = dy
    token = None
    for l in reversed(range(DEPTH)):
        dx, g, dbs, token = _layer_bwd(l, dx, saved[l], Ws[l], biases, lns[l], tables, token, on_grads)
        grads[l] = g
        db_tot = dbs if db_tot is None else [a + b for a, b in zip(db_tot, dbs)]
    small = {"rel_bias": _bias_grad(db_tot, name="bias_grad"),
             "b_in": jnp.stack([grads[l]["b_in"] for l in range(DEPTH)])}
    for n in ("ln1_g", "ln1_b", "ln2_g", "ln2_b"):
        small[n] = jnp.stack([grads[l][n].reshape(D_MODEL) for l in range(DEPTH)])
    return loss_local, dx, grads, small


def kernel(x, rel_bias, w_in, b_in, w_attn_proj, w_ret_proj, w_out, ln1_g, ln1_b, w_ffn_gate, w_ffn_up, w_ffn_down, ln2_g, ln2_b, loss_target, m_rel_bias, m_w_in, m_b_in, m_w_attn_proj, m_w_ret_proj, m_w_out, m_ln1_g, m_ln1_b, m_w_ffn_gate, m_w_ffn_up, m_w_ffn_down, m_ln2_g, m_ln2_b, v_rel_bias, v_w_in, v_b_in, v_w_attn_proj, v_w_ret_proj, v_w_out, v_ln1_g, v_ln1_b, v_w_ffn_gate, v_w_ffn_up, v_w_ffn_down, v_ln2_g, v_ln2_b):
    w = dict(rel_bias=rel_bias, w_in=w_in, b_in=b_in, w_attn_proj=w_attn_proj, w_ret_proj=w_ret_proj, w_out=w_out,
             ln1_g=ln1_g, ln1_b=ln1_b, w_ffn_gate=w_ffn_gate, w_ffn_up=w_ffn_up, w_ffn_down=w_ffn_down,
             ln2_g=ln2_g, ln2_b=ln2_b)
    m = dict(rel_bias=m_rel_bias, w_in=m_w_in, b_in=m_b_in, w_attn_proj=m_w_attn_proj, w_ret_proj=m_w_ret_proj,
             w_out=m_w_out, ln1_g=m_ln1_g, ln1_b=m_ln1_b, w_ffn_gate=m_w_ffn_gate, w_ffn_up=m_w_ffn_up,
             w_ffn_down=m_w_ffn_down, ln2_g=m_ln2_g, ln2_b=m_ln2_b)
    v = dict(rel_bias=v_rel_bias, w_in=v_w_in, b_in=v_b_in, w_attn_proj=v_w_attn_proj, w_ret_proj=v_w_ret_proj,
             w_out=v_w_out, ln1_g=v_ln1_g, ln1_b=v_ln1_b, w_ffn_gate=v_w_ffn_gate, w_ffn_up=v_w_ffn_up,
             w_ffn_down=v_w_ffn_down, ln2_g=v_ln2_g, ln2_b=v_ln2_b)

    assert DEPTH == 2
    me = 4 * lax.axis_index("x") + 2 * lax.axis_index("y") + lax.axis_index("c")
    core = lax.axis_index("c").astype(jnp.int32).reshape(1)

    def own_slot(lands, blocks):
        return [lax.dynamic_update_index_in_dim(land, blk, me, 0) for land, blk in zip(lands, blocks)]

    shard = {(l, n): w[n][l].astype(BF16) for l in range(DEPTH) for n in BIG_WEIGHTS}
    rest = WEIGHT_GROUPS["proj"] + WEIGHT_GROUPS["ffn"]
    (w_in0,) = _all_gather([shard[0, "w_in"]], name="all_gather_l0_in")
    gathers = {0: (rest, _exchange_start(_after([shard[0, n] for n in rest], w_in0), False,
                                         name="all_gather_l0_rest_start"))}
    first_token = gathers[0][1][-1][0, 0].astype(BF16)
    gathers[1] = (BIG_WEIGHTS, _exchange_start([shard[1, n] + first_token for n in BIG_WEIGHTS], False,
                                               name="all_gather_l1_start"))
    b_in_fwd = [_behind(b_in[0], gathers[1][1][-1]), b_in[1]]
    arrived = {}

    def weights_of(l, group, after):
        if (l, group) == (0, "in"):
            return {"w_in": _full_from_gathered("w_in", w_in0)}
        if l not in arrived:
            names, started = gathers[l]
            lands = _exchange_wait(started, after, False, name=f"all_gather_l{l}_wait")
            full = own_slot(lands, [shard[l, n] for n in names])
            arrived[l] = {n: _full_from_gathered(n, g) for n, g in zip(names, full)}
        names = WEIGHT_GROUPS["in"] if group == "in" else rest
        return {n: arrived[l][n] for n in names}

    scatters = {}

    def on_grads(l, group, gd):
        if (l, group) == (0, "in"):
            return None
        names = WEIGHT_GROUPS[group]
        blocks = [_dest_major(n, gd[n]).reshape((N_DEV,) + _shard_shape(n)).astype(BF16) for n in names]
        scatters[l, group] = (names, blocks, _exchange_start(blocks, True, name=f"rs_l{l}_{group}_start"))
        return scatters[l, group][2][-1]

    lns = [{n: w[n][l] for n in ("ln1_g", "ln1_b", "ln2_g", "ln2_b")} for l in range(DEPTH)]
    loss_local, grad_x, grads, small = _local_step(x[0], loss_target[0], rel_bias, b_in_fwd, lns, weights_of,
                                                   on_grads)
    loss = lax.psum(loss_local, ("x", "y", "c"))

    g8 = [_dest_major("w_in", grads[0]["w_in"])]
    from_sibling = _rs_sibling_exchange(g8, name="rs_sibling_exchange_l0_in")
    chip_parts = [_rs_chip_sum(g8[0], from_sibling[0], core, name="rs_chip_sum_l0_in")]
    parts = {(0, "w_in"): _rs_chip_exchange(chip_parts, name="rs_chip_exchange_l0_in")[0]}
    for (l, group), (names, blocks, started) in scatters.items():
        lands = _exchange_wait(started, parts[0, "w_in"], True, name=f"rs_l{l}_{group}_wait")
        own = [lax.dynamic_index_in_dim(b, me, 0, keepdims=False) for b in blocks]
        for n, p in zip(names, own_slot(lands, own)):
            parts[l, n] = p
    big = [{} for _ in range(4)]
    for n in BIG_WEIGHTS:
        res = _adam_sharded([parts[l, n] for l in range(DEPTH)], w[n], m[n], v[n], name=f"adam_{n}")
        for kind in range(4):
            big[kind][n] = res[kind]

    gs = _all_reduce_small(_pack_small(small), name="all_reduce_small")
    ds, ms, vs = _adam_small(gs, _pack_small(w), _pack_small(m), _pack_small(v), name="adam_small")
    sm = [_unpack_small(t) for t in (gs, ds, ms, vs)]

    outs = [loss, grad_x[None]]
    for kind in range(4):
        for n in ALL_WEIGHTS:
            outs.append(big[kind][n] if n in BIG_WEIGHTS else sm[kind][n])
    return tuple(outs)
```

```python
import math

import numpy as np
import jax
import jax.numpy as jnp
from jax import lax
from jax.experimental import pallas as pl
from jax.experimental.pallas import tpu as pltpu

F32 = jnp.float32
BF16 = jnp.bfloat16
MESH = pl.DeviceIdType.MESH

D_MODEL = 1024
DEPTH = 2
HEAD_DIM = 64
ATTN_GROUPS = ((128, 1), (512, 4), (2048, 16))
HEADS_PER_GROUP = 6
GROUP_WIDTH = HEADS_PER_GROUP * HEAD_DIM
ATTN_BLOCK = 128
NUM_BUCKETS = 32
MAX_DISTANCE = 2048
RET_HEADS = 4
RET_QK_DIM = 256
RET_V_DIM = 512
RET_CHUNK = 128
RET_T_CHUNKS = 4
ROPE_BASE = 10000.0
D_FF = 2816
ALPHA = (2 * DEPTH) ** 0.25
LN_EPS = 1e-5
GN_EPS = 1e-5
ATTN_W = 3 * GROUP_WIDTH
IN_COLS = 3 * ATTN_W + 2 * 1024 + 2 * 2048 + 2 * 1024
ADAM_LR, ADAM_B1, ADAM_B2, ADAM_EPS, ADAM_WD, ADAM_STEP = 0.001, 0.9, 0.999, 1e-08, 0.01, 10
N_DEV = 8
NEG = -1e30
LANES = 128
VMEM_LIMIT = 56 * 1024 * 1024
MM_TILE_CAP = 1664
MM_VMEM_BUDGET = 44 * 1024 * 1024

BIG_WEIGHTS = ("w_in", "w_attn_proj", "w_ret_proj", "w_out", "w_ffn_gate", "w_ffn_up", "w_ffn_down")
COL_SHARDED = ("w_in", "w_attn_proj", "w_ffn_gate", "w_ffn_up")
FULL_SHAPE = {"w_in": (D_MODEL, IN_COLS), "w_attn_proj": (GROUP_WIDTH, D_MODEL), "w_ret_proj": (2048, D_MODEL),
              "w_out": (D_MODEL, D_MODEL), "w_ffn_gate": (D_MODEL, D_FF), "w_ffn_up": (D_MODEL, D_FF),
              "w_ffn_down": (D_FF, D_MODEL)}
SMALL_WEIGHTS = ("rel_bias", "b_in", "ln1_g", "ln1_b", "ln2_g", "ln2_b")
SMALL_SHAPE = {"rel_bias": (NUM_BUCKETS, 18), "b_in": (DEPTH, IN_COLS), "ln1_g": (DEPTH, D_MODEL),
               "ln1_b": (DEPTH, D_MODEL), "ln2_g": (DEPTH, D_MODEL), "ln2_b": (DEPTH, D_MODEL)}
SMALL_ROWS = 256
ALL_WEIGHTS = ("rel_bias", "w_in", "b_in", "w_attn_proj", "w_ret_proj", "w_out", "ln1_g", "ln1_b",
               "w_ffn_gate", "w_ffn_up", "w_ffn_down", "ln2_g", "ln2_b")


def _cparams(sem=None):
    return pltpu.CompilerParams(dimension_semantics=sem, vmem_limit_bytes=VMEM_LIMIT)


def _div_tile(n, cap, unit):
    if n <= cap:
        return n
    best = None
    for t in range(unit, cap + 1, unit):
        if n % t == 0:
            best = t
    assert best is not None, (n, cap, unit)
    return best


def _mm(a, b, *, name, out_dtype=F32, bias=None, add=None, groups=None, lane_chunks=False, transpose_b=False,
        a_halves=False, b_halves=False):
    M, K = (a.shape[1], 2 * a.shape[2]) if a_halves else a.shape
    if b_halves:
        assert not transpose_b
        K2, N = b.shape[1], 2 * b.shape[2]
    else:
        N, K2 = b.shape if transpose_b else b.shape[::-1]
    assert K == K2 and a.dtype == BF16 and b.dtype == BF16
    has_bias, has_add = bias is not None, add is not None
    tm = _div_tile(M, 1024, 16)
    tn = N // groups if groups else _div_tile(N // 2 if b_halves else N, MM_TILE_CAP, LANES)
    out_bytes = jnp.dtype(out_dtype).itemsize
    k_span = K // 2 if a_halves else K

    def vmem_bytes(tk):
        return 2 * (2 * tm * tk + 2 * tk * tn + out_bytes * tm * tn + (4 * tm * tn if has_add else 0))

    tk = max(t for t in range(LANES, k_span + 1, LANES)
             if k_span % t == 0 and (t == LANES or vmem_bytes(t) <= MM_VMEM_BUDGET))
    nk = K // tk
    nch = tn // LANES
    assert nk == 1 or (out_dtype == F32 and not lane_chunks and not has_bias)

    def body(*refs):
        a_ref, b_ref = refs[0], refs[1]
        pos = 2
        bias_ref = add_ref = None
        if has_bias:
            bias_ref = refs[pos]
            pos += 1
        if has_add:
            add_ref = refs[pos]
            pos += 1
        o_ref = refs[pos]

        def finish(r):
            if has_bias:
                r = r + bias_ref[...]
            if has_add:
                r = r + add_ref[...]
            if lane_chunks:
                for c in range(nch):
                    o_ref[c] = r[:, c * LANES:(c + 1) * LANES].astype(o_ref.dtype)
            else:
                o_ref[...] = r.astype(o_ref.dtype)

        def product():
            if transpose_b:
                return lax.dot_general(a_ref[...], b_ref[...], (((1,), (1,)), ((), ())), preferred_element_type=F32)
            return jnp.dot(a_ref[...], b_ref[...], preferred_element_type=F32)

        if nk == 1:
            finish(product())
        else:
            @pl.when(pl.program_id(2) == 0)
            def _():
                o_ref[...] = add_ref[...] if has_add else jnp.zeros_like(o_ref)

            o_ref[...] += product()

    kh, nh = k_span // tk, (N // 2) // tn
    if a_halves:
        a_spec = pl.BlockSpec((None, tm, tk), lambda i, j, k: (k // kh, i, k % kh))
    else:
        a_spec = pl.BlockSpec((tm, tk), lambda i, j, k: (i, k))
    if b_halves:
        b_spec = pl.BlockSpec((None, tk, tn), lambda i, j, k: (j // nh, k, j % nh))
    elif transpose_b:
        b_spec = pl.BlockSpec((tn, tk), lambda i, j, k: (j, k))
    else:
        b_spec = pl.BlockSpec((tk, tn), lambda i, j, k: (k, j))
    in_specs = [a_spec, b_spec]
    args = [a, b]
    if has_bias:
        in_specs.append(pl.BlockSpec((1, tn), lambda i, j, k: (0, j)))
        args.append(bias.reshape(1, N).astype(F32))
    if has_add:
        in_specs.append(pl.BlockSpec((tm, tn), lambda i, j, k: (i, j)))
        args.append(add)
    if lane_chunks:
        assert groups
        out_shape = jax.ShapeDtypeStruct((groups, nch, M, LANES), out_dtype)
        out_spec = pl.BlockSpec((None, nch, tm, LANES), lambda i, j, k: (j, 0, i, 0))
    elif groups:
        out_shape = jax.ShapeDtypeStruct((groups, M, tn), out_dtype)
        out_spec = pl.BlockSpec((None, tm, tn), lambda i, j, k: (j, i, 0))
    else:
        out_shape = jax.ShapeDtypeStruct((M, N), out_dtype)
        out_spec = pl.BlockSpec((tm, tn), lambda i, j, k: (i, j))
    out = pl.pallas_call(
        body, name=name, grid=(M // tm, N // tn, nk), in_specs=in_specs, out_specs=out_spec,
        out_shape=out_shape,
        compiler_params=_cparams(("parallel", "parallel", "arbitrary")))(*args)
    return out.reshape(groups * nch, M, LANES) if lane_chunks else out


def _row_spec(tr, w):
    return pl.BlockSpec((tr, w), lambda i: (i, 0))


def _vec_spec(w):
    return pl.BlockSpec((1, w), lambda i: (0, 0))


def _col_spec(w, tr):
    return pl.BlockSpec((w, tr), lambda i: (0, i))


def _cast_transpose(x, *, name):
    S, W = x.shape
    tr = 512

    def body(x_ref, o_ref, ot_ref):
        v = x_ref[...]
        o_ref[...] = v.astype(BF16)
        ot_ref[...] = v.T.astype(BF16)

    return pl.pallas_call(
        body, name=name, grid=(S // tr,), in_specs=[_row_spec(tr, W)],
        out_specs=[_row_spec(tr, W), _col_spec(W, tr)],
        out_shape=[jax.ShapeDtypeStruct((S, W), BF16), jax.ShapeDtypeStruct((W, S), BF16)],
        compiler_params=_cparams(("parallel",)))(x)


def _proj_ln(a, w, x, g, b, *, name):
    S, K = a.shape
    W = w.shape[1]
    tr = 512

    def body(a_ref, w_ref, x_ref, g_ref, b_ref, h_ref, y_ref, yb_ref, ybt_ref):
        h = ALPHA * x_ref[...] + _dot(a_ref[...], w_ref[...])
        mu = jnp.mean(h, axis=-1, keepdims=True)
        d = h - mu
        var = jnp.mean(d * d, axis=-1, keepdims=True)
        y = d * lax.rsqrt(var + LN_EPS) * g_ref[...] + b_ref[...]
        h_ref[...] = h
        y_ref[...] = y
        yb_ref[...] = y.astype(BF16)
        ybt_ref[...] = y.T.astype(BF16)

    return pl.pallas_call(
        body, name=name, grid=(S // tr,),
        in_specs=[_row_spec(tr, K), pl.BlockSpec((K, W), lambda i: (0, 0)), _row_spec(tr, W), _vec_spec(W), _vec_spec(W)],
        out_specs=[_row_spec(tr, W)] * 3 + [_col_spec(W, tr)],
        out_shape=[jax.ShapeDtypeStruct((S, W), F32), jax.ShapeDtypeStruct((S, W), F32),
                   jax.ShapeDtypeStruct((S, W), BF16), jax.ShapeDtypeStruct((W, S), BF16)],
        compiler_params=_cparams(("parallel",)))(a, w, x, g.reshape(1, W), b.reshape(1, W))


def _ln_bwd(dy, h, g, *, name):
    S, W = dy.shape
    tr = 512

    def body(dy_ref, h_ref, g_ref, dhb_ref, res_ref, dg_ref, db_ref):
        @pl.when(pl.program_id(0) == 0)
        def _():
            dg_ref[...] = jnp.zeros_like(dg_ref)
            db_ref[...] = jnp.zeros_like(db_ref)

        hh = h_ref[...]
        mu = jnp.mean(hh, axis=-1, keepdims=True)
        d = hh - mu
        var = jnp.mean(d * d, axis=-1, keepdims=True)
        rstd = lax.rsqrt(var + LN_EPS)
        xhat = d * rstd
        dyv = dy_ref[...]
        dg_ref[...] += jnp.sum(dyv * xhat, axis=0, keepdims=True)
        db_ref[...] += jnp.sum(dyv, axis=0, keepdims=True)
        dxh = dyv * g_ref[...]
        dh = rstd * (dxh - jnp.mean(dxh, axis=-1, keepdims=True)
                     - xhat * jnp.mean(dxh * xhat, axis=-1, keepdims=True))
        dhb_ref[...] = dh.astype(BF16)
        res_ref[...] = ALPHA * dh

    return pl.pallas_call(
        body, name=name, grid=(S // tr,),
        in_specs=[_row_spec(tr, W), _row_spec(tr, W), _vec_spec(W)],
        out_specs=[_row_spec(tr, W), _row_spec(tr, W), _vec_spec(W), _vec_spec(W)],
        out_shape=[jax.ShapeDtypeStruct((S, W), BF16), jax.ShapeDtypeStruct((S, W), F32),
                   jax.ShapeDtypeStruct((1, W), F32), jax.ShapeDtypeStruct((1, W), F32)],
        compiler_params=_cparams(("arbitrary",)))(dy, h, g.reshape(1, W))


def _loss_fwd_bwd(y, target, *, name):
    S, W = y.shape
    tr = 512

    def body(y_ref, t_ref, dy_ref, acc_ref):
        @pl.when(pl.program_id(0) == 0)
        def _():
            acc_ref[...] = jnp.zeros_like(acc_ref)

        e = y_ref[...] - t_ref[...]
        acc_ref[...] += jnp.sum(jnp.sum(e * e, axis=-1, keepdims=True), axis=0, keepdims=True)
        dy_ref[...] = e * (1.0 / W)

    return pl.pallas_call(
        body, name=name, grid=(S // tr,),
        in_specs=[_row_spec(tr, W), _row_spec(tr, W)],
        out_specs=[_row_spec(tr, W), pl.BlockSpec((1, 1), lambda i: (0, 0))],
        out_shape=[jax.ShapeDtypeStruct((S, W), F32), jax.ShapeDtypeStruct((1, 1), F32)],
        compiler_params=_cparams(("arbitrary",)))(y, target)


def _combine_fwd(os_, ls_, *, name):
    NCH, S, _ = os_[0].shape
    W = NCH * LANES
    tr = 512

    def body(o0, o1, o2, l0, l1, l2, yb_ref, ybt_ref, y_ref, w0_ref, w1_ref, w2_ref):
        for c in range(NCH):
            la, lb, lc = l0[c], l1[c], l2[c]
            m = jnp.maximum(jnp.maximum(la, lb), lc)
            ea, eb, ec = jnp.exp(la - m), jnp.exp(lb - m), jnp.exp(lc - m)
            inv = 1.0 / (ea + eb + ec)
            wa, wb, wc = ea * inv, eb * inv, ec * inv
            y = wa * o0[c] + wb * o1[c] + wc * o2[c]
            y_ref[c] = y
            yb_ref[:, c * LANES:(c + 1) * LANES] = y.astype(BF16)
            ybt_ref[c * LANES:(c + 1) * LANES, :] = y.T.astype(BF16)
            w0_ref[c] = wa
            w1_ref[c] = wb
            w2_ref[c] = wc

    ch = pl.BlockSpec((NCH, tr, LANES), lambda i: (0, i, 0))
    yb, ybt, y, w0, w1, w2 = pl.pallas_call(
        body, name=name, grid=(S // tr,),
        in_specs=[ch] * 6,
        out_specs=[_row_spec(tr, W), _col_spec(W, tr)] + [ch] * 4,
        out_shape=[jax.ShapeDtypeStruct((S, W), BF16), jax.ShapeDtypeStruct((W, S), BF16)]
        + [jax.ShapeDtypeStruct((NCH, S, LANES), F32)] * 4,
        compiler_params=_cparams(("parallel",)))(*os_, *ls_)
    return yb, ybt, y, (w0, w1, w2)


def _ret_proj_merge(yb, wrp, gates, pa, *, name):
    S, K = yb.shape
    W = wrp.shape[1]
    tr = 512

    def body(y_ref, w_ref, g_ref, pa_ref, pr_ref, o_ref, ot_ref):
        pr = _dot(y_ref[...], w_ref[...])
        m = jax.nn.sigmoid(g_ref[0]) * pa_ref[...] + jax.nn.sigmoid(g_ref[1]) * pr
        pr_ref[...] = pr
        o_ref[...] = m.astype(BF16)
        ot_ref[...] = m.T.astype(BF16)

    return pl.pallas_call(
        body, name=name, grid=(S // tr,),
        in_specs=[_row_spec(tr, K), pl.BlockSpec((K, W), lambda i: (0, 0)),
                  pl.BlockSpec((2, tr, W), lambda i: (0, i, 0)), _row_spec(tr, W)],
        out_specs=[_row_spec(tr, W), _row_spec(tr, W), _col_spec(W, tr)],
        out_shape=[jax.ShapeDtypeStruct((S, W), F32), jax.ShapeDtypeStruct((S, W), BF16),
                   jax.ShapeDtypeStruct((W, S), BF16)],
        compiler_params=_cparams(("parallel",)))(yb, wrp, gates, pa)


def _out_proj_bwd_merge(dhb, wout, gates, pa, pr, *, name):
    S, W = pa.shape
    tr = 512

    def body(d_ref, w_ref, g_ref, pa_ref, pr_ref, dpa_ref, dpr_ref, dg_ref):
        dmv = _dot_nt(d_ref[...], w_ref[...])
        sa, sb = jax.nn.sigmoid(g_ref[0]), jax.nn.sigmoid(g_ref[1])
        dpa_ref[...] = (dmv * sa).astype(BF16)
        dpr_ref[...] = (dmv * sb).astype(BF16)
        dg_ref[0] = (dmv * pa_ref[...] * (sa * (1.0 - sa))).astype(BF16)
        dg_ref[1] = (dmv * pr_ref[...] * (sb * (1.0 - sb))).astype(BF16)

    g3 = pl.BlockSpec((2, tr, W), lambda i: (0, i, 0))
    return pl.pallas_call(
        body, name=name, grid=(S // tr,),
        in_specs=[_row_spec(tr, W), pl.BlockSpec((W, W), lambda i: (0, 0)), g3, _row_spec(tr, W), _row_spec(tr, W)],
        out_specs=[_row_spec(tr, W), _row_spec(tr, W), g3],
        out_shape=[jax.ShapeDtypeStruct((S, W), BF16), jax.ShapeDtypeStruct((S, W), BF16),
                   jax.ShapeDtypeStruct((2, S, W), BF16)],
        compiler_params=_cparams(("parallel",)))(dhb, wout, gates, pa, pr)


def _ffn_in_swiglu(x, wg, wu, *, name):
    S, D = x.shape
    F = wg.shape[1]
    tm, tf = 512, F // 2

    def body(x_ref, wg_ref, wu_ref, uv_ref, h_ref, ht_ref):
        xv = x_ref[...]
        u = _dot(xv, wg_ref[...])
        v = _dot(xv, wu_ref[...])
        hh = u * jax.nn.sigmoid(u) * v
        uv_ref[0] = u.astype(BF16)
        uv_ref[1] = v.astype(BF16)
        h_ref[...] = hh.astype(BF16)
        ht_ref[...] = hh.T.astype(BF16)

    w_spec = pl.BlockSpec((D, tf), lambda j, i: (0, j))
    return pl.pallas_call(
        body, name=name, grid=(F // tf, S // tm),
        in_specs=[pl.BlockSpec((tm, D), lambda j, i: (i, 0)), w_spec, w_spec],
        out_specs=[pl.BlockSpec((2, tm, tf), lambda j, i: (0, i, j)), pl.BlockSpec((tm, tf), lambda j, i: (i, j)),
                   pl.BlockSpec((tf, tm), lambda j, i: (j, i))],
        out_shape=[jax.ShapeDtypeStruct((2, S, F), BF16), jax.ShapeDtypeStruct((S, F), BF16),
                   jax.ShapeDtypeStruct((F, S), BF16)],
        compiler_params=_cparams(("parallel", "parallel")))(x, wg, wu)


def _ffn_down_bwd_swiglu(dhb, wd, uv, *, name):
    S, D = dhb.shape
    F = wd.shape[0]
    tm, tf = 512, F // 2

    def body(d_ref, w_ref, uv_ref, o_ref):
        dh = _dot_nt(d_ref[...], w_ref[...])
        u, v = uv_ref[0].astype(F32), uv_ref[1].astype(F32)
        sg = jax.nn.sigmoid(u)
        o_ref[0] = (dh * v * (sg * (1.0 + u * (1.0 - sg)))).astype(BF16)
        o_ref[1] = (dh * (u * sg)).astype(BF16)

    half = pl.BlockSpec((2, tm, tf), lambda j, i: (0, i, j))
    return pl.pallas_call(
        body, name=name, grid=(F // tf, S // tm),
        in_specs=[pl.BlockSpec((tm, D), lambda j, i: (i, 0)), pl.BlockSpec((tf, D), lambda j, i: (j, 0)), half],
        out_specs=half, out_shape=jax.ShapeDtypeStruct((2, S, F), BF16),
        compiler_params=_cparams(("parallel", "parallel")))(dhb, wd, uv)


def _assemble_dz(da, dq_r, dk_r, dv_r, dg_r, dgates, *, name):
    S = dv_r.shape[0]
    tr = 256
    GW = GROUP_WIDTH
    NCH = GW // LANES

    def body(*refs):
        a_refs = refs[0:9]
        q_ref, k_ref, v_ref, g_ref, gt_ref, dz_ref, cs_ref = refs[9:]

        @pl.when(pl.program_id(0) == 0)
        def _():
            cs_ref[...] = jnp.zeros_like(cs_ref)

        def put(off, val):
            w = val.shape[-1]
            dz_ref[:, off:off + w] = val.astype(BF16)
            cs_ref[:, off:off + w] += jnp.sum(val.astype(F32), axis=0, keepdims=True)

        for which in range(3):
            for gi in range(3):
                for c in range(NCH):
                    put(which * ATTN_W + gi * GW + c * LANES, a_refs[3 * gi + which][c])
        off = 3 * ATTN_W
        put(off, q_ref[...])
        put(off + 1024, k_ref[...])
        put(off + 2048, v_ref[...])
        put(off + 4096, g_ref[...])
        put(off + 6144, gt_ref[0])
        put(off + 7168, gt_ref[1])

    flat_a = [t for grp in da for t in grp]
    return pl.pallas_call(
        body, name=name, grid=(S // tr,),
        in_specs=[pl.BlockSpec((NCH, tr, LANES), lambda i: (0, i, 0))] * 9 + [_row_spec(tr, 1024), _row_spec(tr, 1024),
                  _row_spec(tr, 2048), _row_spec(tr, 2048), pl.BlockSpec((2, tr, 1024), lambda i: (0, i, 0))],
        out_specs=[_row_spec(tr, IN_COLS), _vec_spec(IN_COLS)],
        out_shape=[jax.ShapeDtypeStruct((S, IN_COLS), BF16), jax.ShapeDtypeStruct((1, IN_COLS), F32)],
        compiler_params=_cparams(("arbitrary",)))(*flat_a, dq_r, dk_r, dv_r, dg_r, dgates)


def _t5_bucket(dist):
    max_exact = NUM_BUCKETS // 2
    large = max_exact + (np.log(np.maximum(dist, max_exact) / max_exact)
                         / np.log(MAX_DISTANCE / max_exact) * (NUM_BUCKETS - max_exact)).astype(np.int32)
    large = np.minimum(large, NUM_BUCKETS - 1)
    return np.where(dist < max_exact, dist, large).astype(np.int32)


def _attn_tables(dilation):
    W = ATTN_BLOCK
    qi = np.arange(W)[:, None]
    kj = np.arange(2 * W)[None, :]
    rel = qi + W - kj
    valid = (rel >= 0) & (rel <= W)
    buckets = _t5_bucket(np.clip(rel, 0, W) * dilation)
    return buckets, valid


def _attn_bias(rel_bias, gi, dilation):
    buckets, valid = _attn_tables(dilation)
    table = rel_bias[:, gi * HEADS_PER_GROUP:(gi + 1) * HEADS_PER_GROUP]
    onehot = (jnp.asarray(buckets.reshape(-1, 1)) == jnp.arange(NUM_BUCKETS)[None, :]).astype(F32)
    bias = jnp.dot(onehot, table.astype(F32), precision=lax.Precision.HIGHEST)
    bias = bias.T.reshape(HEADS_PER_GROUP, ATTN_BLOCK, 2 * ATTN_BLOCK)
    return jnp.where(jnp.asarray(valid)[None], bias, NEG)


def _dot_nt(a, b):
    return lax.dot_general(a, b, (((1,), (1,)), ((), ())), preferred_element_type=F32)


def _dot_tn(a, b):
    return lax.dot_general(a, b, (((0,), (0,)), ((), ())), preferred_element_type=F32)


def _dot(a, b):
    return jnp.dot(a, b, preferred_element_type=F32)


ATTN_RESIDUES_PER_STEP = 4
ATTN_UNITS_AT_ONCE = 8
HEADS_PER_CHUNK = LANES // HEAD_DIM
N_CHUNKS = GROUP_WIDTH // LANES


def _first_block_mask(has_prev):
    col = lax.broadcasted_iota(jnp.int32, (1, 2 * ATTN_BLOCK), 1)
    return jnp.where(jnp.logical_or(has_prev, col >= ATTN_BLOCK), 0.0, NEG).astype(F32)


def _head_lanes(hh):
    return slice(HEAD_DIM * hh, HEAD_DIM * (hh + 1))


def _attn_geometry(S, d):
    rps = ATTN_RESIDUES_PER_STEP if d == 1 else min(d, ATTN_RESIDUES_PER_STEP)
    rows_per_block = ATTN_BLOCK * (rps if d == 1 else d)
    return rows_per_block, S // rows_per_block, rps, 1 if d == 1 else d // rps


def _unit_rows(d, rps, rg, u):
    B = ATTN_BLOCK
    if d == 1:
        return pl.ds(B * u, B), pl.ds(B * (u - 1 if u else rps - 1), B), u == 0
    rows = pl.ds(rg * rps + u, B, stride=d)
    return rows, rows, True


def _attn_in_specs(gi, RB, last):
    def spec(which, prev):
        if prev:
            return pl.BlockSpec((None, RB, LANES),
                                lambda j, n, rg: (9 * which + 3 * gi + j, jnp.clip(n - 1, 0, last), 0))
        return pl.BlockSpec((None, RB, LANES), lambda j, n, rg: (9 * which + 3 * gi + j, jnp.minimum(n, last), 0))
    bias = pl.BlockSpec((HEADS_PER_CHUNK, ATTN_BLOCK, 2 * ATTN_BLOCK), lambda j, n, rg: (j, 0, 0))
    return [spec(0, False), spec(1, True), spec(1, False), spec(2, True), spec(2, False), bias]


def _attn_fwd(qkv, bias, gi, d, *, name):
    _, S, _ = qkv.shape
    B = ATTN_BLOCK
    RB, nb, rps, nrg = _attn_geometry(S, d)
    scale = HEAD_DIM ** -0.5
    units = [(rr, hh) for rr in range(rps) for hh in range(HEADS_PER_CHUNK)]

    def body(q_ref, kp_ref, kc_ref, vp_ref, vc_ref, b_ref, o_ref, l_ref):
        n, rg = pl.program_id(1), pl.program_id(2)
        first = _first_block_mask(n > 0)
        ur = [_unit_rows(d, rps, rg, u) for u in range(rps)]
        rows = [r_ for r_, _, _ in ur]
        edge = [first if in_prev else 0.0 for _, _, in_prev in ur]
        q = [q_ref[r_, :].astype(BF16) for r_ in rows]
        k2 = [jnp.concatenate([(kp_ref if in_prev else kc_ref)[pr, :], kc_ref[r_, :]], axis=0).astype(BF16)
              for r_, pr, in_prev in ur]
        v2 = [jnp.concatenate([(vp_ref if in_prev else vc_ref)[pr, :], vc_ref[r_, :]], axis=0).astype(BF16)
              for r_, pr, in_prev in ur]
        o_part, l_part = {}, {}
        for u0 in range(0, len(units), ATTN_UNITS_AT_ONCE):
            us = units[u0:u0 + ATTN_UNITS_AT_ONCE]
            s = [_dot_nt(q[rr][:, _head_lanes(hh)], k2[rr][:, _head_lanes(hh)]) * scale + b_ref[hh] + edge[rr]
                 for rr, hh in us]
            m = [jnp.max(x, axis=-1, keepdims=True) for x in s]
            p = [jnp.exp(x - mm) for x, mm in zip(s, m)]
            l = [jnp.sum(x, axis=-1, keepdims=True) for x in p]
            pb = [(x * (1.0 / ll)).astype(BF16) for x, ll in zip(p, l)]
            o = [_dot(x, v2[rr][:, _head_lanes(hh)]) for x, (rr, hh) in zip(pb, us)]
            for u, oo, mm, ll in zip(us, o, m, l):
                o_part[u] = oo
                l_part[u] = jnp.broadcast_to(mm + jnp.log(ll), (B, HEAD_DIM))
        for rr in range(rps):
            o_ref[rows[rr], :] = jnp.concatenate([o_part[(rr, hh)] for hh in range(HEADS_PER_CHUNK)], axis=1)
            l_ref[rows[rr], :] = jnp.concatenate([l_part[(rr, hh)] for hh in range(HEADS_PER_CHUNK)], axis=1)

    out_spec = pl.BlockSpec((None, RB, LANES), lambda j, n, rg: (j, n, 0))
    return pl.pallas_call(
        body, name=name, grid=(N_CHUNKS, nb, nrg),
        in_specs=_attn_in_specs(gi, RB, nb - 1),
        out_specs=[out_spec, out_spec],
        out_shape=[jax.ShapeDtypeStruct((N_CHUNKS, S, LANES), F32)] * 2,
        compiler_params=_cparams(("parallel", "arbitrary", "arbitrary")))(qkv, qkv, qkv, qkv, qkv, bias)


def _attn_bwd(qkv, bias, lse, dya, ya, wts, gi, d, *, name):
    _, S, _ = qkv.shape
    B = ATTN_BLOCK
    RB, nb, rps, nrg = _attn_geometry(S, d)
    scale = HEAD_DIM ** -0.5
    units = [(rr, hh) for rr in range(rps) for hh in range(HEADS_PER_CHUNK)]

    def body(q_ref, kp_ref, kc_ref, vp_ref, vc_ref, b_ref, l_ref, dya_ref, ya_ref, w_ref,
             dq_ref, dk_ref, dv_ref, db_ref, dk_carry, dv_carry):
        n, rg = pl.program_id(1), pl.program_id(2)
        ur = [_unit_rows(d, rps, rg, u) for u in range(rps)]
        rows = [r_ for r_, _, _ in ur]

        @pl.when((n == 0) & (rg == 0))
        def _():
            db_ref[...] = jnp.zeros_like(db_ref)
            dk_carry[...] = jnp.zeros_like(dk_carry)
            dv_carry[...] = jnp.zeros_like(dv_carry)

        @pl.when(n < nb)
        def _():
            first = _first_block_mask(n > 0)
            edge = [first if in_prev else 0.0 for _, _, in_prev in ur]
            q = [q_ref[r_, :].astype(BF16) for r_ in rows]
            k2 = [jnp.concatenate([(kp_ref if in_prev else kc_ref)[pr, :], kc_ref[r_, :]], axis=0).astype(BF16)
                  for r_, pr, in_prev in ur]
            v2 = [jnp.concatenate([(vp_ref if in_prev else vc_ref)[pr, :], vc_ref[r_, :]], axis=0).astype(BF16)
                  for r_, pr, in_prev in ur]
            lse_c = [l_ref[r_, :] for r_ in rows]
            dy_c = [dya_ref[r_, :] for r_ in rows]
            ya_c = [ya_ref[r_, :] for r_ in rows]
            w_c = [w_ref[r_, :] for r_ in rows]
            ds_sum = [None] * HEADS_PER_CHUNK
            dq_part, dk_part, dv_part = {}, {}, {}
            for u0 in range(0, len(units), ATTN_UNITS_AT_ONCE):
                us = units[u0:u0 + ATTN_UNITS_AT_ONCE]
                hl = [_head_lanes(hh) for _, hh in us]
                qh = [q[rr][:, sl] for (rr, _), sl in zip(us, hl)]
                kh = [k2[rr][:, sl] for (rr, _), sl in zip(us, hl)]
                vh = [v2[rr][:, sl] for (rr, _), sl in zip(us, hl)]
                s = [_dot_nt(a, k) * scale + b_ref[hh] + edge[rr] for a, k, (rr, hh) in zip(qh, kh, us)]
                p = [jnp.exp(x - lse_c[rr][:, HEAD_DIM * hh:HEAD_DIM * hh + 1]) for x, (rr, hh) in zip(s, us)]
                dy = [dy_c[rr][:, sl] for (rr, _), sl in zip(us, hl)]
                w = [w_c[rr][:, sl] for (rr, _), sl in zip(us, hl)]
                shift = [ww[:, 0:1] * jnp.sum(d_ * ya_c[rr][:, sl], axis=-1, keepdims=True)
                         for ww, d_, (rr, _), sl in zip(w, dy, us, hl)]
                do = [(ww * d_).astype(BF16) for ww, d_ in zip(w, dy)]
                ds = [pp * (_dot_nt(o_, v) - sh) for pp, o_, v, sh in zip(p, do, vh, shift)]
                for x, (_, hh) in zip(ds, us):
                    ds_sum[hh] = x if ds_sum[hh] is None else ds_sum[hh] + x
                dsb = [x.astype(BF16) for x in ds]
                pb = [x.astype(BF16) for x in p]
                for u, x, pp, a, k, o_ in zip(us, dsb, pb, qh, kh, do):
                    dq_part[u] = _dot(x, k) * scale
                    dk_part[u] = _dot_tn(x, a) * scale
                    dv_part[u] = _dot_tn(pp, o_)
            for hh in range(HEADS_PER_CHUNK):
                db_ref[hh] += ds_sum[hh]
            dk2, dv2 = [], []
            for rr in range(rps):
                dq_ref[rows[rr], :] = jnp.concatenate([dq_part[(rr, hh)] for hh in range(HEADS_PER_CHUNK)], axis=1)
                dk2.append(jnp.concatenate([dk_part[(rr, hh)] for hh in range(HEADS_PER_CHUNK)], axis=1))
                dv2.append(jnp.concatenate([dv_part[(rr, hh)] for hh in range(HEADS_PER_CHUNK)], axis=1))
            for out_ref, carry, d2 in ((dk_ref, dk_carry, dk2), (dv_ref, dv_carry, dv2)):
                if d == 1:
                    out_ref[...] = carry[...]
                    out_ref[ur[0][1], :] += d2[0][0:B]
                    for rr in range(rps):
                        nxt = d2[rr + 1][0:B] if rr + 1 < rps else 0.0
                        carry[rows[rr], :] = d2[rr][B:2 * B] + nxt
                else:
                    for rr in range(rps):
                        out_ref[rows[rr], :] = carry[rows[rr], :] + d2[rr][0:B]
                        carry[rows[rr], :] = d2[rr][B:2 * B]

        @pl.when(n == nb)
        def _():
            for r_ in rows:
                dk_ref[r_, :] = dk_carry[r_, :]
                dv_ref[r_, :] = dv_carry[r_, :]

    last = nb - 1
    cur = pl.BlockSpec((None, RB, LANES), lambda j, n, rg: (j, jnp.minimum(n, last), 0))
    lag = pl.BlockSpec((None, RB, LANES), lambda j, n, rg: (j, jnp.maximum(n - 1, 0), 0))
    db_spec = pl.BlockSpec((HEADS_PER_CHUNK, B, 2 * B), lambda j, n, rg: (j, 0, 0))
    dq, dk, dv, db = pl.pallas_call(
        body, name=name, grid=(N_CHUNKS, nb + 1, nrg),
        in_specs=_attn_in_specs(gi, RB, last) + [cur, cur, cur, cur],
        out_specs=[cur, lag, lag, db_spec],
        out_shape=[jax.ShapeDtypeStruct((N_CHUNKS, S, LANES), F32)] * 3
        + [jax.ShapeDtypeStruct((HEADS_PER_GROUP, B, 2 * B), F32)],
        scratch_shapes=[pltpu.VMEM((RB, LANES), F32), pltpu.VMEM((RB, LANES), F32)],
        compiler_params=_cparams(("arbitrary", "arbitrary", "arbitrary")))(
            qkv, qkv, qkv, qkv, qkv, bias, lse, dya, ya, wts)
    return (dq, dk, dv), db


def _bias_grad(dbs, *, name):
    nk = ATTN_BLOCK * 2 * ATTN_BLOCK
    buckets = []
    for (_, dil) in ATTN_GROUPS:
        b, valid = _attn_tables(dil)
        buckets.append(np.where(valid, b, -1).reshape(1, nk))
    bk = jnp.asarray(np.stack(buckets).astype(np.int32))
    flat = [x.reshape(HEADS_PER_GROUP, nk) for x in dbs]

    def body(bk_ref, d0, d1, d2, o_ref):
        ids = lax.broadcasted_iota(jnp.int32, (NUM_BUCKETS, nk), 0)
        for gi, dref in enumerate((d0, d1, d2)):
            onehot = (ids == bk_ref[gi]).astype(F32)
            o_ref[gi] = lax.dot_general(onehot, dref[...], (((1,), (1,)), ((), ())),
                                        preferred_element_type=F32, precision=lax.Precision.HIGHEST)

    out = pl.pallas_call(
        body, name=name,
        out_shape=jax.ShapeDtypeStruct((3, NUM_BUCKETS, HEADS_PER_GROUP), F32),
        compiler_params=_cparams())(bk, *flat)
    return jnp.transpose(out, (1, 0, 2)).reshape(NUM_BUCKETS, 3 * HEADS_PER_GROUP)


def _ret_tables(S):
    half = RET_QK_DIM // 2
    pos = jnp.arange(S, dtype=F32)
    inv_freq = ROPE_BASE ** (-jnp.arange(half, dtype=F32) / half)
    ang = pos[:, None] * inv_freq[None]
    cos, sin = jnp.cos(ang), jnp.sin(ang)
    H, C = RET_HEADS, RET_CHUNK
    log_g = jnp.log(1.0 - 2.0 ** (-5.0 - jnp.arange(H, dtype=F32)))
    n = jnp.arange(C, dtype=F32)
    diff = n[:, None] - n[None, :]
    dmask = jnp.where(diff >= 0, jnp.exp(log_g[:, None, None] * jnp.maximum(diff, 0.0)), 0.0)
    q_dec = jnp.exp(log_g[:, None] * (n + 1.0))
    k_dec = jnp.exp(log_g[:, None] * (C - 1.0 - n))
    chunk_dec = jnp.exp(log_g * C)
    qd = jnp.broadcast_to(q_dec[:, :, None], (H, C, RET_QK_DIM))
    kd = jnp.broadcast_to(k_dec[:, :, None], (H, C, RET_QK_DIM))
    cd = jnp.broadcast_to(chunk_dec[:, None, None], (H, 1, RET_V_DIM))
    return cos, sin, dmask, qd, kd, cd


def _rot(t, cos, sin):
    half = RET_QK_DIM // 2
    t1, t2 = t[:, :half], t[:, half:]
    return jnp.concatenate([t1 * cos - t2 * sin, t1 * sin + t2 * cos], axis=-1)


def _unrot(t, cos, sin):
    half = RET_QK_DIM // 2
    t1, t2 = t[:, :half], t[:, half:]
    return jnp.concatenate([t1 * cos + t2 * sin, t2 * cos - t1 * sin], axis=-1)


def _ret_specs(rev, nC):
    C, DK, DV = RET_CHUNK, RET_QK_DIM, RET_V_DIM
    cidx = (lambda c: nC - 1 - c) if rev else (lambda c: c)
    H = RET_HEADS
    return dict(
        qk=lambda which: pl.BlockSpec((None, C, H * DK), lambda c: (which, cidx(c), 0)),
        q=pl.BlockSpec((C, H * DK), lambda c: (cidx(c), 0)),
        v=pl.BlockSpec((C, H * DV), lambda c: (cidx(c), 0)),
        cs=pl.BlockSpec((C, DK // 2), lambda c: (cidx(c), 0)),
        dmask=pl.BlockSpec((H, C, C), lambda c: (0, 0, 0)),
        dec=pl.BlockSpec((H, C, DK), lambda c: (0, 0, 0)),
        cd=pl.BlockSpec((H, 1, DV), lambda c: (0, 0, 0)),
        st=pl.BlockSpec((H, None, DK, DV), lambda c: (0, cidx(c), 0, 0)),
    )


def _ret_fwd(qk, v, g, tables, *, name):
    _, S, _ = qk.shape
    nC = S // RET_CHUNK
    C, DK, DV, H = RET_CHUNK, RET_QK_DIM, RET_V_DIM, RET_HEADS
    cos, sin, dmask, qd, kd, cd = tables
    kscale = DK ** -0.5

    def body(q_ref, k_ref, v_ref, g_ref, cos_ref, sin_ref, dm_ref, qd_ref, kd_ref, cd_ref,
             o_ref, yb_ref, ybt_ref, st_ref, state):
        @pl.when(pl.program_id(0) == 0)
        def _():
            state[...] = jnp.zeros_like(state)

        tcol = pl.multiple_of((pl.program_id(0) % RET_T_CHUNKS) * C, C)
        cs, sn = cos_ref[...], sin_ref[...]
        for h in range(H):
            qs, vs = slice(DK * h, DK * (h + 1)), slice(DV * h, DV * (h + 1))
            Q = _rot(q_ref[:, qs], cs, sn)
            K = _rot(k_ref[:, qs], cs, sn) * kscale
            Qb, Kb, V = Q.astype(BF16), K.astype(BF16), v_ref[:, vs]
            sb = state[h].astype(BF16)
            st_ref[h] = sb
            A = _dot_nt(Qb, Kb) * dm_ref[h]
            o = _dot(A.astype(BF16), V) + _dot((Q * qd_ref[h]).astype(BF16), sb)
            state[h] = state[h] * cd_ref[h] + _dot_tn((K * kd_ref[h]).astype(BF16), V)
            mu = jnp.mean(o, axis=-1, keepdims=True)
            dd = o - mu
            var = jnp.mean(dd * dd, axis=-1, keepdims=True)
            yn = dd * lax.rsqrt(var + GN_EPS)
            gv = g_ref[:, vs]
            yb = gv * jax.nn.sigmoid(gv) * yn
            o_ref[:, vs] = o
            yb_ref[:, vs] = yb.astype(BF16)
            ybt_ref[vs, pl.ds(tcol, C)] = yb.T.astype(BF16)

    sp = _ret_specs(False, nC)
    return pl.pallas_call(
        body, name=name, grid=(nC,),
        in_specs=[sp["qk"](0), sp["qk"](1), sp["v"], sp["v"], sp["cs"], sp["cs"], sp["dmask"],
                  sp["dec"], sp["dec"], sp["cd"]],
        out_specs=[sp["v"], sp["v"], pl.BlockSpec((H * DV, RET_T_CHUNKS * C), lambda c: (0, c // RET_T_CHUNKS)),
                   sp["st"]],
        out_shape=[jax.ShapeDtypeStruct((S, H * DV), F32), jax.ShapeDtypeStruct((S, H * DV), BF16),
                   jax.ShapeDtypeStruct((H * DV, S), BF16), jax.ShapeDtypeStruct((H, nC, DK, DV), BF16)],
        scratch_shapes=[pltpu.VMEM((H, DK, DV), F32)],
        compiler_params=_cparams(("arbitrary",)))(qk, qk, v, g, cos, sin, dmask, qd, kd, cd)


def _ret_bwd(dyb, qk, v, g, o, states, tables, *, name):
    _, S, _ = qk.shape
    nC = S // RET_CHUNK
    C, DK, DV, H = RET_CHUNK, RET_QK_DIM, RET_V_DIM, RET_HEADS
    cos, sin, dmask, qd, kd, cd = tables
    kscale = DK ** -0.5

    def body(dy_ref, q_ref, k_ref, v_ref, g_ref, o_ref, st_ref, cos_ref, sin_ref, dm_ref, qd_ref, kd_ref,
             cd_ref, dq_ref, dk_ref, dv_ref, dg_ref, dstate):
        @pl.when(pl.program_id(0) == 0)
        def _():
            dstate[...] = jnp.zeros_like(dstate)

        cs, sn = cos_ref[...], sin_ref[...]
        for h in range(H):
            qs, vs = slice(DK * h, DK * (h + 1)), slice(DV * h, DV * (h + 1))
            ov = o_ref[:, vs]
            mu = jnp.mean(ov, axis=-1, keepdims=True)
            dd = ov - mu
            var = jnp.mean(dd * dd, axis=-1, keepdims=True)
            rstd = lax.rsqrt(var + GN_EPS)
            yn = dd * rstd
            gv, dy = g_ref[:, vs], dy_ref[:, vs]
            sg = jax.nn.sigmoid(gv)
            dg_ref[:, vs] = (dy * yn * (sg * (1.0 + gv * (1.0 - sg)))).astype(BF16)
            dyn = dy * (gv * sg)
            dO = rstd * (dyn - jnp.mean(dyn, axis=-1, keepdims=True)
                         - yn * jnp.mean(dyn * yn, axis=-1, keepdims=True))
            dOb = dO.astype(BF16)

            Q = _rot(q_ref[:, qs], cs, sn)
            K = _rot(k_ref[:, qs], cs, sn) * kscale
            Qb, Kb, V = Q.astype(BF16), K.astype(BF16), v_ref[:, vs]
            dm, qd_h, kd_h = dm_ref[h], qd_ref[h], kd_ref[h]
            Sb = st_ref[h]
            dSb = dstate[h].astype(BF16)
            Ab = (_dot_nt(Qb, Kb) * dm).astype(BF16)
            dAb = (_dot_nt(dOb, V) * dm).astype(BF16)
            Qd = (Q * qd_h).astype(BF16)
            Kd = (K * kd_h).astype(BF16)
            dQ = _dot(dAb, Kb) + _dot_nt(dOb, Sb) * qd_h
            dK = _dot_tn(dAb, Qb) + _dot_nt(V, dSb) * kd_h
            dv_ref[:, vs] = (_dot_tn(Ab, dOb) + _dot(Kd, dSb)).astype(BF16)
            dstate[h] = dstate[h] * cd_ref[h] + _dot_tn(Qd, dOb)
            dq_ref[:, qs] = _unrot(dQ, cs, sn).astype(BF16)
            dk_ref[:, qs] = (_unrot(dK, cs, sn) * kscale).astype(BF16)

    sp = _ret_specs(True, nC)
    dq, dk, dv, dg = pl.pallas_call(
        body, name=name, grid=(nC,),
        in_specs=[sp["v"], sp["qk"](0), sp["qk"](1), sp["v"], sp["v"], sp["v"], sp["st"], sp["cs"], sp["cs"],
                  sp["dmask"], sp["dec"], sp["dec"], sp["cd"]],
        out_specs=[sp["q"], sp["q"], sp["v"], sp["v"]],
        out_shape=[jax.ShapeDtypeStruct((S, H * DK), BF16), jax.ShapeDtypeStruct((S, H * DK), BF16),
                   jax.ShapeDtypeStruct((S, H * DV), BF16), jax.ShapeDtypeStruct((S, H * DV), BF16)],
        scratch_shapes=[pltpu.VMEM((H, DK, DV), F32)],
        compiler_params=_cparams(("arbitrary",)))(dyb, qk, qk, v, g, o, states, cos, sin, dmask, qd, kd, cd)
    return dq, dk, dv, dg


def _layer_fwd(l, x, xb, x_t, weights_of, b_in, biases, ln, tables):
    S = x.shape[0]
    tag = f"l{l}"
    W = dict(weights_of(l, "in", x))
    win = W["w_in"]
    c0, c1, c2, c3, c4 = 3 * ATTN_W, 3 * ATTN_W + 2048, 3 * ATTN_W + 4096, 3 * ATTN_W + 6144, IN_COLS
    qkv_a = _mm(xb, win[:, :c0], bias=b_in[:c0], groups=3, lane_chunks=True, name=f"{tag}_in_attn")
    qk_r = _mm(xb, win[:, c0:c1], bias=b_in[c0:c1], groups=2, name=f"{tag}_in_retqk")
    v_r = _mm(xb, win[:, c1:c2], bias=b_in[c1:c2], out_dtype=BF16, name=f"{tag}_in_retv")
    g_r = _mm(xb, win[:, c2:c3], bias=b_in[c2:c3], name=f"{tag}_in_retg")
    gates = _mm(xb, win[:, c3:c4], bias=b_in[c3:c4], groups=2, name=f"{tag}_in_gates")

    os_, ls_ = [], []
    for gi, (_, dil) in enumerate(ATTN_GROUPS):
        o, lse = _attn_fwd(qkv_a, biases[gi], gi, dil, name=f"{tag}_attn_fwd{gi}")
        os_.append(o)
        ls_.append(lse)
    ya_b, ya_t, ya, wts = _combine_fwd(os_, ls_, name=f"{tag}_combine")

    o_r, yb, yb_t, states = _ret_fwd(qk_r, v_r, g_r, tables, name=f"{tag}_ret_fwd")

    W.update(weights_of(l, "rest", yb))
    W["w_gu"] = jnp.concatenate([W["w_ffn_gate"], W["w_ffn_up"]], axis=1)
    pa = _mm(ya_b, W["w_attn_proj"], name=f"{tag}_attn_proj")
    pr, merged, merged_t = _ret_proj_merge(yb, W["w_ret_proj"], gates, pa, name=f"{tag}_ret_proj")
    h1, x1, x1b, x1_t = _proj_ln(merged, W["w_out"], x, ln["ln1_g"], ln["ln1_b"], name=f"{tag}_out_proj_ln1")
    uv, hh, hh_t = _ffn_in_swiglu(x1b, W["w_ffn_gate"], W["w_ffn_up"], name=f"{tag}_ffn_in")
    h2, x2, x2b, x2_t = _proj_ln(hh, W["w_ffn_down"], x1, ln["ln2_g"], ln["ln2_b"], name=f"{tag}_ffn_down_ln2")
    saved = dict(x_t=x_t, qkv_a=qkv_a, qk_r=qk_r, v_r=v_r, g_r=g_r, gates=gates, ls=ls_, ya_t=ya_t, ya=ya,
                 wts=wts, o_r=o_r, yb_t=yb_t, states=states, pa=pa, pr=pr, merged_t=merged_t, h1=h1, x1_t=x1_t,
                 uv=uv, hh_t=hh_t, h2=h2)
    return x2, x2b, x2_t, saved, W


WEIGHT_GROUPS = {"in": ("w_in",), "proj": ("w_attn_proj", "w_ret_proj", "w_out"),
                 "ffn": ("w_ffn_gate", "w_ffn_up", "w_ffn_down")}


def _behind(value, token):
    return value if token is None else value + token[0, 0]


def _layer_bwd(l, dx2, sv, W, biases, ln, tables, token, on_grads):
    S = dx2.shape[0]
    tag = f"l{l}"
    g = {}

    def done(group):
        return None if on_grads is None else on_grads(l, group, {n: g[n] for n in WEIGHT_GROUPS[group]})

    dh2b, res2, g["ln2_g"], g["ln2_b"] = _ln_bwd(dx2, sv["h2"], _behind(ln["ln2_g"], token), name=f"{tag}_ln2_bwd")
    g["w_ffn_down"] = _mm(sv["hh_t"], dh2b, name=f"{tag}_dw_down")
    dudv = _ffn_down_bwd_swiglu(dh2b, W["w_ffn_down"], sv["uv"], name=f"{tag}_d_uv")
    dx1 = _mm(dudv, W["w_gu"], transpose_b=True, a_halves=True, add=res2, name=f"{tag}_d_x1")
    dwgu = _mm(sv["x1_t"], dudv, b_halves=True, name=f"{tag}_dw_gu")
    g["w_ffn_gate"], g["w_ffn_up"] = dwgu[:, :D_FF], dwgu[:, D_FF:]
    token = done("ffn")

    dh1b, res1, g["ln1_g"], g["ln1_b"] = _ln_bwd(dx1, sv["h1"], _behind(ln["ln1_g"], token), name=f"{tag}_ln1_bwd")
    g["w_out"] = _mm(sv["merged_t"], dh1b, name=f"{tag}_dw_out")
    dpa, dpr, dgates = _out_proj_bwd_merge(dh1b, W["w_out"], sv["gates"], sv["pa"], sv["pr"], name=f"{tag}_d_merged")
    dya = _mm(dpa, W["w_attn_proj"], transpose_b=True, groups=1, lane_chunks=True, name=f"{tag}_d_ya")
    g["w_attn_proj"] = _mm(sv["ya_t"], dpa, name=f"{tag}_dw_ap")
    dyb = _mm(dpr, W["w_ret_proj"], transpose_b=True, name=f"{tag}_d_yb")
    g["w_ret_proj"] = _mm(sv["yb_t"], dpr, name=f"{tag}_dw_rp")
    token = done("proj")
    tables = tables[:-1] + (_behind(tables[-1], token),)

    da, dbs = [], []
    for gi, (_, dil) in enumerate(ATTN_GROUPS):
        dqkv, db = _attn_bwd(sv["qkv_a"], biases[gi], sv["ls"][gi], dya, sv["ya"], sv["wts"][gi], gi, dil,
                             name=f"{tag}_attn_bwd{gi}")
        da.append(dqkv)
        dbs.append(db)
    dq_r, dk_r, dv_r, dg_r = _ret_bwd(dyb, sv["qk_r"], sv["v_r"], sv["g_r"], sv["o_r"], sv["states"], tables,
                                 name=f"{tag}_ret_bwd")
    dz, colsum = _assemble_dz(da, dq_r, dk_r, dv_r, dg_r, dgates, name=f"{tag}_assemble_dz")
    g["b_in"] = colsum.reshape(IN_COLS)
    dx = _mm(dz, W["w_in"], transpose_b=True, add=res1, name=f"{tag}_d_x")
    g["w_in"] = _mm(sv["x_t"], dz, name=f"{tag}_dw_in")
    return dx, g, dbs, done("in")


HBM_SPEC = pl.BlockSpec(memory_space=pltpu.HBM)
OTHER_CHIPS = ((1, 0), (0, 1), (1, 1))


def _flip(v, f):
    return 1 - v if f else v


def _all_gather(shards, *, name):
    n = len(shards)

    def body(*refs):
        x_refs, out_refs = refs[:n], refs[n:2 * n]
        send_sems, recv_sems, local_sems = refs[2 * n:]
        x, y, c = lax.axis_index("x"), lax.axis_index("y"), lax.axis_index("c")
        me, sibling = (x, y, c), (x, y, 1 - c)
        chips = [(_flip(x, fx), _flip(y, fy)) for fx, fy in OTHER_CHIPS]

        def copy(a, k, block, to, src=None):
            px, py, pc = block
            rows = out_refs[a].at[4 * px + 2 * py + pc]
            return pltpu.make_async_remote_copy(
                src_ref=rows if src is None else src, dst_ref=rows,
                send_sem=send_sems.at[7 * a + k], recv_sem=recv_sems.at[7 * a + k], device_id=to, device_id_type=MESH)

        mine, first, passed = [], [], []
        for a in range(n):
            cp = pltpu.make_async_copy(x_refs[a], out_refs[a].at[4 * x + 2 * y + c], local_sems.at[a])
            cp.start()
            mine.append(cp)
            first.append(copy(a, 0, me, sibling, src=x_refs[a]))
            first += [copy(a, 1 + j, me, (*chip, c), src=x_refs[a]) for j, chip in enumerate(chips)]
        for cp in first:
            cp.start()
        for j, chip in enumerate(chips):
            for a in range(n):
                copy(a, 1 + j, (*chip, c), me).wait_recv()
                cp = copy(a, 4 + j, (*chip, c), sibling)
                cp.start()
                passed.append(cp)
        for a in range(n):
            copy(a, 0, sibling, me).wait_recv()
            for j, chip in enumerate(chips):
                copy(a, 4 + j, (*chip, 1 - c), me).wait_recv()
        for cp in first + passed:
            cp.wait_send()
        for cp in mine:
            cp.wait()

    return pl.pallas_call(
        body, name=name, out_shape=[jax.ShapeDtypeStruct((N_DEV,) + s.shape, s.dtype) for s in shards],
        in_specs=[HBM_SPEC] * n, out_specs=[HBM_SPEC] * n,
        scratch_shapes=[pltpu.SemaphoreType.DMA((7 * n,)), pltpu.SemaphoreType.DMA((7 * n,)),
                        pltpu.SemaphoreType.DMA((n,))],
    )(*shards)


def _rs_sibling_exchange(g8s, *, name):
    n = len(g8s)

    def body(*refs):
        g_refs, recv_refs = refs[:n], refs[n:2 * n]
        send_sems, recv_sems = refs[2 * n:]
        x, y, c = lax.axis_index("x"), lax.axis_index("y"), lax.axis_index("c")
        copies = []
        for a in range(n):
            for k in range(4):
                cp = pltpu.make_async_remote_copy(
                    src_ref=g_refs[a].at[k, 1 - c], dst_ref=recv_refs[a].at[k], send_sem=send_sems.at[4 * a + k],
                    recv_sem=recv_sems.at[4 * a + k], device_id=(x, y, 1 - c), device_id_type=MESH)
                cp.start()
                copies.append(cp)
        for cp in copies:
            cp.wait()

    return pl.pallas_call(
        body, name=name,
        out_shape=[jax.ShapeDtypeStruct((4,) + g.shape[2:], g.dtype) for g in g8s],
        in_specs=[HBM_SPEC] * n, out_specs=[HBM_SPEC] * n,
        scratch_shapes=[pltpu.SemaphoreType.DMA((4 * n,)), pltpu.SemaphoreType.DMA((4 * n,))],
    )(*g8s)


def _rs_chip_sum(g8, recv, core, *, name):
    _, _, R, Wd = g8.shape
    tr = _div_tile(R, 256, 16)

    def body(core_ref, g_ref, r_ref, o_ref):
        o_ref[...] = (g_ref[...] + r_ref[...]).astype(BF16)

    grid_spec = pltpu.PrefetchScalarGridSpec(
        num_scalar_prefetch=1, grid=(4, R // tr),
        in_specs=[pl.BlockSpec((None, None, tr, Wd), lambda k, i, core_ref: (k, core_ref[0], i, 0)),
                  pl.BlockSpec((None, tr, Wd), lambda k, i, core_ref: (k, i, 0))],
        out_specs=pl.BlockSpec((None, tr, Wd), lambda k, i, core_ref: (k, i, 0)))
    return pl.pallas_call(
        body, name=name, grid_spec=grid_spec, out_shape=jax.ShapeDtypeStruct((4, R, Wd), BF16),
        compiler_params=_cparams(("parallel", "parallel")))(core, g8, recv)


def _rs_chip_exchange(ps, *, name):
    n = len(ps)

    def body(*refs):
        p_refs, out_refs = refs[:n], refs[n:2 * n]
        send_sems, recv_sems, local_sems = refs[2 * n:]
        x, y, c = lax.axis_index("x"), lax.axis_index("y"), lax.axis_index("c")
        my_chip = 2 * x + y
        copies = []
        for a in range(n):
            mine = pltpu.make_async_copy(p_refs[a].at[my_chip], out_refs[a].at[my_chip], local_sems.at[a])
            mine.start()
            copies.append(mine)
            for j, (fx, fy) in enumerate(OTHER_CHIPS):
                px, py = _flip(x, fx), _flip(y, fy)
                cp = pltpu.make_async_remote_copy(
                    src_ref=p_refs[a].at[2 * px + py], dst_ref=out_refs[a].at[my_chip],
                    send_sem=send_sems.at[3 * a + j], recv_sem=recv_sems.at[3 * a + j],
                    device_id=(px, py, c), device_id_type=MESH)
                cp.start()
                copies.append(cp)
        for cp in copies:
            cp.wait()

    return pl.pallas_call(
        body, name=name, out_shape=[jax.ShapeDtypeStruct(p.shape, p.dtype) for p in ps],
        in_specs=[HBM_SPEC] * n, out_specs=[HBM_SPEC] * n,
        scratch_shapes=[pltpu.SemaphoreType.DMA((3 * n,)), pltpu.SemaphoreType.DMA((3 * n,)),
                        pltpu.SemaphoreType.DMA((n,))],
    )(*ps)


SEM_SPEC = pl.BlockSpec(memory_space=pltpu.SEMAPHORE)
DATAFLOW = pltpu.SideEffectType.DATAFLOW_SIDE_EFFECTING


def _direct_copies(src_refs, land_refs, send_sems, recv_sems, per_peer):
    x, y, c = lax.axis_index("x"), lax.axis_index("y"), lax.axis_index("c")
    me = 4 * x + 2 * y + c
    copies = []
    for a, (s, l) in enumerate(zip(src_refs, land_refs)):
        for rel in range(1, N_DEV):
            px, py, pc = _flip(x, rel & 4), _flip(y, rel & 2), _flip(c, rel & 1)
            copies.append(pltpu.make_async_remote_copy(
                src_ref=s.at[4 * px + 2 * py + pc] if per_peer else s, dst_ref=l.at[me],
                send_sem=send_sems.at[7 * a + rel - 1], recv_sem=recv_sems.at[7 * a + rel - 1],
                device_id=(px, py, pc), device_id_type=MESH))
    return copies


def _exchange_start(srcs, per_peer, *, name):
    n = len(srcs)
    lands = [lax.empty((N_DEV,) + (s.shape[1:] if per_peer else s.shape), s.dtype) for s in srcs]
    operands = [pltpu.with_memory_space_constraint(t, pltpu.HBM) for t in list(srcs) + lands]

    def body(*refs):
        src_refs, land_refs = refs[:n], refs[n:2 * n]
        send_sems, recv_sems = refs[2 * n], refs[2 * n + 1]
        token = refs[-1]
        for cp in _direct_copies(src_refs, land_refs, send_sems, recv_sems, per_peer):
            cp.start()
        token[...] = jnp.zeros_like(token)

    return pl.pallas_call(
        body, name=name,
        out_shape=(pltpu.SemaphoreType.DMA((7 * n,)), pltpu.SemaphoreType.DMA((7 * n,)),
                   *[pltpu.HBM(t.shape, t.dtype) for t in operands], jax.ShapeDtypeStruct((8, LANES), F32)),
        in_specs=[HBM_SPEC] * (2 * n),
        out_specs=(SEM_SPEC, SEM_SPEC, *[HBM_SPEC] * (2 * n), pl.BlockSpec(memory_space=pltpu.VMEM)),
        input_output_aliases={i: 2 + i for i in range(2 * n)},
        compiler_params=pltpu.CompilerParams(has_side_effects=DATAFLOW))(*operands)


def _exchange_wait(started, after, per_peer, *, name):
    n = (len(started) - 3) // 2
    send_sems, recv_sems = started[0], started[1]
    thru = list(started[2:2 + 2 * n])

    def body(*refs):
        src_refs, land_refs = refs[:n], refs[n:2 * n]
        send_s, recv_s = refs[2 * n], refs[2 * n + 1]
        for cp in _direct_copies(src_refs, land_refs, send_s, recv_s, per_peer):
            cp.wait_send()
            cp.wait_recv()

    outs = pl.pallas_call(
        body, name=name, out_shape=tuple(pltpu.HBM(t.shape, t.dtype) for t in thru),
        in_specs=[HBM_SPEC] * (2 * n) + [SEM_SPEC, SEM_SPEC, pl.BlockSpec(memory_space=pl.ANY)],
        out_specs=[HBM_SPEC] * (2 * n), input_output_aliases={i: i for i in range(2 * n)},
        compiler_params=pltpu.CompilerParams(has_side_effects=DATAFLOW))(*thru, send_sems, recv_sems, after)
    return list(outs[n:])


def _all_reduce_small(v, *, name):
    R, Wd = v.shape

    def body(v_ref, out_ref, slots, send_sems, recv_sems):
        x, y, c = lax.axis_index("x"), lax.axis_index("y"), lax.axis_index("c")
        me = 4 * x + 2 * y + c
        slots[me] = v_ref[...]
        copies = []
        for rel in range(1, N_DEV):
            peer = (_flip(x, rel & 4), _flip(y, rel & 2), _flip(c, rel & 1))
            cp = pltpu.make_async_remote_copy(
                src_ref=v_ref, dst_ref=slots.at[me], send_sem=send_sems.at[rel - 1],
                recv_sem=recv_sems.at[rel - 1], device_id=peer, device_id_type=MESH)
            cp.start()
            copies.append(cp)
        for cp in copies:
            cp.wait()
        acc = slots[0]
        for j in range(1, N_DEV):
            acc = acc + slots[j]
        out_ref[...] = acc

    vm = pl.BlockSpec(memory_space=pltpu.VMEM)
    return pl.pallas_call(
        body, name=name, out_shape=jax.ShapeDtypeStruct((R, Wd), F32),
        in_specs=[vm], out_specs=vm,
        scratch_shapes=[pltpu.VMEM((N_DEV, R, Wd), F32), pltpu.SemaphoreType.DMA((7,)),
                        pltpu.SemaphoreType.DMA((7,))],
    )(v)


def _adam_math(w, g, m, v):
    m2 = ADAM_B1 * m + (1.0 - ADAM_B1) * g
    v2 = ADAM_B2 * v + (1.0 - ADAM_B2) * (g * g)
    m_hat = m2 / (1.0 - ADAM_B1 ** ADAM_STEP)
    v_hat = v2 / (1.0 - ADAM_B2 ** ADAM_STEP)
    delta = -ADAM_LR * (m_hat / (jnp.sqrt(v_hat) + ADAM_EPS) + ADAM_WD * w)
    return delta, m2, v2


def _adam_sharded(parts, w, m, v, *, name):
    _, R, Wd = w.shape
    tr = _div_tile(R, 256, 16)

    def body(p0_ref, p1_ref, w_ref, m_ref, v_ref, g_ref, d_ref, m2_ref, v2_ref):
        def slot_sum(p_ref):
            g = p_ref[0].astype(F32)
            for s in range(1, p_ref.shape[0]):
                g = g + p_ref[s].astype(F32)
            return g

        g = jnp.where(pl.program_id(0) == 0, slot_sum(p0_ref), slot_sum(p1_ref))
        delta, m2, v2 = _adam_math(w_ref[...], g, m_ref[...], v_ref[...])
        g_ref[...] = g
        d_ref[...] = delta
        m2_ref[...] = m2
        v2_ref[...] = v2

    assert DEPTH == 2
    p_specs = [pl.BlockSpec((parts[0].shape[0], tr, Wd), lambda l, i: (0, i * (1 - l), 0)),
               pl.BlockSpec((parts[1].shape[0], tr, Wd), lambda l, i: (0, i * l, 0))]
    s_spec = pl.BlockSpec((None, tr, Wd), lambda l, i: (l, i, 0))
    return pl.pallas_call(
        body, name=name, grid=(DEPTH, R // tr),
        in_specs=p_specs + [s_spec, s_spec, s_spec],
        out_specs=[s_spec] * 4, out_shape=[jax.ShapeDtypeStruct((DEPTH, R, Wd), F32)] * 4,
        compiler_params=_cparams(("parallel", "parallel")))(parts[0], parts[1], w, m, v)


def _adam_small(g, w, m, v, *, name):
    R, Wd = w.shape

    def body(g_ref, w_ref, m_ref, v_ref, d_ref, m2_ref, v2_ref):
        delta, m2, v2 = _adam_math(w_ref[...], g_ref[...], m_ref[...], v_ref[...])
        d_ref[...] = delta
        m2_ref[...] = m2
        v2_ref[...] = v2

    return pl.pallas_call(
        body, name=name, out_shape=[jax.ShapeDtypeStruct((R, Wd), F32)] * 3,
        compiler_params=_cparams())(g, w, m, v)


def _shard_shape(name):
    r, c = FULL_SHAPE[name]
    return (r, c // N_DEV) if name in COL_SHARDED else (r // N_DEV, c)


def _full_from_gathered(name, g):
    if name in COL_SHARDED:
        return jnp.transpose(g, (1, 0, 2)).reshape(FULL_SHAPE[name])
    return g.reshape(FULL_SHAPE[name])


def _dest_major(name, gfull):
    r, c = _shard_shape(name)
    if name in COL_SHARDED:
        blk = jnp.transpose(gfull.reshape(r, N_DEV, c), (1, 0, 2))
    else:
        blk = gfull.reshape(N_DEV, r, c)
    return blk.reshape(4, 2, r, c)


def _pack_small(t):
    flat = jnp.concatenate([t[n].reshape(-1).astype(F32) for n in SMALL_WEIGHTS])
    return jnp.pad(flat, (0, SMALL_ROWS * LANES - flat.shape[0])).reshape(SMALL_ROWS, LANES)


def _unpack_small(packed):
    flat = packed.reshape(-1)
    out, off = {}, 0
    for n in SMALL_WEIGHTS:
        size = math.prod(SMALL_SHAPE[n])
        out[n] = flat[off:off + size].reshape(SMALL_SHAPE[n])
        off += size
    return out


def _after(value, token):
    return lax.optimization_barrier((value, token))[0]


def _local_step(x, target, rel_bias, b_in, lns, weights_of, on_grads=None):
    S = x.shape[0]
    tables = _ret_tables(S)
    biases = [_attn_bias(rel_bias, gi, dil) for gi, (_, dil) in enumerate(ATTN_GROUPS)]

    h = x
    hb, h_t = _cast_transpose(x, name="cast_x")
    saved, Ws = [], []
    for l in range(DEPTH):
        h, hb, h_t, sv, W = _layer_fwd(l, h, hb, h_t, weights_of, b_in[l], biases, lns[l], tables)
        saved.append(sv)
        Ws.append(W)
    dy, sq = _loss_fwd_bwd(h, target, name="loss")
    loss_local = 0.5 * sq[0, 0] / D_MODEL

    grads = [None] * DEPTH
    db_tot = None
    dx = dy
    token = None
    for l in reversed(range(DEPTH)):
        dx, g, dbs, token = _layer_bwd(l, dx, saved[l], Ws[l], biases, lns[l], tables, token, on_grads)
        grads[l] = g
        db_tot = dbs if db_tot is None else [a + b for a, b in zip(db_tot, dbs)]
    small = {"rel_bias": _bias_grad(db_tot, name="bias_grad"),
             "b_in": jnp.stack([grads[l]["b_in"] for l in range(DEPTH)])}
    for n in ("ln1_g", "ln1_b", "ln2_g", "ln2_b"):
        small[n] = jnp.stack([grads[l][n].reshape(D_MODEL) for l in range(DEPTH)])
    return loss_local, dx, grads, small


def kernel(x, rel_bias, w_in, b_in, w_attn_proj, w_ret_proj, w_out, ln1_g, ln1_b, w_ffn_gate, w_ffn_up, w_ffn_down, ln2_g, ln2_b, loss_target, m_rel_bias, m_w_in, m_b_in, m_w_attn_proj, m_w_ret_proj, m_w_out, m_ln1_g, m_ln1_b, m_w_ffn_gate, m_w_ffn_up, m_w_ffn_down, m_ln2_g, m_ln2_b, v_rel_bias, v_w_in, v_b_in, v_w_attn_proj, v_w_ret_proj, v_w_out, v_ln1_g, v_ln1_b, v_w_ffn_gate, v_w_ffn_up, v_w_ffn_down, v_ln2_g, v_ln2_b):
    w = dict(rel_bias=rel_bias, w_in=w_in, b_in=b_in, w_attn_proj=w_attn_proj, w_ret_proj=w_ret_proj, w_out=w_out,
             ln1_g=ln1_g, ln1_b=ln1_b, w_ffn_gate=w_ffn_gate, w_ffn_up=w_ffn_up, w_ffn_down=w_ffn_down,
             ln2_g=ln2_g, ln2_b=ln2_b)
    m = dict(rel_bias=m_rel_bias, w_in=m_w_in, b_in=m_b_in, w_attn_proj=m_w_attn_proj, w_ret_proj=m_w_ret_proj,
             w_out=m_w_out, ln1_g=m_ln1_g, ln1_b=m_ln1_b, w_ffn_gate=m_w_ffn_gate, w_ffn_up=m_w_ffn_up,
             w_ffn_down=m_w_ffn_down, ln2_g=m_ln2_g, ln2_b=m_ln2_b)
    v = dict(rel_bias=v_rel_bias, w_in=v_w_in, b_in=v_b_in, w_attn_proj=v_w_attn_proj, w_ret_proj=v_w_ret_proj,
             w_out=v_w_out, ln1_g=v_ln1_g, ln1_b=v_ln1_b, w_ffn_gate=v_w_ffn_gate, w_ffn_up=v_w_ffn_up,
             w_ffn_down=v_w_ffn_down, ln2_g=v_ln2_g, ln2_b=v_ln2_b)

    assert DEPTH == 2
    me = 4 * lax.axis_index("x") + 2 * lax.axis_index("y") + lax.axis_index("c")
    core = lax.axis_index("c").astype(jnp.int32).reshape(1)

    def own_slot(lands, blocks):
        return [lax.dynamic_update_index_in_dim(land, blk, me, 0) for land, blk in zip(lands, blocks)]

    shard = {(l, n): w[n][l].astype(BF16) for l in range(DEPTH) for n in BIG_WEIGHTS}
    rest = WEIGHT_GROUPS["proj"] + WEIGHT_GROUPS["ffn"]
    (w_in0,) = _all_gather([shard[0, "w_in"]], name="all_gather_l0_in")
    gathers = {0: (rest, _exchange_start(_after([shard[0, n] for n in rest], w_in0), False,
                                         name="all_gather_l0_rest_start"))}
    first_token = gathers[0][1][-1][0, 0].astype(BF16)
    gathers[1] = (BIG_WEIGHTS, _exchange_start([shard[1, n] + first_token for n in BIG_WEIGHTS], False,
                                               name="all_gather_l1_start"))
    b_in_fwd = [_behind(b_in[0], gathers[1][1][-1]), b_in[1]]
    arrived = {}

    def weights_of(l, group, after):
        if (l, group) == (0, "in"):
            return {"w_in": _full_from_gathered("w_in", w_in0)}
        if l not in arrived:
            names, started = gathers[l]
            lands = _exchange_wait(started, after, False, name=f"all_gather_l{l}_wait")
            full = own_slot(lands, [shard[l, n] for n in names])
            arrived[l] = {n: _full_from_gathered(n, g) for n, g in zip(names, full)}
        names = WEIGHT_GROUPS["in"] if group == "in" else rest
        return {n: arrived[l][n] for n in names}

    scatters = {}

    def on_grads(l, group, gd):
        if (l, group) == (0, "in"):
            return None
        names = WEIGHT_GROUPS[group]
        blocks = [_dest_major(n, gd[n]).reshape((N_DEV,) + _shard_shape(n)).astype(BF16) for n in names]
        scatters[l, group] = (names, blocks, _exchange_start(blocks, True, name=f"rs_l{l}_{group}_start"))
        return scatters[l, group][2][-1]

    lns = [{n: w[n][l] for n in ("ln1_g", "ln1_b", "ln2_g", "ln2_b")} for l in range(DEPTH)]
    loss_local, grad_x, grads, small = _local_step(x[0], loss_target[0], rel_bias, b_in_fwd, lns, weights_of,
                                                   on_grads)
    loss = lax.psum(loss_local, ("x", "y", "c"))

    g8 = [_dest_major("w_in", grads[0]["w_in"])]
    from_sibling = _rs_sibling_exchange(g8, name="rs_sibling_exchange_l0_in")
    chip_parts = [_rs_chip_sum(g8[0], from_sibling[0], core, name="rs_chip_sum_l0_in")]
    parts = {(0, "w_in"): _rs_chip_exchange(chip_parts, name="rs_chip_exchange_l0_in")[0]}
    for (l, group), (names, blocks, started) in scatters.items():
        lands = _exchange_wait(started, parts[0, "w_in"], True, name=f"rs_l{l}_{group}_wait")
        own = [lax.dynamic_index_in_dim(b, me, 0, keepdims=False) for b in blocks]
        for n, p in zip(names, own_slot(lands, own)):
            parts[l, n] = p
    big = [{} for _ in range(4)]
    for n in BIG_WEIGHTS:
        res = _adam_sharded([parts[l, n] for l in range(DEPTH)], w[n], m[n], v[n], name=f"adam_{n}")
        for kind in range(4):
            big[kind][n] = res[kind]

    gs = _all_reduce_small(_pack_small(small), name="all_reduce_small")
    ds, ms, vs = _adam_small(gs, _pack_small(w), _pack_small(m), _pack_small(v), name="adam_small")
    sm = [_unpack_small(t) for t in (gs, ds, ms, vs)]

    outs = [loss, grad_x[None]]
    for kind in range(4):
        for n in ALL_WEIGHTS:
            outs.append(big[kind][n] if n in BIG_WEIGHTS else sm[kind][n])
    return tuple(outs)
```
